```python
import jax, jax.numpy as jnp
from jax import lax
import numpy as np

D_MODEL = 2048
BATCH = 8
SEQ = 2048
DEPTH = 4

HEAD_DIM = 128
N_HEADS = D_MODEL // HEAD_DIM
N_SB_HEADS = N_HEADS // 2
N_FOX_HEADS = N_HEADS - N_SB_HEADS
Q_BLOCK = 128
POOL_WINDOWS = (2, 4, 8, 16)
N_POOL_GROUPS = len(POOL_WINDOWS)
POOL_GROUP = D_MODEL // N_POOL_GROUPS
D_FF = ((int(8 * D_MODEL / 3) + 255) // 256) * 256
CONV_WIDTH = 3
N_MOD = 6
EPS = 1e-6
N_ATTN_LAYERS = (DEPTH + 1) // 2
N_POOL_LAYERS = DEPTH // 2

kernel_name = "stickbreak_fox_pool_convffn_adaln"


def rms_norm(x, gain):
    xf = x.astype(jnp.float32)
    y = xf * lax.rsqrt(jnp.mean(xf * xf, axis=-1, keepdims=True) + EPS)
    return (y * gain.astype(jnp.float32)).astype(x.dtype)


def modulate(h, shift, scale):
    return h * (1.0 + scale[:, None, :]) + shift[:, None, :]


def attention_mixer(h, w_in, b_forget, w_out):
    B, S, D = h.shape
    proj = h @ w_in
    qkv = proj[..., :3 * D].reshape(B, S, 3, N_HEADS, HEAD_DIM)
    q = qkv[:, :, 0].transpose(0, 2, 1, 3)
    k = qkv[:, :, 1].transpose(0, 2, 1, 3)
    v = qkv[:, :, 2].transpose(0, 2, 1, 3)
    f_logit = proj[..., 3 * D:].astype(jnp.float32) + b_forget.astype(jnp.float32)
    F = jnp.cumsum(jax.nn.log_sigmoid(f_logit), axis=1).transpose(0, 2, 1)
    nb = S // Q_BLOCK
    q_blocks = q.reshape(B, N_HEADS, nb, Q_BLOCK, HEAD_DIM).transpose(2, 0, 1, 3, 4)
    F_blocks = F.reshape(B, N_FOX_HEADS, nb, Q_BLOCK).transpose(2, 0, 1, 3)
    starts = jnp.arange(nb, dtype=jnp.int32) * Q_BLOCK
    s_idx = jnp.arange(S)
    v_sb, v_fx = v[:, :N_SB_HEADS], v[:, N_SB_HEADS:]
    scale = HEAD_DIM ** -0.5

    def block(args):
        qb, Fq, start = args
        t_idx = start + jnp.arange(Q_BLOCK)
        z = jnp.einsum('bhqd,bhkd->bhqk', qb, k).astype(jnp.float32) * scale
        z_sb, z_fx = z[:, :N_SB_HEADS], z[:, N_SB_HEADS:]
        strict = s_idx[None, :] < t_idx[:, None]
        log_rest = jnp.where(strict, jax.nn.log_sigmoid(-z_sb), 0.0)
        log_after = lax.cumsum(log_rest, axis=3, reverse=True) - log_rest
        w_sb = jnp.where(strict, jnp.exp(jax.nn.log_sigmoid(z_sb) + log_after), 0.0)
        o_sb = jnp.einsum('bhqk,bhkd->bhqd', w_sb.astype(v.dtype), v_sb)
        causal = s_idx[None, :] <= t_idx[:, None]
        logits = z_fx + Fq[..., :, None] - F[:, :, None, :]
        p = jax.nn.softmax(jnp.where(causal, logits, -jnp.inf), axis=-1)
        o_fx = jnp.einsum('bhqk,bhkd->bhqd', p.astype(v.dtype), v_fx)
        return jnp.concatenate([o_sb, o_fx], axis=1)

    out = lax.map(block, (q_blocks, F_blocks, starts))
    out = out.transpose(1, 0, 3, 2, 4).reshape(B, S, D)
    return out @ w_out


def pool_mixer(h, w_pool, pool_scale):
    B, S, D = h.shape
    hf = h.astype(jnp.float32).reshape(B, S, N_POOL_GROUPS, POOL_GROUP)
    cs = jnp.cumsum(hf, axis=1)
    pos = jnp.arange(S)
    diffs = []
    for g, w in enumerate(POOL_WINDOWS):
        c_g = cs[:, :, g]
        lagged = jnp.pad(c_g, ((0, 0), (w, 0), (0, 0)))[:, :S]
        count = jnp.minimum(pos + 1, w).astype(jnp.float32)[None, :, None]
        diffs.append((c_g - lagged) / count - hf[:, :, g])
    d = jnp.stack(diffs, axis=2).astype(h.dtype)
    y = jnp.einsum('bsgc,gce->bsge', d, w_pool).reshape(B, S, D)
    return y * pool_scale


def conv_ffn(h, w_up, conv_w, conv_b, w_down):
    S = h.shape[1]
    u = h @ w_up
    up = jnp.pad(u, ((0, 0), (CONV_WIDTH - 1, 0), (0, 0)))
    y = conv_b
    for i in range(CONV_WIDTH):
        y = y + up[:, i:i + S] * conv_w[i]
    a, g = jnp.split(y, 2, axis=-1)
    return (jax.nn.silu(g) * a) @ w_down


def _fwd_setup_inputs(seed: int = 0) -> dict:
    key = jax.random.key(seed)
    ks = jax.random.split(key, 16)
    n = jax.random.normal
    D, F2 = D_MODEL, 2 * D_FF
    return {
        "x": n(ks[0], (BATCH, SEQ, D), jnp.float32),
        "c": n(ks[1], (BATCH, D), jnp.float32),
        "w_mod": n(ks[2], (DEPTH, D, N_MOD * D), jnp.float32) * (0.5 * D ** -0.5),
        "b_mod": n(ks[3], (DEPTH, N_MOD * D), jnp.float32) * 0.02,
        "norm_gain": 1.0 + 0.1 * n(ks[4], (DEPTH, 2, D), jnp.float32),
        "w_attn_in": n(ks[5], (N_ATTN_LAYERS, D, 3 * D + N_FOX_HEADS), jnp.float32) * D ** -0.5,
        "b_forget": jax.random.uniform(ks[6], (N_ATTN_LAYERS, N_FOX_HEADS), jnp.float32, 1.0, 4.0),
        "w_attn_out": n(ks[7], (N_ATTN_LAYERS, D, D), jnp.float32) * D ** -0.5,
        "w_pool": n(ks[8], (N_POOL_LAYERS, N_POOL_GROUPS, POOL_GROUP, POOL_GROUP), jnp.float32) * POOL_GROUP ** -0.5,
        "pool_scale": 1.0 + 0.1 * n(ks[9], (N_POOL_LAYERS, D), jnp.float32),
        "w_up": n(ks[10], (DEPTH, D, F2), jnp.float32) * D ** -0.5,
        "conv_w": n(ks[11], (DEPTH, CONV_WIDTH, F2), jnp.float32) * CONV_WIDTH ** -0.5,
        "conv_b": n(ks[12], (DEPTH, F2), jnp.float32) * 0.02,
        "w_down": n(ks[13], (DEPTH, D_FF, D), jnp.float32) * D_FF ** -0.5,
        "final_gain": 1.0 + 0.1 * n(ks[14], (D,), jnp.float32),
    }


def _fwd_reference(x, c, w_mod, b_mod, norm_gain, w_attn_in, b_forget, w_attn_out, w_pool, pool_scale,
              w_up, conv_w, conv_b, w_down, final_gain):
    cond = jax.nn.silu(c)
    for l in range(DEPTH):
        mod = cond @ w_mod[l] + b_mod[l]
        sh1, sc1, g1, sh2, sc2, g2 = jnp.split(mod, N_MOD, axis=-1)
        h = modulate(rms_norm(x, norm_gain[l, 0]), sh1, sc1)
        i = l // 2
        if l % 2 == 0:
            y = attention_mixer(h, w_attn_in[i], b_forget[i], w_attn_out[i])
        else:
            y = pool_mixer(h, w_pool[i], pool_scale[i])
        x = x + g1[:, None, :] * y
        h = modulate(rms_norm(x, norm_gain[l, 1]), sh2, sc2)
        x = x + g2[:, None, :] * conv_ffn(h, w_up[l], conv_w[l], conv_b[l], w_down[l])
    return rms_norm(x, final_gain)


import jax as _jax
import jax.numpy as _jnp

TWIN_FORMAT = 'train_step'
FWD_PARAMS = ['x', 'c', 'w_mod', 'b_mod', 'norm_gain', 'w_attn_in', 'b_forget', 'w_attn_out', 'w_pool', 'pool_scale', 'w_up', 'conv_w', 'conv_b', 'w_down', 'final_gain']
TWIN_WEIGHTS = ['w_mod', 'b_mod', 'norm_gain', 'w_attn_in', 'b_forget', 'w_attn_out', 'w_pool', 'pool_scale', 'w_up', 'conv_w', 'conv_b', 'w_down', 'final_gain']
TWIN_DIFF_INPUT = 'x'
TWIN_INPUTS = ['x', 'c', 'w_mod', 'b_mod', 'norm_gain', 'w_attn_in', 'b_forget', 'w_attn_out', 'w_pool', 'pool_scale', 'w_up', 'conv_w', 'conv_b', 'w_down', 'final_gain', 'loss_target', 'm_w_mod', 'm_b_mod', 'm_norm_gain', 'm_w_attn_in', 'm_b_forget', 'm_w_attn_out', 'm_w_pool', 'm_pool_scale', 'm_w_up', 'm_conv_w', 'm_conv_b', 'm_w_down', 'm_final_gain', 'v_w_mod', 'v_b_mod', 'v_norm_gain', 'v_w_attn_in', 'v_b_forget', 'v_w_attn_out', 'v_w_pool', 'v_pool_scale', 'v_w_up', 'v_conv_w', 'v_conv_b', 'v_w_down', 'v_final_gain']
TWIN_OUTPUTS = ['loss', 'grad_x', 'grad_w_mod', 'grad_b_mod', 'grad_norm_gain', 'grad_w_attn_in', 'grad_b_forget', 'grad_w_attn_out', 'grad_w_pool', 'grad_pool_scale', 'grad_w_up', 'grad_conv_w', 'grad_conv_b', 'grad_w_down', 'grad_final_gain', 'delta_w_mod', 'delta_b_mod', 'delta_norm_gain', 'delta_w_attn_in', 'delta_b_forget', 'delta_w_attn_out', 'delta_w_pool', 'delta_pool_scale', 'delta_w_up', 'delta_conv_w', 'delta_conv_b', 'delta_w_down', 'delta_final_gain', 'new_m_w_mod', 'new_m_b_mod', 'new_m_norm_gain', 'new_m_w_attn_in', 'new_m_b_forget', 'new_m_w_attn_out', 'new_m_w_pool', 'new_m_pool_scale', 'new_m_w_up', 'new_m_conv_w', 'new_m_conv_b', 'new_m_w_down', 'new_m_final_gain', 'new_v_w_mod', 'new_v_b_mod', 'new_v_norm_gain', 'new_v_w_attn_in', 'new_v_b_forget', 'new_v_w_attn_out', 'new_v_w_pool', 'new_v_pool_scale', 'new_v_w_up', 'new_v_conv_w', 'new_v_conv_b', 'new_v_w_down', 'new_v_final_gain']
TWIN_LEAF_KINDS = {'loss': 'loss', 'grad_x': 'grad_x', 'grad_w_mod': 'grad_w', 'grad_b_mod': 'grad_w', 'grad_norm_gain': 'grad_w', 'grad_w_attn_in': 'grad_w', 'grad_b_forget': 'grad_w', 'grad_w_attn_out': 'grad_w', 'grad_w_pool': 'grad_w', 'grad_pool_scale': 'grad_w', 'grad_w_up': 'grad_w', 'grad_conv_w': 'grad_w', 'grad_conv_b': 'grad_w', 'grad_w_down': 'grad_w', 'grad_final_gain': 'grad_w', 'delta_w_mod': 'delta_w', 'delta_b_mod': 'delta_w', 'delta_norm_gain': 'delta_w', 'delta_w_attn_in': 'delta_w', 'delta_b_forget': 'delta_w', 'delta_w_attn_out': 'delta_w', 'delta_w_pool': 'delta_w', 'delta_pool_scale': 'delta_w', 'delta_w_up': 'delta_w', 'delta_conv_w': 'delta_w', 'delta_conv_b': 'delta_w', 'delta_w_down': 'delta_w', 'delta_final_gain': 'delta_w', 'new_m_w_mod': 'new_m', 'new_m_b_mod': 'new_m', 'new_m_norm_gain': 'new_m', 'new_m_w_attn_in': 'new_m', 'new_m_b_forget': 'new_m', 'new_m_w_attn_out': 'new_m', 'new_m_w_pool': 'new_m', 'new_m_pool_scale': 'new_m', 'new_m_w_up': 'new_m', 'new_m_conv_w': 'new_m', 'new_m_conv_b': 'new_m', 'new_m_w_down': 'new_m', 'new_m_final_gain': 'new_m', 'new_v_w_mod': 'new_v', 'new_v_b_mod': 'new_v', 'new_v_norm_gain': 'new_v', 'new_v_w_attn_in': 'new_v', 'new_v_b_forget': 'new_v', 'new_v_w_attn_out': 'new_v', 'new_v_w_pool': 'new_v', 'new_v_pool_scale': 'new_v', 'new_v_w_up': 'new_v', 'new_v_conv_w': 'new_v', 'new_v_conv_b': 'new_v', 'new_v_w_down': 'new_v', 'new_v_final_gain': 'new_v'}


def _forward(args):
    return _fwd_reference(*[args[k] for k in FWD_PARAMS])


def _output_shape():
    out = _jax.eval_shape(lambda: _forward(_fwd_setup_inputs(0)))
    return out.shape, out.dtype

N_MICROBATCH = 1
ADAM_LR = 0.001
ADAM_B1 = 0.9
ADAM_B2 = 0.999
ADAM_EPS = 1e-08
ADAM_WD = 0.01
ADAM_STEP = 10
PER_EXAMPLE_BATCH_AXIS = {'x': 0, 'c': 0, 'loss_target': 0}
SHARED_INPUTS = []
_WEIGHT_DTYPES = {'w_mod': _jnp.float32, 'b_mod': _jnp.float32, 'norm_gain': _jnp.float32, 'w_attn_in': _jnp.float32, 'b_forget': _jnp.float32, 'w_attn_out': _jnp.float32, 'w_pool': _jnp.float32, 'pool_scale': _jnp.float32, 'w_up': _jnp.float32, 'conv_w': _jnp.float32, 'conv_b': _jnp.float32, 'w_down': _jnp.float32, 'final_gain': _jnp.float32}
MOMENT_SCALE = {'w_mod': 3.430380e-02, 'b_mod': 6.467047e-02, 'norm_gain': 1.756458e-02, 'w_attn_in': 9.738177e-03, 'b_forget': 5.474601e-02, 'w_attn_out': 1.429493e-02, 'w_pool': 1.773375e-02, 'pool_scale': 1.013061e-01, 'w_up': 8.231674e-03, 'conv_w': 8.238773e-03, 'conv_b': 7.675043e-03, 'w_down': 1.356020e-02, 'final_gain': 8.045054e+00}


def _to_microbatches(a, axis):
    t = _jnp.moveaxis(a, axis, 0)
    t = t.reshape((N_MICROBATCH, t.shape[0] // N_MICROBATCH) + t.shape[1:])
    return _jnp.moveaxis(t, 1, axis + 1)


def setup_inputs(seed: int = 0) -> dict:
    inp = _fwd_setup_inputs(seed)
    key = _jax.random.fold_in(_jax.random.key(seed), 7919)
    shape, _ = _output_shape()
    out = dict(inp)
    out["loss_target"] = _jax.random.normal(_jax.random.fold_in(key, 0), shape, _jnp.float32)
    for i, name in enumerate(TWIN_WEIGHTS):
        w = inp[name].astype(_jnp.float32)
        if MOMENT_SCALE is None:
            s = _jnp.sqrt(_jnp.mean(_jnp.square(w)) + 1e-30)
        else:
            s = MOMENT_SCALE[name]
        km, kv = _jax.random.split(_jax.random.fold_in(key, i + 1))
        out[name] = w
        out["m_" + name] = s * _jax.random.normal(km, w.shape, _jnp.float32)
        out["v_" + name] = (s * s) * _jax.random.uniform(kv, w.shape, _jnp.float32, 0.5, 1.5)
    if N_MICROBATCH > 1:
        for name, axis in PER_EXAMPLE_BATCH_AXIS.items():
            out[name] = _to_microbatches(out[name], axis)
    return {'x': out['x'], 'c': out['c'], 'w_mod': out['w_mod'], 'b_mod': out['b_mod'], 'norm_gain': out['norm_gain'], 'w_attn_in': out['w_attn_in'], 'b_forget': out['b_forget'], 'w_attn_out': out['w_attn_out'], 'w_pool': out['w_pool'], 'pool_scale': out['pool_scale'], 'w_up': out['w_up'], 'conv_w': out['conv_w'], 'conv_b': out['conv_b'], 'w_down': out['w_down'], 'final_gain': out['final_gain'], 'loss_target': out['loss_target'], 'm_w_mod': out['m_w_mod'], 'm_b_mod': out['m_b_mod'], 'm_norm_gain': out['m_norm_gain'], 'm_w_attn_in': out['m_w_attn_in'], 'm_b_forget': out['m_b_forget'], 'm_w_attn_out': out['m_w_attn_out'], 'm_w_pool': out['m_w_pool'], 'm_pool_scale': out['m_pool_scale'], 'm_w_up': out['m_w_up'], 'm_conv_w': out['m_conv_w'], 'm_conv_b': out['m_conv_b'], 'm_w_down': out['m_w_down'], 'm_final_gain': out['m_final_gain'], 'v_w_mod': out['v_w_mod'], 'v_b_mod': out['v_b_mod'], 'v_norm_gain': out['v_norm_gain'], 'v_w_attn_in': out['v_w_attn_in'], 'v_b_forget': out['v_b_forget'], 'v_w_attn_out': out['v_w_attn_out'], 'v_w_pool': out['v_w_pool'], 'v_pool_scale': out['v_pool_scale'], 'v_w_up': out['v_w_up'], 'v_conv_w': out['v_conv_w'], 'v_conv_b': out['v_conv_b'], 'v_w_down': out['v_w_down'], 'v_final_gain': out['v_final_gain']}


def _loss(weights, diff, rest, loss_target):
    with _jax.named_scope("forward"):
        args = {**rest, TWIN_DIFF_INPUT: diff, **{k: w.astype(_WEIGHT_DTYPES[k]) for k, w in weights.items()}}
        y = _forward(args)
    with _jax.named_scope("loss_head"):
        err = _jnp.square(y.astype(_jnp.float32) - loss_target)
        return 0.5 * _jnp.sum(_jnp.mean(err, axis=-1)) if err.ndim else 0.5 * err


def _adamw(w, g, m, v):
    m = ADAM_B1 * m + (1.0 - ADAM_B1) * g
    v = ADAM_B2 * v + (1.0 - ADAM_B2) * _jnp.square(g)
    m_hat = m / (1.0 - ADAM_B1 ** ADAM_STEP)
    v_hat = v / (1.0 - ADAM_B2 ** ADAM_STEP)
    delta = -ADAM_LR * (m_hat / (_jnp.sqrt(v_hat) + ADAM_EPS) + ADAM_WD * w)
    return delta, m, v


def reference(x, c, w_mod, b_mod, norm_gain, w_attn_in, b_forget, w_attn_out, w_pool, pool_scale, w_up, conv_w, conv_b, w_down, final_gain, loss_target, m_w_mod, m_b_mod, m_norm_gain, m_w_attn_in, m_b_forget, m_w_attn_out, m_w_pool, m_pool_scale, m_w_up, m_conv_w, m_conv_b, m_w_down, m_final_gain, v_w_mod, v_b_mod, v_norm_gain, v_w_attn_in, v_b_forget, v_w_attn_out, v_w_pool, v_pool_scale, v_w_up, v_conv_w, v_conv_b, v_w_down, v_final_gain):
    given = dict(x=x, c=c, w_mod=w_mod, b_mod=b_mod, norm_gain=norm_gain, w_attn_in=w_attn_in, b_forget=b_forget, w_attn_out=w_attn_out, w_pool=w_pool, pool_scale=pool_scale, w_up=w_up, conv_w=conv_w, conv_b=conv_b, w_down=w_down, final_gain=final_gain, loss_target=loss_target, m_w_mod=m_w_mod, m_b_mod=m_b_mod, m_norm_gain=m_norm_gain, m_w_attn_in=m_w_attn_in, m_b_forget=m_b_forget, m_w_attn_out=m_w_attn_out, m_w_pool=m_w_pool, m_pool_scale=m_pool_scale, m_w_up=m_w_up, m_conv_w=m_conv_w, m_conv_b=m_conv_b, m_w_down=m_w_down, m_final_gain=m_final_gain, v_w_mod=v_w_mod, v_b_mod=v_b_mod, v_norm_gain=v_norm_gain, v_w_attn_in=v_w_attn_in, v_b_forget=v_b_forget, v_w_attn_out=v_w_attn_out, v_w_pool=v_w_pool, v_pool_scale=v_pool_scale, v_w_up=v_w_up, v_conv_w=v_conv_w, v_conv_b=v_conv_b, v_w_down=v_w_down, v_final_gain=v_final_gain)
    weights = {n: given[n] for n in TWIN_WEIGHTS}
    shared = {n: given[n] for n in SHARED_INPUTS}
    per_example = {n: given[n] for n in ['x', 'c']}
    grad_fn = _jax.value_and_grad(_loss, argnums=(0, 1))

    def one_microbatch(ex, loss_target):
        ex = dict(ex)
        diff = ex.pop(TWIN_DIFF_INPUT)
        return grad_fn(weights, diff, {**shared, **ex}, loss_target)

    if N_MICROBATCH == 1:
        loss, (grad_w, grad_x) = one_microbatch(per_example, given["loss_target"])
    else:
        def body(carry, xs):
            loss_sum, grad_sum = carry
            l_k, (gw_k, gx_k) = one_microbatch(xs[0], xs[1])
            with _jax.named_scope("update"):
                return (loss_sum + l_k, _jax.tree.map(_jnp.add, grad_sum, gw_k)), gx_k

        init = (_jnp.zeros((), _jnp.float32), _jax.tree.map(_jnp.zeros_like, weights))
        (loss, grad_w), grad_x = _jax.lax.scan(body, init, (per_example, given["loss_target"]))
    with _jax.named_scope("update"):
        delta_w, new_m, new_v = {}, {}, {}
        for n in TWIN_WEIGHTS:
            delta_w[n], new_m[n], new_v[n] = _adamw(weights[n], grad_w[n], given["m_" + n], given["v_" + n])
    return (loss, grad_x, *[grad_w[n] for n in TWIN_WEIGHTS], *[delta_w[n] for n in TWIN_WEIGHTS],
            *[new_m[n] for n in TWIN_WEIGHTS], *[new_v[n] for n in TWIN_WEIGHTS])
```

```python
import jax
import jax.numpy as jnp
from jax import lax
from jax.experimental import pallas as pl
from jax.experimental.pallas import tpu as pltpu

NDEV = 8
F32 = jnp.float32
BF16 = jnp.bfloat16
MESH = pl.DeviceIdType.MESH
VMEM_LIMIT_BYTES = 56 * 1024 * 1024
LANES = 128
POOL_WINDOWS = (2, 4, 8, 16)
EPS = 1e-6
ADAM_LR = 0.001
ADAM_B1 = 0.9
ADAM_B2 = 0.999
ADAM_EPS = 1e-08
ADAM_WD = 0.01
ADAM_STEP = 10
NEG_BIG = -1e30
SDS = jax.ShapeDtypeStruct
ANY = pl.BlockSpec(memory_space=pl.ANY)


def _pc(body, **kw):
    return pl.pallas_call(body, **kw)


def _params(n_axes):
    return pltpu.CompilerParams(dimension_semantics=("arbitrary",) * n_axes, vmem_limit_bytes=VMEM_LIMIT_BYTES)


def _pick(n, prefs):
    for p in prefs:
        if p <= n and n % p == 0:
            return p
    return n


def _idx(p):
    return 4 * p[0] + 2 * p[1] + p[2]


def _me():
    return lax.axis_index("x"), lax.axis_index("y"), lax.axis_index("c")


def _all_gather(arrs, name):
    n = len(arrs)

    def body(*refs):
        ins, outs = refs[:n], refs[n:2 * n]
        send_sems, recv_sems, local_sems = refs[2 * n:]
        x, y, c = _me()
        me, sib = (x, y, c), (x, y, 1 - c)
        chips = [(1 - x, y), (x, 1 - y), (1 - x, 1 - y)]

        def copy(t, k, block, to, src=None):
            dst = outs[t].at[_idx(block)]
            return pltpu.make_async_remote_copy(
                src_ref=dst if src is None else src, dst_ref=dst,
                send_sem=send_sems.at[7 * t + k], recv_sem=recv_sems.at[7 * t + k],
                device_id=to, device_id_type=MESH)

        mine = [pltpu.make_async_copy(ins[t], outs[t].at[_idx(me)], local_sems.at[t]) for t in range(n)]
        for cp in mine:
            cp.start()
        first = []
        for t in range(n):
            first.append(copy(t, 0, me, sib, src=ins[t]))
            for j, chip in enumerate(chips):
                first.append(copy(t, 1 + j, me, (*chip, c), src=ins[t]))
        for cp in first:
            cp.start()
        passed = []
        for t in range(n):
            for j, chip in enumerate(chips):
                copy(t, 1 + j, (*chip, c), me).wait_recv()
                cp = copy(t, 4 + j, (*chip, c), sib)
                cp.start()
                passed.append(cp)
        for t in range(n):
            copy(t, 0, sib, me).wait_recv()
            for j, chip in enumerate(chips):
                copy(t, 4 + j, (*chip, 1 - c), me).wait_recv()
        for cp in first + passed:
            cp.wait_send()
        for cp in mine:
            cp.wait()

    return _pc(
        body, name=name,
        out_shape=[SDS((NDEV,) + a.shape, a.dtype) for a in arrs],
        in_specs=[ANY] * n, out_specs=[ANY] * n,
        scratch_shapes=[pltpu.SemaphoreType.DMA((7 * n,)), pltpu.SemaphoreType.DMA((7 * n,)),
                        pltpu.SemaphoreType.DMA((n,))],
    )(*arrs)


def _scatter_partials(arrs, name):
    n = len(arrs)

    def body(*refs):
        ins, outs = refs[:n], refs[n:2 * n]
        send_sems, recv_sems, local_sems = refs[2 * n:]
        x, y, c = _me()
        me_i = _idx((x, y, c))

        def peer(r):
            return (1 - x if r & 4 else x, 1 - y if r & 2 else y, 1 - c if r & 1 else c)

        def copy(t, r):
            p = peer(r)
            return pltpu.make_async_remote_copy(
                src_ref=ins[t].at[_idx(p)], dst_ref=outs[t].at[me_i],
                send_sem=send_sems.at[7 * t + r - 1], recv_sem=recv_sems.at[7 * t + r - 1],
                device_id=p, device_id_type=MESH)

        def arrival(t, r):
            p = peer(r)
            return pltpu.make_async_remote_copy(
                src_ref=ins[t].at[_idx(p)], dst_ref=outs[t].at[_idx(p)],
                send_sem=send_sems.at[7 * t + r - 1], recv_sem=recv_sems.at[7 * t + r - 1],
                device_id=p, device_id_type=MESH)

        mine = [pltpu.make_async_copy(ins[t].at[me_i], outs[t].at[me_i], local_sems.at[t]) for t in range(n)]
        for cp in mine:
            cp.start()
        sends = [copy(t, r) for t in range(n) for r in range(1, NDEV)]
        for cp in sends:
            cp.start()
        for t in range(n):
            for r in range(1, NDEV):
                arrival(t, r).wait_recv()
        for cp in sends:
            cp.wait_send()
        for cp in mine:
            cp.wait()

    return _pc(
        body, name=name,
        out_shape=[SDS(a.shape, a.dtype) for a in arrs],
        in_specs=[ANY] * n, out_specs=[ANY] * n,
        scratch_shapes=[pltpu.SemaphoreType.DMA((7 * n,)), pltpu.SemaphoreType.DMA((7 * n,)),
                        pltpu.SemaphoreType.DMA((n,))],
    )(*arrs)


def _matmul(a, b, *, mode, name, out_dtype, tm=1024, tn=1024, tk=512, b_blocked=False, out_blocked=False,
            a_split=False, b_split=False, res=None, gate=None, y_dtype=None):
    if mode == "tn":
        K, M = (a.shape[0], a.shape[1]) if not a_split else (a.shape[1], 2 * a.shape[2])
    else:
        M, K = (a.shape[0], a.shape[1]) if not a_split else (a.shape[1], 2 * a.shape[2])
    if b_blocked:
        if mode == "nn":
            N, tn = b.shape[0] * b.shape[2], b.shape[2]
        else:
            N, tk = b.shape[1], b.shape[2]
    elif b_split:
        N = 2 * b.shape[2]
    else:
        N = b.shape[0] if mode == "nt" else b.shape[1]
    tm = _pick(M, (tm, 704, 512, 384, 256, 128))
    if not (b_blocked and mode == "nn"):
        tn = _pick(N, (tn, 1024, 768, 512, 384, 256, 128))
    if not (b_blocked and mode == "nt"):
        tk = _pick(K, (tk, 512, 384, 256, 128))
    nm, nn_, nk = M // tm, N // tn, K // tk

    if mode == "tn":
        a_spec = pl.BlockSpec((tk, tm), lambda i, j, k: (k, i))
        dims = (((0,), (0,)), ((), ()))
    elif a_split:
        per = a.shape[2] // tk
        a_spec = pl.BlockSpec((None, tm, tk), lambda i, j, k: (k // per, i, k % per))
    else:
        a_spec = pl.BlockSpec((tm, tk), lambda i, j, k: (i, k))
    if mode == "nn":
        dims = (((1,), (0,)), ((), ()))
        if b_blocked:
            b_spec = pl.BlockSpec((None, tk, tn), lambda i, j, k: (j, k, 0))
        else:
            b_spec = pl.BlockSpec((tk, tn), lambda i, j, k: (k, j))
    elif mode == "nt":
        dims = (((1,), (1,)), ((), ()))
        if b_blocked:
            b_spec = pl.BlockSpec((None, tn, tk), lambda i, j, k: (k, j, 0))
        else:
            b_spec = pl.BlockSpec((tn, tk), lambda i, j, k: (j, k))
    else:
        if b_split:
            per_b = b.shape[2] // tn
            b_spec = pl.BlockSpec((None, tk, tn), lambda i, j, k: (j // per_b, k, j % per_b))
        else:
            b_spec = pl.BlockSpec((tk, tn), lambda i, j, k: (k, j))
    if out_blocked:
        o_spec = pl.BlockSpec((None, tm, tn), lambda i, j, k: (j, i, 0))
        o_shape = SDS((nn_, M, tn), out_dtype)
    else:
        o_spec = pl.BlockSpec((tm, tn), lambda i, j, k: (i, j))
        o_shape = SDS((M, N), out_dtype)
    fused = res is not None
    in_specs, operands = [a_spec, b_spec], [a, b]
    out_specs, out_shapes = [o_spec], [o_shape]
    if fused:
        in_specs += [pl.BlockSpec((tm, tn), lambda i, j, k: (i, j)), pl.BlockSpec((1, tn), lambda i, j, k: (0, j))]
        operands += [res, gate]
        if y_dtype is not None:
            out_specs.append(pl.BlockSpec((tm, tn), lambda i, j, k: (i, j)))
            out_shapes.append(SDS((M, N), y_dtype))

    def body(*refs):
        a_ref, b_ref = refs[0], refs[1]
        acc_ref = refs[-1]
        k = pl.program_id(2)

        @pl.when(k == 0)
        def _():
            acc_ref[...] = jnp.zeros_like(acc_ref)

        acc_ref[...] += lax.dot_general(a_ref[...], b_ref[...], dims, preferred_element_type=F32)

        @pl.when(k == nk - 1)
        def _():
            acc = acc_ref[...]
            if fused:
                res_ref, gate_ref, o_ref = refs[2], refs[3], refs[4]
                o_ref[...] = (res_ref[...] + gate_ref[...] * acc).astype(o_ref.dtype)
                if y_dtype is not None:
                    refs[5][...] = acc.astype(y_dtype)
            else:
                refs[2][...] = acc.astype(refs[2].dtype)

    outs = _pc(body, name=name, grid=(nm, nn_, nk), in_specs=in_specs, out_specs=out_specs, out_shape=out_shapes,
               scratch_shapes=[pltpu.VMEM((tm, tn), F32)], compiler_params=_params(3))(*operands)
    return outs[0] if len(outs) == 1 else tuple(outs)


def _norm_mod(x, gain, sc, sh, out_dtype, name):
    S, D = x.shape
    tr = _pick(S, (256, 128))

    def body(x_ref, g_ref, sc_ref, sh_ref, o_ref):
        xv = x_ref[...]
        r = lax.rsqrt(jnp.mean(xv * xv, axis=-1, keepdims=True) + EPS)
        n = (xv * r) * g_ref[...]
        o_ref[...] = (n * (1.0 + sc_ref[...]) + sh_ref[...]).astype(o_ref.dtype)

    row = pl.BlockSpec((tr, D), lambda i: (i, 0))
    vec = pl.BlockSpec((1, D), lambda i: (0, 0))
    return _pc(body, name=name, grid=(S // tr,), in_specs=[row, vec, vec, vec], out_specs=row,
               out_shape=SDS((S, D), out_dtype), compiler_params=_params(1))(x, gain, sc, sh)


def _norm_mod_bwd(dh, x, dxres, gain, sc, name):
    S, D = x.shape
    tr = _pick(S, (256, 128))

    def body(dh_ref, x_ref, dxres_ref, g_ref, sc_ref, dx_ref, dgain_ref, dsc_ref, dsh_ref):
        @pl.when(pl.program_id(0) == 0)
        def _():
            dgain_ref[...] = jnp.zeros_like(dgain_ref)
            dsc_ref[...] = jnp.zeros_like(dsc_ref)
            dsh_ref[...] = jnp.zeros_like(dsh_ref)

        xv = x_ref[...]
        dh = dh_ref[...].astype(F32)
        r = lax.rsqrt(jnp.mean(xv * xv, axis=-1, keepdims=True) + EPS)
        nh = xv * r
        gn = g_ref[...]
        dn = dh * (1.0 + sc_ref[...])
        dgain_ref[...] += jnp.sum(dn * nh, axis=0, keepdims=True)
        dsc_ref[...] += jnp.sum(dh * (nh * gn), axis=0, keepdims=True)
        dsh_ref[...] += jnp.sum(dh, axis=0, keepdims=True)
        dnh = dn * gn
        dx = r * (dnh - nh * jnp.mean(dnh * nh, axis=-1, keepdims=True))
        dx_ref[...] = dxres_ref[...] + dx

    row = pl.BlockSpec((tr, D), lambda i: (i, 0))
    vec = pl.BlockSpec((1, D), lambda i: (0, 0))
    return _pc(body, name=name, grid=(S // tr,), in_specs=[row, row, row, vec, vec], out_specs=[row, vec, vec, vec],
               out_shape=[SDS((S, D), F32), SDS((1, D), F32), SDS((1, D), F32), SDS((1, D), F32)],
               compiler_params=_params(1))(dh, x, dxres, gain, sc)


def _gate_bwd(dx, y, gate, name):
    S, D = dx.shape
    tr = _pick(S, (256, 128))

    def body(dx_ref, y_ref, gate_ref, dy_ref, dgate_ref):
        @pl.when(pl.program_id(0) == 0)
        def _():
            dgate_ref[...] = jnp.zeros_like(dgate_ref)

        dxv = dx_ref[...]
        dgate_ref[...] += jnp.sum(dxv * y_ref[...].astype(F32), axis=0, keepdims=True)
        dy_ref[...] = (dxv * gate_ref[...]).astype(BF16)

    row = pl.BlockSpec((tr, D), lambda i: (i, 0))
    vec = pl.BlockSpec((1, D), lambda i: (0, 0))
    return _pc(body, name=name, grid=(S // tr,), in_specs=[row, row, vec], out_specs=[row, vec],
               out_shape=[SDS((S, D), BF16), SDS((1, D), F32)], compiler_params=_params(1))(dx, y, gate)


def _loss_head(x, target, fgain, name):
    S, D = x.shape
    tr = _pick(S, (256, 128))

    def body(x_ref, t_ref, fg_ref, dx_ref, dfg_ref, loss_ref):
        @pl.when(pl.program_id(0) == 0)
        def _():
            dfg_ref[...] = jnp.zeros_like(dfg_ref)
            loss_ref[...] = jnp.zeros_like(loss_ref)

        xv = x_ref[...]
        fg = fg_ref[...]
        r = lax.rsqrt(jnp.mean(xv * xv, axis=-1, keepdims=True) + EPS)
        nh = xv * r
        e = nh * fg - t_ref[...]
        loss_ref[...] += 0.5 * jnp.sum(jnp.mean(e * e, axis=-1, keepdims=True))
        dy = e * (1.0 / D)
        dfg_ref[...] += jnp.sum(dy * nh, axis=0, keepdims=True)
        dnh = dy * fg
        dx_ref[...] = r * (dnh - nh * jnp.mean(dnh * nh, axis=-1, keepdims=True))

    row = pl.BlockSpec((tr, D), lambda i: (i, 0))
    vec = pl.BlockSpec((1, D), lambda i: (0, 0))
    tile = pl.BlockSpec((8, LANES), lambda i: (0, 0))
    return _pc(body, name=name, grid=(S // tr,), in_specs=[row, row, vec], out_specs=[row, vec, tile],
               out_shape=[SDS((S, D), F32), SDS((1, D), F32), SDS((8, LANES), F32)],
               compiler_params=_params(1))(x, target, fgain)


def _shift_down(v, k, rows):
    return jnp.where(rows >= k, pltpu.roll(v, k, axis=0), 0.0)


def _shift_up(v, k, rows):
    n = v.shape[0]
    return jnp.where(rows < n - k, pltpu.roll(v, n - k, axis=0), 0.0)


def _conv(uv, w, b, rows):
    return ((b + _shift_down(uv, 2, rows) * w[0:1]) + _shift_down(uv, 1, rows) * w[1:2]) + uv * w[2:3]


def _convgate_fwd(u, cw, cb, name):
    S, F2 = u.shape
    DFF = F2 // 2
    tc = _pick(DFF, (256, 128))
    sub = min(tc, LANES)
    nj = DFF // tc

    def body(ua_ref, ug_ref, wa_ref, wg_ref, ba_ref, bg_ref, o_ref):
        rows = lax.broadcasted_iota(jnp.int32, (S, sub), 0)
        for q in range(tc // sub):
            sl = slice(q * sub, (q + 1) * sub)
            ya = _conv(ua_ref[:, sl], wa_ref[:, sl], ba_ref[:, sl], rows)
            yg = _conv(ug_ref[:, sl], wg_ref[:, sl], bg_ref[:, sl], rows)
            o_ref[:, sl] = (yg * jax.nn.sigmoid(yg) * ya).astype(BF16)

    col = lambda off: pl.BlockSpec((S, tc), lambda j: (0, j + off))
    w3 = lambda off: pl.BlockSpec((3, tc), lambda j: (0, j + off))
    b1 = lambda off: pl.BlockSpec((1, tc), lambda j: (0, j + off))
    return _pc(body, name=name, grid=(nj,), in_specs=[col(0), col(nj), w3(0), w3(nj), b1(0), b1(nj)],
               out_specs=col(0), out_shape=SDS((S, DFF), BF16), compiler_params=_params(1))(u, u, cw, cw, cb, cb)


def _convgate_bwd(u, dact, cw, cb, name):
    S, F2 = u.shape
    DFF = F2 // 2
    tc = _pick(DFF, (256, 128))
    sub = min(tc, LANES)
    nj = DFF // tc

    def body(ua_ref, ug_ref, da_ref, wa_ref, wg_ref, ba_ref, bg_ref, du_ref, dcw_ref, dcb_ref):
        rows = lax.broadcasted_iota(jnp.int32, (S, sub), 0)
        for q in range(tc // sub):
            sl = slice(q * sub, (q + 1) * sub)
            ua, ug = ua_ref[:, sl], ug_ref[:, sl]
            wa, wg = wa_ref[:, sl], wg_ref[:, sl]
            ya = _conv(ua, wa, ba_ref[:, sl], rows)
            yg = _conv(ug, wg, bg_ref[:, sl], rows)
            s = jax.nn.sigmoid(yg)
            da = da_ref[:, sl]
            dya = da * (yg * s)
            dyg = da * ya * (s * (1.0 + yg * (1.0 - s)))
            for h, (dy, uv, w) in enumerate(((dya, ua, wa), (dyg, ug, wg))):
                du = (dy * w[2:3] + _shift_up(dy, 1, rows) * w[1:2]) + _shift_up(dy, 2, rows) * w[0:1]
                du_ref[h, :, sl] = du.astype(BF16)
                dcw_ref[h, 0:1, sl] = jnp.sum(dy * _shift_down(uv, 2, rows), axis=0, keepdims=True)
                dcw_ref[h, 1:2, sl] = jnp.sum(dy * _shift_down(uv, 1, rows), axis=0, keepdims=True)
                dcw_ref[h, 2:3, sl] = jnp.sum(dy * uv, axis=0, keepdims=True)
                dcb_ref[h, :, sl] = jnp.sum(dy, axis=0, keepdims=True)

    col = lambda off: pl.BlockSpec((S, tc), lambda j: (0, j + off))
    w3 = lambda off: pl.BlockSpec((3, tc), lambda j: (0, j + off))
    b1 = lambda off: pl.BlockSpec((1, tc), lambda j: (0, j + off))
    return _pc(body, name=name, grid=(nj,),
               in_specs=[col(0), col(nj), col(0), w3(0), w3(nj), b1(0), b1(nj)],
               out_specs=[pl.BlockSpec((2, S, tc), lambda j: (0, 0, j)), pl.BlockSpec((2, 3, tc), lambda j: (0, 0, j)),
                          pl.BlockSpec((2, 1, tc), lambda j: (0, 0, j))],
               out_shape=[SDS((2, S, DFF), BF16), SDS((2, 3, DFF), F32), SDS((2, 1, DFF), F32)],
               compiler_params=_params(1))(u, u, dact, cw, cw, cb, cb)


def _pool_diff(h, name):
    S, D = h.shape
    G = len(POOL_WINDOWS)
    CG = D // G
    tc = min(CG, LANES)
    per = CG // tc

    def body(h_ref, d_ref):
        g = pl.program_id(0)
        rows = lax.broadcasted_iota(jnp.int32, (S, tc), 0)
        for gi, w in enumerate(POOL_WINDOWS):
            @pl.when(g == gi)
            def _(w=w):
                hv = h_ref[...]
                s, k = hv, 1
                while k < w:
                    s = s + _shift_down(s, k, rows)
                    k *= 2
                count = jnp.minimum(rows + 1, w).astype(F32)
                d_ref[...] = (s / count - hv).astype(BF16)

    spec = pl.BlockSpec((S, tc), lambda g, j: (0, g * per + j))
    return _pc(body, name=name, grid=(G, per), in_specs=[spec], out_specs=spec, out_shape=SDS((S, D), BF16),
               compiler_params=_params(2))(h)


def _pool_diff_bwd(dd, name):
    S, D = dd.shape
    G = len(POOL_WINDOWS)
    CG = D // G
    tc = min(CG, LANES)
    per = CG // tc

    def body(dd_ref, o_ref):
        g = pl.program_id(0)
        rows = lax.broadcasted_iota(jnp.int32, (S, tc), 0)
        for gi, w in enumerate(POOL_WINDOWS):
            @pl.when(g == gi)
            def _(w=w):
                dv = dd_ref[...]
                count = jnp.minimum(rows + 1, w).astype(F32)
                s, k = dv / count, 1
                while k < w:
                    s = s + _shift_up(s, k, rows)
                    k *= 2
                o_ref[...] = s - dv

    spec = pl.BlockSpec((S, tc), lambda g, j: (0, g * per + j))
    return _pc(body, name=name, grid=(G, per), in_specs=[spec], out_specs=spec, out_shape=SDS((S, D), F32),
               compiler_params=_params(2))(dd)


def _pool_mm(d, w, res, gate, name):
    S, D = d.shape
    G, CG, _ = w.shape
    tm = _pick(S, (512, 256, 128))

    def body(d_ref, w_ref, res_ref, gate_ref, o_ref, e_ref):
        acc = jnp.dot(d_ref[...], w_ref[...], preferred_element_type=F32)
        o_ref[...] = res_ref[...] + gate_ref[...] * acc
        e_ref[...] = acc.astype(BF16)

    blk = pl.BlockSpec((tm, CG), lambda g, i: (i, g))
    return _pc(body, name=name, grid=(G, S // tm),
               in_specs=[blk, pl.BlockSpec((None, CG, CG), lambda g, i: (g, 0, 0)), blk,
                         pl.BlockSpec((1, CG), lambda g, i: (0, g))],
               out_specs=[blk, blk], out_shape=[SDS((S, D), F32), SDS((S, D), BF16)],
               compiler_params=_params(2))(d, w, res, gate)


def _pool_mm_bwd(de, d, w, name):
    S, D = de.shape
    G, CG, _ = w.shape
    tm = _pick(S, (512, 256, 128))
    ns = S // tm

    def body(de_ref, d_ref, w_ref, dd_ref, dw_ref, acc_ref):
        i = pl.program_id(1)

        @pl.when(i == 0)
        def _():
            acc_ref[...] = jnp.zeros_like(acc_ref)

        dev = de_ref[...]
        dd_ref[...] = lax.dot_general(dev, w_ref[...], (((1,), (1,)), ((), ())), preferred_element_type=F32)
        acc_ref[...] += lax.dot_general(d_ref[...], dev, (((0,), (0,)), ((), ())), preferred_element_type=F32)

        @pl.when(i == ns - 1)
        def _():
            dw_ref[...] = acc_ref[...].astype(BF16)

    blk = pl.BlockSpec((tm, CG), lambda g, i: (i, g))
    wsp = pl.BlockSpec((None, CG, CG), lambda g, i: (g, 0, 0))
    return _pc(body, name=name, grid=(G, ns), in_specs=[blk, blk, wsp], out_specs=[blk, wsp],
               out_shape=[SDS((S, D), F32), SDS((G, CG, CG), BF16)], scratch_shapes=[pltpu.VMEM((CG, CG), F32)],
               compiler_params=_params(2))(de, d, w)


def _log_sigmoid(z):
    return jnp.minimum(z, 0.0) - jnp.log(1.0 + jnp.exp(-jnp.abs(z)))


def _dot2(a, tri):
    hi = a.astype(BF16)
    lo = (a - hi.astype(F32)).astype(BF16)
    return jnp.dot(hi, tri, preferred_element_type=F32) + jnp.dot(lo, tri, preferred_element_type=F32)


_NT = (((1,), (1,)), ((), ()))
_TN = (((0,), (0,)), ((), ()))


def _forget_cumsum(flog, bf, name):
    S, W = flog.shape
    tb = _pick(S, (128,))

    def body(f_ref, b_ref, o_ref):
        r = lax.broadcasted_iota(jnp.int32, (tb, tb), 0)
        c = lax.broadcasted_iota(jnp.int32, (tb, tb), 1)
        tri = (c <= r).astype(F32)
        carry = jnp.zeros((1, W), F32)
        for q in range(S // tb):
            ls = _log_sigmoid(f_ref[q * tb:(q + 1) * tb, :] + b_ref[...])
            o_ref[q * tb:(q + 1) * tb, :] = carry + jnp.dot(tri, ls, preferred_element_type=F32,
                                                            precision=lax.Precision.HIGHEST)
            carry = carry + jnp.sum(ls, axis=0, keepdims=True)

    return _pc(body, name=name, out_shape=SDS((S, W), F32))(flog, bf)


def _forget_cumsum_bwd(dF, flog, bf, name):
    S, W = flog.shape
    tb = _pick(S, (128,))

    def body(d_ref, f_ref, b_ref, o_ref, db_ref):
        r = lax.broadcasted_iota(jnp.int32, (tb, tb), 0)
        c = lax.broadcasted_iota(jnp.int32, (tb, tb), 1)
        tri = (c >= r).astype(F32)
        carry = jnp.zeros((1, W), F32)
        db = jnp.zeros((1, W), F32)
        for q in reversed(range(S // tb)):
            dv = d_ref[q * tb:(q + 1) * tb, :]
            dls = carry + jnp.dot(tri, dv, preferred_element_type=F32, precision=lax.Precision.HIGHEST)
            carry = carry + jnp.sum(dv, axis=0, keepdims=True)
            dfl = dls * jax.nn.sigmoid(-(f_ref[q * tb:(q + 1) * tb, :] + b_ref[...]))
            o_ref[q * tb:(q + 1) * tb, :] = dfl
            db = db + jnp.sum(dfl, axis=0, keepdims=True)
        db_ref[...] = db

    return _pc(body, name=name, out_shape=[SDS((S, W), F32), SDS((1, W), F32)])(dF, flog, bf)


def _sb_fwd(qkv, NH, NSB, HD, T, name):
    S = qkv.shape[0]
    nq = S // T
    scale = HD ** -0.5

    def body(q_ref, k_ref, v_ref, o_ref, tot_ref):
        i = pl.program_id(1)
        q = q_ref[...]
        row = lax.broadcasted_iota(jnp.int32, (T, T), 0)
        col = lax.broadcasted_iota(jnp.int32, (T, T), 1)
        upper = (row > col).astype(BF16)

        def blk(kb, carry, acc, diag):
            sl = pl.ds(pl.multiple_of(kb * T, T), T)
            k, v = k_ref[sl, :], v_ref[sl, :]
            z = lax.dot_general(q, k, _NT, preferred_element_type=F32) * scale
            ls = _log_sigmoid(z)
            lr = ls - z
            if diag:
                lr = jnp.where(col < row, lr, 0.0)
            rest = _dot2(lr, upper) + carry
            w = jnp.exp(ls + rest)
            if diag:
                w = jnp.where(col < row, w, 0.0)
            acc = acc + jnp.dot(w.astype(BF16), v, preferred_element_type=F32)
            return carry + jnp.sum(lr, axis=1, keepdims=True), acc

        carry, acc = blk(i, jnp.zeros((T, 1), F32), jnp.zeros((T, HD), F32), True)
        carry, acc = lax.fori_loop(0, i, lambda jj, ca: blk(i - 1 - jj, ca[0], ca[1], False), (carry, acc))
        o_ref[...] = acc.astype(BF16)
        tot_ref[...] = carry

    return _pc(body, name=name, grid=(NSB, nq),
               in_specs=[pl.BlockSpec((T, HD), lambda h, i: (i, h)),
                         pl.BlockSpec((S, HD), lambda h, i: (0, NH + h)),
                         pl.BlockSpec((S, HD), lambda h, i: (0, 2 * NH + h))],
               out_specs=[pl.BlockSpec((T, HD), lambda h, i: (i, h)), pl.BlockSpec((None, T, 1), lambda h, i: (h, i, 0))],
               out_shape=[SDS((S, NSB * HD), BF16), SDS((NSB, S, 1), F32)],
               compiler_params=_params(2))(qkv, qkv, qkv)


def _sb_bwd(qkv, do, tot, NH, NSB, HD, T, name):
    S = qkv.shape[0]
    nq = S // T
    scale = HD ** -0.5

    def body(q_ref, k_ref, v_ref, do_ref, tot_ref, dq_ref, dk_ref, dv_ref, dk_acc, dv_acc):
        i = pl.program_id(1)

        @pl.when(i == 0)
        def _():
            dk_acc[...] = jnp.zeros_like(dk_acc)
            dv_acc[...] = jnp.zeros_like(dv_acc)

        q, do_, tot_ = q_ref[...], do_ref[...], tot_ref[...]
        row = lax.broadcasted_iota(jnp.int32, (T, T), 0)
        col = lax.broadcasted_iota(jnp.int32, (T, T), 1)
        incl = (row <= col).astype(BF16)
        strict = (row < col).astype(BF16)

        def blk(kb, cl, cg, dq, diag):
            sl = pl.ds(pl.multiple_of(kb * T, T), T)
            k, v = k_ref[sl, :], v_ref[sl, :]
            z = lax.dot_general(q, k, _NT, preferred_element_type=F32) * scale
            ls = _log_sigmoid(z)
            lr = ls - z
            if diag:
                lr = jnp.where(col < row, lr, 0.0)
            rest = tot_ - (cl + _dot2(lr, incl))
            w = jnp.exp(ls + rest)
            if diag:
                w = jnp.where(col < row, w, 0.0)
            g = lax.dot_general(do_, v, _NT, preferred_element_type=F32) * w
            dv_acc[sl, :] += lax.dot_general(w.astype(BF16), do_, _TN, preferred_element_type=F32)
            dlr = cg + _dot2(g, strict)
            dz = g * jnp.exp(lr) - dlr * jnp.exp(ls)
            if diag:
                dz = jnp.where(col < row, dz, 0.0)
            dzb = (dz * scale).astype(BF16)
            dq = dq + jnp.dot(dzb, k, preferred_element_type=F32)
            dk_acc[sl, :] += lax.dot_general(dzb, q, _TN, preferred_element_type=F32)
            return cl + jnp.sum(lr, axis=1, keepdims=True), cg + jnp.sum(g, axis=1, keepdims=True), dq

        zero = jnp.zeros((T, 1), F32)
        cl, cg, dq = lax.fori_loop(0, i, lambda kb, ca: blk(kb, ca[0], ca[1], ca[2], False),
                                   (zero, zero, jnp.zeros((T, HD), F32)))
        _, _, dq = blk(i, cl, cg, dq, True)
        dq_ref[...] = dq.astype(BF16)

        @pl.when(i == nq - 1)
        def _():
            dk_ref[...] = dk_acc[...].astype(BF16)
            dv_ref[...] = dv_acc[...].astype(BF16)

    qblk = pl.BlockSpec((T, HD), lambda h, i: (i, h))
    full = pl.BlockSpec((S, HD), lambda h, i: (0, h))
    return _pc(body, name=name, grid=(NSB, nq),
               in_specs=[qblk, pl.BlockSpec((S, HD), lambda h, i: (0, NH + h)),
                         pl.BlockSpec((S, HD), lambda h, i: (0, 2 * NH + h)), qblk,
                         pl.BlockSpec((None, T, 1), lambda h, i: (h, i, 0))],
               out_specs=[qblk, full, full],
               out_shape=[SDS((S, NSB * HD), BF16)] * 3,
               scratch_shapes=[pltpu.VMEM((S, HD), F32), pltpu.VMEM((S, HD), F32)],
               compiler_params=_params(2))(qkv, qkv, qkv, do, tot)


def _fox_fwd(qkv, fcol, frow, NH, NSB, HD, T, name):
    S = qkv.shape[0]
    NFX = NH - NSB
    nq = S // T
    scale = HD ** -0.5

    def body(q_ref, k_ref, v_ref, fq_ref, fk_ref, o_ref, o32_ref, lse_ref):
        i = pl.program_id(1)
        q, fq = q_ref[...], fq_ref[...]
        row = lax.broadcasted_iota(jnp.int32, (T, T), 0)
        col = lax.broadcasted_iota(jnp.int32, (T, T), 1)

        def blk(kb, m, l, acc, rem, diag):
            sl = pl.ds(pl.multiple_of(kb * T, T), T)
            k, v = k_ref[sl, :], v_ref[sl, :]
            s = lax.dot_general(q, k, _NT, preferred_element_type=F32) * scale + (fq - fk_ref[kb])
            if diag:
                s = jnp.where(col <= row, s, NEG_BIG)
            m_new = jnp.maximum(m, jnp.max(s, axis=1, keepdims=True))
            p = jnp.exp(s - m_new)
            alpha = jnp.exp(m - m_new)
            l = alpha * l + jnp.sum(p, axis=1, keepdims=True)
            hi = p.astype(BF16)
            lo = (p - hi.astype(F32)).astype(BF16)
            acc = alpha * acc + jnp.dot(hi, v, preferred_element_type=F32)
            rem = alpha * rem + jnp.dot(lo, v, preferred_element_type=F32)
            return m_new, l, acc, rem

        zero = jnp.zeros((T, HD), F32)
        m, l, acc, rem = blk(i, jnp.full((T, 1), NEG_BIG, F32), jnp.zeros((T, 1), F32), zero, zero, True)
        m, l, acc, rem = lax.fori_loop(0, i, lambda kb, ca: blk(kb, ca[0], ca[1], ca[2], ca[3], False), (m, l, acc, rem))
        o_ref[...] = (acc / l).astype(BF16)
        o32_ref[...] = (acc + rem) / l
        lse_ref[...] = m + jnp.log(l)

    vec = pl.BlockSpec((None, T, 1), lambda h, i: (h, i, 0))
    return _pc(body, name=name, grid=(NFX, nq),
               in_specs=[pl.BlockSpec((T, HD), lambda h, i: (i, NSB + h)),
                         pl.BlockSpec((S, HD), lambda h, i: (0, NH + NSB + h)),
                         pl.BlockSpec((S, HD), lambda h, i: (0, 2 * NH + NSB + h)),
                         vec, pl.BlockSpec((None, nq, 1, T), lambda h, i: (h, 0, 0, 0))],
               out_specs=[pl.BlockSpec((T, HD), lambda h, i: (i, h)), pl.BlockSpec((T, HD), lambda h, i: (i, h)), vec],
               out_shape=[SDS((S, NFX * HD), BF16), SDS((S, NFX * HD), F32), SDS((NFX, S, 1), F32)],
               compiler_params=_params(2))(qkv, qkv, qkv, fcol, frow)


def _fox_bwd(qkv, do, o, fcol, frow, lse, NH, NSB, HD, T, name):
    S = qkv.shape[0]
    NFX = NH - NSB
    nq = S // T
    scale = HD ** -0.5

    def body(q_ref, k_ref, v_ref, do_ref, o_ref, fq_ref, fk_ref, lse_ref, dq_ref, dk_ref, dv_ref, dfk_ref,
             dk_acc, dv_acc, dfk_acc):
        i = pl.program_id(1)

        @pl.when(i == 0)
        def _():
            dk_acc[...] = jnp.zeros_like(dk_acc)
            dv_acc[...] = jnp.zeros_like(dv_acc)
            dfk_acc[...] = jnp.zeros_like(dfk_acc)

        q, do_, fq, lse_ = q_ref[...], do_ref[...], fq_ref[...], lse_ref[...]
        delta = jnp.sum(do_.astype(F32) * o_ref[...], axis=1, keepdims=True)
        row = lax.broadcasted_iota(jnp.int32, (T, T), 0)
        col = lax.broadcasted_iota(jnp.int32, (T, T), 1)

        def blk(kb, dq, diag):
            sl = pl.ds(pl.multiple_of(kb * T, T), T)
            k, v = k_ref[sl, :], v_ref[sl, :]
            s = lax.dot_general(q, k, _NT, preferred_element_type=F32) * scale + (fq - fk_ref[kb])
            p = jnp.exp(s - lse_)
            if diag:
                p = jnp.where(col <= row, p, 0.0)
            ds = p * (lax.dot_general(do_, v, _NT, preferred_element_type=F32) - delta)
            dv_acc[sl, :] += lax.dot_general(p.astype(BF16), do_, _TN, preferred_element_type=F32)
            dsb = (ds * scale).astype(BF16)
            dk_acc[sl, :] += lax.dot_general(dsb, q, _TN, preferred_element_type=F32)
            dfk_acc[kb] -= jnp.sum(ds, axis=0, keepdims=True)
            return dq + jnp.dot(dsb, k, preferred_element_type=F32)

        dq = lax.fori_loop(0, i, lambda kb, dq: blk(kb, dq, False), jnp.zeros((T, HD), F32))
        dq = blk(i, dq, True)
        dq_ref[...] = dq.astype(BF16)

        @pl.when(i == nq - 1)
        def _():
            dk_ref[...] = dk_acc[...].astype(BF16)
            dv_ref[...] = dv_acc[...].astype(BF16)
            dfk_ref[...] = dfk_acc[...]

    vec = pl.BlockSpec((None, T, 1), lambda h, i: (h, i, 0))
    rowv = pl.BlockSpec((None, nq, 1, T), lambda h, i: (h, 0, 0, 0))
    oblk = pl.BlockSpec((T, HD), lambda h, i: (i, h))
    full = pl.BlockSpec((S, HD), lambda h, i: (0, h))
    return _pc(body, name=name, grid=(NFX, nq),
               in_specs=[pl.BlockSpec((T, HD), lambda h, i: (i, NSB + h)),
                         pl.BlockSpec((S, HD), lambda h, i: (0, NH + NSB + h)),
                         pl.BlockSpec((S, HD), lambda h, i: (0, 2 * NH + NSB + h)),
                         pl.BlockSpec((T, HD), lambda h, i: (i, NSB + h)), oblk, vec, rowv, vec],
               out_specs=[oblk, full, full, rowv],
               out_shape=[SDS((S, NFX * HD), BF16)] * 3 + [SDS((NFX, nq, 1, T), F32)],
               scratch_shapes=[pltpu.VMEM((S, HD), F32), pltpu.VMEM((S, HD), F32), pltpu.VMEM((nq, 1, T), F32)],
               compiler_params=_params(2))(qkv, qkv, qkv, do, o, fcol, frow, lse)


def _silu(c_all, name):
    def body(c_ref, o_ref):
        cv = c_ref[...]
        o_ref[...] = cv * jax.nn.sigmoid(cv)

    return _pc(body, name=name, out_shape=SDS(c_all.shape, F32))(c_all)


def _mod_project(cond, w_mod, name):
    L, D, C = w_mod.shape
    tk = _pick(D, (512, 256, 128))

    def body(c_ref, w_ref, o_ref):
        @pl.when(pl.program_id(1) == 0)
        def _():
            o_ref[...] = jnp.zeros_like(o_ref)

        o_ref[...] += jnp.dot(c_ref[...].astype(BF16), w_ref[...].astype(BF16), preferred_element_type=F32)

    return _pc(body, name=name, grid=(L, D // tk),
               in_specs=[pl.BlockSpec((16, tk), lambda l, k: (0, k)), pl.BlockSpec((None, tk, C), lambda l, k: (l, k, 0))],
               out_specs=pl.BlockSpec((None, 16, C), lambda l, k: (l, 0, 0)),
               out_shape=SDS((L, 16, C), F32), compiler_params=_params(2))(cond, w_mod)


def _adam_math(w, g, m, v):
    m = ADAM_B1 * m + (1.0 - ADAM_B1) * g
    v = ADAM_B2 * v + (1.0 - ADAM_B2) * (g * g)
    m_hat = m / (1.0 - ADAM_B1 ** ADAM_STEP)
    v_hat = v / (1.0 - ADAM_B2 ** ADAM_STEP)
    delta = -ADAM_LR * (m_hat / (jnp.sqrt(v_hat) + ADAM_EPS) + ADAM_WD * w)
    return delta, m, v


def _adamw(w, m, v, parts, layer, prev, name):
    L, R, C = w.shape
    NP = parts.shape[0]
    tr = _pick(R, (128, 64, 88, 32, 16, 8))
    nprev = 0 if prev is None else 4

    def body(w_ref, m_ref, v_ref, p_ref, *rest):
        g_ref, d_ref, mo_ref, vo_ref = rest[nprev:]
        g = p_ref[0].astype(F32)
        for j in range(1, NP):
            g = g + p_ref[j].astype(F32)
        delta, mn, vn = _adam_math(w_ref[...], g, m_ref[...], v_ref[...])
        g_ref[...] = g
        d_ref[...] = delta
        mo_ref[...] = mn
        vo_ref[...] = vn

    blk = pl.BlockSpec((None, tr, C), lambda i: (layer, i, 0))
    in_specs = [blk, blk, blk, pl.BlockSpec((NP, tr, C), lambda i: (0, i, 0))]
    operands = [w, m, v, parts]
    aliases = {}
    if prev is not None:
        in_specs += [ANY] * 4
        operands += list(prev)
        aliases = {4 + q: q for q in range(4)}
    return _pc(body, name=name, grid=(R // tr,), in_specs=in_specs, out_specs=[blk] * 4,
               out_shape=[SDS(w.shape, F32)] * 4, input_output_aliases=aliases,
               compiler_params=_params(1))(*operands)


def _adamw_mod(w, m, v, cond_t, dmod, name):
    L, D, C = w.shape
    tr = _pick(D, (128, 64))

    def body(w_ref, m_ref, v_ref, ct_ref, dm_ref, g_ref, d_ref, mo_ref, vo_ref):
        ct = ct_ref[...]
        g = ct[:, 0:1] * dm_ref[0]
        for b in range(1, NDEV):
            g = g + ct[:, b:b + 1] * dm_ref[b]
        delta, mn, vn = _adam_math(w_ref[...], g, m_ref[...], v_ref[...])
        g_ref[...] = g
        d_ref[...] = delta
        mo_ref[...] = mn
        vo_ref[...] = vn

    blk = pl.BlockSpec((None, tr, C), lambda l, i: (l, i, 0))
    return _pc(body, name=name, grid=(L, D // tr),
               in_specs=[blk, blk, blk, pl.BlockSpec((tr, LANES), lambda l, i: (i, 0)),
                         pl.BlockSpec((NDEV, None, 1, C), lambda l, i: (0, l, 0, 0))],
               out_specs=[blk] * 4, out_shape=[SDS(w.shape, F32)] * 4, compiler_params=_params(2))(w, m, v, cond_t, dmod)


def _sum_parts(parts, name):
    NP, R, C = parts.shape
    tr = _pick(R, (256, 128, 64, 32, 16, 8))

    def body(p_ref, o_ref):
        g = p_ref[0]
        for j in range(1, NP):
            g = g + p_ref[j]
        o_ref[...] = g

    return _pc(body, name=name, grid=(R // tr,), in_specs=[pl.BlockSpec((NP, tr, C), lambda i: (0, i, 0))],
               out_specs=pl.BlockSpec((tr, C), lambda i: (i, 0)), out_shape=SDS((R, C), F32),
               compiler_params=_params(1))(parts)


def _pack(vecs, rows=None):
    flat = jnp.concatenate([v.reshape(-1).astype(F32) for v in vecs])
    n = flat.shape[0]
    r = rows if rows is not None else -(-n // (8 * LANES)) * 8
    return jnp.pad(flat, (0, r * LANES - n)).reshape(r, LANES)


def _unpack(packed, shapes):
    flat = packed.reshape(-1)
    out, off = [], 0
    for s in shapes:
        n = 1
        for d in s:
            n *= d
        out.append(flat[off:off + n].reshape(s))
        off += n
    return out


def kernel(x, c, w_mod, b_mod, norm_gain, w_attn_in, b_forget, w_attn_out, w_pool, pool_scale, w_up, conv_w, conv_b, w_down, final_gain, loss_target, m_w_mod, m_b_mod, m_norm_gain, m_w_attn_in, m_b_forget, m_w_attn_out, m_w_pool, m_pool_scale, m_w_up, m_conv_w, m_conv_b, m_w_down, m_final_gain, v_w_mod, v_b_mod, v_norm_gain, v_w_attn_in, v_b_forget, v_w_attn_out, v_w_pool, v_pool_scale, v_w_up, v_conv_w, v_conv_b, v_w_down, v_final_gain):
    _, S, D = x.shape
    L = w_mod.shape[0]
    CM = w_mod.shape[2]
    NFX = b_forget.shape[1]
    NH = 2 * NFX
    NSB = NH - NFX
    HD = D // NH
    CI = w_attn_in.shape[2]
    CU = w_up.shape[2]
    F2 = NDEV * CU
    DFF = F2 // 2
    G = len(POOL_WINDOWS)
    CG = D // G
    T = _pick(S, (256, 128))
    me = _idx(_me())
    x0 = x[0]
    target = loss_target[0]

    small_shapes = [(1, D), norm_gain.shape, pool_scale.shape, conv_w.shape]
    small_all = _all_gather([_pack([c, norm_gain, pool_scale, conv_w])], "gather_small")[0]
    per_dev = [_unpack(small_all[j], small_shapes) for j in range(NDEV)]
    c_all = jnp.concatenate([p[0] for p in per_dev] + [jnp.zeros((16 - NDEV, D), F32)], axis=0)
    gain_f = jnp.concatenate([p[1] for p in per_dev], axis=2)
    pscale_f = jnp.concatenate([p[2] for p in per_dev], axis=1)
    convw_f = jnp.concatenate([p[3] for p in per_dev], axis=2)

    cond_all = _silu(c_all, "cond_silu")
    mod_part = _mod_project(cond_all, w_mod, "mod_project")
    mod_all = _all_gather([mod_part], "gather_mod")[0]
    mod = lax.dynamic_index_in_dim(mod_all, me, axis=2, keepdims=False)
    mod = mod.transpose(1, 0, 2).reshape(L, NDEV * CM) + b_mod
    mods = mod.reshape(L, 6, 1, D)

    saved = []
    xl = x0
    for l in range(L):
        i = l // 2
        sh1, sc1, g1, sh2, sc2, g2 = [mods[l, q] for q in range(6)]
        gn1, gn2 = gain_f[l, 0:1], gain_f[l, 1:2]
        st = {"x": xl}
        if l % 2 == 0:
            shards = [w_attn_in[i], w_attn_out[i], w_up[l], w_down[l]]
        else:
            shards = [w_pool[i].reshape(G * (CG // NDEV), CG), w_up[l], w_down[l]]
        full = _all_gather([s.astype(BF16) for s in shards], f"gather_w{l}")
        wup_g = full[-2]
        wdown_f = full[-1].reshape(DFF, D)
        if l % 2 == 0:
            win = full[0].transpose(1, 0, 2).reshape(D, NDEV * CI)
            wqkv = win[:, :3 * D]
            wf = jnp.pad(win[:, 3 * D:], ((0, 0), (0, LANES - NFX)))
            wout = full[1].reshape(D, D)
            h1 = _norm_mod(xl, gn1, sc1, sh1, BF16, f"norm1_{l}")
            qkv = _matmul(h1, wqkv, mode="nn", name=f"qkv_{l}", out_dtype=BF16)
            flog = _matmul(h1, wf, mode="nn", name=f"flog_{l}", out_dtype=F32, tn=LANES)
            bfp = jnp.pad(b_forget[i], (0, LANES - NFX)).reshape(1, LANES)
            Fc = _forget_cumsum(flog, bfp, f"fcum_{l}")
            f8 = Fc[:, :NFX].T
            fcol, frow = f8[:, :, None], f8.reshape(NFX, S // T, 1, T)
            o_sb, tot = _sb_fwd(qkv, NH, NSB, HD, T, f"sb_fwd_{l}")
            o_fx, o_fx32, lse = _fox_fwd(qkv, fcol, frow, NH, NSB, HD, T, f"fox_fwd_{l}")
            o = jnp.concatenate([o_sb, o_fx], axis=1)
            x1, y1 = _matmul(o, wout, mode="nn", name=f"attn_out_{l}", out_dtype=F32, res=xl, gate=g1, y_dtype=BF16)
            st.update(h1=h1, qkv=qkv, flog=flog, bfp=bfp, fcol=fcol, frow=frow, tot=tot, lse=lse, o=o, o_fx=o_fx32,
                      wqkv=wqkv, wf=wf, wout=wout, y1=y1)
        else:
            wpool = full[0].reshape(NDEV, G, CG // NDEV, CG).transpose(1, 0, 2, 3).reshape(G, CG, CG)
            gp = g1 * pscale_f[i:i + 1]
            h1 = _norm_mod(xl, gn1, sc1, sh1, F32, f"norm1_{l}")
            dpool = _pool_diff(h1, f"pool_diff_{l}")
            x1, e1 = _pool_mm(dpool, wpool, xl, gp, f"pool_mm_{l}")
            st.update(dpool=dpool, wpool=wpool, gp=gp, y1=e1)
        h2 = _norm_mod(x1, gn2, sc2, sh2, BF16, f"norm2_{l}")
        u = _matmul(h2, wup_g, mode="nn", name=f"ffn_up_{l}", out_dtype=F32, b_blocked=True)
        cb = conv_b[l].reshape(1, F2)
        act = _convgate_fwd(u, convw_f[l], cb, f"convgate_{l}")
        x2, y2 = _matmul(act, wdown_f, mode="nn", name=f"ffn_down_{l}", out_dtype=F32, res=x1, gate=g2, y_dtype=BF16)
        st.update(x1=x1, h2=h2, u=u, cb=cb, act=act, y2=y2, wup_g=wup_g, wdown_f=wdown_f,
                  mod=(sh1, sc1, g1, sh2, sc2, g2), gn=(gn1, gn2))
        saved.append(st)
        xl = x2

    dx, d_fgain, loss_tile = _loss_head(xl, target, final_gain.reshape(1, D), "loss_head")
    loss = lax.psum(loss_tile[0, 0], ("x", "y", "c"))

    dmod_rows = [None] * L
    d_gain = [None] * L
    d_convw = [None] * L
    d_convb = [None] * L
    d_pscale = [None] * (L // 2)
    d_bf = [None] * ((L + 1) // 2)
    big = {"w_up": None, "w_down": None, "w_attn_in": None, "w_attn_out": None, "w_pool": None}

    def update(key, w, m, v, bufs, layer, tag):
        big[key] = _adamw(w, m, v, bufs, layer, big[key], f"adamw_{tag}")

    for l in reversed(range(L)):
        i = l // 2
        st = saved[l]
        sh1, sc1, g1, sh2, sc2, g2 = st["mod"]
        gn1, gn2 = st["gn"]
        dffn, dg2 = _gate_bwd(dx, st["y2"], g2, f"gate2_bwd_{l}")
        dact = _matmul(dffn, st["wdown_f"], mode="nt", name=f"ffn_down_dx_{l}", out_dtype=F32, tn=CU)
        dwdown = _matmul(st["act"], dffn, mode="tn", name=f"ffn_down_dw_{l}", out_dtype=BF16, tm=CU)
        du, dcw, dcb = _convgate_bwd(st["u"], dact, convw_f[l], st["cb"], f"convgate_bwd_{l}")
        dh2 = _matmul(du, st["wup_g"], mode="nt", name=f"ffn_up_dx_{l}", out_dtype=F32, a_split=True, b_blocked=True)
        dwup = _matmul(st["h2"], du, mode="tn", name=f"ffn_up_dw_{l}", out_dtype=BF16, tn=CU, b_split=True,
                       out_blocked=True)
        dx, dgn2, dsc2, dsh2 = _norm_mod_bwd(dh2, st["x1"], dx, gn2, sc2, f"norm2_bwd_{l}")
        d_convw[l] = jnp.concatenate([dcw[0], dcw[1]], axis=1)
        d_convb[l] = jnp.concatenate([dcb[0], dcb[1]], axis=1)
        if l % 2 == 0:
            dy1, dg1 = _gate_bwd(dx, st["y1"], g1, f"gate1_bwd_{l}")
            do = _matmul(dy1, st["wout"], mode="nt", name=f"attn_out_dx_{l}", out_dtype=BF16)
            dwout = _matmul(st["o"], dy1, mode="tn", name=f"attn_out_dw_{l}", out_dtype=BF16)
            dq_s, dk_s, dv_s = _sb_bwd(st["qkv"], do, st["tot"], NH, NSB, HD, T, f"sb_bwd_{l}")
            dq_f, dk_f, dv_f, dfk = _fox_bwd(st["qkv"], do, st["o_fx"], st["fcol"], st["frow"], st["lse"], NH, NSB, HD, T,
                                             f"fox_bwd_{l}")
            dqkv = jnp.concatenate([dq_s, dq_f, dk_s, dk_f, dv_s, dv_f], axis=1)
            dF = jnp.pad(dfk.reshape(NFX, S).T, ((0, 0), (0, LANES - NFX)))
            dflog, dbf = _forget_cumsum_bwd(dF, st["flog"], st["bfp"], f"fcum_bwd_{l}")
            dflog_b = dflog.astype(BF16)
            dh1 = _matmul(dqkv, st["wqkv"], mode="nt", name=f"qkv_dx_{l}", out_dtype=F32)
            dh1 = _matmul(dflog_b, st["wf"], mode="nt", name=f"flog_dx_{l}", out_dtype=F32, res=dh1,
                          gate=jnp.ones((1, D), F32))
            dwqkv = _matmul(st["h1"], dqkv, mode="tn", name=f"qkv_dw_{l}", out_dtype=BF16)
            dwf = _matmul(st["h1"], dflog_b, mode="tn", name=f"flog_dw_{l}", out_dtype=BF16, tn=LANES)
            dwin = jnp.concatenate([dwqkv, dwf[:, :NFX]], axis=1).reshape(D, NDEV, CI).transpose(1, 0, 2)
            d_bf[i] = dbf[0, :NFX]
            parts = [dwin, dwout.reshape(NDEV, D // NDEV, D), dwup, dwdown.reshape(NDEV, DFF // NDEV, D)]
        else:
            de, dgp = _gate_bwd(dx, st["y1"], st["gp"], f"gate1_bwd_{l}")
            dg1 = dgp * pscale_f[i:i + 1]
            d_pscale[i] = dgp * g1
            dd, dwp = _pool_mm_bwd(de, st["dpool"], st["wpool"], f"pool_mm_bwd_{l}")
            dh1 = _pool_diff_bwd(dd, f"pool_diff_bwd_{l}")
            dwp = dwp.reshape(G, NDEV, CG // NDEV, CG).transpose(1, 0, 2, 3).reshape(NDEV, G * (CG // NDEV), CG)
            parts = [dwp, dwup, dwdown.reshape(NDEV, DFF // NDEV, D)]
        dx, dgn1, dsc1, dsh1 = _norm_mod_bwd(dh1, st["x"], dx, gn1, sc1, f"norm1_bwd_{l}")
        dmod_rows[l] = jnp.concatenate([dsh1, dsc1, dg1, dsh2, dsc2, dg2], axis=1)
        d_gain[l] = jnp.concatenate([dgn1, dgn2], axis=0)
        bufs = _scatter_partials(parts, f"scatter_grads_{l}")
        if l % 2 == 0:
            update("w_attn_in", w_attn_in, m_w_attn_in, v_w_attn_in, bufs[0], i, f"attn_in_{l}")
            update("w_attn_out", w_attn_out, m_w_attn_out, v_w_attn_out, bufs[1], i, f"attn_out_{l}")
        else:
            wp3 = lambda a: a.reshape(a.shape[0], G * (CG // NDEV), CG)
            update("w_pool", wp3(w_pool), wp3(m_w_pool), wp3(v_w_pool), bufs[0], i, f"pool_{l}")
        update("w_up", w_up, m_w_up, v_w_up, bufs[-2], l, f"up_{l}")
        update("w_down", w_down, m_w_down, v_w_down, bufs[-1], l, f"down_{l}")

    grad_x = dx[None]

    small_grads = [jnp.stack(dmod_rows), jnp.stack(d_gain), jnp.stack(d_pscale), jnp.stack(d_convw),
                   jnp.stack(d_convb), jnp.stack(d_bf), d_fgain]
    sg_shapes = [(L, 6 * D), (L, 2, D), (L // 2, D), (L, 3, F2), (L, F2), ((L + 1) // 2, NFX), (D,)]
    sg_all = _all_gather([_pack(small_grads)], "gather_small_grads")[0]
    sg_sum = _unpack(_sum_parts(sg_all, "sum_small_grads"), sg_shapes)
    g_bmod, g_gain_f, g_pscale_f, g_convw_f, g_convb, g_bf, g_fgain = sg_sum
    shard = lambda a, n, axis: lax.dynamic_slice_in_dim(a, me * n, n, axis=axis)
    g_small = [g_bmod, shard(g_gain_f, D // NDEV, 2), shard(g_pscale_f, D // NDEV, 1), shard(g_convw_f, CU, 2), g_convb,
               g_bf, g_fgain]
    w_small = [b_mod, norm_gain, pool_scale, conv_w, conv_b, b_forget, final_gain]
    m_small = [m_b_mod, m_norm_gain, m_pool_scale, m_conv_w, m_conv_b, m_b_forget, m_final_gain]
    v_small = [v_b_mod, v_norm_gain, v_pool_scale, v_conv_w, v_conv_b, v_b_forget, v_final_gain]
    small_out = _adamw(_pack(w_small)[None], _pack(m_small)[None], _pack(v_small)[None], _pack(g_small)[None], 0, None,
                       "adamw_small")
    small_out = [_unpack(a[0], [w.shape for w in w_small]) for a in small_out]

    dmod_all = jnp.stack([_unpack(sg_all[j], sg_shapes[:1])[0] for j in range(NDEV)])
    dmod_mine = lax.dynamic_slice_in_dim(dmod_all.reshape(NDEV, L, NDEV, CM), me, 1, axis=2)
    cond_t = jnp.pad(cond_all[:NDEV].T, ((0, 0), (0, LANES - NDEV)))
    mod_out = _adamw_mod(w_mod, m_w_mod, v_w_mod, cond_t, dmod_mine, "adamw_mod")

    pool4 = lambda a: a.reshape(w_pool.shape)
    names = ["w_mod", "b_mod", "norm_gain", "w_attn_in", "b_forget", "w_attn_out", "w_pool", "pool_scale", "w_up", "conv_w",
             "conv_b", "w_down", "final_gain"]
    small_pos = {"b_mod": 0, "norm_gain": 1, "pool_scale": 2, "conv_w": 3, "conv_b": 4, "b_forget": 5, "final_gain": 6}
    outs = []
    for kind in range(4):
        for nm in names:
            if nm == "w_mod":
                outs.append(mod_out[kind])
            elif nm in small_pos:
                outs.append(small_out[kind][small_pos[nm]])
            elif nm == "w_pool":
                outs.append(pool4(big[nm][kind]))
            else:
                outs.append(big[nm][kind])
    return (loss, grad_x, *outs)
```

```python
import jax
import jax.numpy as jnp
from jax import lax
from jax.experimental import pallas as pl
from jax.experimental.pallas import tpu as pltpu

NDEV = 8
F32 = jnp.float32
BF16 = jnp.bfloat16
MESH = pl.DeviceIdType.MESH
VMEM_LIMIT_BYTES = 56 * 1024 * 1024
LANES = 128
POOL_WINDOWS = (2, 4, 8, 16)
EPS = 1e-6
ADAM_LR = 0.001
ADAM_B1 = 0.9
ADAM_B2 = 0.999
ADAM_EPS = 1e-08
ADAM_WD = 0.01
ADAM_STEP = 10
NEG_BIG = -1e30
SDS = jax.ShapeDtypeStruct
ANY = pl.BlockSpec(memory_space=pl.ANY)


def _pc(body, **kw):
    return pl.pallas_call(body, **kw)


def _params(n_axes):
    return pltpu.CompilerParams(dimension_semantics=("arbitrary",) * n_axes, vmem_limit_bytes=VMEM_LIMIT_BYTES)


def _pick(n, prefs):
    for p in prefs:
        if p <= n and n % p == 0:
            return p
    return n


def _idx(p):
    return 4 * p[0] + 2 * p[1] + p[2]


def _me():
    return lax.axis_index("x"), lax.axis_index("y"), lax.axis_index("c")


def _all_gather(arrs, name):
    n = len(arrs)

    def body(*refs):
        ins, outs = refs[:n], refs[n:2 * n]
        send_sems, recv_sems, local_sems = refs[2 * n:]
        x, y, c = _me()
        me, sib = (x, y, c), (x, y, 1 - c)
        chips = [(1 - x, y), (x, 1 - y), (1 - x, 1 - y)]

        def copy(t, k, block, to, src=None):
            dst = outs[t].at[_idx(block)]
            return pltpu.make_async_remote_copy(
                src_ref=dst if src is None else src, dst_ref=dst,
                send_sem=send_sems.at[7 * t + k], recv_sem=recv_sems.at[7 * t + k],
                device_id=to, device_id_type=MESH)

        mine = [pltpu.make_async_copy(ins[t], outs[t].at[_idx(me)], local_sems.at[t]) for t in range(n)]
        for cp in mine:
            cp.start()
        first = []
        for t in range(n):
            first.append(copy(t, 0, me, sib, src=ins[t]))
            for j, chip in enumerate(chips):
                first.append(copy(t, 1 + j, me, (*chip, c), src=ins[t]))
        for cp in first:
            cp.start()
        passed = []
        for t in range(n):
            for j, chip in enumerate(chips):
                copy(t, 1 + j, (*chip, c), me).wait_recv()
                cp = copy(t, 4 + j, (*chip, c), sib)
                cp.start()
                passed.append(cp)
        for t in range(n):
            copy(t, 0, sib, me).wait_recv()
            for j, chip in enumerate(chips):
                copy(t, 4 + j, (*chip, 1 - c), me).wait_recv()
        for cp in first + passed:
            cp.wait_send()
        for cp in mine:
            cp.wait()

    return _pc(
        body, name=name,
        out_shape=[SDS((NDEV,) + a.shape, a.dtype) for a in arrs],
        in_specs=[ANY] * n, out_specs=[ANY] * n,
        scratch_shapes=[pltpu.SemaphoreType.DMA((7 * n,)), pltpu.SemaphoreType.DMA((7 * n,)),
                        pltpu.SemaphoreType.DMA((n,))],
    )(*arrs)


HBM = pl.BlockSpec(memory_space=pltpu.HBM)
SEM = pl.BlockSpec(memory_space=pltpu.SEMAPHORE)
EFFECT = pltpu.SideEffectType.DATAFLOW_SIDE_EFFECTING
TOKEN = SDS((8, LANES), F32)


def _hbm(a):
    return pltpu.with_memory_space_constraint(a, pltpu.HBM)


def _landing(own, me):
    return lax.dynamic_update_index_in_dim(lax.empty((NDEV,) + own.shape, own.dtype), own, me, 0)


def _split_call(body, name, n_thru, thru, sems_in, after, sems_out):
    n_sem = len(sems_out)
    operands = [_hbm(a) for a in thru] + list(sems_in) + list(after)
    in_specs = [HBM] * n_thru + [SEM] * len(sems_in) + [ANY] * len(after)
    out_shape = [pltpu.SemaphoreType.DMA((k,)) for k in sems_out] + [pltpu.HBM(a.shape, a.dtype) for a in thru] + [TOKEN]
    out_specs = [SEM] * n_sem + [HBM] * n_thru + [pl.BlockSpec(memory_space=pltpu.VMEM)]
    outs = _pc(body, name=name, in_specs=in_specs, out_specs=out_specs, out_shape=out_shape,
               input_output_aliases={q: n_sem + q for q in range(n_thru)},
               compiler_params=pltpu.CompilerParams(has_side_effects=EFFECT))(*operands)
    return list(outs[:n_sem]), list(outs[n_sem:n_sem + n_thru]), outs[-1]


def _gather_start(shards, me, after, name):
    n = len(shards)
    lands = [_landing(s, me) for s in shards]

    def body(*refs):
        shard_refs, land_refs = refs[:n], refs[n:2 * n]
        send_sems, recv_sems = refs[2 * n + len(after)], refs[2 * n + len(after) + 1]
        x, y, c = _me()
        me_i = _idx((x, y, c))
        peers = [(x, y, 1 - c), (1 - x, y, c), (x, 1 - y, c), (1 - x, 1 - y, c)]
        for t in range(n):
            for k, p in enumerate(peers):
                pltpu.make_async_remote_copy(
                    src_ref=shard_refs[t], dst_ref=land_refs[t].at[me_i], send_sem=send_sems.at[4 * t + k],
                    recv_sem=recv_sems.at[4 * t + k], device_id=p, device_id_type=MESH).start()
        refs[-1][...] = jnp.zeros((8, LANES), F32)

    sems, thru, token = _split_call(body, name, 2 * n, list(shards) + lands, [], after, [4 * n, 4 * n])
    return dict(n=n, sems=sems, shards=thru[:n], lands=thru[n:]), token


def _gather_forward(st, after, name):
    n = st["n"]

    def body(*refs):
        shard_refs, land_refs = refs[:n], refs[n:2 * n]
        send1, recv1 = refs[2 * n], refs[2 * n + 1]
        send2, recv2 = refs[2 * n + 2 + len(after)], refs[2 * n + 3 + len(after)]
        x, y, c = _me()
        sib = (x, y, 1 - c)
        senders = [sib, (1 - x, y, c), (x, 1 - y, c), (1 - x, 1 - y, c)]
        for t in range(n):
            for k, p in enumerate(senders):
                cp = pltpu.make_async_remote_copy(
                    src_ref=shard_refs[t], dst_ref=land_refs[t].at[_idx(p)], send_sem=send1.at[4 * t + k],
                    recv_sem=recv1.at[4 * t + k], device_id=p, device_id_type=MESH)
                cp.wait_send()
                cp.wait_recv()
        for t in range(n):
            for j, p in enumerate(senders[1:]):
                slab = land_refs[t].at[_idx(p)]
                pltpu.make_async_remote_copy(
                    src_ref=slab, dst_ref=slab, send_sem=send2.at[3 * t + j], recv_sem=recv2.at[3 * t + j],
                    device_id=sib, device_id_type=MESH).start()
        refs[-1][...] = jnp.zeros((8, LANES), F32)

    sems, thru, token = _split_call(body, name, 2 * n, st["shards"] + st["lands"], st["sems"], after, [3 * n, 3 * n])
    return dict(n=n, sems=sems, shards=thru[:n], lands=thru[n:]), token


def _gather_wait(st, after, name):
    n = st["n"]

    def body(*refs):
        land_refs = refs[n:2 * n]
        send2, recv2 = refs[2 * n], refs[2 * n + 1]
        x, y, c = _me()
        sib = (x, y, 1 - c)
        for t in range(n):
            for j, chip in enumerate([(1 - x, y), (x, 1 - y), (1 - x, 1 - y)]):
                sent, got = land_refs[t].at[_idx((*chip, c))], land_refs[t].at[_idx((*chip, 1 - c))]
                cp = pltpu.make_async_remote_copy(
                    src_ref=sent, dst_ref=got, send_sem=send2.at[3 * t + j], recv_sem=recv2.at[3 * t + j],
                    device_id=sib, device_id_type=MESH)
                cp.wait_send()
                cp.wait_recv()
        refs[-1][...] = jnp.zeros((8, LANES), F32)

    _, thru, _ = _split_call(body, name, 2 * n, st["shards"] + st["lands"], st["sems"], after, [])
    return thru[n:]


def _scatter_start(parts, me, after, name):
    n = len(parts)
    lands = [_landing(lax.dynamic_index_in_dim(p, me, 0, keepdims=False), me) for p in parts]

    def body(*refs):
        part_refs, land_refs = refs[:n], refs[n:2 * n]
        send_sems, recv_sems = refs[2 * n + len(after)], refs[2 * n + len(after) + 1]
        x, y, c = _me()
        me_i = _idx((x, y, c))
        for t in range(n):
            for r in range(1, NDEV):
                p = (1 - x if r & 4 else x, 1 - y if r & 2 else y, 1 - c if r & 1 else c)
                pltpu.make_async_remote_copy(
                    src_ref=part_refs[t].at[_idx(p)], dst_ref=land_refs[t].at[me_i], send_sem=send_sems.at[7 * t + r - 1],
                    recv_sem=recv_sems.at[7 * t + r - 1], device_id=p, device_id_type=MESH).start()
        refs[-1][...] = jnp.zeros((8, LANES), F32)

    sems, thru, token = _split_call(body, name, 2 * n, list(parts) + lands, [], after, [7 * n, 7 * n])
    return dict(n=n, sems=sems, parts=thru[:n], lands=thru[n:]), token


def _scatter_wait(st, after, name):
    n = st["n"]

    def body(*refs):
        part_refs, land_refs = refs[:n], refs[n:2 * n]
        send_sems, recv_sems = refs[2 * n], refs[2 * n + 1]
        x, y, c = _me()
        for t in range(n):
            for r in range(1, NDEV):
                p = (1 - x if r & 4 else x, 1 - y if r & 2 else y, 1 - c if r & 1 else c)
                cp = pltpu.make_async_remote_copy(
                    src_ref=part_refs[t].at[_idx(p)], dst_ref=land_refs[t].at[_idx(p)], send_sem=send_sems.at[7 * t + r - 1],
                    recv_sem=recv_sems.at[7 * t + r - 1], device_id=p, device_id_type=MESH)
                cp.wait_send()
                cp.wait_recv()
        refs[-1][...] = jnp.zeros((8, LANES), F32)

    _, thru, _ = _split_call(body, name, 2 * n, st["parts"] + st["lands"], st["sems"], after, [])
    return thru[n:]


def _matmul(a, b, *, mode, name, out_dtype, tm=1024, tn=1024, tk=512, b_blocked=False, out_blocked=False,
            a_split=False, b_split=False, res=None, gate=None, y_dtype=None):
    if mode == "tn":
        K, M = (a.shape[0], a.shape[1]) if not a_split else (a.shape[1], 2 * a.shape[2])
    else:
        M, K = (a.shape[0], a.shape[1]) if not a_split else (a.shape[1], 2 * a.shape[2])
    if b_blocked:
        if mode == "nn":
            N, tn = b.shape[0] * b.shape[2], b.shape[2]
        else:
            N, tk = b.shape[1], b.shape[2]
    elif b_split:
        N = 2 * b.shape[2]
    else:
        N = b.shape[0] if mode == "nt" else b.shape[1]
    tm = _pick(M, (tm, 704, 512, 384, 256, 128))
    if not (b_blocked and mode == "nn"):
        tn = _pick(N, (tn, 1024, 768, 512, 384, 256, 128))
    if not (b_blocked and mode == "nt"):
        tk = _pick(K, (tk, 512, 384, 256, 128))
    nm, nn_, nk = M // tm, N // tn, K // tk

    if mode == "tn":
        a_spec = pl.BlockSpec((tk, tm), lambda i, j, k: (k, i))
        dims = (((0,), (0,)), ((), ()))
    elif a_split:
        per = a.shape[2] // tk
        a_spec = pl.BlockSpec((None, tm, tk), lambda i, j, k: (k // per, i, k % per))
    else:
        a_spec = pl.BlockSpec((tm, tk), lambda i, j, k: (i, k))
    if mode == "nn":
        dims = (((1,), (0,)), ((), ()))
        if b_blocked:
            b_spec = pl.BlockSpec((None, tk, tn), lambda i, j, k: (j, k, 0))
        else:
            b_spec = pl.BlockSpec((tk, tn), lambda i, j, k: (k, j))
    elif mode == "nt":
        dims = (((1,), (1,)), ((), ()))
        if b_blocked:
            b_spec = pl.BlockSpec((None, tn, tk), lambda i, j, k: (k, j, 0))
        else:
            b_spec = pl.BlockSpec((tn, tk), lambda i, j, k: (j, k))
    else:
        if b_split:
            per_b = b.shape[2] // tn
            b_spec = pl.BlockSpec((None, tk, tn), lambda i, j, k: (j // per_b, k, j % per_b))
        else:
            b_spec = pl.BlockSpec((tk, tn), lambda i, j, k: (k, j))
    if out_blocked:
        o_spec = pl.BlockSpec((None, tm, tn), lambda i, j, k: (j, i, 0))
        o_shape = SDS((nn_, M, tn), out_dtype)
    else:
        o_spec = pl.BlockSpec((tm, tn), lambda i, j, k: (i, j))
        o_shape = SDS((M, N), out_dtype)
    fused = res is not None
    in_specs, operands = [a_spec, b_spec], [a, b]
    out_specs, out_shapes = [o_spec], [o_shape]
    if fused:
        in_specs += [pl.BlockSpec((tm, tn), lambda i, j, k: (i, j)), pl.BlockSpec((1, tn), lambda i, j, k: (0, j))]
        operands += [res, gate]
        if y_dtype is not None:
            out_specs.append(pl.BlockSpec((tm, tn), lambda i, j, k: (i, j)))
            out_shapes.append(SDS((M, N), y_dtype))

    def body(*refs):
        a_ref, b_ref = refs[0], refs[1]
        acc_ref = refs[-1]
        k = pl.program_id(2)

        @pl.when(k == 0)
        def _():
            acc_ref[...] = jnp.zeros_like(acc_ref)

        acc_ref[...] += lax.dot_general(a_ref[...], b_ref[...], dims, preferred_element_type=F32)

        @pl.when(k == nk - 1)
        def _():
            acc = acc_ref[...]
            if fused:
                res_ref, gate_ref, o_ref = refs[2], refs[3], refs[4]
                o_ref[...] = (res_ref[...] + gate_ref[...] * acc).astype(o_ref.dtype)
                if y_dtype is not None:
                    refs[5][...] = acc.astype(y_dtype)
            else:
                refs[2][...] = acc.astype(refs[2].dtype)

    outs = _pc(body, name=name, grid=(nm, nn_, nk), in_specs=in_specs, out_specs=out_specs, out_shape=out_shapes,
               scratch_shapes=[pltpu.VMEM((tm, tn), F32)], compiler_params=_params(3))(*operands)
    return outs[0] if len(outs) == 1 else tuple(outs)


def _norm_mod(x, gain, sc, sh, out_dtype, name):
    S, D = x.shape
    tr = _pick(S, (256, 128))

    def body(x_ref, g_ref, sc_ref, sh_ref, o_ref):
        xv = x_ref[...]
        r = lax.rsqrt(jnp.mean(xv * xv, axis=-1, keepdims=True) + EPS)
        n = (xv * r) * g_ref[...]
        o_ref[...] = (n * (1.0 + sc_ref[...]) + sh_ref[...]).astype(o_ref.dtype)

    row = pl.BlockSpec((tr, D), lambda i: (i, 0))
    vec = pl.BlockSpec((1, D), lambda i: (0, 0))
    return _pc(body, name=name, grid=(S // tr,), in_specs=[row, vec, vec, vec], out_specs=row,
               out_shape=SDS((S, D), out_dtype), compiler_params=_params(1))(x, gain, sc, sh)


def _norm_mod_bwd(dh, x, dxres, gain, sc, name):
    S, D = x.shape
    tr = _pick(S, (256, 128))

    def body(dh_ref, x_ref, dxres_ref, g_ref, sc_ref, dx_ref, dgain_ref, dsc_ref, dsh_ref):
        @pl.when(pl.program_id(0) == 0)
        def _():
            dgain_ref[...] = jnp.zeros_like(dgain_ref)
            dsc_ref[...] = jnp.zeros_like(dsc_ref)
            dsh_ref[...] = jnp.zeros_like(dsh_ref)

        xv = x_ref[...]
        dh = dh_ref[...].astype(F32)
        r = lax.rsqrt(jnp.mean(xv * xv, axis=-1, keepdims=True) + EPS)
        nh = xv * r
        gn = g_ref[...]
        dn = dh * (1.0 + sc_ref[...])
        dgain_ref[...] += jnp.sum(dn * nh, axis=0, keepdims=True)
        dsc_ref[...] += jnp.sum(dh * (nh * gn), axis=0, keepdims=True)
        dsh_ref[...] += jnp.sum(dh, axis=0, keepdims=True)
        dnh = dn * gn
        dx = r * (dnh - nh * jnp.mean(dnh * nh, axis=-1, keepdims=True))
        dx_ref[...] = dxres_ref[...] + dx

    row = pl.BlockSpec((tr, D), lambda i: (i, 0))
    vec = pl.BlockSpec((1, D), lambda i: (0, 0))
    return _pc(body, name=name, grid=(S // tr,), in_specs=[row, row, row, vec, vec], out_specs=[row, vec, vec, vec],
               out_shape=[SDS((S, D), F32), SDS((1, D), F32), SDS((1, D), F32), SDS((1, D), F32)],
               compiler_params=_params(1))(dh, x, dxres, gain, sc)


def _gate_bwd(dx, y, gate, name):
    S, D = dx.shape
    tr = _pick(S, (256, 128))

    def body(dx_ref, y_ref, gate_ref, dy_ref, dgate_ref):
        @pl.when(pl.program_id(0) == 0)
        def _():
            dgate_ref[...] = jnp.zeros_like(dgate_ref)

        dxv = dx_ref[...]
        dgate_ref[...] += jnp.sum(dxv * y_ref[...].astype(F32), axis=0, keepdims=True)
        dy_ref[...] = (dxv * gate_ref[...]).astype(BF16)

    row = pl.BlockSpec((tr, D), lambda i: (i, 0))
    vec = pl.BlockSpec((1, D), lambda i: (0, 0))
    return _pc(body, name=name, grid=(S // tr,), in_specs=[row, row, vec], out_specs=[row, vec],
               out_shape=[SDS((S, D), BF16), SDS((1, D), F32)], compiler_params=_params(1))(dx, y, gate)


def _loss_head(x, target, fgain, name):
    S, D = x.shape
    tr = _pick(S, (256, 128))

    def body(x_ref, t_ref, fg_ref, dx_ref, dfg_ref, loss_ref):
        @pl.when(pl.program_id(0) == 0)
        def _():
            dfg_ref[...] = jnp.zeros_like(dfg_ref)
            loss_ref[...] = jnp.zeros_like(loss_ref)

        xv = x_ref[...]
        fg = fg_ref[...]
        r = lax.rsqrt(jnp.mean(xv * xv, axis=-1, keepdims=True) + EPS)
        nh = xv * r
        e = nh * fg - t_ref[...]
        loss_ref[...] += 0.5 * jnp.sum(jnp.mean(e * e, axis=-1, keepdims=True))
        dy = e * (1.0 / D)
        dfg_ref[...] += jnp.sum(dy * nh, axis=0, keepdims=True)
        dnh = dy * fg
        dx_ref[...] = r * (dnh - nh * jnp.mean(dnh * nh, axis=-1, keepdims=True))

    row = pl.BlockSpec((tr, D), lambda i: (i, 0))
    vec = pl.BlockSpec((1, D), lambda i: (0, 0))
    tile = pl.BlockSpec((8, LANES), lambda i: (0, 0))
    return _pc(body, name=name, grid=(S // tr,), in_specs=[row, row, vec], out_specs=[row, vec, tile],
               out_shape=[SDS((S, D), F32), SDS((1, D), F32), SDS((8, LANES), F32)],
               compiler_params=_params(1))(x, target, fgain)


def _shift_down(v, k, rows):
    return jnp.where(rows >= k, pltpu.roll(v, k, axis=0), 0.0)


def _shift_up(v, k, rows):
    n = v.shape[0]
    return jnp.where(rows < n - k, pltpu.roll(v, n - k, axis=0), 0.0)


def _conv(uv, w, b, rows):
    return ((b + _shift_down(uv, 2, rows) * w[0:1]) + _shift_down(uv, 1, rows) * w[1:2]) + uv * w[2:3]


def _convgate_fwd(u, cw, cb, name):
    S, F2 = u.shape
    DFF = F2 // 2
    tc = _pick(DFF, (256, 128))
    sub = min(tc, LANES)
    nj = DFF // tc

    def body(ua_ref, ug_ref, wa_ref, wg_ref, ba_ref, bg_ref, o_ref):
        rows = lax.broadcasted_iota(jnp.int32, (S, sub), 0)
        for q in range(tc // sub):
            sl = slice(q * sub, (q + 1) * sub)
            ya = _conv(ua_ref[:, sl], wa_ref[:, sl], ba_ref[:, sl], rows)
            yg = _conv(ug_ref[:, sl], wg_ref[:, sl], bg_ref[:, sl], rows)
            o_ref[:, sl] = (yg * jax.nn.sigmoid(yg) * ya).astype(BF16)

    col = lambda off: pl.BlockSpec((S, tc), lambda j: (0, j + off))
    w3 = lambda off: pl.BlockSpec((3, tc), lambda j: (0, j + off))
    b1 = lambda off: pl.BlockSpec((1, tc), lambda j: (0, j + off))
    return _pc(body, name=name, grid=(nj,), in_specs=[col(0), col(nj), w3(0), w3(nj), b1(0), b1(nj)],
               out_specs=col(0), out_shape=SDS((S, DFF), BF16), compiler_params=_params(1))(u, u, cw, cw, cb, cb)


def _convgate_bwd(u, dact, cw, cb, name):
    S, F2 = u.shape
    DFF = F2 // 2
    tc = _pick(DFF, (256, 128))
    sub = min(tc, LANES)
    nj = DFF // tc

    def body(ua_ref, ug_ref, da_ref, wa_ref, wg_ref, ba_ref, bg_ref, du_ref, dcw_ref, dcb_ref):
        rows = lax.broadcasted_iota(jnp.int32, (S, sub), 0)
        for q in range(tc // sub):
            sl = slice(q * sub, (q + 1) * sub)
            ua, ug = ua_ref[:, sl], ug_ref[:, sl]
            wa, wg = wa_ref[:, sl], wg_ref[:, sl]
            ya = _conv(ua, wa, ba_ref[:, sl], rows)
            yg = _conv(ug, wg, bg_ref[:, sl], rows)
            s = jax.nn.sigmoid(yg)
            da = da_ref[:, sl]
            dya = da * (yg * s)
            dyg = da * ya * (s * (1.0 + yg * (1.0 - s)))
            for h, (dy, uv, w) in enumerate(((dya, ua, wa), (dyg, ug, wg))):
                du = (dy * w[2:3] + _shift_up(dy, 1, rows) * w[1:2]) + _shift_up(dy, 2, rows) * w[0:1]
                du_ref[h, :, sl] = du.astype(BF16)
                dcw_ref[h, 0:1, sl] = jnp.sum(dy * _shift_down(uv, 2, rows), axis=0, keepdims=True)
                dcw_ref[h, 1:2, sl] = jnp.sum(dy * _shift_down(uv, 1, rows), axis=0, keepdims=True)
                dcw_ref[h, 2:3, sl] = jnp.sum(dy * uv, axis=0, keepdims=True)
                dcb_ref[h, :, sl] = jnp.sum(dy, axis=0, keepdims=True)

    col = lambda off: pl.BlockSpec((S, tc), lambda j: (0, j + off))
    w3 = lambda off: pl.BlockSpec((3, tc), lambda j: (0, j + off))
    b1 = lambda off: pl.BlockSpec((1, tc), lambda j: (0, j + off))
    return _pc(body, name=name, grid=(nj,),
               in_specs=[col(0), col(nj), col(0), w3(0), w3(nj), b1(0), b1(nj)],
               out_specs=[pl.BlockSpec((2, S, tc), lambda j: (0, 0, j)), pl.BlockSpec((2, 3, tc), lambda j: (0, 0, j)),
                          pl.BlockSpec((2, 1, tc), lambda j: (0, 0, j))],
               out_shape=[SDS((2, S, DFF), BF16), SDS((2, 3, DFF), F32), SDS((2, 1, DFF), F32)],
               compiler_params=_params(1))(u, u, dact, cw, cw, cb, cb)


def _pool_diff(h, name):
    S, D = h.shape
    G = len(POOL_WINDOWS)
    CG = D // G
    tc = min(CG, LANES)
    per = CG // tc

    def body(h_ref, d_ref):
        g = pl.program_id(0)
        rows = lax.broadcasted_iota(jnp.int32, (S, tc), 0)
        for gi, w in enumerate(POOL_WINDOWS):
            @pl.when(g == gi)
            def _(w=w):
                hv = h_ref[...]
                s, k = hv, 1
                while k < w:
                    s = s + _shift_down(s, k, rows)
                    k *= 2
                count = jnp.minimum(rows + 1, w).astype(F32)
                d_ref[...] = (s / count - hv).astype(BF16)

    spec = pl.BlockSpec((S, tc), lambda g, j: (0, g * per + j))
    return _pc(body, name=name, grid=(G, per), in_specs=[spec], out_specs=spec, out_shape=SDS((S, D), BF16),
               compiler_params=_params(2))(h)


def _pool_diff_bwd(dd, name):
    S, D = dd.shape
    G = len(POOL_WINDOWS)
    CG = D // G
    tc = min(CG, LANES)
    per = CG // tc

    def body(dd_ref, o_ref):
        g = pl.program_id(0)
        rows = lax.broadcasted_iota(jnp.int32, (S, tc), 0)
        for gi, w in enumerate(POOL_WINDOWS):
            @pl.when(g == gi)
            def _(w=w):
                dv = dd_ref[...]
                count = jnp.minimum(rows + 1, w).astype(F32)
                s, k = dv / count, 1
                while k < w:
                    s = s + _shift_up(s, k, rows)
                    k *= 2
                o_ref[...] = s - dv

    spec = pl.BlockSpec((S, tc), lambda g, j: (0, g * per + j))
    return _pc(body, name=name, grid=(G, per), in_specs=[spec], out_specs=spec, out_shape=SDS((S, D), F32),
               compiler_params=_params(2))(dd)


def _pool_mm(d, w, res, gate, name):
    S, D = d.shape
    G, CG, _ = w.shape
    tm = _pick(S, (512, 256, 128))

    def body(d_ref, w_ref, res_ref, gate_ref, o_ref, e_ref):
        acc = jnp.dot(d_ref[...], w_ref[...], preferred_element_type=F32)
        o_ref[...] = res_ref[...] + gate_ref[...] * acc
        e_ref[...] = acc.astype(BF16)

    blk = pl.BlockSpec((tm, CG), lambda g, i: (i, g))
    return _pc(body, name=name, grid=(G, S // tm),
               in_specs=[blk, pl.BlockSpec((None, CG, CG), lambda g, i: (g, 0, 0)), blk,
                         pl.BlockSpec((1, CG), lambda g, i: (0, g))],
               out_specs=[blk, blk], out_shape=[SDS((S, D), F32), SDS((S, D), BF16)],
               compiler_params=_params(2))(d, w, res, gate)


def _pool_mm_bwd(de, d, w, name):
    S, D = de.shape
    G, CG, _ = w.shape
    tm = _pick(S, (512, 256, 128))
    ns = S // tm

    def body(de_ref, d_ref, w_ref, dd_ref, dw_ref, acc_ref):
        i = pl.program_id(1)

        @pl.when(i == 0)
        def _():
            acc_ref[...] = jnp.zeros_like(acc_ref)

        dev = de_ref[...]
        dd_ref[...] = lax.dot_general(dev, w_ref[...], (((1,), (1,)), ((), ())), preferred_element_type=F32)
        acc_ref[...] += lax.dot_general(d_ref[...], dev, (((0,), (0,)), ((), ())), preferred_element_type=F32)

        @pl.when(i == ns - 1)
        def _():
            dw_ref[...] = acc_ref[...].astype(BF16)

    blk = pl.BlockSpec((tm, CG), lambda g, i: (i, g))
    wsp = pl.BlockSpec((None, CG, CG), lambda g, i: (g, 0, 0))
    return _pc(body, name=name, grid=(G, ns), in_specs=[blk, blk, wsp], out_specs=[blk, wsp],
               out_shape=[SDS((S, D), F32), SDS((G, CG, CG), BF16)], scratch_shapes=[pltpu.VMEM((CG, CG), F32)],
               compiler_params=_params(2))(de, d, w)


def _log_sigmoid(z):
    return jnp.minimum(z, 0.0) - jnp.log(1.0 + jnp.exp(-jnp.abs(z)))


def _dot2(a, tri):
    hi = a.astype(BF16)
    lo = (a - hi.astype(F32)).astype(BF16)
    return jnp.dot(hi, tri, preferred_element_type=F32) + jnp.dot(lo, tri, preferred_element_type=F32)


_NT = (((1,), (1,)), ((), ()))
_TN = (((0,), (0,)), ((), ()))


def _forget_cumsum(flog, bf, name):
    S, W = flog.shape
    tb = _pick(S, (128,))

    def body(f_ref, b_ref, o_ref):
        r = lax.broadcasted_iota(jnp.int32, (tb, tb), 0)
        c = lax.broadcasted_iota(jnp.int32, (tb, tb), 1)
        tri = (c <= r).astype(F32)
        carry = jnp.zeros((1, W), F32)
        for q in range(S // tb):
            ls = _log_sigmoid(f_ref[q * tb:(q + 1) * tb, :] + b_ref[...])
            o_ref[q * tb:(q + 1) * tb, :] = carry + jnp.dot(tri, ls, preferred_element_type=F32,
                                                            precision=lax.Precision.HIGHEST)
            carry = carry + jnp.sum(ls, axis=0, keepdims=True)

    return _pc(body, name=name, out_shape=SDS((S, W), F32))(flog, bf)


def _forget_cumsum_bwd(dF, flog, bf, name):
    S, W = flog.shape
    tb = _pick(S, (128,))

    def body(d_ref, f_ref, b_ref, o_ref, db_ref):
        r = lax.broadcasted_iota(jnp.int32, (tb, tb), 0)
        c = lax.broadcasted_iota(jnp.int32, (tb, tb), 1)
        tri = (c >= r).astype(F32)
        carry = jnp.zeros((1, W), F32)
        db = jnp.zeros((1, W), F32)
        for q in reversed(range(S // tb)):
            dv = d_ref[q * tb:(q + 1) * tb, :]
            dls = carry + jnp.dot(tri, dv, preferred_element_type=F32, precision=lax.Precision.HIGHEST)
            carry = carry + jnp.sum(dv, axis=0, keepdims=True)
            dfl = dls * jax.nn.sigmoid(-(f_ref[q * tb:(q + 1) * tb, :] + b_ref[...]))
            o_ref[q * tb:(q + 1) * tb, :] = dfl
            db = db + jnp.sum(dfl, axis=0, keepdims=True)
        db_ref[...] = db

    return _pc(body, name=name, out_shape=[SDS((S, W), F32), SDS((1, W), F32)])(dF, flog, bf)


def _sb_fwd(qkv, NH, NSB, HD, T, name):
    S = qkv.shape[0]
    nq = S // T
    scale = HD ** -0.5

    def body(q_ref, k_ref, v_ref, o_ref, tot_ref):
        i = pl.program_id(1)
        q = q_ref[...]
        row = lax.broadcasted_iota(jnp.int32, (T, T), 0)
        col = lax.broadcasted_iota(jnp.int32, (T, T), 1)
        upper = (row > col).astype(BF16)

        def blk(kb, carry, acc, diag):
            sl = pl.ds(pl.multiple_of(kb * T, T), T)
            k, v = k_ref[sl, :], v_ref[sl, :]
            z = lax.dot_general(q, k, _NT, preferred_element_type=F32) * scale
            ls = _log_sigmoid(z)
            lr = ls - z
            if diag:
                lr = jnp.where(col < row, lr, 0.0)
            rest = _dot2(lr, upper) + carry
            w = jnp.exp(ls + rest)
            if diag:
                w = jnp.where(col < row, w, 0.0)
            acc = acc + jnp.dot(w.astype(BF16), v, preferred_element_type=F32)
            return carry + jnp.sum(lr, axis=1, keepdims=True), acc

        carry, acc = blk(i, jnp.zeros((T, 1), F32), jnp.zeros((T, HD), F32), True)
        carry, acc = lax.fori_loop(0, i, lambda jj, ca: blk(i - 1 - jj, ca[0], ca[1], False), (carry, acc))
        o_ref[...] = acc.astype(BF16)
        tot_ref[...] = carry

    return _pc(body, name=name, grid=(NSB, nq),
               in_specs=[pl.BlockSpec((T, HD), lambda h, i: (i, h)),
                         pl.BlockSpec((S, HD), lambda h, i: (0, NH + h)),
                         pl.BlockSpec((S, HD), lambda h, i: (0, 2 * NH + h))],
               out_specs=[pl.BlockSpec((T, HD), lambda h, i: (i, h)), pl.BlockSpec((None, T, 1), lambda h, i: (h, i, 0))],
               out_shape=[SDS((S, NSB * HD), BF16), SDS((NSB, S, 1), F32)],
               compiler_params=_params(2))(qkv, qkv, qkv)


def _sb_bwd(qkv, do, tot, NH, NSB, HD, T, name):
    S = qkv.shape[0]
    nq = S // T
    scale = HD ** -0.5

    def body(q_ref, k_ref, v_ref, do_ref, tot_ref, dq_ref, dk_ref, dv_ref, dk_acc, dv_acc):
        i = pl.program_id(1)

        @pl.when(i == 0)
        def _():
            dk_acc[...] = jnp.zeros_like(dk_acc)
            dv_acc[...] = jnp.zeros_like(dv_acc)

        q, do_, tot_ = q_ref[...], do_ref[...], tot_ref[...]
        row = lax.broadcasted_iota(jnp.int32, (T, T), 0)
        col = lax.broadcasted_iota(jnp.int32, (T, T), 1)
        incl = (row <= col).astype(BF16)
        strict = (row < col).astype(BF16)

        def blk(kb, cl, cg, dq, diag):
            sl = pl.ds(pl.multiple_of(kb * T, T), T)
            k, v = k_ref[sl, :], v_ref[sl, :]
            z = lax.dot_general(q, k, _NT, preferred_element_type=F32) * scale
            ls = _log_sigmoid(z)
            lr = ls - z
            if diag:
                lr = jnp.where(col < row, lr, 0.0)
            rest = tot_ - (cl + _dot2(lr, incl))
            w = jnp.exp(ls + rest)
            if diag:
                w = jnp.where(col < row, w, 0.0)
            g = lax.dot_general(do_, v, _NT, preferred_element_type=F32) * w
            dv_acc[sl, :] += lax.dot_general(w.astype(BF16), do_, _TN, preferred_element_type=F32)
            dlr = cg + _dot2(g, strict)
            dz = g * jnp.exp(lr) - dlr * jnp.exp(ls)
            if diag:
                dz = jnp.where(col < row, dz, 0.0)
            dzb = (dz * scale).astype(BF16)
            dq = dq + jnp.dot(dzb, k, preferred_element_type=F32)
            dk_acc[sl, :] += lax.dot_general(dzb, q, _TN, preferred_element_type=F32)
            return cl + jnp.sum(lr, axis=1, keepdims=True), cg + jnp.sum(g, axis=1, keepdims=True), dq

        zero = jnp.zeros((T, 1), F32)
        cl, cg, dq = lax.fori_loop(0, i, lambda kb, ca: blk(kb, ca[0], ca[1], ca[2], False),
                                   (zero, zero, jnp.zeros((T, HD), F32)))
        _, _, dq = blk(i, cl, cg, dq, True)
        dq_ref[...] = dq.astype(BF16)

        @pl.when(i == nq - 1)
        def _():
            dk_ref[...] = dk_acc[...].astype(BF16)
            dv_ref[...] = dv_acc[...].astype(BF16)

    qblk = pl.BlockSpec((T, HD), lambda h, i: (i, h))
    full = pl.BlockSpec((S, HD), lambda h, i: (0, h))
    return _pc(body, name=name, grid=(NSB, nq),
               in_specs=[qblk, pl.BlockSpec((S, HD), lambda h, i: (0, NH + h)),
                         pl.BlockSpec((S, HD), lambda h, i: (0, 2 * NH + h)), qblk,
                         pl.BlockSpec((None, T, 1), lambda h, i: (h, i, 0))],
               out_specs=[qblk, full, full],
               out_shape=[SDS((S, NSB * HD), BF16)] * 3,
               scratch_shapes=[pltpu.VMEM((S, HD), F32), pltpu.VMEM((S, HD), F32)],
               compiler_params=_params(2))(qkv, qkv, qkv, do, tot)


def _fox_fwd(qkv, fcol, frow, NH, NSB, HD, T, name):
    S = qkv.shape[0]
    NFX = NH - NSB
    nq = S // T
    scale = HD ** -0.5

    def body(q_ref, k_ref, v_ref, fq_ref, fk_ref, o_ref, o32_ref, lse_ref):
        i = pl.program_id(1)
        q, fq = q_ref[...], fq_ref[...]
        row = lax.broadcasted_iota(jnp.int32, (T, T), 0)
        col = lax.broadcasted_iota(jnp.int32, (T, T), 1)

        def blk(kb, m, l, acc, rem, diag):
            sl = pl.ds(pl.multiple_of(kb * T, T), T)
            k, v = k_ref[sl, :], v_ref[sl, :]
            s = lax.dot_general(q, k, _NT, preferred_element_type=F32) * scale + (fq - fk_ref[kb])
            if diag:
                s = jnp.where(col <= row, s, NEG_BIG)
            m_new = jnp.maximum(m, jnp.max(s, axis=1, keepdims=True))
            p = jnp.exp(s - m_new)
            alpha = jnp.exp(m - m_new)
            l = alpha * l + jnp.sum(p, axis=1, keepdims=True)
            hi = p.astype(BF16)
            lo = (p - hi.astype(F32)).astype(BF16)
            acc = alpha * acc + jnp.dot(hi, v, preferred_element_type=F32)
            rem = alpha * rem + jnp.dot(lo, v, preferred_element_type=F32)
            return m_new, l, acc, rem

        zero = jnp.zeros((T, HD), F32)
        m, l, acc, rem = blk(i, jnp.full((T, 1), NEG_BIG, F32), jnp.zeros((T, 1), F32), zero, zero, True)
        m, l, acc, rem = lax.fori_loop(0, i, lambda kb, ca: blk(kb, ca[0], ca[1], ca[2], ca[3], False), (m, l, acc, rem))
        o_ref[...] = (acc / l).astype(BF16)
        o32_ref[...] = (acc + rem) / l
        lse_ref[...] = m + jnp.log(l)

    vec = pl.BlockSpec((None, T, 1), lambda h, i: (h, i, 0))
    return _pc(body, name=name, grid=(NFX, nq),
               in_specs=[pl.BlockSpec((T, HD), lambda h, i: (i, NSB + h)),
                         pl.BlockSpec((S, HD), lambda h, i: (0, NH + NSB + h)),
                         pl.BlockSpec((S, HD), lambda h, i: (0, 2 * NH + NSB + h)),
                         vec, pl.BlockSpec((None, nq, 1, T), lambda h, i: (h, 0, 0, 0))],
               out_specs=[pl.BlockSpec((T, HD), lambda h, i: (i, h)), pl.BlockSpec((T, HD), lambda h, i: (i, h)), vec],
               out_shape=[SDS((S, NFX * HD), BF16), SDS((S, NFX * HD), F32), SDS((NFX, S, 1), F32)],
               compiler_params=_params(2))(qkv, qkv, qkv, fcol, frow)


def _fox_bwd(qkv, do, o, fcol, frow, lse, NH, NSB, HD, T, name):
    S = qkv.shape[0]
    NFX = NH - NSB
    nq = S // T
    scale = HD ** -0.5

    def body(q_ref, k_ref, v_ref, do_ref, o_ref, fq_ref, fk_ref, lse_ref, dq_ref, dk_ref, dv_ref, dfk_ref,
             dk_acc, dv_acc, dfk_acc):
        i = pl.program_id(1)

        @pl.when(i == 0)
        def _():
            dk_acc[...] = jnp.zeros_like(dk_acc)
            dv_acc[...] = jnp.zeros_like(dv_acc)
            dfk_acc[...] = jnp.zeros_like(dfk_acc)

        q, do_, fq, lse_ = q_ref[...], do_ref[...], fq_ref[...], lse_ref[...]
        delta = jnp.sum(do_.astype(F32) * o_ref[...], axis=1, keepdims=True)
        row = lax.broadcasted_iota(jnp.int32, (T, T), 0)
        col = lax.broadcasted_iota(jnp.int32, (T, T), 1)

        def blk(kb, dq, diag):
            sl = pl.ds(pl.multiple_of(kb * T, T), T)
            k, v = k_ref[sl, :], v_ref[sl, :]
            s = lax.dot_general(q, k, _NT, preferred_element_type=F32) * scale + (fq - fk_ref[kb])
            p = jnp.exp(s - lse_)
            if diag:
                p = jnp.where(col <= row, p, 0.0)
            ds = p * (lax.dot_general(do_, v, _NT, preferred_element_type=F32) - delta)
            dv_acc[sl, :] += lax.dot_general(p.astype(BF16), do_, _TN, preferred_element_type=F32)
            dsb = (ds * scale).astype(BF16)
            dk_acc[sl, :] += lax.dot_general(dsb, q, _TN, preferred_element_type=F32)
            dfk_acc[kb] -= jnp.sum(ds, axis=0, keepdims=True)
            return dq + jnp.dot(dsb, k, preferred_element_type=F32)

        dq = lax.fori_loop(0, i, lambda kb, dq: blk(kb, dq, False), jnp.zeros((T, HD), F32))
        dq = blk(i, dq, True)
        dq_ref[...] = dq.astype(BF16)

        @pl.when(i == nq - 1)
        def _():
            dk_ref[...] = dk_acc[...].astype(BF16)
            dv_ref[...] = dv_acc[...].astype(BF16)
            dfk_ref[...] = dfk_acc[...]

    vec = pl.BlockSpec((None, T, 1), lambda h, i: (h, i, 0))
    rowv = pl.BlockSpec((None, nq, 1, T), lambda h, i: (h, 0, 0, 0))
    oblk = pl.BlockSpec((T, HD), lambda h, i: (i, h))
    full = pl.BlockSpec((S, HD), lambda h, i: (0, h))
    return _pc(body, name=name, grid=(NFX, nq),
               in_specs=[pl.BlockSpec((T, HD), lambda h, i: (i, NSB + h)),
                         pl.BlockSpec((S, HD), lambda h, i: (0, NH + NSB + h)),
                         pl.BlockSpec((S, HD), lambda h, i: (0, 2 * NH + NSB + h)),
                         pl.BlockSpec((T, HD), lambda h, i: (i, NSB + h)), oblk, vec, rowv, vec],
               out_specs=[oblk, full, full, rowv],
               out_shape=[SDS((S, NFX * HD), BF16)] * 3 + [SDS((NFX, nq, 1, T), F32)],
               scratch_shapes=[pltpu.VMEM((S, HD), F32), pltpu.VMEM((S, HD), F32), pltpu.VMEM((nq, 1, T), F32)],
               compiler_params=_params(2))(qkv, qkv, qkv, do, o, fcol, frow, lse)


def _silu(c_all, name):
    def body(c_ref, o_ref):
        cv = c_ref[...]
        o_ref[...] = cv * jax.nn.sigmoid(cv)

    return _pc(body, name=name, out_shape=SDS(c_all.shape, F32))(c_all)


def _mod_project(cond, w_mod, name):
    L, D, C = w_mod.shape
    tk = _pick(D, (512, 256, 128))

    def body(c_ref, w_ref, o_ref):
        @pl.when(pl.program_id(1) == 0)
        def _():
            o_ref[...] = jnp.zeros_like(o_ref)

        o_ref[...] += jnp.dot(c_ref[...].astype(BF16), w_ref[...].astype(BF16), preferred_element_type=F32)

    return _pc(body, name=name, grid=(L, D // tk),
               in_specs=[pl.BlockSpec((16, tk), lambda l, k: (0, k)), pl.BlockSpec((None, tk, C), lambda l, k: (l, k, 0))],
               out_specs=pl.BlockSpec((None, 16, C), lambda l, k: (l, 0, 0)),
               out_shape=SDS((L, 16, C), F32), compiler_params=_params(2))(cond, w_mod)


def _adam_math(w, g, m, v):
    m = ADAM_B1 * m + (1.0 - ADAM_B1) * g
    v = ADAM_B2 * v + (1.0 - ADAM_B2) * (g * g)
    m_hat = m / (1.0 - ADAM_B1 ** ADAM_STEP)
    v_hat = v / (1.0 - ADAM_B2 ** ADAM_STEP)
    delta = -ADAM_LR * (m_hat / (jnp.sqrt(v_hat) + ADAM_EPS) + ADAM_WD * w)
    return delta, m, v


def _adamw(w, m, v, parts, layer, prev, name):
    L, R, C = w.shape
    NP = parts.shape[0]
    tr = _pick(R, (128, 64, 88, 32, 16, 8))
    nprev = 0 if prev is None else 4

    def body(w_ref, m_ref, v_ref, p_ref, *rest):
        g_ref, d_ref, mo_ref, vo_ref = rest[nprev:]
        g = p_ref[0].astype(F32)
        for j in range(1, NP):
            g = g + p_ref[j].astype(F32)
        delta, mn, vn = _adam_math(w_ref[...], g, m_ref[...], v_ref[...])
        g_ref[...] = g
        d_ref[...] = delta
        mo_ref[...] = mn
        vo_ref[...] = vn

    blk = pl.BlockSpec((None, tr, C), lambda i: (layer, i, 0))
    in_specs = [blk, blk, blk, pl.BlockSpec((NP, tr, C), lambda i: (0, i, 0))]
    operands = [w, m, v, parts]
    aliases = {}
    if prev is not None:
        in_specs += [ANY] * 4
        operands += list(prev)
        aliases = {4 + q: q for q in range(4)}
    return _pc(body, name=name, grid=(R // tr,), in_specs=in_specs, out_specs=[blk] * 4,
               out_shape=[SDS(w.shape, F32)] * 4, input_output_aliases=aliases,
               compiler_params=_params(1))(*operands)


def _adamw_mod(w, m, v, cond_t, dmod, name):
    L, D, C = w.shape
    tr = _pick(D, (128, 64))

    def body(w_ref, m_ref, v_ref, ct_ref, dm_ref, g_ref, d_ref, mo_ref, vo_ref):
        ct = ct_ref[...]
        g = ct[:, 0:1] * dm_ref[0]
        for b in range(1, NDEV):
            g = g + ct[:, b:b + 1] * dm_ref[b]
        delta, mn, vn = _adam_math(w_ref[...], g, m_ref[...], v_ref[...])
        g_ref[...] = g
        d_ref[...] = delta
        mo_ref[...] = mn
        vo_ref[...] = vn

    blk = pl.BlockSpec((None, tr, C), lambda l, i: (l, i, 0))
    return _pc(body, name=name, grid=(L, D // tr),
               in_specs=[blk, blk, blk, pl.BlockSpec((tr, LANES), lambda l, i: (i, 0)),
                         pl.BlockSpec((NDEV, None, 1, C), lambda l, i: (0, l, 0, 0))],
               out_specs=[blk] * 4, out_shape=[SDS(w.shape, F32)] * 4, compiler_params=_params(2))(w, m, v, cond_t, dmod)


def _sum_parts(parts, name):
    NP, R, C = parts.shape
    tr = _pick(R, (256, 128, 64, 32, 16, 8))

    def body(p_ref, o_ref):
        g = p_ref[0]
        for j in range(1, NP):
            g = g + p_ref[j]
        o_ref[...] = g

    return _pc(body, name=name, grid=(R // tr,), in_specs=[pl.BlockSpec((NP, tr, C), lambda i: (0, i, 0))],
               out_specs=pl.BlockSpec((tr, C), lambda i: (i, 0)), out_shape=SDS((R, C), F32),
               compiler_params=_params(1))(parts)


def _pack(vecs, rows=None):
    flat = jnp.concatenate([v.reshape(-1).astype(F32) for v in vecs])
    n = flat.shape[0]
    r = rows if rows is not None else -(-n // (8 * LANES)) * 8
    return jnp.pad(flat, (0, r * LANES - n)).reshape(r, LANES)


def _unpack(packed, shapes):
    flat = packed.reshape(-1)
    out, off = [], 0
    for s in shapes:
        n = 1
        for d in s:
            n *= d
        out.append(flat[off:off + n].reshape(s))
        off += n
    return out


def kernel(x, c, w_mod, b_mod, norm_gain, w_attn_in, b_forget, w_attn_out, w_pool, pool_scale, w_up, conv_w, conv_b, w_down, final_gain, loss_target, m_w_mod, m_b_mod, m_norm_gain, m_w_attn_in, m_b_forget, m_w_attn_out, m_w_pool, m_pool_scale, m_w_up, m_conv_w, m_conv_b, m_w_down, m_final_gain, v_w_mod, v_b_mod, v_norm_gain, v_w_attn_in, v_b_forget, v_w_attn_out, v_w_pool, v_pool_scale, v_w_up, v_conv_w, v_conv_b, v_w_down, v_final_gain):
    _, S, D = x.shape
    L = w_mod.shape[0]
    CM = w_mod.shape[2]
    NFX = b_forget.shape[1]
    NH = 2 * NFX
    NSB = NH - NFX
    HD = D // NH
    CI = w_attn_in.shape[2]
    CU = w_up.shape[2]
    F2 = NDEV * CU
    DFF = F2 // 2
    G = len(POOL_WINDOWS)
    CG = D // G
    T = _pick(S, (256, 128))
    me = _idx(_me())
    x0 = x[0]
    target = loss_target[0]

    def layer_shards(l):
        if l % 2 == 0:
            shards = [w_attn_in[l // 2], w_attn_out[l // 2], w_up[l], w_down[l]]
        else:
            shards = [w_pool[l // 2].reshape(G * (CG // NDEV), CG), w_up[l], w_down[l]]
        return [s.astype(BF16) for s in shards]

    gather, tok = _gather_start(layer_shards(0), me, [c], "gather0_start")

    small_shapes = [(1, D), norm_gain.shape, pool_scale.shape, conv_w.shape]
    small_all = _all_gather([_pack([c + tok[0, 0], norm_gain, pool_scale, conv_w])], "gather_small")[0]
    per_dev = [_unpack(small_all[j], small_shapes) for j in range(NDEV)]
    c_all = jnp.concatenate([p[0] for p in per_dev] + [jnp.zeros((16 - NDEV, D), F32)], axis=0)
    gain_f = jnp.concatenate([p[1] for p in per_dev], axis=2)
    pscale_f = jnp.concatenate([p[2] for p in per_dev], axis=1)
    convw_f = jnp.concatenate([p[3] for p in per_dev], axis=2)

    cond_all = _silu(c_all, "cond_silu")
    mod_part = _mod_project(cond_all, w_mod, "mod_project")
    mod_all = _all_gather([mod_part], "gather_mod")[0]
    mod = lax.dynamic_index_in_dim(mod_all, me, axis=2, keepdims=False)
    gather, tok = _gather_forward(gather, [mod_part], "gather0_forward")
    mod = mod.transpose(1, 0, 2).reshape(L, NDEV * CM) + (b_mod + tok[0, 0])
    mods = mod.reshape(L, 6, 1, D)
    full = _gather_wait(gather, [mods], "gather0_wait")

    saved = []
    xl = x0
    for l in range(L):
        i = l // 2
        sh1, sc1, g1, sh2, sc2, g2 = [mods[l, q] for q in range(6)]
        gn1, gn2 = gain_f[l, 0:1], gain_f[l, 1:2]
        st = {"x": xl}
        if l + 1 < L:
            gather, tok = _gather_start(layer_shards(l + 1), me, [full[0]], f"gather{l + 1}_start")
            sh1 = sh1 + tok[0, 0]
        wup_g = full[-2]
        wdown_f = full[-1].reshape(DFF, D)
        if l % 2 == 0:
            win = full[0].transpose(1, 0, 2).reshape(D, NDEV * CI)
            wqkv = win[:, :3 * D]
            wf = jnp.pad(win[:, 3 * D:], ((0, 0), (0, LANES - NFX)))
            wout = full[1].reshape(D, D)
            h1 = _norm_mod(xl, gn1, sc1, sh1, BF16, f"norm1_{l}")
            qkv = _matmul(h1, wqkv, mode="nn", name=f"qkv_{l}", out_dtype=BF16)
            flog = _matmul(h1, wf, mode="nn", name=f"flog_{l}", out_dtype=F32, tn=LANES)
            bfp = jnp.pad(b_forget[i], (0, LANES - NFX)).reshape(1, LANES)
            Fc = _forget_cumsum(flog, bfp, f"fcum_{l}")
            f8 = Fc[:, :NFX].T
            fcol, frow = f8[:, :, None], f8.reshape(NFX, S // T, 1, T)
            o_sb, tot = _sb_fwd(qkv, NH, NSB, HD, T, f"sb_fwd_{l}")
            o_fx, o_fx32, lse = _fox_fwd(qkv, fcol, frow, NH, NSB, HD, T, f"fox_fwd_{l}")
            o = jnp.concatenate([o_sb, o_fx], axis=1)
            x1, y1 = _matmul(o, wout, mode="nn", name=f"attn_out_{l}", out_dtype=F32, res=xl, gate=g1, y_dtype=BF16)
            st.update(h1=h1, qkv=qkv, flog=flog, bfp=bfp, fcol=fcol, frow=frow, tot=tot, lse=lse, o=o, o_fx=o_fx32,
                      wqkv=wqkv, wf=wf, wout=wout, y1=y1)
        else:
            wpool = full[0].reshape(NDEV, G, CG // NDEV, CG).transpose(1, 0, 2, 3).reshape(G, CG, CG)
            gp = g1 * pscale_f[i:i + 1]
            h1 = _norm_mod(xl, gn1, sc1, sh1, F32, f"norm1_{l}")
            dpool = _pool_diff(h1, f"pool_diff_{l}")
            x1, e1 = _pool_mm(dpool, wpool, xl, gp, f"pool_mm_{l}")
            st.update(dpool=dpool, wpool=wpool, gp=gp, y1=e1)
        h2 = _norm_mod(x1, gn2, sc2, sh2, BF16, f"norm2_{l}")
        u = _matmul(h2, wup_g, mode="nn", name=f"ffn_up_{l}", out_dtype=F32, b_blocked=True)
        cb = conv_b[l].reshape(1, F2)
        act = _convgate_fwd(u, convw_f[l], cb, f"convgate_{l}")
        if l + 1 < L:
            gather, tok = _gather_forward(gather, [act], f"gather{l + 1}_forward")
            g2 = g2 + tok[0, 0]
        x2, y2 = _matmul(act, wdown_f, mode="nn", name=f"ffn_down_{l}", out_dtype=F32, res=x1, gate=g2, y_dtype=BF16)
        if l + 1 < L:
            full = _gather_wait(gather, [x2], f"gather{l + 1}_wait")
        st.update(x1=x1, h2=h2, u=u, cb=cb, act=act, y2=y2, wup_g=wup_g, wdown_f=wdown_f,
                  mod=(sh1, sc1, g1, sh2, sc2, g2), gn=(gn1, gn2))
        saved.append(st)
        xl = x2

    dx, d_fgain, loss_tile = _loss_head(xl, target, final_gain.reshape(1, D), "loss_head")
    loss = lax.psum(loss_tile[0, 0], ("x", "y", "c"))

    dmod_rows = [None] * L
    d_gain = [None] * L
    d_convw = [None] * L
    d_convb = [None] * L
    d_pscale = [None] * (L // 2)
    d_bf = [None] * ((L + 1) // 2)
    big = {"w_up": None, "w_down": None, "w_attn_in": None, "w_attn_out": None, "w_pool": None}

    def update(key, w, m, v, bufs, layer, tag):
        big[key] = _adamw(w, m, v, bufs, layer, big[key], f"adamw_{tag}")

    def finish_exchange(pending, after):
        pl_, scat = pending
        pi = pl_ // 2
        bufs = _scatter_wait(scat, after, f"scatter{pl_}_wait")
        if pl_ % 2 == 0:
            update("w_attn_in", w_attn_in, m_w_attn_in, v_w_attn_in, bufs[0], pi, f"attn_in_{pl_}")
            update("w_attn_out", w_attn_out, m_w_attn_out, v_w_attn_out, bufs[1], pi, f"attn_out_{pl_}")
        else:
            wp3 = lambda a: a.reshape(a.shape[0], G * (CG // NDEV), CG)
            update("w_pool", wp3(w_pool), wp3(m_w_pool), wp3(v_w_pool), bufs[0], pi, f"pool_{pl_}")
        update("w_up", w_up, m_w_up, v_w_up, bufs[-2], pl_, f"up_{pl_}")
        update("w_down", w_down, m_w_down, v_w_down, bufs[-1], pl_, f"down_{pl_}")

    pending, tok = None, None
    for l in reversed(range(L)):
        i = l // 2
        st = saved[l]
        sh1, sc1, g1, sh2, sc2, g2 = st["mod"]
        if tok is not None:
            g2 = g2 + tok[0, 0]
        gn1, gn2 = st["gn"]
        dffn, dg2 = _gate_bwd(dx, st["y2"], g2, f"gate2_bwd_{l}")
        dact = _matmul(dffn, st["wdown_f"], mode="nt", name=f"ffn_down_dx_{l}", out_dtype=F32, tn=CU)
        dwdown = _matmul(st["act"], dffn, mode="tn", name=f"ffn_down_dw_{l}", out_dtype=BF16, tm=CU)
        du, dcw, dcb = _convgate_bwd(st["u"], dact, convw_f[l], st["cb"], f"convgate_bwd_{l}")
        dh2 = _matmul(du, st["wup_g"], mode="nt", name=f"ffn_up_dx_{l}", out_dtype=F32, a_split=True, b_blocked=True)
        dwup = _matmul(st["h2"], du, mode="tn", name=f"ffn_up_dw_{l}", out_dtype=BF16, tn=CU, b_split=True,
                       out_blocked=True)
        dx, dgn2, dsc2, dsh2 = _norm_mod_bwd(dh2, st["x1"], dx, gn2, sc2, f"norm2_bwd_{l}")
        d_convw[l] = jnp.concatenate([dcw[0], dcw[1]], axis=1)
        d_convb[l] = jnp.concatenate([dcb[0], dcb[1]], axis=1)
        if l % 2 == 0:
            dy1, dg1 = _gate_bwd(dx, st["y1"], g1, f"gate1_bwd_{l}")
            do = _matmul(dy1, st["wout"], mode="nt", name=f"attn_out_dx_{l}", out_dtype=BF16)
            dwout = _matmul(st["o"], dy1, mode="tn", name=f"attn_out_dw_{l}", out_dtype=BF16)
            dq_s, dk_s, dv_s = _sb_bwd(st["qkv"], do, st["tot"], NH, NSB, HD, T, f"sb_bwd_{l}")
            dq_f, dk_f, dv_f, dfk = _fox_bwd(st["qkv"], do, st["o_fx"], st["fcol"], st["frow"], st["lse"], NH, NSB, HD, T,
                                             f"fox_bwd_{l}")
            dqkv = jnp.concatenate([dq_s, dq_f, dk_s, dk_f, dv_s, dv_f], axis=1)
            dF = jnp.pad(dfk.reshape(NFX, S).T, ((0, 0), (0, LANES - NFX)))
            dflog, dbf = _forget_cumsum_bwd(dF, st["flog"], st["bfp"], f"fcum_bwd_{l}")
            dflog_b = dflog.astype(BF16)
            dh1 = _matmul(dqkv, st["wqkv"], mode="nt", name=f"qkv_dx_{l}", out_dtype=F32)
            dh1 = _matmul(dflog_b, st["wf"], mode="nt", name=f"flog_dx_{l}", out_dtype=F32, res=dh1,
                          gate=jnp.ones((1, D), F32))
            dwqkv = _matmul(st["h1"], dqkv, mode="tn", name=f"qkv_dw_{l}", out_dtype=BF16)
            dwf = _matmul(st["h1"], dflog_b, mode="tn", name=f"flog_dw_{l}", out_dtype=BF16, tn=LANES)
            dwin = jnp.concatenate([dwqkv, dwf[:, :NFX]], axis=1).reshape(D, NDEV, CI).transpose(1, 0, 2)
            d_bf[i] = dbf[0, :NFX]
            parts = [dwin, dwout.reshape(NDEV, D // NDEV, D), dwup, dwdown.reshape(NDEV, DFF // NDEV, D)]
        else:
            de, dgp = _gate_bwd(dx, st["y1"], st["gp"], f"gate1_bwd_{l}")
            dg1 = dgp * pscale_f[i:i + 1]
            d_pscale[i] = dgp * g1
            dd, dwp = _pool_mm_bwd(de, st["dpool"], st["wpool"], f"pool_mm_bwd_{l}")
            dh1 = _pool_diff_bwd(dd, f"pool_diff_bwd_{l}")
            dwp = dwp.reshape(G, NDEV, CG // NDEV, CG).transpose(1, 0, 2, 3).reshape(NDEV, G * (CG // NDEV), CG)
            parts = [dwp, dwup, dwdown.reshape(NDEV, DFF // NDEV, D)]
        dx, dgn1, dsc1, dsh1 = _norm_mod_bwd(dh1, st["x"], dx, gn1, sc1, f"norm1_bwd_{l}")
        dmod_rows[l] = jnp.concatenate([dsh1, dsc1, dg1, dsh2, dsc2, dg2], axis=1)
        d_gain[l] = jnp.concatenate([dgn1, dgn2], axis=0)
        scat, tok = _scatter_start(parts, me, [], f"scatter{l}_start")
        if pending is not None:
            finish_exchange(pending, [dx, tok])
        pending = (l, scat)

    grad_x = dx[None]
    d_fgain = d_fgain + tok[0, 0]

    small_grads = [jnp.stack(dmod_rows), jnp.stack(d_gain), jnp.stack(d_pscale), jnp.stack(d_convw),
                   jnp.stack(d_convb), jnp.stack(d_bf), d_fgain]
    sg_shapes = [(L, 6 * D), (L, 2, D), (L // 2, D), (L, 3, F2), (L, F2), ((L + 1) // 2, NFX), (D,)]
    sg_all = _all_gather([_pack(small_grads)], "gather_small_grads")[0]
    sg_sum = _unpack(_sum_parts(sg_all, "sum_small_grads"), sg_shapes)
    g_bmod, g_gain_f, g_pscale_f, g_convw_f, g_convb, g_bf, g_fgain = sg_sum
    shard = lambda a, n, axis: lax.dynamic_slice_in_dim(a, me * n, n, axis=axis)
    g_small = [g_bmod, shard(g_gain_f, D // NDEV, 2), shard(g_pscale_f, D // NDEV, 1), shard(g_convw_f, CU, 2), g_convb,
               g_bf, g_fgain]
    w_small = [b_mod, norm_gain, pool_scale, conv_w, conv_b, b_forget, final_gain]
    m_small = [m_b_mod, m_norm_gain, m_pool_scale, m_conv_w, m_conv_b, m_b_forget, m_final_gain]
    v_small = [v_b_mod, v_norm_gain, v_pool_scale, v_conv_w, v_conv_b, v_b_forget, v_final_gain]
    small_raw = _adamw(_pack(w_small)[None], _pack(m_small)[None], _pack(v_small)[None], _pack(g_small)[None], 0, None,
                       "adamw_small")
    small_out = [_unpack(a[0], [w.shape for w in w_small]) for a in small_raw]

    dmod_all = jnp.stack([_unpack(sg_all[j], sg_shapes[:1])[0] for j in range(NDEV)])
    dmod_mine = lax.dynamic_slice_in_dim(dmod_all.reshape(NDEV, L, NDEV, CM), me, 1, axis=2)
    cond_t = jnp.pad(cond_all[:NDEV].T, ((0, 0), (0, LANES - NDEV)))
    mod_out = _adamw_mod(w_mod, m_w_mod, v_w_mod, cond_t, dmod_mine, "adamw_mod")
    finish_exchange(pending, [mod_out[0], small_raw[0]])

    pool4 = lambda a: a.reshape(w_pool.shape)
    names = ["w_mod", "b_mod", "norm_gain", "w_attn_in", "b_forget", "w_attn_out", "w_pool", "pool_scale", "w_up", "conv_w",
             "conv_b", "w_down", "final_gain"]
    small_pos = {"b_mod": 0, "norm_gain": 1, "pool_scale": 2, "conv_w": 3, "conv_b": 4, "b_forget": 5, "final_gain": 6}
    outs = []
    for kind in range(4):
        for nm in names:
            if nm == "w_mod":
                outs.append(mod_out[kind])
            elif nm in small_pos:
                outs.append(small_out[kind][small_pos[nm]])
            elif nm == "w_pool":
                outs.append(pool4(big[nm][kind]))
            else:
                outs.append(big[nm][kind])
    return (loss, grad_x, *outs)
```

```python
import jax
import jax.numpy as jnp
from jax import lax
from jax.experimental import pallas as pl
from jax.experimental.pallas import tpu as pltpu

NDEV = 8
F32 = jnp.float32
BF16 = jnp.bfloat16
MESH = pl.DeviceIdType.MESH
VMEM_LIMIT_BYTES = 56 * 1024 * 1024
LANES = 128
POOL_WINDOWS = (2, 4, 8, 16)
EPS = 1e-6
ADAM_LR = 0.001
ADAM_B1 = 0.9
ADAM_B2 = 0.999
ADAM_EPS = 1e-08
ADAM_WD = 0.01
ADAM_STEP = 10
NEG_BIG = -1e30
SDS = jax.ShapeDtypeStruct
ANY = pl.BlockSpec(memory_space=pl.ANY)


def _pc(body, **kw):
    return pl.pallas_call(body, **kw)


def _params(n_axes):
    return pltpu.CompilerParams(dimension_semantics=("arbitrary",) * n_axes, vmem_limit_bytes=VMEM_LIMIT_BYTES)


def _pick(n, prefs):
    for p in prefs:
        if p <= n and n % p == 0:
            return p
    return n


def _idx(p):
    return 4 * p[0] + 2 * p[1] + p[2]


def _me():
    return lax.axis_index("x"), lax.axis_index("y"), lax.axis_index("c")


def _all_gather(arrs, name):
    n = len(arrs)

    def body(*refs):
        ins, outs = refs[:n], refs[n:2 * n]
        send_sems, recv_sems, local_sems = refs[2 * n:]
        x, y, c = _me()
        me, sib = (x, y, c), (x, y, 1 - c)
        chips = [(1 - x, y), (x, 1 - y), (1 - x, 1 - y)]

        def copy(t, k, block, to, src=None):
            dst = outs[t].at[_idx(block)]
            return pltpu.make_async_remote_copy(
                src_ref=dst if src is None else src, dst_ref=dst,
                send_sem=send_sems.at[7 * t + k], recv_sem=recv_sems.at[7 * t + k],
                device_id=to, device_id_type=MESH)

        mine = [pltpu.make_async_copy(ins[t], outs[t].at[_idx(me)], local_sems.at[t]) for t in range(n)]
        for cp in mine:
            cp.start()
        first = []
        for t in range(n):
            first.append(copy(t, 0, me, sib, src=ins[t]))
            for j, chip in enumerate(chips):
                first.append(copy(t, 1 + j, me, (*chip, c), src=ins[t]))
        for cp in first:
            cp.start()
        passed = []
        for t in range(n):
            for j, chip in enumerate(chips):
                copy(t, 1 + j, (*chip, c), me).wait_recv()
                cp = copy(t, 4 + j, (*chip, c), sib)
                cp.start()
                passed.append(cp)
        for t in range(n):
            copy(t, 0, sib, me).wait_recv()
            for j, chip in enumerate(chips):
                copy(t, 4 + j, (*chip, 1 - c), me).wait_recv()
        for cp in first + passed:
            cp.wait_send()
        for cp in mine:
            cp.wait()

    return _pc(
        body, name=name,
        out_shape=[SDS((NDEV,) + a.shape, a.dtype) for a in arrs],
        in_specs=[ANY] * n, out_specs=[ANY] * n,
        scratch_shapes=[pltpu.SemaphoreType.DMA((7 * n,)), pltpu.SemaphoreType.DMA((7 * n,)),
                        pltpu.SemaphoreType.DMA((n,))],
    )(*arrs)


HBM = pl.BlockSpec(memory_space=pltpu.HBM)
SEM = pl.BlockSpec(memory_space=pltpu.SEMAPHORE)
EFFECT = pltpu.SideEffectType.DATAFLOW_SIDE_EFFECTING
TOKEN = SDS((8, LANES), F32)


def _hbm(a):
    return pltpu.with_memory_space_constraint(a, pltpu.HBM)


def _landing(own, me):
    return lax.dynamic_update_index_in_dim(lax.empty((NDEV,) + own.shape, own.dtype), own, me, 0)


def _split_call(body, name, n_thru, thru, sems_in, after, sems_out):
    n_sem = len(sems_out)
    operands = [_hbm(a) for a in thru] + list(sems_in) + list(after)
    in_specs = [HBM] * n_thru + [SEM] * len(sems_in) + [ANY] * len(after)
    out_shape = [pltpu.SemaphoreType.DMA((k,)) for k in sems_out] + [pltpu.HBM(a.shape, a.dtype) for a in thru] + [TOKEN]
    out_specs = [SEM] * n_sem + [HBM] * n_thru + [pl.BlockSpec(memory_space=pltpu.VMEM)]
    outs = _pc(body, name=name, in_specs=in_specs, out_specs=out_specs, out_shape=out_shape,
               input_output_aliases={q: n_sem + q for q in range(n_thru)},
               compiler_params=pltpu.CompilerParams(has_side_effects=EFFECT))(*operands)
    return list(outs[:n_sem]), list(outs[n_sem:n_sem + n_thru]), outs[-1]


def _gather_start(shards, me, after, name):
    n = len(shards)
    lands = [_landing(s, me) for s in shards]

    def body(*refs):
        shard_refs, land_refs = refs[:n], refs[n:2 * n]
        send_sems, recv_sems = refs[2 * n + len(after)], refs[2 * n + len(after) + 1]
        x, y, c = _me()
        me_i = _idx((x, y, c))
        peers = [(x, y, 1 - c), (1 - x, y, c), (x, 1 - y, c), (1 - x, 1 - y, c)]
        for t in range(n):
            for k, p in enumerate(peers):
                pltpu.make_async_remote_copy(
                    src_ref=shard_refs[t], dst_ref=land_refs[t].at[me_i], send_sem=send_sems.at[4 * t + k],
                    recv_sem=recv_sems.at[4 * t + k], device_id=p, device_id_type=MESH).start()
        refs[-1][...] = jnp.zeros((8, LANES), F32)

    sems, thru, token = _split_call(body, name, 2 * n, list(shards) + lands, [], after, [4 * n, 4 * n])
    return dict(n=n, sems=sems, shards=thru[:n], lands=thru[n:]), token


def _gather_forward(st, after, name):
    n = st["n"]

    def body(*refs):
        shard_refs, land_refs = refs[:n], refs[n:2 * n]
        send1, recv1 = refs[2 * n], refs[2 * n + 1]
        send2, recv2 = refs[2 * n + 2 + len(after)], refs[2 * n + 3 + len(after)]
        x, y, c = _me()
        sib = (x, y, 1 - c)
        senders = [sib, (1 - x, y, c), (x, 1 - y, c), (1 - x, 1 - y, c)]
        for t in range(n):
            for k, p in enumerate(senders):
                cp = pltpu.make_async_remote_copy(
                    src_ref=shard_refs[t], dst_ref=land_refs[t].at[_idx(p)], send_sem=send1.at[4 * t + k],
                    recv_sem=recv1.at[4 * t + k], device_id=p, device_id_type=MESH)
                cp.wait_send()
                cp.wait_recv()
        for t in range(n):
            for j, p in enumerate(senders[1:]):
                slab = land_refs[t].at[_idx(p)]
                pltpu.make_async_remote_copy(
                    src_ref=slab, dst_ref=slab, send_sem=send2.at[3 * t + j], recv_sem=recv2.at[3 * t + j],
                    device_id=sib, device_id_type=MESH).start()
        refs[-1][...] = jnp.zeros((8, LANES), F32)

    sems, thru, token = _split_call(body, name, 2 * n, st["shards"] + st["lands"], st["sems"], after, [3 * n, 3 * n])
    return dict(n=n, sems=sems, shards=thru[:n], lands=thru[n:]), token


def _gather_wait(st, after, name):
    n = st["n"]

    def body(*refs):
        land_refs = refs[n:2 * n]
        send2, recv2 = refs[2 * n], refs[2 * n + 1]
        x, y, c = _me()
        sib = (x, y, 1 - c)
        for t in range(n):
            for j, chip in enumerate([(1 - x, y), (x, 1 - y), (1 - x, 1 - y)]):
                sent, got = land_refs[t].at[_idx((*chip, c))], land_refs[t].at[_idx((*chip, 1 - c))]
                cp = pltpu.make_async_remote_copy(
                    src_ref=sent, dst_ref=got, send_sem=send2.at[3 * t + j], recv_sem=recv2.at[3 * t + j],
                    device_id=sib, device_id_type=MESH)
                cp.wait_send()
                cp.wait_recv()
        refs[-1][...] = jnp.zeros((8, LANES), F32)

    _, thru, _ = _split_call(body, name, 2 * n, st["shards"] + st["lands"], st["sems"], after, [])
    return thru[n:]


def _scatter_start(parts, me, after, name):
    n = len(parts)
    lands = [_landing(lax.dynamic_index_in_dim(p, me, 0, keepdims=False), me) for p in parts]

    def body(*refs):
        part_refs, land_refs = refs[:n], refs[n:2 * n]
        send_sems, recv_sems = refs[2 * n + len(after)], refs[2 * n + len(after) + 1]
        x, y, c = _me()
        me_i = _idx((x, y, c))
        for t in range(n):
            for r in range(1, NDEV):
                p = (1 - x if r & 4 else x, 1 - y if r & 2 else y, 1 - c if r & 1 else c)
                pltpu.make_async_remote_copy(
                    src_ref=part_refs[t].at[_idx(p)], dst_ref=land_refs[t].at[me_i], send_sem=send_sems.at[7 * t + r - 1],
                    recv_sem=recv_sems.at[7 * t + r - 1], device_id=p, device_id_type=MESH).start()
        refs[-1][...] = jnp.zeros((8, LANES), F32)

    sems, thru, token = _split_call(body, name, 2 * n, list(parts) + lands, [], after, [7 * n, 7 * n])
    return dict(n=n, sems=sems, parts=thru[:n], lands=thru[n:]), token


def _scatter_wait(st, after, name):
    n = st["n"]

    def body(*refs):
        part_refs, land_refs = refs[:n], refs[n:2 * n]
        send_sems, recv_sems = refs[2 * n], refs[2 * n + 1]
        x, y, c = _me()
        for t in range(n):
            for r in range(1, NDEV):
                p = (1 - x if r & 4 else x, 1 - y if r & 2 else y, 1 - c if r & 1 else c)
                cp = pltpu.make_async_remote_copy(
                    src_ref=part_refs[t].at[_idx(p)], dst_ref=land_refs[t].at[_idx(p)], send_sem=send_sems.at[7 * t + r - 1],
                    recv_sem=recv_sems.at[7 * t + r - 1], device_id=p, device_id_type=MESH)
                cp.wait_send()
                cp.wait_recv()
        refs[-1][...] = jnp.zeros((8, LANES), F32)

    _, thru, _ = _split_call(body, name, 2 * n, st["parts"] + st["lands"], st["sems"], after, [])
    return thru[n:]


def _matmul(a, b, *, mode, name, out_dtype, tm=1024, tn=1024, tk=512, b_blocked=False, out_blocked=False,
            a_split=False, b_split=False, res=None, gate=None, y_dtype=None):
    if mode == "tn":
        K, M = (a.shape[0], a.shape[1]) if not a_split else (a.shape[1], 2 * a.shape[2])
    else:
        M, K = (a.shape[0], a.shape[1]) if not a_split else (a.shape[1], 2 * a.shape[2])
    if b_blocked:
        if mode == "nn":
            N, tn = b.shape[0] * b.shape[2], b.shape[2]
        else:
            N, tk = b.shape[1], b.shape[2]
    elif b_split:
        N = 2 * b.shape[2]
    else:
        N = b.shape[0] if mode == "nt" else b.shape[1]
    tm = _pick(M, (tm, 704, 512, 384, 256, 128))
    if not (b_blocked and mode == "nn"):
        tn = _pick(N, (tn, 1024, 768, 512, 384, 256, 128))
    if not (b_blocked and mode == "nt"):
        tk = _pick(K, (tk, 512, 384, 256, 128))
    nm, nn_, nk = M // tm, N // tn, K // tk

    if mode == "tn":
        a_spec = pl.BlockSpec((tk, tm), lambda i, j, k: (k, i))
        dims = (((0,), (0,)), ((), ()))
    elif a_split:
        per = a.shape[2] // tk
        a_spec = pl.BlockSpec((None, tm, tk), lambda i, j, k: (k // per, i, k % per))
    else:
        a_spec = pl.BlockSpec((tm, tk), lambda i, j, k: (i, k))
    if mode == "nn":
        dims = (((1,), (0,)), ((), ()))
        if b_blocked:
            b_spec = pl.BlockSpec((None, tk, tn), lambda i, j, k: (j, k, 0))
        else:
            b_spec = pl.BlockSpec((tk, tn), lambda i, j, k: (k, j))
    elif mode == "nt":
        dims = (((1,), (1,)), ((), ()))
        if b_blocked:
            b_spec = pl.BlockSpec((None, tn, tk), lambda i, j, k: (k, j, 0))
        else:
            b_spec = pl.BlockSpec((tn, tk), lambda i, j, k: (j, k))
    else:
        if b_split:
            per_b = b.shape[2] // tn
            b_spec = pl.BlockSpec((None, tk, tn), lambda i, j, k: (j // per_b, k, j % per_b))
        else:
            b_spec = pl.BlockSpec((tk, tn), lambda i, j, k: (k, j))
    if out_blocked:
        o_spec = pl.BlockSpec((None, tm, tn), lambda i, j, k: (j, i, 0))
        o_shape = SDS((nn_, M, tn), out_dtype)
    else:
        o_spec = pl.BlockSpec((tm, tn), lambda i, j, k: (i, j))
        o_shape = SDS((M, N), out_dtype)
    fused = res is not None
    in_specs, operands = [a_spec, b_spec], [a, b]
    out_specs, out_shapes = [o_spec], [o_shape]
    if fused:
        in_specs += [pl.BlockSpec((tm, tn), lambda i, j, k: (i, j)), pl.BlockSpec((1, tn), lambda i, j, k: (0, j))]
        operands += [res, gate]
        if y_dtype is not None:
            out_specs.append(pl.BlockSpec((tm, tn), lambda i, j, k: (i, j)))
            out_shapes.append(SDS((M, N), y_dtype))

    def body(*refs):
        a_ref, b_ref = refs[0], refs[1]
        acc_ref = refs[-1]
        k = pl.program_id(2)

        @pl.when(k == 0)
        def _():
            acc_ref[...] = jnp.zeros_like(acc_ref)

        acc_ref[...] += lax.dot_general(a_ref[...], b_ref[...], dims, preferred_element_type=F32)

        @pl.when(k == nk - 1)
        def _():
            acc = acc_ref[...]
            if fused:
                res_ref, gate_ref, o_ref = refs[2], refs[3], refs[4]
                o_ref[...] = (res_ref[...] + gate_ref[...] * acc).astype(o_ref.dtype)
                if y_dtype is not None:
                    refs[5][...] = acc.astype(y_dtype)
            else:
                refs[2][...] = acc.astype(refs[2].dtype)

    outs = _pc(body, name=name, grid=(nm, nn_, nk), in_specs=in_specs, out_specs=out_specs, out_shape=out_shapes,
               scratch_shapes=[pltpu.VMEM((tm, tn), F32)], compiler_params=_params(3))(*operands)
    return outs[0] if len(outs) == 1 else tuple(outs)


def _norm_mod(x, gain, sc, sh, out_dtype, name):
    S, D = x.shape
    tr = _pick(S, (256, 128))

    def body(x_ref, g_ref, sc_ref, sh_ref, o_ref):
        xv = x_ref[...]
        r = lax.rsqrt(jnp.mean(xv * xv, axis=-1, keepdims=True) + EPS)
        n = (xv * r) * g_ref[...]
        o_ref[...] = (n * (1.0 + sc_ref[...]) + sh_ref[...]).astype(o_ref.dtype)

    row = pl.BlockSpec((tr, D), lambda i: (i, 0))
    vec = pl.BlockSpec((1, D), lambda i: (0, 0))
    return _pc(body, name=name, grid=(S // tr,), in_specs=[row, vec, vec, vec], out_specs=row,
               out_shape=SDS((S, D), out_dtype), compiler_params=_params(1))(x, gain, sc, sh)


def _norm_mod_bwd(dh, x, dxres, gain, sc, name):
    S, D = x.shape
    tr = _pick(S, (256, 128))

    def body(dh_ref, x_ref, dxres_ref, g_ref, sc_ref, dx_ref, dgain_ref, dsc_ref, dsh_ref):
        @pl.when(pl.program_id(0) == 0)
        def _():
            dgain_ref[...] = jnp.zeros_like(dgain_ref)
            dsc_ref[...] = jnp.zeros_like(dsc_ref)
            dsh_ref[...] = jnp.zeros_like(dsh_ref)

        xv = x_ref[...]
        dh = dh_ref[...].astype(F32)
        r = lax.rsqrt(jnp.mean(xv * xv, axis=-1, keepdims=True) + EPS)
        nh = xv * r
        gn = g_ref[...]
        dn = dh * (1.0 + sc_ref[...])
        dgain_ref[...] += jnp.sum(dn * nh, axis=0, keepdims=True)
        dsc_ref[...] += jnp.sum(dh * (nh * gn), axis=0, keepdims=True)
        dsh_ref[...] += jnp.sum(dh, axis=0, keepdims=True)
        dnh = dn * gn
        dx = r * (dnh - nh * jnp.mean(dnh * nh, axis=-1, keepdims=True))
        dx_ref[...] = dxres_ref[...] + dx

    row = pl.BlockSpec((tr, D), lambda i: (i, 0))
    vec = pl.BlockSpec((1, D), lambda i: (0, 0))
    return _pc(body, name=name, grid=(S // tr,), in_specs=[row, row, row, vec, vec], out_specs=[row, vec, vec, vec],
               out_shape=[SDS((S, D), F32), SDS((1, D), F32), SDS((1, D), F32), SDS((1, D), F32)],
               compiler_params=_params(1))(dh, x, dxres, gain, sc)


def _gate_bwd(dx, y, gate, name):
    S, D = dx.shape
    tr = _pick(S, (256, 128))

    def body(dx_ref, y_ref, gate_ref, dy_ref, dgate_ref):
        @pl.when(pl.program_id(0) == 0)
        def _():
            dgate_ref[...] = jnp.zeros_like(dgate_ref)

        dxv = dx_ref[...]
        dgate_ref[...] += jnp.sum(dxv * y_ref[...].astype(F32), axis=0, keepdims=True)
        dy_ref[...] = (dxv * gate_ref[...]).astype(BF16)

    row = pl.BlockSpec((tr, D), lambda i: (i, 0))
    vec = pl.BlockSpec((1, D), lambda i: (0, 0))
    return _pc(body, name=name, grid=(S // tr,), in_specs=[row, row, vec], out_specs=[row, vec],
               out_shape=[SDS((S, D), BF16), SDS((1, D), F32)], compiler_params=_params(1))(dx, y, gate)


def _loss_head(x, target, fgain, name):
    S, D = x.shape
    tr = _pick(S, (256, 128))

    def body(x_ref, t_ref, fg_ref, dx_ref, dfg_ref, loss_ref):
        @pl.when(pl.program_id(0) == 0)
        def _():
            dfg_ref[...] = jnp.zeros_like(dfg_ref)
            loss_ref[...] = jnp.zeros_like(loss_ref)

        xv = x_ref[...]
        fg = fg_ref[...]
        r = lax.rsqrt(jnp.mean(xv * xv, axis=-1, keepdims=True) + EPS)
        nh = xv * r
        e = nh * fg - t_ref[...]
        loss_ref[...] += 0.5 * jnp.sum(jnp.mean(e * e, axis=-1, keepdims=True))
        dy = e * (1.0 / D)
        dfg_ref[...] += jnp.sum(dy * nh, axis=0, keepdims=True)
        dnh = dy * fg
        dx_ref[...] = r * (dnh - nh * jnp.mean(dnh * nh, axis=-1, keepdims=True))

    row = pl.BlockSpec((tr, D), lambda i: (i, 0))
    vec = pl.BlockSpec((1, D), lambda i: (0, 0))
    tile = pl.BlockSpec((8, LANES), lambda i: (0, 0))
    return _pc(body, name=name, grid=(S // tr,), in_specs=[row, row, vec], out_specs=[row, vec, tile],
               out_shape=[SDS((S, D), F32), SDS((1, D), F32), SDS((8, LANES), F32)],
               compiler_params=_params(1))(x, target, fgain)


def _shift_down(v, k, rows):
    return jnp.where(rows >= k, pltpu.roll(v, k, axis=0), 0.0)


def _shift_up(v, k, rows):
    n = v.shape[0]
    return jnp.where(rows < n - k, pltpu.roll(v, n - k, axis=0), 0.0)


def _conv(uv, w, b, rows):
    return ((b + _shift_down(uv, 2, rows) * w[0:1]) + _shift_down(uv, 1, rows) * w[1:2]) + uv * w[2:3]


def _convgate_fwd(u, cw, cb, name):
    S, F2 = u.shape
    DFF = F2 // 2
    tc = _pick(DFF, (256, 128))
    sub = min(tc, LANES)
    nj = DFF // tc

    def body(ua_ref, ug_ref, wa_ref, wg_ref, ba_ref, bg_ref, o_ref):
        rows = lax.broadcasted_iota(jnp.int32, (S, sub), 0)
        for q in range(tc // sub):
            sl = slice(q * sub, (q + 1) * sub)
            ya = _conv(ua_ref[:, sl], wa_ref[:, sl], ba_ref[:, sl], rows)
            yg = _conv(ug_ref[:, sl], wg_ref[:, sl], bg_ref[:, sl], rows)
            o_ref[:, sl] = (yg * jax.nn.sigmoid(yg) * ya).astype(BF16)

    col = lambda off: pl.BlockSpec((S, tc), lambda j: (0, j + off))
    w3 = lambda off: pl.BlockSpec((3, tc), lambda j: (0, j + off))
    b1 = lambda off: pl.BlockSpec((1, tc), lambda j: (0, j + off))
    return _pc(body, name=name, grid=(nj,), in_specs=[col(0), col(nj), w3(0), w3(nj), b1(0), b1(nj)],
               out_specs=col(0), out_shape=SDS((S, DFF), BF16), compiler_params=_params(1))(u, u, cw, cw, cb, cb)


def _convgate_bwd(u, dact, cw, cb, name):
    S, F2 = u.shape
    DFF = F2 // 2
    tc = _pick(DFF, (256, 128))
    sub = min(tc, LANES)
    nj = DFF // tc

    def body(ua_ref, ug_ref, da_ref, wa_ref, wg_ref, ba_ref, bg_ref, du_ref, dcw_ref, dcb_ref):
        rows = lax.broadcasted_iota(jnp.int32, (S, sub), 0)
        for q in range(tc // sub):
            sl = slice(q * sub, (q + 1) * sub)
            ua, ug = ua_ref[:, sl], ug_ref[:, sl]
            wa, wg = wa_ref[:, sl], wg_ref[:, sl]
            ya = _conv(ua, wa, ba_ref[:, sl], rows)
            yg = _conv(ug, wg, bg_ref[:, sl], rows)
            s = jax.nn.sigmoid(yg)
            da = da_ref[:, sl]
            dya = da * (yg * s)
            dyg = da * ya * (s * (1.0 + yg * (1.0 - s)))
            for h, (dy, uv, w) in enumerate(((dya, ua, wa), (dyg, ug, wg))):
                du = (dy * w[2:3] + _shift_up(dy, 1, rows) * w[1:2]) + _shift_up(dy, 2, rows) * w[0:1]
                du_ref[h, :, sl] = du.astype(BF16)
                dcw_ref[h, 0:1, sl] = jnp.sum(dy * _shift_down(uv, 2, rows), axis=0, keepdims=True)
                dcw_ref[h, 1:2, sl] = jnp.sum(dy * _shift_down(uv, 1, rows), axis=0, keepdims=True)
                dcw_ref[h, 2:3, sl] = jnp.sum(dy * uv, axis=0, keepdims=True)
                dcb_ref[h, :, sl] = jnp.sum(dy, axis=0, keepdims=True)

    col = lambda off: pl.BlockSpec((S, tc), lambda j: (0, j + off))
    w3 = lambda off: pl.BlockSpec((3, tc), lambda j: (0, j + off))
    b1 = lambda off: pl.BlockSpec((1, tc), lambda j: (0, j + off))
    return _pc(body, name=name, grid=(nj,),
               in_specs=[col(0), col(nj), col(0), w3(0), w3(nj), b1(0), b1(nj)],
               out_specs=[pl.BlockSpec((2, S, tc), lambda j: (0, 0, j)), pl.BlockSpec((2, 3, tc), lambda j: (0, 0, j)),
                          pl.BlockSpec((2, 1, tc), lambda j: (0, 0, j))],
               out_shape=[SDS((2, S, DFF), BF16), SDS((2, 3, DFF), F32), SDS((2, 1, DFF), F32)],
               compiler_params=_params(1))(u, u, dact, cw, cw, cb, cb)


def _pool_diff(h, name):
    S, D = h.shape
    G = len(POOL_WINDOWS)
    CG = D // G
    tc = min(CG, LANES)
    per = CG // tc

    def body(h_ref, d_ref):
        g = pl.program_id(0)
        rows = lax.broadcasted_iota(jnp.int32, (S, tc), 0)
        for gi, w in enumerate(POOL_WINDOWS):
            @pl.when(g == gi)
            def _(w=w):
                hv = h_ref[...]
                s, k = hv, 1
                while k < w:
                    s = s + _shift_down(s, k, rows)
                    k *= 2
                count = jnp.minimum(rows + 1, w).astype(F32)
                d_ref[...] = (s / count - hv).astype(BF16)

    spec = pl.BlockSpec((S, tc), lambda g, j: (0, g * per + j))
    return _pc(body, name=name, grid=(G, per), in_specs=[spec], out_specs=spec, out_shape=SDS((S, D), BF16),
               compiler_params=_params(2))(h)


def _pool_diff_bwd(dd, name):
    S, D = dd.shape
    G = len(POOL_WINDOWS)
    CG = D // G
    tc = min(CG, LANES)
    per = CG // tc

    def body(dd_ref, o_ref):
        g = pl.program_id(0)
        rows = lax.broadcasted_iota(jnp.int32, (S, tc), 0)
        for gi, w in enumerate(POOL_WINDOWS):
            @pl.when(g == gi)
            def _(w=w):
                dv = dd_ref[...]
                count = jnp.minimum(rows + 1, w).astype(F32)
                s, k = dv / count, 1
                while k < w:
                    s = s + _shift_up(s, k, rows)
                    k *= 2
                o_ref[...] = s - dv

    spec = pl.BlockSpec((S, tc), lambda g, j: (0, g * per + j))
    return _pc(body, name=name, grid=(G, per), in_specs=[spec], out_specs=spec, out_shape=SDS((S, D), F32),
               compiler_params=_params(2))(dd)


def _pool_mm(d, w, res, gate, name):
    S, D = d.shape
    G, CG, _ = w.shape
    tm = _pick(S, (512, 256, 128))

    def body(d_ref, w_ref, res_ref, gate_ref, o_ref, e_ref):
        acc = jnp.dot(d_ref[...], w_ref[...], preferred_element_type=F32)
        o_ref[...] = res_ref[...] + gate_ref[...] * acc
        e_ref[...] = acc.astype(BF16)

    blk = pl.BlockSpec((tm, CG), lambda g, i: (i, g))
    return _pc(body, name=name, grid=(G, S // tm),
               in_specs=[blk, pl.BlockSpec((None, CG, CG), lambda g, i: (g, 0, 0)), blk,
                         pl.BlockSpec((1, CG), lambda g, i: (0, g))],
               out_specs=[blk, blk], out_shape=[SDS((S, D), F32), SDS((S, D), BF16)],
               compiler_params=_params(2))(d, w, res, gate)


def _pool_mm_bwd(de, d, w, name):
    S, D = de.shape
    G, CG, _ = w.shape
    tm = _pick(S, (512, 256, 128))
    ns = S // tm

    def body(de_ref, d_ref, w_ref, dd_ref, dw_ref, acc_ref):
        i = pl.program_id(1)

        @pl.when(i == 0)
        def _():
            acc_ref[...] = jnp.zeros_like(acc_ref)

        dev = de_ref[...]
        dd_ref[...] = lax.dot_general(dev, w_ref[...], (((1,), (1,)), ((), ())), preferred_element_type=F32)
        acc_ref[...] += lax.dot_general(d_ref[...], dev, (((0,), (0,)), ((), ())), preferred_element_type=F32)

        @pl.when(i == ns - 1)
        def _():
            dw_ref[...] = acc_ref[...].astype(BF16)

    blk = pl.BlockSpec((tm, CG), lambda g, i: (i, g))
    wsp = pl.BlockSpec((None, CG, CG), lambda g, i: (g, 0, 0))
    return _pc(body, name=name, grid=(G, ns), in_specs=[blk, blk, wsp], out_specs=[blk, wsp],
               out_shape=[SDS((S, D), F32), SDS((G, CG, CG), BF16)], scratch_shapes=[pltpu.VMEM((CG, CG), F32)],
               compiler_params=_params(2))(de, d, w)


def _log_sigmoid(z):
    return jnp.minimum(z, 0.0) - jnp.log(1.0 + jnp.exp(-jnp.abs(z)))


def _dot2(a, tri):
    hi = a.astype(BF16)
    lo = (a - hi.astype(F32)).astype(BF16)
    return jnp.dot(hi, tri, preferred_element_type=F32) + jnp.dot(lo, tri, preferred_element_type=F32)


_NT = (((1,), (1,)), ((), ()))
_TN = (((0,), (0,)), ((), ()))


def _forget_cumsum(flog, bf, name):
    S, W = flog.shape
    tb = _pick(S, (128,))

    def body(f_ref, b_ref, o_ref):
        r = lax.broadcasted_iota(jnp.int32, (tb, tb), 0)
        c = lax.broadcasted_iota(jnp.int32, (tb, tb), 1)
        tri = (c <= r).astype(F32)
        carry = jnp.zeros((1, W), F32)
        for q in range(S // tb):
            ls = _log_sigmoid(f_ref[q * tb:(q + 1) * tb, :] + b_ref[...])
            o_ref[q * tb:(q + 1) * tb, :] = carry + jnp.dot(tri, ls, preferred_element_type=F32,
                                                            precision=lax.Precision.HIGHEST)
            carry = carry + jnp.sum(ls, axis=0, keepdims=True)

    return _pc(body, name=name, out_shape=SDS((S, W), F32))(flog, bf)


def _forget_cumsum_bwd(dF, flog, bf, name):
    S, W = flog.shape
    tb = _pick(S, (128,))

    def body(d_ref, f_ref, b_ref, o_ref, db_ref):
        r = lax.broadcasted_iota(jnp.int32, (tb, tb), 0)
        c = lax.broadcasted_iota(jnp.int32, (tb, tb), 1)
        tri = (c >= r).astype(F32)
        carry = jnp.zeros((1, W), F32)
        db = jnp.zeros((1, W), F32)
        for q in reversed(range(S // tb)):
            dv = d_ref[q * tb:(q + 1) * tb, :]
            dls = carry + jnp.dot(tri, dv, preferred_element_type=F32, precision=lax.Precision.HIGHEST)
            carry = carry + jnp.sum(dv, axis=0, keepdims=True)
            dfl = dls * jax.nn.sigmoid(-(f_ref[q * tb:(q + 1) * tb, :] + b_ref[...]))
            o_ref[q * tb:(q + 1) * tb, :] = dfl
            db = db + jnp.sum(dfl, axis=0, keepdims=True)
        db_ref[...] = db

    return _pc(body, name=name, out_shape=[SDS((S, W), F32), SDS((1, W), F32)])(dF, flog, bf)


def _sb_fwd(qkv, NH, NSB, HD, T, name):
    S = qkv.shape[0]
    nq = S // T
    scale = HD ** -0.5

    def body(q_ref, k_ref, v_ref, o_ref, tot_ref):
        i = pl.program_id(1)
        q = q_ref[...]
        row = lax.broadcasted_iota(jnp.int32, (T, T), 0)
        col = lax.broadcasted_iota(jnp.int32, (T, T), 1)
        upper = (row > col).astype(BF16)

        def blk(kb, carry, acc, diag):
            sl = pl.ds(pl.multiple_of(kb * T, T), T)
            k, v = k_ref[sl, :], v_ref[sl, :]
            z = lax.dot_general(q, k, _NT, preferred_element_type=F32) * scale
            ls = _log_sigmoid(z)
            lr = ls - z
            if diag:
                lr = jnp.where(col < row, lr, 0.0)
            rest = _dot2(lr, upper) + carry
            w = jnp.exp(ls + rest)
            if diag:
                w = jnp.where(col < row, w, 0.0)
            acc = acc + jnp.dot(w.astype(BF16), v, preferred_element_type=F32)
            return carry + jnp.sum(lr, axis=1, keepdims=True), acc

        carry, acc = blk(i, jnp.zeros((T, 1), F32), jnp.zeros((T, HD), F32), True)
        carry, acc = lax.fori_loop(0, i, lambda jj, ca: blk(i - 1 - jj, ca[0], ca[1], False), (carry, acc))
        o_ref[...] = acc.astype(BF16)
        tot_ref[...] = carry

    return _pc(body, name=name, grid=(NSB, nq),
               in_specs=[pl.BlockSpec((T, HD), lambda h, i: (i, h)),
                         pl.BlockSpec((S, HD), lambda h, i: (0, NH + h)),
                         pl.BlockSpec((S, HD), lambda h, i: (0, 2 * NH + h))],
               out_specs=[pl.BlockSpec((T, HD), lambda h, i: (i, h)), pl.BlockSpec((None, T, 1), lambda h, i: (h, i, 0))],
               out_shape=[SDS((S, NSB * HD), BF16), SDS((NSB, S, 1), F32)],
               compiler_params=_params(2))(qkv, qkv, qkv)


def _sb_bwd(qkv, do, tot, NH, NSB, HD, T, name):
    S = qkv.shape[0]
    nq = S // T
    scale = HD ** -0.5

    def body(q_ref, k_ref, v_ref, do_ref, tot_ref, dq_ref, dk_ref, dv_ref, dk_acc, dv_acc):
        i = pl.program_id(1)

        @pl.when(i == 0)
        def _():
            dk_acc[...] = jnp.zeros_like(dk_acc)
            dv_acc[...] = jnp.zeros_like(dv_acc)

        q, do_, tot_ = q_ref[...], do_ref[...], tot_ref[...]
        row = lax.broadcasted_iota(jnp.int32, (T, T), 0)
        col = lax.broadcasted_iota(jnp.int32, (T, T), 1)
        incl = (row <= col).astype(BF16)
        strict = (row < col).astype(BF16)

        def blk(kb, cl, cg, dq, diag):
            sl = pl.ds(pl.multiple_of(kb * T, T), T)
            k, v = k_ref[sl, :], v_ref[sl, :]
            z = lax.dot_general(q, k, _NT, preferred_element_type=F32) * scale
            ls = _log_sigmoid(z)
            lr = ls - z
            if diag:
                lr = jnp.where(col < row, lr, 0.0)
            rest = tot_ - (cl + _dot2(lr, incl))
            w = jnp.exp(ls + rest)
            if diag:
                w = jnp.where(col < row, w, 0.0)
            g = lax.dot_general(do_, v, _NT, preferred_element_type=F32) * w
            dv_acc[sl, :] += lax.dot_general(w.astype(BF16), do_, _TN, preferred_element_type=F32)
            dlr = cg + _dot2(g, strict)
            dz = g * jnp.exp(lr) - dlr * jnp.exp(ls)
            if diag:
                dz = jnp.where(col < row, dz, 0.0)
            dzb = (dz * scale).astype(BF16)
            dq = dq + jnp.dot(dzb, k, preferred_element_type=F32)
            dk_acc[sl, :] += lax.dot_general(dzb, q, _TN, preferred_element_type=F32)
            return cl + jnp.sum(lr, axis=1, keepdims=True), cg + jnp.sum(g, axis=1, keepdims=True), dq

        zero = jnp.zeros((T, 1), F32)
        cl, cg, dq = lax.fori_loop(0, i, lambda kb, ca: blk(kb, ca[0], ca[1], ca[2], False),
                                   (zero, zero, jnp.zeros((T, HD), F32)))
        _, _, dq = blk(i, cl, cg, dq, True)
        dq_ref[...] = dq.astype(BF16)

        @pl.when(i == nq - 1)
        def _():
            dk_ref[...] = dk_acc[...].astype(BF16)
            dv_ref[...] = dv_acc[...].astype(BF16)

    qblk = pl.BlockSpec((T, HD), lambda h, i: (i, h))
    full = pl.BlockSpec((S, HD), lambda h, i: (0, h))
    return _pc(body, name=name, grid=(NSB, nq),
               in_specs=[qblk, pl.BlockSpec((S, HD), lambda h, i: (0, NH + h)),
                         pl.BlockSpec((S, HD), lambda h, i: (0, 2 * NH + h)), qblk,
                         pl.BlockSpec((None, T, 1), lambda h, i: (h, i, 0))],
               out_specs=[qblk, full, full],
               out_shape=[SDS((S, NSB * HD), BF16)] * 3,
               scratch_shapes=[pltpu.VMEM((S, HD), F32), pltpu.VMEM((S, HD), F32)],
               compiler_params=_params(2))(qkv, qkv, qkv, do, tot)


def _fox_fwd(qkv, fcol, frow, NH, NSB, HD, T, name):
    S = qkv.shape[0]
    NFX = NH - NSB
    nq = S // T
    scale = HD ** -0.5

    def body(q_ref, k_ref, v_ref, fq_ref, fk_ref, o_ref, o32_ref, lse_ref):
        i = pl.program_id(1)
        q, fq = q_ref[...], fq_ref[...]
        row = lax.broadcasted_iota(jnp.int32, (T, T), 0)
        col = lax.broadcasted_iota(jnp.int32, (T, T), 1)

        def blk(kb, m, l, acc, rem, diag):
            sl = pl.ds(pl.multiple_of(kb * T, T), T)
            k, v = k_ref[sl, :], v_ref[sl, :]
            s = lax.dot_general(q, k, _NT, preferred_element_type=F32) * scale + (fq - fk_ref[kb])
            if diag:
                s = jnp.where(col <= row, s, NEG_BIG)
            m_new = jnp.maximum(m, jnp.max(s, axis=1, keepdims=True))
            p = jnp.exp(s - m_new)
            alpha = jnp.exp(m - m_new)
            l = alpha * l + jnp.sum(p, axis=1, keepdims=True)
            hi = p.astype(BF16)
            lo = (p - hi.astype(F32)).astype(BF16)
            acc = alpha * acc + jnp.dot(hi, v, preferred_element_type=F32)
            rem = alpha * rem + jnp.dot(lo, v, preferred_element_type=F32)
            return m_new, l, acc, rem

        zero = jnp.zeros((T, HD), F32)
        m, l, acc, rem = blk(i, jnp.full((T, 1), NEG_BIG, F32), jnp.zeros((T, 1), F32), zero, zero, True)
        m, l, acc, rem = lax.fori_loop(0, i, lambda kb, ca: blk(kb, ca[0], ca[1], ca[2], ca[3], False), (m, l, acc, rem))
        o_ref[...] = (acc / l).astype(BF16)
        o32_ref[...] = (acc + rem) / l
        lse_ref[...] = m + jnp.log(l)

    vec = pl.BlockSpec((None, T, 1), lambda h, i: (h, i, 0))
    return _pc(body, name=name, grid=(NFX, nq),
               in_specs=[pl.BlockSpec((T, HD), lambda h, i: (i, NSB + h)),
                         pl.BlockSpec((S, HD), lambda h, i: (0, NH + NSB + h)),
                         pl.BlockSpec((S, HD), lambda h, i: (0, 2 * NH + NSB + h)),
                         vec, pl.BlockSpec((None, nq, 1, T), lambda h, i: (h, 0, 0, 0))],
               out_specs=[pl.BlockSpec((T, HD), lambda h, i: (i, h)), pl.BlockSpec((T, HD), lambda h, i: (i, h)), vec],
               out_shape=[SDS((S, NFX * HD), BF16), SDS((S, NFX * HD), F32), SDS((NFX, S, 1), F32)],
               compiler_params=_params(2))(qkv, qkv, qkv, fcol, frow)


def _fox_bwd(qkv, do, o, fcol, frow, lse, NH, NSB, HD, T, name):
    S = qkv.shape[0]
    NFX = NH - NSB
    nq = S // T
    scale = HD ** -0.5

    def body(q_ref, k_ref, v_ref, do_ref, o_ref, fq_ref, fk_ref, lse_ref, dq_ref, dk_ref, dv_ref, dfk_ref,
             dk_acc, dv_acc, dfk_acc):
        i = pl.program_id(1)

        @pl.when(i == 0)
        def _():
            dk_acc[...] = jnp.zeros_like(dk_acc)
            dv_acc[...] = jnp.zeros_like(dv_acc)
            dfk_acc[...] = jnp.zeros_like(dfk_acc)

        q, do_, fq, lse_ = q_ref[...], do_ref[...], fq_ref[...], lse_ref[...]
        delta = jnp.sum(do_.astype(F32) * o_ref[...], axis=1, keepdims=True)
        row = lax.broadcasted_iota(jnp.int32, (T, T), 0)
        col = lax.broadcasted_iota(jnp.int32, (T, T), 1)

        def blk(kb, dq, diag):
            sl = pl.ds(pl.multiple_of(kb * T, T), T)
            k, v = k_ref[sl, :], v_ref[sl, :]
            s = lax.dot_general(q, k, _NT, preferred_element_type=F32) * scale + (fq - fk_ref[kb])
            p = jnp.exp(s - lse_)
            if diag:
                p = jnp.where(col <= row, p, 0.0)
            ds = p * (lax.dot_general(do_, v, _NT, preferred_element_type=F32) - delta)
            dv_acc[sl, :] += lax.dot_general(p.astype(BF16), do_, _TN, preferred_element_type=F32)
            dsb = (ds * scale).astype(BF16)
            dk_acc[sl, :] += lax.dot_general(dsb, q, _TN, preferred_element_type=F32)
            dfk_acc[kb] -= jnp.sum(ds, axis=0, keepdims=True)
            return dq + jnp.dot(dsb, k, preferred_element_type=F32)

        dq = lax.fori_loop(0, i, lambda kb, dq: blk(kb, dq, False), jnp.zeros((T, HD), F32))
        dq = blk(i, dq, True)
        dq_ref[...] = dq.astype(BF16)

        @pl.when(i == nq - 1)
        def _():
            dk_ref[...] = dk_acc[...].astype(BF16)
            dv_ref[...] = dv_acc[...].astype(BF16)
            dfk_ref[...] = dfk_acc[...]

    vec = pl.BlockSpec((None, T, 1), lambda h, i: (h, i, 0))
    rowv = pl.BlockSpec((None, nq, 1, T), lambda h, i: (h, 0, 0, 0))
    oblk = pl.BlockSpec((T, HD), lambda h, i: (i, h))
    full = pl.BlockSpec((S, HD), lambda h, i: (0, h))
    return _pc(body, name=name, grid=(NFX, nq),
               in_specs=[pl.BlockSpec((T, HD), lambda h, i: (i, NSB + h)),
                         pl.BlockSpec((S, HD), lambda h, i: (0, NH + NSB + h)),
                         pl.BlockSpec((S, HD), lambda h, i: (0, 2 * NH + NSB + h)),
                         pl.BlockSpec((T, HD), lambda h, i: (i, NSB + h)), oblk, vec, rowv, vec],
               out_specs=[oblk, full, full, rowv],
               out_shape=[SDS((S, NFX * HD), BF16)] * 3 + [SDS((NFX, nq, 1, T), F32)],
               scratch_shapes=[pltpu.VMEM((S, HD), F32), pltpu.VMEM((S, HD), F32), pltpu.VMEM((nq, 1, T), F32)],
               compiler_params=_params(2))(qkv, qkv, qkv, do, o, fcol, frow, lse)


def _silu(c_all, name):
    def body(c_ref, o_ref):
        cv = c_ref[...]
        o_ref[...] = cv * jax.nn.sigmoid(cv)

    return _pc(body, name=name, out_shape=SDS(c_all.shape, F32))(c_all)


def _mod_project(cond, w_mod, name):
    L, D, C = w_mod.shape
    tk = _pick(D, (512, 256, 128))

    def body(c_ref, w_ref, o_ref):
        @pl.when(pl.program_id(1) == 0)
        def _():
            o_ref[...] = jnp.zeros_like(o_ref)

        o_ref[...] += jnp.dot(c_ref[...].astype(BF16), w_ref[...].astype(BF16), preferred_element_type=F32)

    return _pc(body, name=name, grid=(L, D // tk),
               in_specs=[pl.BlockSpec((16, tk), lambda l, k: (0, k)), pl.BlockSpec((None, tk, C), lambda l, k: (l, k, 0))],
               out_specs=pl.BlockSpec((None, 16, C), lambda l, k: (l, 0, 0)),
               out_shape=SDS((L, 16, C), F32), compiler_params=_params(2))(cond, w_mod)


def _adam_math(w, g, m, v):
    m = ADAM_B1 * m + (1.0 - ADAM_B1) * g
    v = ADAM_B2 * v + (1.0 - ADAM_B2) * (g * g)
    m_hat = m / (1.0 - ADAM_B1 ** ADAM_STEP)
    v_hat = v / (1.0 - ADAM_B2 ** ADAM_STEP)
    delta = -ADAM_LR * (m_hat / (jnp.sqrt(v_hat) + ADAM_EPS) + ADAM_WD * w)
    return delta, m, v


def _adamw(w, m, v, parts, layer, prev, name):
    L, R, C = w.shape
    NP = parts.shape[0]
    tr = _pick(R, (128, 64, 88, 32, 16, 8))
    nprev = 0 if prev is None else 4

    def body(w_ref, m_ref, v_ref, p_ref, *rest):
        g_ref, d_ref, mo_ref, vo_ref = rest[nprev:]
        g = p_ref[0].astype(F32)
        for j in range(1, NP):
            g = g + p_ref[j].astype(F32)
        delta, mn, vn = _adam_math(w_ref[...], g, m_ref[...], v_ref[...])
        g_ref[...] = g
        d_ref[...] = delta
        mo_ref[...] = mn
        vo_ref[...] = vn

    blk = pl.BlockSpec((None, tr, C), lambda i: (layer, i, 0))
    in_specs = [blk, blk, blk, pl.BlockSpec((NP, tr, C), lambda i: (0, i, 0))]
    operands = [w, m, v, parts]
    aliases = {}
    if prev is not None:
        in_specs += [ANY] * 4
        operands += list(prev)
        aliases = {4 + q: q for q in range(4)}
    return _pc(body, name=name, grid=(R // tr,), in_specs=in_specs, out_specs=[blk] * 4,
               out_shape=[SDS(w.shape, F32)] * 4, input_output_aliases=aliases,
               compiler_params=_params(1))(*operands)


def _adamw_mod(w, m, v, cond_t, dmod, name):
    L, D, C = w.shape
    tr = _pick(D, (128, 64))

    def body(w_ref, m_ref, v_ref, ct_ref, dm_ref, g_ref, d_ref, mo_ref, vo_ref):
        ct = ct_ref[...]
        g = ct[:, 0:1] * dm_ref[0]
        for b in range(1, NDEV):
            g = g + ct[:, b:b + 1] * dm_ref[b]
        delta, mn, vn = _adam_math(w_ref[...], g, m_ref[...], v_ref[...])
        g_ref[...] = g
        d_ref[...] = delta
        mo_ref[...] = mn
        vo_ref[...] = vn

    blk = pl.BlockSpec((None, tr, C), lambda l, i: (l, i, 0))
    return _pc(body, name=name, grid=(L, D // tr),
               in_specs=[blk, blk, blk, pl.BlockSpec((tr, LANES), lambda l, i: (i, 0)),
                         pl.BlockSpec((NDEV, None, 1, C), lambda l, i: (0, l, 0, 0))],
               out_specs=[blk] * 4, out_shape=[SDS(w.shape, F32)] * 4, compiler_params=_params(2))(w, m, v, cond_t, dmod)


def _sum_parts(parts, name):
    NP, R, C = parts.shape
    tr = _pick(R, (256, 128, 64, 32, 16, 8))

    def body(p_ref, o_ref):
        g = p_ref[0]
        for j in range(1, NP):
            g = g + p_ref[j]
        o_ref[...] = g

    return _pc(body, name=name, grid=(R // tr,), in_specs=[pl.BlockSpec((NP, tr, C), lambda i: (0, i, 0))],
               out_specs=pl.BlockSpec((tr, C), lambda i: (i, 0)), out_shape=SDS((R, C), F32),
               compiler_params=_params(1))(parts)


def _pack(vecs, rows=None):
    flat = jnp.concatenate([v.reshape(-1).astype(F32) for v in vecs])
    n = flat.shape[0]
    r = rows if rows is not None else -(-n // (256 * LANES)) * 256
    return jnp.pad(flat, (0, r * LANES - n)).reshape(r, LANES)


def _unpack(packed, shapes):
    flat = packed.reshape(-1)
    out, off = [], 0
    for s in shapes:
        n = 1
        for d in s:
            n *= d
        out.append(flat[off:off + n].reshape(s))
        off += n
    return out


def kernel(x, c, w_mod, b_mod, norm_gain, w_attn_in, b_forget, w_attn_out, w_pool, pool_scale, w_up, conv_w, conv_b, w_down, final_gain, loss_target, m_w_mod, m_b_mod, m_norm_gain, m_w_attn_in, m_b_forget, m_w_attn_out, m_w_pool, m_pool_scale, m_w_up, m_conv_w, m_conv_b, m_w_down, m_final_gain, v_w_mod, v_b_mod, v_norm_gain, v_w_attn_in, v_b_forget, v_w_attn_out, v_w_pool, v_pool_scale, v_w_up, v_conv_w, v_conv_b, v_w_down, v_final_gain):
    _, S, D = x.shape
    L = w_mod.shape[0]
    CM = w_mod.shape[2]
    NFX = b_forget.shape[1]
    NH = 2 * NFX
    NSB = NH - NFX
    HD = D // NH
    CI = w_attn_in.shape[2]
    CU = w_up.shape[2]
    F2 = NDEV * CU
    DFF = F2 // 2
    G = len(POOL_WINDOWS)
    CG = D // G
    T = _pick(S, (256, 128))
    me = _idx(_me())
    x0 = x[0]
    target = loss_target[0]

    def layer_shards(l, group):
        if group == 1:
            shards = [w_up[l], w_down[l]]
        elif l % 2 == 0:
            shards = [w_attn_in[l // 2], w_attn_out[l // 2]]
        else:
            shards = [w_pool[l // 2].reshape(G * (CG // NDEV), CG)]
        return [s.astype(BF16) for s in shards]

    small_shapes = [(1, D), norm_gain.shape, pool_scale.shape, conv_w.shape]
    small_all = _all_gather([_pack([c, norm_gain, pool_scale, conv_w])], "gather_small")[0]
    per_dev = [_unpack(small_all[j], small_shapes) for j in range(NDEV)]
    c_all = jnp.concatenate([p[0] for p in per_dev] + [jnp.zeros((16 - NDEV, D), F32)], axis=0)
    gain_f = jnp.concatenate([p[1] for p in per_dev], axis=2)
    pscale_f = jnp.concatenate([p[2] for p in per_dev], axis=1)
    convw_f = jnp.concatenate([p[3] for p in per_dev], axis=2)

    cond_all = _silu(c_all, "cond_silu")
    mod_part = _mod_project(cond_all, w_mod, "mod_project")
    mod_all = _all_gather([mod_part], "gather_mod")[0]
    mod = lax.dynamic_index_in_dim(mod_all, me, axis=2, keepdims=False)
    mod = mod.transpose(1, 0, 2).reshape(L, NDEV * CM) + b_mod
    mods = mod.reshape(L, 6, 1, D)

    inflight = {}

    def gather_start(l, group, after):
        inflight[l, group], t = _gather_start(layer_shards(l, group), me, after, f"gather{l}_{group}_start")
        return t

    def gather_forward(l, group, after):
        inflight[l, group], t = _gather_forward(inflight[l, group], after, f"gather{l}_{group}_forward")
        return t

    def gather_wait(l, group, after):
        return _gather_wait(inflight.pop((l, group)), after, f"gather{l}_{group}_wait")

    def starts_at(l):
        if l % 2 == 0:
            return [(k, g) for k, g in [(l + 1, 0), (l + 1, 1), (l + 2, 0)] if k < L]
        return [(k, g) for k, g in [(l + 1, 1)] if k < L]

    tok = gather_start(0, 0, [mods])
    tok = gather_start(0, 1, [tok])
    tok = gather_forward(0, 0, [tok])
    mixer_w = gather_wait(0, 0, [tok])

    saved = []
    xl = x0
    for l in range(L):
        i = l // 2
        sh1, sc1, g1, sh2, sc2, g2 = [mods[l, q] for q in range(6)]
        gn1, gn2 = gain_f[l, 0:1], gain_f[l, 1:2]
        st = {"x": xl}
        after = [mixer_w[0]]
        for k, g in starts_at(l):
            after = [gather_start(k, g, after)]
        if starts_at(l):
            sh1 = sh1 + after[0][0, 0]
        if l % 2 == 0:
            win = mixer_w[0].transpose(1, 0, 2).reshape(D, NDEV * CI)
            wqkv = win[:, :3 * D]
            wf = jnp.pad(win[:, 3 * D:], ((0, 0), (0, LANES - NFX)))
            wout = mixer_w[1].reshape(D, D)
            h1 = _norm_mod(xl, gn1, sc1, sh1, BF16, f"norm1_{l}")
            qkv = _matmul(h1, wqkv, mode="nn", name=f"qkv_{l}", out_dtype=BF16)
            flog = _matmul(h1, wf, mode="nn", name=f"flog_{l}", out_dtype=F32, tn=LANES)
            bfp = jnp.pad(b_forget[i], (0, LANES - NFX)).reshape(1, LANES)
            Fc = _forget_cumsum(flog, bfp, f"fcum_{l}")
            f8 = Fc[:, :NFX].T
            fcol, frow = f8[:, :, None], f8.reshape(NFX, S // T, 1, T)
            o_sb, tot = _sb_fwd(qkv, NH, NSB, HD, T, f"sb_fwd_{l}")
            o_fx, o_fx32, lse = _fox_fwd(qkv, fcol, frow, NH, NSB, HD, T, f"fox_fwd_{l}")
            o = jnp.concatenate([o_sb, o_fx], axis=1)
            g1 = g1 + gather_forward(l, 1, [o])[0, 0]
            x1, y1 = _matmul(o, wout, mode="nn", name=f"attn_out_{l}", out_dtype=F32, res=xl, gate=g1, y_dtype=BF16)
            st.update(h1=h1, qkv=qkv, flog=flog, bfp=bfp, fcol=fcol, frow=frow, tot=tot, lse=lse, o=o, o_fx=o_fx32,
                      wqkv=wqkv, wf=wf, wout=wout, y1=y1)
        else:
            wpool = mixer_w[0].reshape(NDEV, G, CG // NDEV, CG).transpose(1, 0, 2, 3).reshape(G, CG, CG)
            h1 = _norm_mod(xl, gn1, sc1, sh1, F32, f"norm1_{l}")
            dpool = _pool_diff(h1, f"pool_diff_{l}")
            g1 = g1 + gather_forward(l, 1, [dpool])[0, 0]
            gp = g1 * pscale_f[i:i + 1]
            x1, e1 = _pool_mm(dpool, wpool, xl, gp, f"pool_mm_{l}")
            st.update(dpool=dpool, wpool=wpool, gp=gp, y1=e1)
        wup_g, wdown_g = gather_wait(l, 1, [x1])
        wdown_f = wdown_g.reshape(DFF, D)
        h2 = _norm_mod(x1, gn2, sc2, sh2, BF16, f"norm2_{l}")
        u = _matmul(h2, wup_g, mode="nn", name=f"ffn_up_{l}", out_dtype=F32, b_blocked=True)
        cb = conv_b[l].reshape(1, F2)
        act = _convgate_fwd(u, convw_f[l], cb, f"convgate_{l}")
        if l + 1 < L:
            g2 = g2 + gather_forward(l + 1, 0, [act])[0, 0]
        x2, y2 = _matmul(act, wdown_f, mode="nn", name=f"ffn_down_{l}", out_dtype=F32, res=x1, gate=g2, y_dtype=BF16)
        if l + 1 < L:
            mixer_w = gather_wait(l + 1, 0, [x2])
        st.update(x1=x1, h2=h2, u=u, cb=cb, act=act, y2=y2, wup_g=wup_g, wdown_f=wdown_f,
                  mod=(sh1, sc1, g1, sh2, sc2, g2), gn=(gn1, gn2))
        saved.append(st)
        xl = x2

    dx, d_fgain, loss_tile = _loss_head(xl, target, final_gain.reshape(1, D), "loss_head")
    loss = lax.psum(loss_tile[0, 0], ("x", "y", "c"))

    dmod_rows = [None] * L
    d_gain = [None] * L
    d_convw = [None] * L
    d_convb = [None] * L
    d_pscale = [None] * (L // 2)
    d_bf = [None] * ((L + 1) // 2)
    big = {"w_up": None, "w_down": None, "w_attn_in": None, "w_attn_out": None, "w_pool": None}

    def update(key, w, m, v, bufs, layer, tag):
        big[key] = _adamw(w, m, v, bufs, layer, big[key], f"adamw_{tag}")

    exchanges = []

    def exchange_start(l, group, parts, after):
        scat, t = _scatter_start(parts, me, after, f"scatter{l}_{group}_start")
        exchanges.append((l, group, scat))
        return t

    def exchanges_finish(after, first_layer):
        for entry in [e for e in exchanges if e[0] >= first_layer]:
            exchanges.remove(entry)
            pl_, group, scat = entry
            bufs = _scatter_wait(scat, after, f"scatter{pl_}_{group}_wait")
            if group == 1:
                update("w_up", w_up, m_w_up, v_w_up, bufs[0], pl_, f"up_{pl_}")
                update("w_down", w_down, m_w_down, v_w_down, bufs[1], pl_, f"down_{pl_}")
            elif pl_ % 2 == 0:
                update("w_attn_in", w_attn_in, m_w_attn_in, v_w_attn_in, bufs[0], pl_ // 2, f"attn_in_{pl_}")
                update("w_attn_out", w_attn_out, m_w_attn_out, v_w_attn_out, bufs[1], pl_ // 2, f"attn_out_{pl_}")
            else:
                wp3 = lambda a: a.reshape(a.shape[0], G * (CG // NDEV), CG)
                update("w_pool", wp3(w_pool), wp3(m_w_pool), wp3(v_w_pool), bufs[0], pl_ // 2, f"pool_{pl_}")

    tok = None
    for l in reversed(range(L)):
        i = l // 2
        st = saved[l]
        sh1, sc1, g1, sh2, sc2, g2 = st["mod"]
        if tok is not None:
            g2 = g2 + tok[0, 0]
        gn1, gn2 = st["gn"]
        dffn, dg2 = _gate_bwd(dx, st["y2"], g2, f"gate2_bwd_{l}")
        dact = _matmul(dffn, st["wdown_f"], mode="nt", name=f"ffn_down_dx_{l}", out_dtype=F32, tn=CU)
        dwdown = _matmul(st["act"], dffn, mode="tn", name=f"ffn_down_dw_{l}", out_dtype=BF16, tm=CU)
        du, dcw, dcb = _convgate_bwd(st["u"], dact, convw_f[l], st["cb"], f"convgate_bwd_{l}")
        dh2 = _matmul(du, st["wup_g"], mode="nt", name=f"ffn_up_dx_{l}", out_dtype=F32, a_split=True, b_blocked=True)
        dwup = _matmul(st["h2"], du, mode="tn", name=f"ffn_up_dw_{l}", out_dtype=BF16, tn=CU, b_split=True,
                       out_blocked=True)
        dx, dgn2, dsc2, dsh2 = _norm_mod_bwd(dh2, st["x1"], dx, gn2, sc2, f"norm2_bwd_{l}")
        d_convw[l] = jnp.concatenate([dcw[0], dcw[1]], axis=1)
        d_convb[l] = jnp.concatenate([dcb[0], dcb[1]], axis=1)
        tok = exchange_start(l, 1, [dwup, dwdown.reshape(NDEV, DFF // NDEV, D)], [])
        if l % 2 == 0:
            dy1, dg1 = _gate_bwd(dx, st["y1"], g1 + tok[0, 0], f"gate1_bwd_{l}")
            do = _matmul(dy1, st["wout"], mode="nt", name=f"attn_out_dx_{l}", out_dtype=BF16)
            dwout = _matmul(st["o"], dy1, mode="tn", name=f"attn_out_dw_{l}", out_dtype=BF16)
            dq_s, dk_s, dv_s = _sb_bwd(st["qkv"], do, st["tot"], NH, NSB, HD, T, f"sb_bwd_{l}")
            dq_f, dk_f, dv_f, dfk = _fox_bwd(st["qkv"], do, st["o_fx"], st["fcol"], st["frow"], st["lse"], NH, NSB, HD, T,
                                             f"fox_bwd_{l}")
            dqkv = jnp.concatenate([dq_s, dq_f, dk_s, dk_f, dv_s, dv_f], axis=1)
            dF = jnp.pad(dfk.reshape(NFX, S).T, ((0, 0), (0, LANES - NFX)))
            dflog, dbf = _forget_cumsum_bwd(dF, st["flog"], st["bfp"], f"fcum_bwd_{l}")
            dflog_b = dflog.astype(BF16)
            dh1 = _matmul(dqkv, st["wqkv"], mode="nt", name=f"qkv_dx_{l}", out_dtype=F32)
            dh1 = _matmul(dflog_b, st["wf"], mode="nt", name=f"flog_dx_{l}", out_dtype=F32, res=dh1,
                          gate=jnp.ones((1, D), F32))
            dwqkv = _matmul(st["h1"], dqkv, mode="tn", name=f"qkv_dw_{l}", out_dtype=BF16)
            dwf = _matmul(st["h1"], dflog_b, mode="tn", name=f"flog_dw_{l}", out_dtype=BF16, tn=LANES)
            dwin = jnp.concatenate([dwqkv, dwf[:, :NFX]], axis=1).reshape(D, NDEV, CI).transpose(1, 0, 2)
            d_bf[i] = dbf[0, :NFX]
            parts = [dwin, dwout.reshape(NDEV, D // NDEV, D)]
        else:
            de, dgp = _gate_bwd(dx, st["y1"], st["gp"] + tok[0, 0], f"gate1_bwd_{l}")
            dg1 = dgp * pscale_f[i:i + 1]
            d_pscale[i] = dgp * g1
            dd, dwp = _pool_mm_bwd(de, st["dpool"], st["wpool"], f"pool_mm_bwd_{l}")
            dh1 = _pool_diff_bwd(dd, f"pool_diff_bwd_{l}")
            parts = [dwp.reshape(G, NDEV, CG // NDEV, CG).transpose(1, 0, 2, 3).reshape(NDEV, G * (CG // NDEV), CG)]
        dx, dgn1, dsc1, dsh1 = _norm_mod_bwd(dh1, st["x"], dx, gn1, sc1, f"norm1_bwd_{l}")
        dmod_rows[l] = jnp.concatenate([dsh1, dsc1, dg1, dsh2, dsc2, dg2], axis=1)
        d_gain[l] = jnp.concatenate([dgn1, dgn2], axis=0)
        if l > 0:
            tok = exchange_start(l, 0, parts, [])
            exchanges_finish([dx, tok], l + 1)

    grad_x = dx[None]

    small_grads = [jnp.stack(dmod_rows), jnp.stack(d_gain), jnp.stack(d_pscale), jnp.stack(d_convw),
                   jnp.stack(d_convb), jnp.stack(d_bf), d_fgain]
    sg_shapes = [(L, 6 * D), (L, 2, D), (L // 2, D), (L, 3, F2), (L, F2), ((L + 1) // 2, NFX), (D,)]
    sg_all = _all_gather([_pack(small_grads)], "gather_small_grads")[0]
    tok = exchange_start(0, 0, parts, [sg_all])
    exchanges_finish([tok], 1)
    sg_all = sg_all + tok[0, 0]
    sg_sum = _unpack(_sum_parts(sg_all, "sum_small_grads"), sg_shapes)
    g_bmod, g_gain_f, g_pscale_f, g_convw_f, g_convb, g_bf, g_fgain = sg_sum
    shard = lambda a, n, axis: lax.dynamic_slice_in_dim(a, me * n, n, axis=axis)
    g_small = [g_bmod, shard(g_gain_f, D // NDEV, 2), shard(g_pscale_f, D // NDEV, 1), shard(g_convw_f, CU, 2), g_convb,
               g_bf, g_fgain]
    w_small = [b_mod, norm_gain, pool_scale, conv_w, conv_b, b_forget, final_gain]
    m_small = [m_b_mod, m_norm_gain, m_pool_scale, m_conv_w, m_conv_b, m_b_forget, m_final_gain]
    v_small = [v_b_mod, v_norm_gain, v_pool_scale, v_conv_w, v_conv_b, v_b_forget, v_final_gain]
    small_raw = _adamw(_pack(w_small)[None], _pack(m_small)[None], _pack(v_small)[None], _pack(g_small)[None], 0, None,
                       "adamw_small")
    small_out = [_unpack(a[0], [w.shape for w in w_small]) for a in small_raw]

    dmod_all = jnp.stack([_unpack(sg_all[j], sg_shapes[:1])[0] for j in range(NDEV)])
    dmod_mine = lax.dynamic_slice_in_dim(dmod_all.reshape(NDEV, L, NDEV, CM), me, 1, axis=2)
    cond_t = jnp.pad(cond_all[:NDEV].T, ((0, 0), (0, LANES - NDEV)))
    mod_out = _adamw_mod(w_mod, m_w_mod, v_w_mod, cond_t, dmod_mine, "adamw_mod")
    exchanges_finish([mod_out[0], small_raw[0]], 0)

    pool4 = lambda a: a.reshape(w_pool.shape)
    names = ["w_mod", "b_mod", "norm_gain", "w_attn_in", "b_forget", "w_attn_out", "w_pool", "pool_scale", "w_up", "conv_w",
             "conv_b", "w_down", "final_gain"]
    small_pos = {"b_mod": 0, "norm_gain": 1, "pool_scale": 2, "conv_w": 3, "conv_b": 4, "b_forget": 5, "final_gain": 6}
    outs = []
    for kind in range(4):
        for nm in names:
            if nm == "w_mod":
                outs.append(mod_out[kind])
            elif nm in small_pos:
                outs.append(small_out[kind][small_pos[nm]])
            elif nm == "w_pool":
                outs.append(pool4(big[nm][kind]))
            else:
                outs.append(big[nm][kind])
    return (loss, grad_x, *outs)
```

```python
import jax
import jax.numpy as jnp
from jax import lax
from jax.experimental import pallas as pl
from jax.experimental.pallas import tpu as pltpu

NDEV = 8
F32 = jnp.float32
BF16 = jnp.bfloat16
MESH = pl.DeviceIdType.MESH
VMEM_LIMIT_BYTES = 56 * 1024 * 1024
LANES = 128
POOL_WINDOWS = (2, 4, 8, 16)
EPS = 1e-6
ADAM_LR = 0.001
ADAM_B1 = 0.9
ADAM_B2 = 0.999
ADAM_EPS = 1e-08
ADAM_WD = 0.01
ADAM_STEP = 10
NEG_BIG = -1e30
SDS = jax.ShapeDtypeStruct
ANY = pl.BlockSpec(memory_space=pl.ANY)


def _pc(body, **kw):
    return pl.pallas_call(body, **kw)


def _params(n_axes):
    return pltpu.CompilerParams(dimension_semantics=("arbitrary",) * n_axes, vmem_limit_bytes=VMEM_LIMIT_BYTES)


def _pick(n, prefs):
    for p in prefs:
        if p <= n and n % p == 0:
            return p
    return n


def _idx(p):
    return 4 * p[0] + 2 * p[1] + p[2]


def _me():
    return lax.axis_index("x"), lax.axis_index("y"), lax.axis_index("c")


def _all_gather(arrs, name):
    n = len(arrs)

    def body(*refs):
        ins, outs = refs[:n], refs[n:2 * n]
        send_sems, recv_sems, local_sems = refs[2 * n:]
        x, y, c = _me()
        me, sib = (x, y, c), (x, y, 1 - c)
        chips = [(1 - x, y), (x, 1 - y), (1 - x, 1 - y)]

        def copy(t, k, block, to, src=None):
            dst = outs[t].at[_idx(block)]
            return pltpu.make_async_remote_copy(
                src_ref=dst if src is None else src, dst_ref=dst,
                send_sem=send_sems.at[7 * t + k], recv_sem=recv_sems.at[7 * t + k],
                device_id=to, device_id_type=MESH)

        mine = [pltpu.make_async_copy(ins[t], outs[t].at[_idx(me)], local_sems.at[t]) for t in range(n)]
        for cp in mine:
            cp.start()
        first = []
        for t in range(n):
            first.append(copy(t, 0, me, sib, src=ins[t]))
            for j, chip in enumerate(chips):
                first.append(copy(t, 1 + j, me, (*chip, c), src=ins[t]))
        for cp in first:
            cp.start()
        passed = []
        for t in range(n):
            for j, chip in enumerate(chips):
                copy(t, 1 + j, (*chip, c), me).wait_recv()
                cp = copy(t, 4 + j, (*chip, c), sib)
                cp.start()
                passed.append(cp)
        for t in range(n):
            copy(t, 0, sib, me).wait_recv()
            for j, chip in enumerate(chips):
                copy(t, 4 + j, (*chip, 1 - c), me).wait_recv()
        for cp in first + passed:
            cp.wait_send()
        for cp in mine:
            cp.wait()

    return _pc(
        body, name=name,
        out_shape=[SDS((NDEV,) + a.shape, a.dtype) for a in arrs],
        in_specs=[ANY] * n, out_specs=[ANY] * n,
        scratch_shapes=[pltpu.SemaphoreType.DMA((7 * n,)), pltpu.SemaphoreType.DMA((7 * n,)),
                        pltpu.SemaphoreType.DMA((n,))],
    )(*arrs)


HBM = pl.BlockSpec(memory_space=pltpu.HBM)
SEM = pl.BlockSpec(memory_space=pltpu.SEMAPHORE)
EFFECT = pltpu.SideEffectType.DATAFLOW_SIDE_EFFECTING
TOKEN = SDS((8, LANES), F32)


def _hbm(a):
    return pltpu.with_memory_space_constraint(a, pltpu.HBM)


def _landing(own, me):
    return lax.dynamic_update_index_in_dim(lax.empty((NDEV,) + own.shape, own.dtype), own, me, 0)


def _split_call(body, name, n_thru, thru, sems_in, after, sems_out):
    n_sem = len(sems_out)
    operands = [_hbm(a) for a in thru] + list(sems_in) + list(after)
    in_specs = [HBM] * n_thru + [SEM] * len(sems_in) + [ANY] * len(after)
    out_shape = [pltpu.SemaphoreType.DMA((k,)) for k in sems_out] + [pltpu.HBM(a.shape, a.dtype) for a in thru] + [TOKEN]
    out_specs = [SEM] * n_sem + [HBM] * n_thru + [pl.BlockSpec(memory_space=pltpu.VMEM)]
    outs = _pc(body, name=name, in_specs=in_specs, out_specs=out_specs, out_shape=out_shape,
               input_output_aliases={q: n_sem + q for q in range(n_thru)},
               compiler_params=pltpu.CompilerParams(has_side_effects=EFFECT))(*operands)
    return list(outs[:n_sem]), list(outs[n_sem:n_sem + n_thru]), outs[-1]


def _gather_start(shards, me, after, name):
    n = len(shards)
    lands = [_landing(s, me) for s in shards]

    def body(*refs):
        shard_refs, land_refs = refs[:n], refs[n:2 * n]
        send_sems, recv_sems = refs[2 * n + len(after)], refs[2 * n + len(after) + 1]
        x, y, c = _me()
        me_i = _idx((x, y, c))
        peers = [(x, y, 1 - c), (1 - x, y, c), (x, 1 - y, c), (1 - x, 1 - y, c)]
        for t in range(n):
            for k, p in enumerate(peers):
                pltpu.make_async_remote_copy(
                    src_ref=shard_refs[t], dst_ref=land_refs[t].at[me_i], send_sem=send_sems.at[4 * t + k],
                    recv_sem=recv_sems.at[4 * t + k], device_id=p, device_id_type=MESH).start()
        refs[-1][...] = jnp.zeros((8, LANES), F32)

    sems, thru, token = _split_call(body, name, 2 * n, list(shards) + lands, [], after, [4 * n, 4 * n])
    return dict(n=n, sems=sems, shards=thru[:n], lands=thru[n:]), token


def _gather_forward(st, after, name):
    n = st["n"]

    def body(*refs):
        shard_refs, land_refs = refs[:n], refs[n:2 * n]
        send1, recv1 = refs[2 * n], refs[2 * n + 1]
        send2, recv2 = refs[2 * n + 2 + len(after)], refs[2 * n + 3 + len(after)]
        x, y, c = _me()
        sib = (x, y, 1 - c)
        senders = [sib, (1 - x, y, c), (x, 1 - y, c), (1 - x, 1 - y, c)]
        for t in range(n):
            for k, p in enumerate(senders):
                cp = pltpu.make_async_remote_copy(
                    src_ref=shard_refs[t], dst_ref=land_refs[t].at[_idx(p)], send_sem=send1.at[4 * t + k],
                    recv_sem=recv1.at[4 * t + k], device_id=p, device_id_type=MESH)
                cp.wait_send()
                cp.wait_recv()
        for t in range(n):
            for j, p in enumerate(senders[1:]):
                slab = land_refs[t].at[_idx(p)]
                pltpu.make_async_remote_copy(
                    src_ref=slab, dst_ref=slab, send_sem=send2.at[3 * t + j], recv_sem=recv2.at[3 * t + j],
                    device_id=sib, device_id_type=MESH).start()
        refs[-1][...] = jnp.zeros((8, LANES), F32)

    sems, thru, token = _split_call(body, name, 2 * n, st["shards"] + st["lands"], st["sems"], after, [3 * n, 3 * n])
    return dict(n=n, sems=sems, shards=thru[:n], lands=thru[n:]), token


def _gather_wait(st, after, name):
    n = st["n"]

    def body(*refs):
        land_refs = refs[n:2 * n]
        send2, recv2 = refs[2 * n], refs[2 * n + 1]
        x, y, c = _me()
        sib = (x, y, 1 - c)
        for t in range(n):
            for j, chip in enumerate([(1 - x, y), (x, 1 - y), (1 - x, 1 - y)]):
                sent, got = land_refs[t].at[_idx((*chip, c))], land_refs[t].at[_idx((*chip, 1 - c))]
                cp = pltpu.make_async_remote_copy(
                    src_ref=sent, dst_ref=got, send_sem=send2.at[3 * t + j], recv_sem=recv2.at[3 * t + j],
                    device_id=sib, device_id_type=MESH)
                cp.wait_send()
                cp.wait_recv()
        refs[-1][...] = jnp.zeros((8, LANES), F32)

    _, thru, _ = _split_call(body, name, 2 * n, st["shards"] + st["lands"], st["sems"], after, [])
    return thru[n:]


def _scatter_start(parts, me, after, name):
    n = len(parts)
    lands = [_landing(lax.dynamic_index_in_dim(p, me, 0, keepdims=False), me) for p in parts]

    def body(*refs):
        part_refs, land_refs = refs[:n], refs[n:2 * n]
        send_sems, recv_sems = refs[2 * n + len(after)], refs[2 * n + len(after) + 1]
        x, y, c = _me()
        me_i = _idx((x, y, c))
        for t in range(n):
            for r in range(1, NDEV):
                p = (1 - x if r & 4 else x, 1 - y if r & 2 else y, 1 - c if r & 1 else c)
                pltpu.make_async_remote_copy(
                    src_ref=part_refs[t].at[_idx(p)], dst_ref=land_refs[t].at[me_i], send_sem=send_sems.at[7 * t + r - 1],
                    recv_sem=recv_sems.at[7 * t + r - 1], device_id=p, device_id_type=MESH).start()
        refs[-1][...] = jnp.zeros((8, LANES), F32)

    sems, thru, token = _split_call(body, name, 2 * n, list(parts) + lands, [], after, [7 * n, 7 * n])
    return dict(n=n, sems=sems, parts=thru[:n], lands=thru[n:]), token


def _scatter_wait(st, after, name):
    n = st["n"]

    def body(*refs):
        part_refs, land_refs = refs[:n], refs[n:2 * n]
        send_sems, recv_sems = refs[2 * n], refs[2 * n + 1]
        x, y, c = _me()
        for t in range(n):
            for r in range(1, NDEV):
                p = (1 - x if r & 4 else x, 1 - y if r & 2 else y, 1 - c if r & 1 else c)
                cp = pltpu.make_async_remote_copy(
                    src_ref=part_refs[t].at[_idx(p)], dst_ref=land_refs[t].at[_idx(p)], send_sem=send_sems.at[7 * t + r - 1],
                    recv_sem=recv_sems.at[7 * t + r - 1], device_id=p, device_id_type=MESH)
                cp.wait_send()
                cp.wait_recv()
        refs[-1][...] = jnp.zeros((8, LANES), F32)

    _, thru, _ = _split_call(body, name, 2 * n, st["parts"] + st["lands"], st["sems"], after, [])
    return thru[n:]


def _matmul(a, b, *, mode, name, out_dtype, tm=1024, tn=1024, tk=1024, b_blocked=False, out_blocked=False,
            a_split=False, b_split=False, res=None, gate=None, y_dtype=None):
    if mode == "tn":
        K, M = (a.shape[0], a.shape[1]) if not a_split else (a.shape[1], 2 * a.shape[2])
    else:
        M, K = (a.shape[0], a.shape[1]) if not a_split else (a.shape[1], 2 * a.shape[2])
    if b_blocked:
        if mode == "nn":
            N, tn = b.shape[0] * b.shape[2], b.shape[2]
        else:
            N, tk = b.shape[1], b.shape[2]
    elif b_split:
        N = 2 * b.shape[2]
    else:
        N = b.shape[0] if mode == "nt" else b.shape[1]
    tm = _pick(M, (tm, 704, 512, 384, 256, 128))
    if not (b_blocked and mode == "nn"):
        tn = _pick(N, (tn, 1024, 768, 512, 384, 256, 128))
    if not (b_blocked and mode == "nt"):
        tk = _pick(K, (tk, 1024, 512, 384, 256, 128))
    nm, nn_, nk = M // tm, N // tn, K // tk

    if mode == "tn":
        a_spec = pl.BlockSpec((tk, tm), lambda i, j, k: (k, i))
        dims = (((0,), (0,)), ((), ()))
    elif a_split:
        per = a.shape[2] // tk
        a_spec = pl.BlockSpec((None, tm, tk), lambda i, j, k: (k // per, i, k % per))
    else:
        a_spec = pl.BlockSpec((tm, tk), lambda i, j, k: (i, k))
    if mode == "nn":
        dims = (((1,), (0,)), ((), ()))
        if b_blocked:
            b_spec = pl.BlockSpec((None, tk, tn), lambda i, j, k: (j, k, 0))
        else:
            b_spec = pl.BlockSpec((tk, tn), lambda i, j, k: (k, j))
    elif mode == "nt":
        dims = (((1,), (1,)), ((), ()))
        if b_blocked:
            b_spec = pl.BlockSpec((None, tn, tk), lambda i, j, k: (k, j, 0))
        else:
            b_spec = pl.BlockSpec((tn, tk), lambda i, j, k: (j, k))
    else:
        if b_split:
            per_b = b.shape[2] // tn
            b_spec = pl.BlockSpec((None, tk, tn), lambda i, j, k: (j // per_b, k, j % per_b))
        else:
            b_spec = pl.BlockSpec((tk, tn), lambda i, j, k: (k, j))
    if out_blocked:
        o_spec = pl.BlockSpec((None, tm, tn), lambda i, j, k: (j, i, 0))
        o_shape = SDS((nn_, M, tn), out_dtype)
    else:
        o_spec = pl.BlockSpec((tm, tn), lambda i, j, k: (i, j))
        o_shape = SDS((M, N), out_dtype)
    fused = res is not None
    in_specs, operands = [a_spec, b_spec], [a, b]
    out_specs, out_shapes = [o_spec], [o_shape]
    if fused:
        in_specs += [pl.BlockSpec((tm, tn), lambda i, j, k: (i, j)), pl.BlockSpec((1, tn), lambda i, j, k: (0, j))]
        operands += [res, gate]
        if y_dtype is not None:
            out_specs.append(pl.BlockSpec((tm, tn), lambda i, j, k: (i, j)))
            out_shapes.append(SDS((M, N), y_dtype))

    def body(*refs):
        a_ref, b_ref = refs[0], refs[1]
        acc_ref = refs[-1]
        k = pl.program_id(2)

        def product():
            return lax.dot_general(a_ref[...], b_ref[...], dims, preferred_element_type=F32)

        def finish(acc):
            if fused:
                res_ref, gate_ref, o_ref = refs[2], refs[3], refs[4]
                o_ref[...] = (res_ref[...] + gate_ref[...] * acc).astype(o_ref.dtype)
                if y_dtype is not None:
                    refs[5][...] = acc.astype(y_dtype)
            else:
                refs[2][...] = acc.astype(refs[2].dtype)

        if nk == 1:
            finish(product())
        else:
            @pl.when(k == 0)
            def _():
                acc_ref[...] = product()

            @pl.when(jnp.logical_and(k > 0, k < nk - 1))
            def _():
                acc_ref[...] += product()

            @pl.when(k == nk - 1)
            def _():
                finish(acc_ref[...] + product())

    outs = _pc(body, name=name, grid=(nm, nn_, nk), in_specs=in_specs, out_specs=out_specs, out_shape=out_shapes,
               scratch_shapes=[pltpu.VMEM((tm, tn), F32)], compiler_params=_params(3))(*operands)
    return outs[0] if len(outs) == 1 else tuple(outs)


def _norm_mod(x, gain, sc, sh, out_dtype, name):
    S, D = x.shape
    tr = _pick(S, (256, 128))

    def body(x_ref, g_ref, sc_ref, sh_ref, o_ref):
        xv = x_ref[...]
        r = lax.rsqrt(jnp.mean(xv * xv, axis=-1, keepdims=True) + EPS)
        n = (xv * r) * g_ref[...]
        o_ref[...] = (n * (1.0 + sc_ref[...]) + sh_ref[...]).astype(o_ref.dtype)

    row = pl.BlockSpec((tr, D), lambda i: (i, 0))
    vec = pl.BlockSpec((1, D), lambda i: (0, 0))
    return _pc(body, name=name, grid=(S // tr,), in_specs=[row, vec, vec, vec], out_specs=row,
               out_shape=SDS((S, D), out_dtype), compiler_params=_params(1))(x, gain, sc, sh)


def _norm_mod_bwd(dh, x, dxres, gain, sc, name):
    S, D = x.shape
    tr = _pick(S, (256, 128))

    def body(dh_ref, x_ref, dxres_ref, g_ref, sc_ref, dx_ref, dgain_ref, dsc_ref, dsh_ref):
        @pl.when(pl.program_id(0) == 0)
        def _():
            dgain_ref[...] = jnp.zeros_like(dgain_ref)
            dsc_ref[...] = jnp.zeros_like(dsc_ref)
            dsh_ref[...] = jnp.zeros_like(dsh_ref)

        xv = x_ref[...]
        dh = dh_ref[...].astype(F32)
        r = lax.rsqrt(jnp.mean(xv * xv, axis=-1, keepdims=True) + EPS)
        nh = xv * r
        gn = g_ref[...]
        dn = dh * (1.0 + sc_ref[...])
        dgain_ref[...] += jnp.sum(dn * nh, axis=0, keepdims=True)
        dsc_ref[...] += jnp.sum(dh * (nh * gn), axis=0, keepdims=True)
        dsh_ref[...] += jnp.sum(dh, axis=0, keepdims=True)
        dnh = dn * gn
        dx = r * (dnh - nh * jnp.mean(dnh * nh, axis=-1, keepdims=True))
        dx_ref[...] = dxres_ref[...] + dx

    row = pl.BlockSpec((tr, D), lambda i: (i, 0))
    vec = pl.BlockSpec((1, D), lambda i: (0, 0))
    return _pc(body, name=name, grid=(S // tr,), in_specs=[row, row, row, vec, vec], out_specs=[row, vec, vec, vec],
               out_shape=[SDS((S, D), F32), SDS((1, D), F32), SDS((1, D), F32), SDS((1, D), F32)],
               compiler_params=_params(1))(dh, x, dxres, gain, sc)


def _gate_bwd(dx, y, gate, name):
    S, D = dx.shape
    tr = _pick(S, (256, 128))

    def body(dx_ref, y_ref, gate_ref, dy_ref, dgate_ref):
        @pl.when(pl.program_id(0) == 0)
        def _():
            dgate_ref[...] = jnp.zeros_like(dgate_ref)

        dxv = dx_ref[...]
        dgate_ref[...] += jnp.sum(dxv * y_ref[...].astype(F32), axis=0, keepdims=True)
        dy_ref[...] = (dxv * gate_ref[...]).astype(BF16)

    row = pl.BlockSpec((tr, D), lambda i: (i, 0))
    vec = pl.BlockSpec((1, D), lambda i: (0, 0))
    return _pc(body, name=name, grid=(S // tr,), in_specs=[row, row, vec], out_specs=[row, vec],
               out_shape=[SDS((S, D), BF16), SDS((1, D), F32)], compiler_params=_params(1))(dx, y, gate)


def _loss_head(x, target, fgain, name):
    S, D = x.shape
    tr = _pick(S, (256, 128))

    def body(x_ref, t_ref, fg_ref, dx_ref, dfg_ref, loss_ref):
        @pl.when(pl.program_id(0) == 0)
        def _():
            dfg_ref[...] = jnp.zeros_like(dfg_ref)
            loss_ref[...] = jnp.zeros_like(loss_ref)

        xv = x_ref[...]
        fg = fg_ref[...]
        r = lax.rsqrt(jnp.mean(xv * xv, axis=-1, keepdims=True) + EPS)
        nh = xv * r
        e = nh * fg - t_ref[...]
        loss_ref[...] += 0.5 * jnp.sum(jnp.mean(e * e, axis=-1, keepdims=True))
        dy = e * (1.0 / D)
        dfg_ref[...] += jnp.sum(dy * nh, axis=0, keepdims=True)
        dnh = dy * fg
        dx_ref[...] = r * (dnh - nh * jnp.mean(dnh * nh, axis=-1, keepdims=True))

    row = pl.BlockSpec((tr, D), lambda i: (i, 0))
    vec = pl.BlockSpec((1, D), lambda i: (0, 0))
    tile = pl.BlockSpec((8, LANES), lambda i: (0, 0))
    return _pc(body, name=name, grid=(S // tr,), in_specs=[row, row, vec], out_specs=[row, vec, tile],
               out_shape=[SDS((S, D), F32), SDS((1, D), F32), SDS((8, LANES), F32)],
               compiler_params=_params(1))(x, target, fgain)


def _shift_down(v, k, rows):
    return jnp.where(rows >= k, pltpu.roll(v, k, axis=0), 0.0)


def _shift_up(v, k, rows):
    n = v.shape[0]
    return jnp.where(rows < n - k, pltpu.roll(v, n - k, axis=0), 0.0)


def _conv(uv, w, b, rows):
    return ((b + _shift_down(uv, 2, rows) * w[0:1]) + _shift_down(uv, 1, rows) * w[1:2]) + uv * w[2:3]


def _convgate_fwd(u, cw, cb, name):
    S, F2 = u.shape
    DFF = F2 // 2
    tc = _pick(DFF, (256, 128))
    sub = min(tc, LANES)
    nj = DFF // tc

    def body(ua_ref, ug_ref, wa_ref, wg_ref, ba_ref, bg_ref, o_ref):
        rows = lax.broadcasted_iota(jnp.int32, (S, sub), 0)
        for q in range(tc // sub):
            sl = slice(q * sub, (q + 1) * sub)
            ya = _conv(ua_ref[:, sl], wa_ref[:, sl], ba_ref[:, sl], rows)
            yg = _conv(ug_ref[:, sl], wg_ref[:, sl], bg_ref[:, sl], rows)
            o_ref[:, sl] = (yg * jax.nn.sigmoid(yg) * ya).astype(BF16)

    col = lambda off: pl.BlockSpec((S, tc), lambda j: (0, j + off))
    w3 = lambda off: pl.BlockSpec((3, tc), lambda j: (0, j + off))
    b1 = lambda off: pl.BlockSpec((1, tc), lambda j: (0, j + off))
    return _pc(body, name=name, grid=(nj,), in_specs=[col(0), col(nj), w3(0), w3(nj), b1(0), b1(nj)],
               out_specs=col(0), out_shape=SDS((S, DFF), BF16), compiler_params=_params(1))(u, u, cw, cw, cb, cb)


def _convgate_bwd(u, dact, cw, cb, name):
    S, F2 = u.shape
    DFF = F2 // 2
    tc = _pick(DFF, (256, 128))
    sub = min(tc, LANES)
    nj = DFF // tc

    def body(ua_ref, ug_ref, da_ref, wa_ref, wg_ref, ba_ref, bg_ref, du_ref, dcw_ref, dcb_ref):
        rows = lax.broadcasted_iota(jnp.int32, (S, sub), 0)
        for q in range(tc // sub):
            sl = slice(q * sub, (q + 1) * sub)
            ua, ug = ua_ref[:, sl], ug_ref[:, sl]
            wa, wg = wa_ref[:, sl], wg_ref[:, sl]
            ya = _conv(ua, wa, ba_ref[:, sl], rows)
            yg = _conv(ug, wg, bg_ref[:, sl], rows)
            s = jax.nn.sigmoid(yg)
            da = da_ref[:, sl]
            dya = da * (yg * s)
            dyg = da * ya * (s * (1.0 + yg * (1.0 - s)))
            for h, (dy, uv, w) in enumerate(((dya, ua, wa), (dyg, ug, wg))):
                du = (dy * w[2:3] + _shift_up(dy, 1, rows) * w[1:2]) + _shift_up(dy, 2, rows) * w[0:1]
                du_ref[h, :, sl] = du.astype(BF16)
                dcw_ref[h, 0:1, sl] = jnp.sum(dy * _shift_down(uv, 2, rows), axis=0, keepdims=True)
                dcw_ref[h, 1:2, sl] = jnp.sum(dy * _shift_down(uv, 1, rows), axis=0, keepdims=True)
                dcw_ref[h, 2:3, sl] = jnp.sum(dy * uv, axis=0, keepdims=True)
                dcb_ref[h, :, sl] = jnp.sum(dy, axis=0, keepdims=True)

    col = lambda off: pl.BlockSpec((S, tc), lambda j: (0, j + off))
    w3 = lambda off: pl.BlockSpec((3, tc), lambda j: (0, j + off))
    b1 = lambda off: pl.BlockSpec((1, tc), lambda j: (0, j + off))
    return _pc(body, name=name, grid=(nj,),
               in_specs=[col(0), col(nj), col(0), w3(0), w3(nj), b1(0), b1(nj)],
               out_specs=[pl.BlockSpec((2, S, tc), lambda j: (0, 0, j)), pl.BlockSpec((2, 3, tc), lambda j: (0, 0, j)),
                          pl.BlockSpec((2, 1, tc), lambda j: (0, 0, j))],
               out_shape=[SDS((2, S, DFF), BF16), SDS((2, 3, DFF), F32), SDS((2, 1, DFF), F32)],
               compiler_params=_params(1))(u, u, dact, cw, cw, cb, cb)


def _pool_diff(h, name):
    S, D = h.shape
    G = len(POOL_WINDOWS)
    CG = D // G
    tc = min(CG, LANES)
    per = CG // tc

    def body(h_ref, d_ref):
        g = pl.program_id(0)
        rows = lax.broadcasted_iota(jnp.int32, (S, tc), 0)
        for gi, w in enumerate(POOL_WINDOWS):
            @pl.when(g == gi)
            def _(w=w):
                hv = h_ref[...]
                s, k = hv, 1
                while k < w:
                    s = s + _shift_down(s, k, rows)
                    k *= 2
                count = jnp.minimum(rows + 1, w).astype(F32)
                d_ref[...] = (s / count - hv).astype(BF16)

    spec = pl.BlockSpec((S, tc), lambda g, j: (0, g * per + j))
    return _pc(body, name=name, grid=(G, per), in_specs=[spec], out_specs=spec, out_shape=SDS((S, D), BF16),
               compiler_params=_params(2))(h)


def _pool_diff_bwd(dd, name):
    S, D = dd.shape
    G = len(POOL_WINDOWS)
    CG = D // G
    tc = min(CG, LANES)
    per = CG // tc

    def body(dd_ref, o_ref):
        g = pl.program_id(0)
        rows = lax.broadcasted_iota(jnp.int32, (S, tc), 0)
        for gi, w in enumerate(POOL_WINDOWS):
            @pl.when(g == gi)
            def _(w=w):
                dv = dd_ref[...]
                count = jnp.minimum(rows + 1, w).astype(F32)
                s, k = dv / count, 1
                while k < w:
                    s = s + _shift_up(s, k, rows)
                    k *= 2
                o_ref[...] = s - dv

    spec = pl.BlockSpec((S, tc), lambda g, j: (0, g * per + j))
    return _pc(body, name=name, grid=(G, per), in_specs=[spec], out_specs=spec, out_shape=SDS((S, D), F32),
               compiler_params=_params(2))(dd)


def _pool_mm(d, w, res, gate, name):
    S, D = d.shape
    G, CG, _ = w.shape
    tm = _pick(S, (512, 256, 128))

    def body(d_ref, w_ref, res_ref, gate_ref, o_ref, e_ref):
        acc = jnp.dot(d_ref[...], w_ref[...], preferred_element_type=F32)
        o_ref[...] = res_ref[...] + gate_ref[...] * acc
        e_ref[...] = acc.astype(BF16)

    blk = pl.BlockSpec((tm, CG), lambda g, i: (i, g))
    return _pc(body, name=name, grid=(G, S // tm),
               in_specs=[blk, pl.BlockSpec((None, CG, CG), lambda g, i: (g, 0, 0)), blk,
                         pl.BlockSpec((1, CG), lambda g, i: (0, g))],
               out_specs=[blk, blk], out_shape=[SDS((S, D), F32), SDS((S, D), BF16)],
               compiler_params=_params(2))(d, w, res, gate)


def _pool_mm_bwd(de, d, w, name):
    S, D = de.shape
    G, CG, _ = w.shape
    tm = _pick(S, (512, 256, 128))
    ns = S // tm

    def body(de_ref, d_ref, w_ref, dd_ref, dw_ref, acc_ref):
        i = pl.program_id(1)

        @pl.when(i == 0)
        def _():
            acc_ref[...] = jnp.zeros_like(acc_ref)

        dev = de_ref[...]
        dd_ref[...] = lax.dot_general(dev, w_ref[...], (((1,), (1,)), ((), ())), preferred_element_type=F32)
        acc_ref[...] += lax.dot_general(d_ref[...], dev, (((0,), (0,)), ((), ())), preferred_element_type=F32)

        @pl.when(i == ns - 1)
        def _():
            dw_ref[...] = acc_ref[...].astype(BF16)

    blk = pl.BlockSpec((tm, CG), lambda g, i: (i, g))
    wsp = pl.BlockSpec((None, CG, CG), lambda g, i: (g, 0, 0))
    return _pc(body, name=name, grid=(G, ns), in_specs=[blk, blk, wsp], out_specs=[blk, wsp],
               out_shape=[SDS((S, D), F32), SDS((G, CG, CG), BF16)], scratch_shapes=[pltpu.VMEM((CG, CG), F32)],
               compiler_params=_params(2))(de, d, w)


def _log_sigmoid(z):
    return jnp.minimum(z, 0.0) - jnp.log(1.0 + jnp.exp(-jnp.abs(z)))


def _dot2(a, tri):
    hi = a.astype(BF16)
    lo = (a - hi.astype(F32)).astype(BF16)
    return jnp.dot(hi, tri, preferred_element_type=F32) + jnp.dot(lo, tri, preferred_element_type=F32)


_NT = (((1,), (1,)), ((), ()))
_TN = (((0,), (0,)), ((), ()))


def _forget_cumsum(flog, bf, name):
    S, W = flog.shape
    tb = _pick(S, (128,))

    def body(f_ref, b_ref, o_ref):
        r = lax.broadcasted_iota(jnp.int32, (tb, tb), 0)
        c = lax.broadcasted_iota(jnp.int32, (tb, tb), 1)
        tri = (c <= r).astype(F32)
        carry = jnp.zeros((1, W), F32)
        for q in range(S // tb):
            ls = _log_sigmoid(f_ref[q * tb:(q + 1) * tb, :] + b_ref[...])
            o_ref[q * tb:(q + 1) * tb, :] = carry + jnp.dot(tri, ls, preferred_element_type=F32,
                                                            precision=lax.Precision.HIGHEST)
            carry = carry + jnp.sum(ls, axis=0, keepdims=True)

    return _pc(body, name=name, out_shape=SDS((S, W), F32))(flog, bf)


def _forget_cumsum_bwd(dF, flog, bf, name):
    S, W = flog.shape
    tb = _pick(S, (128,))

    def body(d_ref, f_ref, b_ref, o_ref, db_ref):
        r = lax.broadcasted_iota(jnp.int32, (tb, tb), 0)
        c = lax.broadcasted_iota(jnp.int32, (tb, tb), 1)
        tri = (c >= r).astype(F32)
        carry = jnp.zeros((1, W), F32)
        db = jnp.zeros((1, W), F32)
        for q in reversed(range(S // tb)):
            dv = d_ref[q * tb:(q + 1) * tb, :]
            dls = carry + jnp.dot(tri, dv, preferred_element_type=F32, precision=lax.Precision.HIGHEST)
            carry = carry + jnp.sum(dv, axis=0, keepdims=True)
            dfl = dls * jax.nn.sigmoid(-(f_ref[q * tb:(q + 1) * tb, :] + b_ref[...]))
            o_ref[q * tb:(q + 1) * tb, :] = dfl
            db = db + jnp.sum(dfl, axis=0, keepdims=True)
        db_ref[...] = db

    return _pc(body, name=name, out_shape=[SDS((S, W), F32), SDS((1, W), F32)])(dF, flog, bf)


def _heads_per_step(n_heads):
    return 2 if n_heads % 2 == 0 else 1


def _sb_fwd(qkv, NH, NSB, HD, T, name):
    S = qkv.shape[0]
    nq = S // T
    scale = HD ** -0.5
    HB = _heads_per_step(NSB)
    W = HB * HD

    def body(q_ref, k_ref, v_ref, o_ref, tot_ref):
        i = pl.program_id(1)
        row = lax.broadcasted_iota(jnp.int32, (T, T), 0)
        col = lax.broadcasted_iota(jnp.int32, (T, T), 1)
        upper = (row > col).astype(BF16)
        heads = [slice(hh * HD, (hh + 1) * HD) for hh in range(HB)]
        qs = [q_ref[:, cs] for cs in heads]

        def blk(kb, state, diag):
            sl = pl.ds(pl.multiple_of(kb * T, T), T)
            out = []
            for hh, cs in enumerate(heads):
                carry, acc = state[2 * hh], state[2 * hh + 1]
                k, v = k_ref[sl, cs], v_ref[sl, cs]
                z = lax.dot_general(qs[hh], k, _NT, preferred_element_type=F32) * scale
                ls = _log_sigmoid(z)
                lr = ls - z
                if diag:
                    lr = jnp.where(col < row, lr, 0.0)
                rest = _dot2(lr, upper) + carry
                w = jnp.exp(ls + rest)
                if diag:
                    w = jnp.where(col < row, w, 0.0)
                out += [carry + jnp.sum(lr, axis=1, keepdims=True),
                        acc + jnp.dot(w.astype(BF16), v, preferred_element_type=F32)]
            return tuple(out)

        state = blk(i, (jnp.zeros((T, 1), F32), jnp.zeros((T, HD), F32)) * HB, True)
        state = lax.fori_loop(0, i, lambda jj, st: blk(i - 1 - jj, st, False), state)
        for hh, cs in enumerate(heads):
            o_ref[:, cs] = state[2 * hh + 1].astype(BF16)
            tot_ref[hh] = state[2 * hh]

    return _pc(body, name=name, grid=(NSB // HB, nq),
               in_specs=[pl.BlockSpec((T, W), lambda h, i: (i, h)),
                         pl.BlockSpec((S, W), lambda h, i: (0, NH // HB + h)),
                         pl.BlockSpec((S, W), lambda h, i: (0, 2 * NH // HB + h))],
               out_specs=[pl.BlockSpec((T, W), lambda h, i: (i, h)), pl.BlockSpec((HB, T, 1), lambda h, i: (h, i, 0))],
               out_shape=[SDS((S, NSB * HD), BF16), SDS((NSB, S, 1), F32)],
               compiler_params=_params(2))(qkv, qkv, qkv)


def _sb_bwd(qkv, do, tot, NH, NSB, HD, T, name):
    S = qkv.shape[0]
    nq = S // T
    scale = HD ** -0.5
    HB = _heads_per_step(NSB)
    W = HB * HD

    def body(q_ref, k_ref, v_ref, do_ref, tot_ref, dq_ref, dk_ref, dv_ref, dk_acc, dv_acc):
        i = pl.program_id(1)

        @pl.when(i == 0)
        def _():
            dk_acc[...] = jnp.zeros_like(dk_acc)
            dv_acc[...] = jnp.zeros_like(dv_acc)

        row = lax.broadcasted_iota(jnp.int32, (T, T), 0)
        col = lax.broadcasted_iota(jnp.int32, (T, T), 1)
        incl = (row <= col).astype(BF16)
        strict = (row < col).astype(BF16)
        heads = [slice(hh * HD, (hh + 1) * HD) for hh in range(HB)]
        qs = [q_ref[:, cs] for cs in heads]
        dos = [do_ref[:, cs] for cs in heads]
        tots = [tot_ref[hh] for hh in range(HB)]

        def blk(kb, state, diag):
            sl = pl.ds(pl.multiple_of(kb * T, T), T)
            out = []
            for hh, cs in enumerate(heads):
                cl, cg, dq = state[3 * hh], state[3 * hh + 1], state[3 * hh + 2]
                q, do_ = qs[hh], dos[hh]
                k, v = k_ref[sl, cs], v_ref[sl, cs]
                z = lax.dot_general(q, k, _NT, preferred_element_type=F32) * scale
                ls = _log_sigmoid(z)
                lr = ls - z
                if diag:
                    lr = jnp.where(col < row, lr, 0.0)
                rest = tots[hh] - (cl + _dot2(lr, incl))
                w = jnp.exp(ls + rest)
                if diag:
                    w = jnp.where(col < row, w, 0.0)
                g = lax.dot_general(do_, v, _NT, preferred_element_type=F32) * w
                dv_acc[sl, cs] += lax.dot_general(w.astype(BF16), do_, _TN, preferred_element_type=F32)
                dlr = cg + _dot2(g, strict)
                dz = g * jnp.exp(lr) - dlr * jnp.exp(ls)
                if diag:
                    dz = jnp.where(col < row, dz, 0.0)
                dzb = (dz * scale).astype(BF16)
                dk_acc[sl, cs] += lax.dot_general(dzb, q, _TN, preferred_element_type=F32)
                out += [cl + jnp.sum(lr, axis=1, keepdims=True), cg + jnp.sum(g, axis=1, keepdims=True),
                        dq + jnp.dot(dzb, k, preferred_element_type=F32)]
            return tuple(out)

        zero = jnp.zeros((T, 1), F32)
        state = lax.fori_loop(0, i, lambda kb, st: blk(kb, st, False), (zero, zero, jnp.zeros((T, HD), F32)) * HB)
        state = blk(i, state, True)
        for hh, cs in enumerate(heads):
            dq_ref[:, cs] = state[3 * hh + 2].astype(BF16)

        @pl.when(i == nq - 1)
        def _():
            dk_ref[...] = dk_acc[...].astype(BF16)
            dv_ref[...] = dv_acc[...].astype(BF16)

    qblk = pl.BlockSpec((T, W), lambda h, i: (i, h))
    full = pl.BlockSpec((S, W), lambda h, i: (0, h))
    return _pc(body, name=name, grid=(NSB // HB, nq),
               in_specs=[qblk, pl.BlockSpec((S, W), lambda h, i: (0, NH // HB + h)),
                         pl.BlockSpec((S, W), lambda h, i: (0, 2 * NH // HB + h)), qblk,
                         pl.BlockSpec((HB, T, 1), lambda h, i: (h, i, 0))],
               out_specs=[qblk, full, full],
               out_shape=[SDS((S, NSB * HD), BF16)] * 3,
               scratch_shapes=[pltpu.VMEM((S, W), F32), pltpu.VMEM((S, W), F32)],
               compiler_params=_params(2))(qkv, qkv, qkv, do, tot)


def _fox_fwd(qkv, fcol, frow, NH, NSB, HD, T, name):
    S = qkv.shape[0]
    NFX = NH - NSB
    nq = S // T
    scale = HD ** -0.5
    HB = _heads_per_step(NFX) if NSB % _heads_per_step(NFX) == 0 else 1
    W = HB * HD

    def body(q_ref, k_ref, v_ref, fq_ref, fk_ref, o_ref, o32_ref, lse_ref):
        i = pl.program_id(1)
        row = lax.broadcasted_iota(jnp.int32, (T, T), 0)
        col = lax.broadcasted_iota(jnp.int32, (T, T), 1)
        heads = [slice(hh * HD, (hh + 1) * HD) for hh in range(HB)]
        qs = [q_ref[:, cs] for cs in heads]
        fqs = [fq_ref[hh] for hh in range(HB)]

        def blk(kb, state, diag):
            sl = pl.ds(pl.multiple_of(kb * T, T), T)
            out = []
            for hh, cs in enumerate(heads):
                m, l, acc, rem = state[4 * hh:4 * hh + 4]
                k, v = k_ref[sl, cs], v_ref[sl, cs]
                s = lax.dot_general(qs[hh], k, _NT, preferred_element_type=F32) * scale + (fqs[hh] - fk_ref[hh, kb])
                if diag:
                    s = jnp.where(col <= row, s, NEG_BIG)
                m_new = jnp.maximum(m, jnp.max(s, axis=1, keepdims=True))
                p = jnp.exp(s - m_new)
                alpha = jnp.exp(m - m_new)
                hi = p.astype(BF16)
                lo = (p - hi.astype(F32)).astype(BF16)
                out += [m_new, alpha * l + jnp.sum(p, axis=1, keepdims=True),
                        alpha * acc + jnp.dot(hi, v, preferred_element_type=F32),
                        alpha * rem + jnp.dot(lo, v, preferred_element_type=F32)]
            return tuple(out)

        zero = jnp.zeros((T, HD), F32)
        state = blk(i, (jnp.full((T, 1), NEG_BIG, F32), jnp.zeros((T, 1), F32), zero, zero) * HB, True)
        state = lax.fori_loop(0, i, lambda kb, st: blk(kb, st, False), state)
        for hh, cs in enumerate(heads):
            m, l, acc, rem = state[4 * hh:4 * hh + 4]
            o_ref[:, cs] = (acc / l).astype(BF16)
            o32_ref[:, cs] = (acc + rem) / l
            lse_ref[hh] = m + jnp.log(l)

    vec = pl.BlockSpec((HB, T, 1), lambda h, i: (h, i, 0))
    oblk = pl.BlockSpec((T, W), lambda h, i: (i, h))
    return _pc(body, name=name, grid=(NFX // HB, nq),
               in_specs=[pl.BlockSpec((T, W), lambda h, i: (i, NSB // HB + h)),
                         pl.BlockSpec((S, W), lambda h, i: (0, (NH + NSB) // HB + h)),
                         pl.BlockSpec((S, W), lambda h, i: (0, (2 * NH + NSB) // HB + h)),
                         vec, pl.BlockSpec((HB, nq, 1, T), lambda h, i: (h, 0, 0, 0))],
               out_specs=[oblk, oblk, vec],
               out_shape=[SDS((S, NFX * HD), BF16), SDS((S, NFX * HD), F32), SDS((NFX, S, 1), F32)],
               compiler_params=_params(2))(qkv, qkv, qkv, fcol, frow)


def _fox_bwd(qkv, do, o, fcol, frow, lse, NH, NSB, HD, T, name):
    S = qkv.shape[0]
    NFX = NH - NSB
    nq = S // T
    scale = HD ** -0.5
    HB = _heads_per_step(NFX) if NSB % _heads_per_step(NFX) == 0 else 1
    W = HB * HD

    def body(q_ref, k_ref, v_ref, do_ref, o_ref, fq_ref, fk_ref, lse_ref, dq_ref, dk_ref, dv_ref, dfk_ref,
             dk_acc, dv_acc, dfk_acc):
        i = pl.program_id(1)

        @pl.when(i == 0)
        def _():
            dk_acc[...] = jnp.zeros_like(dk_acc)
            dv_acc[...] = jnp.zeros_like(dv_acc)
            dfk_acc[...] = jnp.zeros_like(dfk_acc)

        row = lax.broadcasted_iota(jnp.int32, (T, T), 0)
        col = lax.broadcasted_iota(jnp.int32, (T, T), 1)
        heads = [slice(hh * HD, (hh + 1) * HD) for hh in range(HB)]
        qs = [q_ref[:, cs] for cs in heads]
        dos = [do_ref[:, cs] for cs in heads]
        fqs = [fq_ref[hh] for hh in range(HB)]
        lses = [lse_ref[hh] for hh in range(HB)]
        deltas = [jnp.sum(dos[hh].astype(F32) * o_ref[:, cs], axis=1, keepdims=True) for hh, cs in enumerate(heads)]

        def blk(kb, dqs, diag):
            sl = pl.ds(pl.multiple_of(kb * T, T), T)
            out = []
            for hh, cs in enumerate(heads):
                q, do_ = qs[hh], dos[hh]
                k, v = k_ref[sl, cs], v_ref[sl, cs]
                s = lax.dot_general(q, k, _NT, preferred_element_type=F32) * scale + (fqs[hh] - fk_ref[hh, kb])
                p = jnp.exp(s - lses[hh])
                if diag:
                    p = jnp.where(col <= row, p, 0.0)
                ds = p * (lax.dot_general(do_, v, _NT, preferred_element_type=F32) - deltas[hh])
                dv_acc[sl, cs] += lax.dot_general(p.astype(BF16), do_, _TN, preferred_element_type=F32)
                dsb = (ds * scale).astype(BF16)
                dk_acc[sl, cs] += lax.dot_general(dsb, q, _TN, preferred_element_type=F32)
                dfk_acc[hh, kb] -= jnp.sum(ds, axis=0, keepdims=True)
                out.append(dqs[hh] + jnp.dot(dsb, k, preferred_element_type=F32))
            return tuple(out)

        dqs = lax.fori_loop(0, i, lambda kb, st: blk(kb, st, False), (jnp.zeros((T, HD), F32),) * HB)
        dqs = blk(i, dqs, True)
        for hh, cs in enumerate(heads):
            dq_ref[:, cs] = dqs[hh].astype(BF16)

        @pl.when(i == nq - 1)
        def _():
            dk_ref[...] = dk_acc[...].astype(BF16)
            dv_ref[...] = dv_acc[...].astype(BF16)
            dfk_ref[...] = dfk_acc[...]

    vec = pl.BlockSpec((HB, T, 1), lambda h, i: (h, i, 0))
    rowv = pl.BlockSpec((HB, nq, 1, T), lambda h, i: (h, 0, 0, 0))
    oblk = pl.BlockSpec((T, W), lambda h, i: (i, h))
    qblk = pl.BlockSpec((T, W), lambda h, i: (i, NSB // HB + h))
    full = pl.BlockSpec((S, W), lambda h, i: (0, h))
    return _pc(body, name=name, grid=(NFX // HB, nq),
               in_specs=[qblk, pl.BlockSpec((S, W), lambda h, i: (0, (NH + NSB) // HB + h)),
                         pl.BlockSpec((S, W), lambda h, i: (0, (2 * NH + NSB) // HB + h)),
                         qblk, oblk, vec, rowv, vec],
               out_specs=[oblk, full, full, rowv],
               out_shape=[SDS((S, NFX * HD), BF16)] * 3 + [SDS((NFX, nq, 1, T), F32)],
               scratch_shapes=[pltpu.VMEM((S, W), F32), pltpu.VMEM((S, W), F32), pltpu.VMEM((HB, nq, 1, T), F32)],
               compiler_params=_params(2))(qkv, qkv, qkv, do, o, fcol, frow, lse)


def _silu(c_all, name):
    def body(c_ref, o_ref):
        cv = c_ref[...]
        o_ref[...] = cv * jax.nn.sigmoid(cv)

    return _pc(body, name=name, out_shape=SDS(c_all.shape, F32))(c_all)


def _mod_project(cond, w_mod, name):
    L, D, C = w_mod.shape
    tk = _pick(D, (512, 256, 128))

    def body(c_ref, w_ref, o_ref):
        @pl.when(pl.program_id(1) == 0)
        def _():
            o_ref[...] = jnp.zeros_like(o_ref)

        o_ref[...] += jnp.dot(c_ref[...].astype(BF16), w_ref[...].astype(BF16), preferred_element_type=F32)

    return _pc(body, name=name, grid=(L, D // tk),
               in_specs=[pl.BlockSpec((16, tk), lambda l, k: (0, k)), pl.BlockSpec((None, tk, C), lambda l, k: (l, k, 0))],
               out_specs=pl.BlockSpec((None, 16, C), lambda l, k: (l, 0, 0)),
               out_shape=SDS((L, 16, C), F32), compiler_params=_params(2))(cond, w_mod)


def _adam_math(w, g, m, v):
    m = ADAM_B1 * m + (1.0 - ADAM_B1) * g
    v = ADAM_B2 * v + (1.0 - ADAM_B2) * (g * g)
    m_hat = m / (1.0 - ADAM_B1 ** ADAM_STEP)
    v_hat = v / (1.0 - ADAM_B2 ** ADAM_STEP)
    delta = -ADAM_LR * (m_hat / (jnp.sqrt(v_hat) + ADAM_EPS) + ADAM_WD * w)
    return delta, m, v


def _adamw(w, m, v, parts, layer, prev, name):
    L, R, C = w.shape
    NP = parts.shape[0]
    tr = _pick(R, (128, 64, 88, 32, 16, 8))
    nprev = 0 if prev is None else 4

    def body(w_ref, m_ref, v_ref, p_ref, *rest):
        g_ref, d_ref, mo_ref, vo_ref = rest[nprev:]
        g = p_ref[0].astype(F32)
        for j in range(1, NP):
            g = g + p_ref[j].astype(F32)
        delta, mn, vn = _adam_math(w_ref[...], g, m_ref[...], v_ref[...])
        g_ref[...] = g
        d_ref[...] = delta
        mo_ref[...] = mn
        vo_ref[...] = vn

    blk = pl.BlockSpec((None, tr, C), lambda i: (layer, i, 0))
    in_specs = [blk, blk, blk, pl.BlockSpec((NP, tr, C), lambda i: (0, i, 0))]
    operands = [w, m, v, parts]
    aliases = {}
    if prev is not None:
        in_specs += [ANY] * 4
        operands += list(prev)
        aliases = {4 + q: q for q in range(4)}
    return _pc(body, name=name, grid=(R // tr,), in_specs=in_specs, out_specs=[blk] * 4,
               out_shape=[SDS(w.shape, F32)] * 4, input_output_aliases=aliases,
               compiler_params=_params(1))(*operands)


def _adamw_mod(w, m, v, cond_t, dmod, name):
    L, D, C = w.shape
    tr = _pick(D, (128, 64))

    def body(w_ref, m_ref, v_ref, ct_ref, dm_ref, g_ref, d_ref, mo_ref, vo_ref):
        ct = ct_ref[...]
        g = ct[:, 0:1] * dm_ref[0]
        for b in range(1, NDEV):
            g = g + ct[:, b:b + 1] * dm_ref[b]
        delta, mn, vn = _adam_math(w_ref[...], g, m_ref[...], v_ref[...])
        g_ref[...] = g
        d_ref[...] = delta
        mo_ref[...] = mn
        vo_ref[...] = vn

    blk = pl.BlockSpec((None, tr, C), lambda l, i: (l, i, 0))
    return _pc(body, name=name, grid=(L, D // tr),
               in_specs=[blk, blk, blk, pl.BlockSpec((tr, LANES), lambda l, i: (i, 0)),
                         pl.BlockSpec((NDEV, None, 1, C), lambda l, i: (0, l, 0, 0))],
               out_specs=[blk] * 4, out_shape=[SDS(w.shape, F32)] * 4, compiler_params=_params(2))(w, m, v, cond_t, dmod)


def _sum_parts(parts, name):
    NP, R, C = parts.shape
    tr = _pick(R, (256, 128, 64, 32, 16, 8))

    def body(p_ref, o_ref):
        g = p_ref[0]
        for j in range(1, NP):
            g = g + p_ref[j]
        o_ref[...] = g

    return _pc(body, name=name, grid=(R // tr,), in_specs=[pl.BlockSpec((NP, tr, C), lambda i: (0, i, 0))],
               out_specs=pl.BlockSpec((tr, C), lambda i: (i, 0)), out_shape=SDS((R, C), F32),
               compiler_params=_params(1))(parts)


def _pack(vecs, rows=None):
    flat = jnp.concatenate([v.reshape(-1).astype(F32) for v in vecs])
    n = flat.shape[0]
    r = rows if rows is not None else -(-n // (256 * LANES)) * 256
    return jnp.pad(flat, (0, r * LANES - n)).reshape(r, LANES)


def _unpack(packed, shapes):
    flat = packed.reshape(-1)
    out, off = [], 0
    for s in shapes:
        n = 1
        for d in s:
            n *= d
        out.append(flat[off:off + n].reshape(s))
        off += n
    return out


def kernel(x, c, w_mod, b_mod, norm_gain, w_attn_in, b_forget, w_attn_out, w_pool, pool_scale, w_up, conv_w, conv_b, w_down, final_gain, loss_target, m_w_mod, m_b_mod, m_norm_gain, m_w_attn_in, m_b_forget, m_w_attn_out, m_w_pool, m_pool_scale, m_w_up, m_conv_w, m_conv_b, m_w_down, m_final_gain, v_w_mod, v_b_mod, v_norm_gain, v_w_attn_in, v_b_forget, v_w_attn_out, v_w_pool, v_pool_scale, v_w_up, v_conv_w, v_conv_b, v_w_down, v_final_gain):
    _, S, D = x.shape
    L = w_mod.shape[0]
    CM = w_mod.shape[2]
    NFX = b_forget.shape[1]
    NH = 2 * NFX
    NSB = NH - NFX
    HD = D // NH
    CI = w_attn_in.shape[2]
    CU = w_up.shape[2]
    F2 = NDEV * CU
    DFF = F2 // 2
    G = len(POOL_WINDOWS)
    CG = D // G
    T = _pick(S, (256, 128))
    me = _idx(_me())
    x0 = x[0]
    target = loss_target[0]

    def layer_shards(l, group):
        if group == 1:
            shards = [w_up[l], w_down[l]]
        elif l % 2 == 0:
            shards = [w_attn_in[l // 2], w_attn_out[l // 2]]
        else:
            shards = [w_pool[l // 2].reshape(G * (CG // NDEV), CG)]
        return [s.astype(BF16) for s in shards]

    small_shapes = [(1, D), norm_gain.shape, pool_scale.shape, conv_w.shape]
    small_all = _all_gather([_pack([c, norm_gain, pool_scale, conv_w])], "gather_small")[0]
    per_dev = [_unpack(small_all[j], small_shapes) for j in range(NDEV)]
    c_all = jnp.concatenate([p[0] for p in per_dev] + [jnp.zeros((16 - NDEV, D), F32)], axis=0)
    gain_f = jnp.concatenate([p[1] for p in per_dev], axis=2)
    pscale_f = jnp.concatenate([p[2] for p in per_dev], axis=1)
    convw_f = jnp.concatenate([p[3] for p in per_dev], axis=2)

    cond_all = _silu(c_all, "cond_silu")
    mod_part = _mod_project(cond_all, w_mod, "mod_project")
    mod_all = _all_gather([mod_part], "gather_mod")[0]
    mod = lax.dynamic_index_in_dim(mod_all, me, axis=2, keepdims=False)
    mod = mod.transpose(1, 0, 2).reshape(L, NDEV * CM) + b_mod
    mods = mod.reshape(L, 6, 1, D)

    inflight = {}

    def gather_start(l, group, after):
        inflight[l, group], t = _gather_start(layer_shards(l, group), me, after, f"gather{l}_{group}_start")
        return t

    def gather_forward(l, group, after):
        inflight[l, group], t = _gather_forward(inflight[l, group], after, f"gather{l}_{group}_forward")
        return t

    def gather_wait(l, group, after):
        return _gather_wait(inflight.pop((l, group)), after, f"gather{l}_{group}_wait")

    def starts_at(l):
        if l % 2 == 0:
            return [(k, g) for k, g in [(l + 1, 0), (l + 1, 1), (l + 2, 0)] if k < L]
        return [(k, g) for k, g in [(l + 1, 1)] if k < L]

    tok = gather_start(0, 0, [mods])
    tok = gather_start(0, 1, [tok])
    tok = gather_forward(0, 0, [tok])
    mixer_w = gather_wait(0, 0, [tok])

    saved = []
    xl = x0
    for l in range(L):
        i = l // 2
        sh1, sc1, g1, sh2, sc2, g2 = [mods[l, q] for q in range(6)]
        gn1, gn2 = gain_f[l, 0:1], gain_f[l, 1:2]
        st = {"x": xl}
        after = [mixer_w[0]]
        for k, g in starts_at(l):
            after = [gather_start(k, g, after)]
        if starts_at(l):
            sh1 = sh1 + after[0][0, 0]
        if l % 2 == 0:
            win = mixer_w[0].transpose(1, 0, 2).reshape(D, NDEV * CI)
            wqkv = win[:, :3 * D]
            wf = jnp.pad(win[:, 3 * D:], ((0, 0), (0, LANES - NFX)))
            wout = mixer_w[1].reshape(D, D)
            h1 = _norm_mod(xl, gn1, sc1, sh1, BF16, f"norm1_{l}")
            qkv = _matmul(h1, wqkv, mode="nn", name=f"qkv_{l}", out_dtype=BF16)
            flog = _matmul(h1, wf, mode="nn", name=f"flog_{l}", out_dtype=F32, tn=LANES)
            bfp = jnp.pad(b_forget[i], (0, LANES - NFX)).reshape(1, LANES)
            Fc = _forget_cumsum(flog, bfp, f"fcum_{l}")
            f8 = Fc[:, :NFX].T
            fcol, frow = f8[:, :, None], f8.reshape(NFX, S // T, 1, T)
            o_sb, tot = _sb_fwd(qkv, NH, NSB, HD, T, f"sb_fwd_{l}")
            o_fx, o_fx32, lse = _fox_fwd(qkv, fcol, frow, NH, NSB, HD, T, f"fox_fwd_{l}")
            o = jnp.concatenate([o_sb, o_fx], axis=1)
            g1 = g1 + gather_forward(l, 1, [o])[0, 0]
            x1, y1 = _matmul(o, wout, mode="nn", name=f"attn_out_{l}", out_dtype=F32, res=xl, gate=g1, y_dtype=BF16)
            st.update(h1=h1, qkv=qkv, flog=flog, bfp=bfp, fcol=fcol, frow=frow, tot=tot, lse=lse, o=o, o_fx=o_fx32,
                      wqkv=wqkv, wf=wf, wout=wout, y1=y1)
        else:
            wpool = mixer_w[0].reshape(NDEV, G, CG // NDEV, CG).transpose(1, 0, 2, 3).reshape(G, CG, CG)
            h1 = _norm_mod(xl, gn1, sc1, sh1, F32, f"norm1_{l}")
            dpool = _pool_diff(h1, f"pool_diff_{l}")
            g1 = g1 + gather_forward(l, 1, [dpool])[0, 0]
            gp = g1 * pscale_f[i:i + 1]
            x1, e1 = _pool_mm(dpool, wpool, xl, gp, f"pool_mm_{l}")
            st.update(dpool=dpool, wpool=wpool, gp=gp, y1=e1)
        wup_g, wdown_g = gather_wait(l, 1, [x1])
        wdown_f = wdown_g.reshape(DFF, D)
        h2 = _norm_mod(x1, gn2, sc2, sh2, BF16, f"norm2_{l}")
        u = _matmul(h2, wup_g, mode="nn", name=f"ffn_up_{l}", out_dtype=F32, b_blocked=True)
        cb = conv_b[l].reshape(1, F2)
        act = _convgate_fwd(u, convw_f[l], cb, f"convgate_{l}")
        if l + 1 < L:
            g2 = g2 + gather_forward(l + 1, 0, [act])[0, 0]
        x2, y2 = _matmul(act, wdown_f, mode="nn", name=f"ffn_down_{l}", out_dtype=F32, res=x1, gate=g2, y_dtype=BF16,
                         tk=CU)
        if l + 1 < L:
            mixer_w = gather_wait(l + 1, 0, [x2])
        st.update(x1=x1, h2=h2, u=u, cb=cb, act=act, y2=y2, wup_g=wup_g, wdown_f=wdown_f,
                  mod=(sh1, sc1, g1, sh2, sc2, g2), gn=(gn1, gn2))
        saved.append(st)
        xl = x2

    dx, d_fgain, loss_tile = _loss_head(xl, target, final_gain.reshape(1, D), "loss_head")
    loss = lax.psum(loss_tile[0, 0], ("x", "y", "c"))

    dmod_rows = [None] * L
    d_gain = [None] * L
    d_convw = [None] * L
    d_convb = [None] * L
    d_pscale = [None] * (L // 2)
    d_bf = [None] * ((L + 1) // 2)
    big = {"w_up": None, "w_down": None, "w_attn_in": None, "w_attn_out": None, "w_pool": None}

    def update(key, w, m, v, bufs, layer, tag):
        big[key] = _adamw(w, m, v, bufs, layer, big[key], f"adamw_{tag}")

    exchanges = []

    def exchange_start(l, group, parts, after):
        scat, t = _scatter_start(parts, me, after, f"scatter{l}_{group}_start")
        exchanges.append((l, group, scat))
        return t

    def exchanges_finish(after, first_layer):
        for entry in [e for e in exchanges if e[0] >= first_layer]:
            exchanges.remove(entry)
            pl_, group, scat = entry
            bufs = _scatter_wait(scat, after, f"scatter{pl_}_{group}_wait")
            if group == 1:
                update("w_up", w_up, m_w_up, v_w_up, bufs[0], pl_, f"up_{pl_}")
                update("w_down", w_down, m_w_down, v_w_down, bufs[1], pl_, f"down_{pl_}")
            elif pl_ % 2 == 0:
                update("w_attn_in", w_attn_in, m_w_attn_in, v_w_attn_in, bufs[0], pl_ // 2, f"attn_in_{pl_}")
                update("w_attn_out", w_attn_out, m_w_attn_out, v_w_attn_out, bufs[1], pl_ // 2, f"attn_out_{pl_}")
            else:
                wp3 = lambda a: a.reshape(a.shape[0], G * (CG // NDEV), CG)
                update("w_pool", wp3(w_pool), wp3(m_w_pool), wp3(v_w_pool), bufs[0], pl_ // 2, f"pool_{pl_}")

    tok = None
    for l in reversed(range(L)):
        i = l // 2
        st = saved[l]
        sh1, sc1, g1, sh2, sc2, g2 = st["mod"]
        if tok is not None:
            g2 = g2 + tok[0, 0]
        gn1, gn2 = st["gn"]
        dffn, dg2 = _gate_bwd(dx, st["y2"], g2, f"gate2_bwd_{l}")
        dact = _matmul(dffn, st["wdown_f"], mode="nt", name=f"ffn_down_dx_{l}", out_dtype=F32, tn=CU)
        dwdown = _matmul(st["act"], dffn, mode="tn", name=f"ffn_down_dw_{l}", out_dtype=BF16, tm=CU)
        du, dcw, dcb = _convgate_bwd(st["u"], dact, convw_f[l], st["cb"], f"convgate_bwd_{l}")
        dh2 = _matmul(du, st["wup_g"], mode="nt", name=f"ffn_up_dx_{l}", out_dtype=F32, a_split=True, b_blocked=True)
        dwup = _matmul(st["h2"], du, mode="tn", name=f"ffn_up_dw_{l}", out_dtype=BF16, tn=CU, b_split=True,
                       out_blocked=True)
        dx, dgn2, dsc2, dsh2 = _norm_mod_bwd(dh2, st["x1"], dx, gn2, sc2, f"norm2_bwd_{l}")
        d_convw[l] = jnp.concatenate([dcw[0], dcw[1]], axis=1)
        d_convb[l] = jnp.concatenate([dcb[0], dcb[1]], axis=1)
        tok = exchange_start(l, 1, [dwup, dwdown.reshape(NDEV, DFF // NDEV, D)], [])
        if l % 2 == 0:
            dy1, dg1 = _gate_bwd(dx, st["y1"], g1 + tok[0, 0], f"gate1_bwd_{l}")
            do = _matmul(dy1, st["wout"], mode="nt", name=f"attn_out_dx_{l}", out_dtype=BF16)
            dwout = _matmul(st["o"], dy1, mode="tn", name=f"attn_out_dw_{l}", out_dtype=BF16)
            dq_s, dk_s, dv_s = _sb_bwd(st["qkv"], do, st["tot"], NH, NSB, HD, T, f"sb_bwd_{l}")
            dq_f, dk_f, dv_f, dfk = _fox_bwd(st["qkv"], do, st["o_fx"], st["fcol"], st["frow"], st["lse"], NH, NSB, HD, T,
                                             f"fox_bwd_{l}")
            dqkv = jnp.concatenate([dq_s, dq_f, dk_s, dk_f, dv_s, dv_f], axis=1)
            dF = jnp.pad(dfk.reshape(NFX, S).T, ((0, 0), (0, LANES - NFX)))
            dflog, dbf = _forget_cumsum_bwd(dF, st["flog"], st["bfp"], f"fcum_bwd_{l}")
            dflog_b = dflog.astype(BF16)
            dh1 = _matmul(dqkv, st["wqkv"], mode="nt", name=f"qkv_dx_{l}", out_dtype=F32)
            dh1 = _matmul(dflog_b, st["wf"], mode="nt", name=f"flog_dx_{l}", out_dtype=F32, res=dh1,
                          gate=jnp.ones((1, D), F32))
            dwqkv = _matmul(st["h1"], dqkv, mode="tn", name=f"qkv_dw_{l}", out_dtype=BF16)
            dwf = _matmul(st["h1"], dflog_b, mode="tn", name=f"flog_dw_{l}", out_dtype=BF16, tn=LANES)
            dwin = jnp.concatenate([dwqkv, dwf[:, :NFX]], axis=1).reshape(D, NDEV, CI).transpose(1, 0, 2)
            d_bf[i] = dbf[0, :NFX]
            parts = [dwin, dwout.reshape(NDEV, D // NDEV, D)]
        else:
            de, dgp = _gate_bwd(dx, st["y1"], st["gp"] + tok[0, 0], f"gate1_bwd_{l}")
            dg1 = dgp * pscale_f[i:i + 1]
            d_pscale[i] = dgp * g1
            dd, dwp = _pool_mm_bwd(de, st["dpool"], st["wpool"], f"pool_mm_bwd_{l}")
            dh1 = _pool_diff_bwd(dd, f"pool_diff_bwd_{l}")
            parts = [dwp.reshape(G, NDEV, CG // NDEV, CG).transpose(1, 0, 2, 3).reshape(NDEV, G * (CG // NDEV), CG)]
        dx, dgn1, dsc1, dsh1 = _norm_mod_bwd(dh1, st["x"], dx, gn1, sc1, f"norm1_bwd_{l}")
        dmod_rows[l] = jnp.concatenate([dsh1, dsc1, dg1, dsh2, dsc2, dg2], axis=1)
        d_gain[l] = jnp.concatenate([dgn1, dgn2], axis=0)
        if l > 0:
            tok = exchange_start(l, 0, parts, [])
            exchanges_finish([dx, tok], l + 1)

    grad_x = dx[None]

    small_grads = [jnp.stack(dmod_rows), jnp.stack(d_gain), jnp.stack(d_pscale), jnp.stack(d_convw),
                   jnp.stack(d_convb), jnp.stack(d_bf), d_fgain]
    sg_shapes = [(L, 6 * D), (L, 2, D), (L // 2, D), (L, 3, F2), (L, F2), ((L + 1) // 2, NFX), (D,)]
    sg_all = _all_gather([_pack(small_grads)], "gather_small_grads")[0]
    tok = exchange_start(0, 0, parts, [sg_all])
    exchanges_finish([tok], 1)
    sg_all = sg_all + tok[0, 0]
    sg_sum = _unpack(_sum_parts(sg_all, "sum_small_grads"), sg_shapes)
    g_bmod, g_gain_f, g_pscale_f, g_convw_f, g_convb, g_bf, g_fgain = sg_sum
    shard = lambda a, n, axis: lax.dynamic_slice_in_dim(a, me * n, n, axis=axis)
    g_small = [g_bmod, shard(g_gain_f, D // NDEV, 2), shard(g_pscale_f, D // NDEV, 1), shard(g_convw_f, CU, 2), g_convb,
               g_bf, g_fgain]
    w_small = [b_mod, norm_gain, pool_scale, conv_w, conv_b, b_forget, final_gain]
    m_small = [m_b_mod, m_norm_gain, m_pool_scale, m_conv_w, m_conv_b, m_b_forget, m_final_gain]
    v_small = [v_b_mod, v_norm_gain, v_pool_scale, v_conv_w, v_conv_b, v_b_forget, v_final_gain]
    small_raw = _adamw(_pack(w_small)[None], _pack(m_small)[None], _pack(v_small)[None], _pack(g_small)[None], 0, None,
                       "adamw_small")
    small_out = [_unpack(a[0], [w.shape for w in w_small]) for a in small_raw]

    dmod_all = jnp.stack([_unpack(sg_all[j], sg_shapes[:1])[0] for j in range(NDEV)])
    dmod_mine = lax.dynamic_slice_in_dim(dmod_all.reshape(NDEV, L, NDEV, CM), me, 1, axis=2)
    cond_t = jnp.pad(cond_all[:NDEV].T, ((0, 0), (0, LANES - NDEV)))
    mod_out = _adamw_mod(w_mod, m_w_mod, v_w_mod, cond_t, dmod_mine, "adamw_mod")
    exchanges_finish([mod_out[0], small_raw[0]], 0)

    pool4 = lambda a: a.reshape(w_pool.shape)
    names = ["w_mod", "b_mod", "norm_gain", "w_attn_in", "b_forget", "w_attn_out", "w_pool", "pool_scale", "w_up", "conv_w",
             "conv_b", "w_down", "final_gain"]
    small_pos = {"b_mod": 0, "norm_gain": 1, "pool_scale": 2, "conv_w": 3, "conv_b": 4, "b_forget": 5, "final_gain": 6}
    outs = []
    for kind in range(4):
        for nm in names:
            if nm == "w_mod":
                outs.append(mod_out[kind])
            elif nm in small_pos:
                outs.append(small_out[kind][small_pos[nm]])
            elif nm == "w_pool":
                outs.append(pool4(big[nm][kind]))
            else:
                outs.append(big[nm][kind])
    return (loss, grad_x, *outs)
```

```python
import jax
import jax.numpy as jnp
from jax import lax
from jax.experimental import pallas as pl
from jax.experimental.pallas import tpu as pltpu

NDEV = 8
F32 = jnp.float32
BF16 = jnp.bfloat16
MESH = pl.DeviceIdType.MESH
VMEM_LIMIT_BYTES = 56 * 1024 * 1024
LANES = 128
POOL_WINDOWS = (2, 4, 8, 16)
EPS = 1e-6
ADAM_LR = 0.001
ADAM_B1 = 0.9
ADAM_B2 = 0.999
ADAM_EPS = 1e-08
ADAM_WD = 0.01
ADAM_STEP = 10
NEG_BIG = -1e30
SDS = jax.ShapeDtypeStruct
ANY = pl.BlockSpec(memory_space=pl.ANY)


def _pc(body, **kw):
    return pl.pallas_call(body, **kw)


def _params(n_axes):
    return pltpu.CompilerParams(dimension_semantics=("arbitrary",) * n_axes, vmem_limit_bytes=VMEM_LIMIT_BYTES)


def _pick(n, prefs):
    for p in prefs:
        if p <= n and n % p == 0:
            return p
    return n


def _idx(p):
    return 4 * p[0] + 2 * p[1] + p[2]


def _me():
    return lax.axis_index("x"), lax.axis_index("y"), lax.axis_index("c")


def _all_gather(arrs, name):
    n = len(arrs)

    def body(*refs):
        ins, outs = refs[:n], refs[n:2 * n]
        send_sems, recv_sems, local_sems = refs[2 * n:]
        x, y, c = _me()
        me, sib = (x, y, c), (x, y, 1 - c)
        chips = [(1 - x, y), (x, 1 - y), (1 - x, 1 - y)]

        def copy(t, k, block, to, src=None):
            dst = outs[t].at[_idx(block)]
            return pltpu.make_async_remote_copy(
                src_ref=dst if src is None else src, dst_ref=dst,
                send_sem=send_sems.at[7 * t + k], recv_sem=recv_sems.at[7 * t + k],
                device_id=to, device_id_type=MESH)

        mine = [pltpu.make_async_copy(ins[t], outs[t].at[_idx(me)], local_sems.at[t]) for t in range(n)]
        for cp in mine:
            cp.start()
        first = []
        for t in range(n):
            first.append(copy(t, 0, me, sib, src=ins[t]))
            for j, chip in enumerate(chips):
                first.append(copy(t, 1 + j, me, (*chip, c), src=ins[t]))
        for cp in first:
            cp.start()
        passed = []
        for t in range(n):
            for j, chip in enumerate(chips):
                copy(t, 1 + j, (*chip, c), me).wait_recv()
                cp = copy(t, 4 + j, (*chip, c), sib)
                cp.start()
                passed.append(cp)
        for t in range(n):
            copy(t, 0, sib, me).wait_recv()
            for j, chip in enumerate(chips):
                copy(t, 4 + j, (*chip, 1 - c), me).wait_recv()
        for cp in first + passed:
            cp.wait_send()
        for cp in mine:
            cp.wait()

    return _pc(
        body, name=name,
        out_shape=[SDS((NDEV,) + a.shape, a.dtype) for a in arrs],
        in_specs=[ANY] * n, out_specs=[ANY] * n,
        scratch_shapes=[pltpu.SemaphoreType.DMA((7 * n,)), pltpu.SemaphoreType.DMA((7 * n,)),
                        pltpu.SemaphoreType.DMA((n,))],
    )(*arrs)


HBM = pl.BlockSpec(memory_space=pltpu.HBM)
SEM = pl.BlockSpec(memory_space=pltpu.SEMAPHORE)
EFFECT = pltpu.SideEffectType.DATAFLOW_SIDE_EFFECTING
TOKEN = SDS((8, LANES), F32)


def _hbm(a):
    return pltpu.with_memory_space_constraint(a, pltpu.HBM)


def _landing(own, me):
    return lax.dynamic_update_index_in_dim(lax.empty((NDEV,) + own.shape, own.dtype), own, me, 0)


def _split_call(body, name, n_thru, thru, sems_in, after, sems_out):
    n_sem = len(sems_out)
    operands = [_hbm(a) for a in thru] + list(sems_in) + list(after)
    in_specs = [HBM] * n_thru + [SEM] * len(sems_in) + [ANY] * len(after)
    out_shape = [pltpu.SemaphoreType.DMA((k,)) for k in sems_out] + [pltpu.HBM(a.shape, a.dtype) for a in thru] + [TOKEN]
    out_specs = [SEM] * n_sem + [HBM] * n_thru + [pl.BlockSpec(memory_space=pltpu.VMEM)]
    outs = _pc(body, name=name, in_specs=in_specs, out_specs=out_specs, out_shape=out_shape,
               input_output_aliases={q: n_sem + q for q in range(n_thru)},
               compiler_params=pltpu.CompilerParams(has_side_effects=EFFECT))(*operands)
    return list(outs[:n_sem]), list(outs[n_sem:n_sem + n_thru]), outs[-1]


def _gather_start(shards, me, after, name):
    n = len(shards)
    lands = [_landing(s, me) for s in shards]

    def body(*refs):
        shard_refs, land_refs = refs[:n], refs[n:2 * n]
        send_sems, recv_sems = refs[2 * n + len(after)], refs[2 * n + len(after) + 1]
        x, y, c = _me()
        me_i = _idx((x, y, c))
        peers = [(x, y, 1 - c), (1 - x, y, c), (x, 1 - y, c), (1 - x, 1 - y, c)]
        for t in range(n):
            for k, p in enumerate(peers):
                pltpu.make_async_remote_copy(
                    src_ref=shard_refs[t], dst_ref=land_refs[t].at[me_i], send_sem=send_sems.at[4 * t + k],
                    recv_sem=recv_sems.at[4 * t + k], device_id=p, device_id_type=MESH).start()
        refs[-1][...] = jnp.zeros((8, LANES), F32)

    sems, thru, token = _split_call(body, name, 2 * n, list(shards) + lands, [], after, [4 * n, 4 * n])
    return dict(n=n, sems=sems, shards=thru[:n], lands=thru[n:]), token


def _gather_forward(st, after, name):
    n = st["n"]

    def body(*refs):
        shard_refs, land_refs = refs[:n], refs[n:2 * n]
        send1, recv1 = refs[2 * n], refs[2 * n + 1]
        send2, recv2 = refs[2 * n + 2 + len(after)], refs[2 * n + 3 + len(after)]
        x, y, c = _me()
        sib = (x, y, 1 - c)
        senders = [sib, (1 - x, y, c), (x, 1 - y, c), (1 - x, 1 - y, c)]
        for t in range(n):
            for k, p in enumerate(senders):
                cp = pltpu.make_async_remote_copy(
                    src_ref=shard_refs[t], dst_ref=land_refs[t].at[_idx(p)], send_sem=send1.at[4 * t + k],
                    recv_sem=recv1.at[4 * t + k], device_id=p, device_id_type=MESH)
                cp.wait_send()
                cp.wait_recv()
        for t in range(n):
            for j, p in enumerate(senders[1:]):
                slab = land_refs[t].at[_idx(p)]
                pltpu.make_async_remote_copy(
                    src_ref=slab, dst_ref=slab, send_sem=send2.at[3 * t + j], recv_sem=recv2.at[3 * t + j],
                    device_id=sib, device_id_type=MESH).start()
        refs[-1][...] = jnp.zeros((8, LANES), F32)

    sems, thru, token = _split_call(body, name, 2 * n, st["shards"] + st["lands"], st["sems"], after, [3 * n, 3 * n])
    return dict(n=n, sems=sems, shards=thru[:n], lands=thru[n:]), token


def _gather_wait(st, after, name):
    n = st["n"]

    def body(*refs):
        land_refs = refs[n:2 * n]
        send2, recv2 = refs[2 * n], refs[2 * n + 1]
        x, y, c = _me()
        sib = (x, y, 1 - c)
        for t in range(n):
            for j, chip in enumerate([(1 - x, y), (x, 1 - y), (1 - x, 1 - y)]):
                sent, got = land_refs[t].at[_idx((*chip, c))], land_refs[t].at[_idx((*chip, 1 - c))]
                cp = pltpu.make_async_remote_copy(
                    src_ref=sent, dst_ref=got, send_sem=send2.at[3 * t + j], recv_sem=recv2.at[3 * t + j],
                    device_id=sib, device_id_type=MESH)
                cp.wait_send()
                cp.wait_recv()
        refs[-1][...] = jnp.zeros((8, LANES), F32)

    _, thru, _ = _split_call(body, name, 2 * n, st["shards"] + st["lands"], st["sems"], after, [])
    return thru[n:]


def _scatter_start(parts, me, after, name):
    n = len(parts)
    lands = [_landing(lax.dynamic_index_in_dim(p, me, 0, keepdims=False), me) for p in parts]

    def body(*refs):
        part_refs, land_refs = refs[:n], refs[n:2 * n]
        send_sems, recv_sems = refs[2 * n + len(after)], refs[2 * n + len(after) + 1]
        x, y, c = _me()
        me_i = _idx((x, y, c))
        for t in range(n):
            for r in range(1, NDEV):
                p = (1 - x if r & 4 else x, 1 - y if r & 2 else y, 1 - c if r & 1 else c)
                pltpu.make_async_remote_copy(
                    src_ref=part_refs[t].at[_idx(p)], dst_ref=land_refs[t].at[me_i], send_sem=send_sems.at[7 * t + r - 1],
                    recv_sem=recv_sems.at[7 * t + r - 1], device_id=p, device_id_type=MESH).start()
        refs[-1][...] = jnp.zeros((8, LANES), F32)

    sems, thru, token = _split_call(body, name, 2 * n, list(parts) + lands, [], after, [7 * n, 7 * n])
    return dict(n=n, sems=sems, parts=thru[:n], lands=thru[n:]), token


def _scatter_wait(st, after, name):
    n = st["n"]

    def body(*refs):
        part_refs, land_refs = refs[:n], refs[n:2 * n]
        send_sems, recv_sems = refs[2 * n], refs[2 * n + 1]
        x, y, c = _me()
        for t in range(n):
            for r in range(1, NDEV):
                p = (1 - x if r & 4 else x, 1 - y if r & 2 else y, 1 - c if r & 1 else c)
                cp = pltpu.make_async_remote_copy(
                    src_ref=part_refs[t].at[_idx(p)], dst_ref=land_refs[t].at[_idx(p)], send_sem=send_sems.at[7 * t + r - 1],
                    recv_sem=recv_sems.at[7 * t + r - 1], device_id=p, device_id_type=MESH)
                cp.wait_send()
                cp.wait_recv()
        refs[-1][...] = jnp.zeros((8, LANES), F32)

    _, thru, _ = _split_call(body, name, 2 * n, st["parts"] + st["lands"], st["sems"], after, [])
    return thru[n:]


def _matmul(a, b, *, mode, name, out_dtype, tm=1024, tn=1024, tk=2048, b_blocked=False, out_blocked=False,
            a_split=False, b_split=False, res=None, gate=None, y_dtype=None, n=None):
    if mode == "tn":
        K, M = (a.shape[0], a.shape[1]) if not a_split else (a.shape[1], 2 * a.shape[2])
    else:
        M, K = (a.shape[0], a.shape[1]) if not a_split else (a.shape[1], 2 * a.shape[2])
    if b_blocked:
        if mode == "nn":
            N, tn = b.shape[0] * b.shape[2], b.shape[2]
        else:
            N, tk = b.shape[1], b.shape[2]
    elif b_split:
        N = 2 * b.shape[2]
    else:
        N = b.shape[0] if mode == "nt" else b.shape[1]
    if n is not None:
        N = n
    tm = _pick(M, (tm, 704, 512, 384, 256, 128))
    if not (b_blocked and mode == "nn"):
        tn = _pick(N, (tn, 1024, 768, 512, 384, 256, 128))
    if not (b_blocked and mode == "nt"):
        tk = _pick(K, (tk, 1024, 512, 384, 256, 128))
    nm, nn_, nk = M // tm, N // tn, K // tk

    if mode == "tn":
        a_spec = pl.BlockSpec((tk, tm), lambda i, j, k: (k, i))
        dims = (((0,), (0,)), ((), ()))
    elif a_split:
        per = a.shape[2] // tk
        a_spec = pl.BlockSpec((None, tm, tk), lambda i, j, k: (k // per, i, k % per))
    else:
        a_spec = pl.BlockSpec((tm, tk), lambda i, j, k: (i, k))
    if mode == "nn":
        dims = (((1,), (0,)), ((), ()))
        if b_blocked:
            b_spec = pl.BlockSpec((None, tk, tn), lambda i, j, k: (j, k, 0))
        else:
            b_spec = pl.BlockSpec((tk, tn), lambda i, j, k: (k, j))
    elif mode == "nt":
        dims = (((1,), (1,)), ((), ()))
        if b_blocked:
            b_spec = pl.BlockSpec((None, tn, tk), lambda i, j, k: (k, j, 0))
        else:
            b_spec = pl.BlockSpec((tn, tk), lambda i, j, k: (j, k))
    else:
        if b_split:
            per_b = b.shape[2] // tn
            b_spec = pl.BlockSpec((None, tk, tn), lambda i, j, k: (j // per_b, k, j % per_b))
        else:
            b_spec = pl.BlockSpec((tk, tn), lambda i, j, k: (k, j))
    if out_blocked:
        o_spec = pl.BlockSpec((None, tm, tn), lambda i, j, k: (j, i, 0))
        o_shape = SDS((nn_, M, tn), out_dtype)
    else:
        o_spec = pl.BlockSpec((tm, tn), lambda i, j, k: (i, j))
        o_shape = SDS((M, N), out_dtype)
    fused = res is not None
    in_specs, operands = [a_spec, b_spec], [a, b]
    out_specs, out_shapes = [o_spec], [o_shape]
    if fused:
        in_specs += [pl.BlockSpec((tm, tn), lambda i, j, k: (i, j)), pl.BlockSpec((1, tn), lambda i, j, k: (0, j))]
        operands += [res, gate]
        if y_dtype is not None:
            out_specs.append(pl.BlockSpec((tm, tn), lambda i, j, k: (i, j)))
            out_shapes.append(SDS((M, N), y_dtype))

    def body(*refs):
        a_ref, b_ref = refs[0], refs[1]
        acc_ref = refs[-1]
        k = pl.program_id(2)

        def product():
            return lax.dot_general(a_ref[...], b_ref[...], dims, preferred_element_type=F32)

        def finish(acc):
            if fused:
                res_ref, gate_ref, o_ref = refs[2], refs[3], refs[4]
                o_ref[...] = (res_ref[...] + gate_ref[...] * acc).astype(o_ref.dtype)
                if y_dtype is not None:
                    refs[5][...] = acc.astype(y_dtype)
            else:
                refs[2][...] = acc.astype(refs[2].dtype)

        if nk == 1:
            finish(product())
        else:
            @pl.when(k == 0)
            def _():
                acc_ref[...] = product()

            @pl.when(jnp.logical_and(k > 0, k < nk - 1))
            def _():
                acc_ref[...] += product()

            @pl.when(k == nk - 1)
            def _():
                finish(acc_ref[...] + product())

    outs = _pc(body, name=name, grid=(nm, nn_, nk), in_specs=in_specs, out_specs=out_specs, out_shape=out_shapes,
               scratch_shapes=[pltpu.VMEM((tm, tn), F32)], compiler_params=_params(3))(*operands)
    return outs[0] if len(outs) == 1 else tuple(outs)


def _norm_mod(x, gain, sc, sh, out_dtype, name):
    S, D = x.shape
    tr = _pick(S, (256, 128))

    def body(x_ref, g_ref, sc_ref, sh_ref, o_ref):
        xv = x_ref[...]
        r = lax.rsqrt(jnp.mean(xv * xv, axis=-1, keepdims=True) + EPS)
        n = (xv * r) * g_ref[...]
        o_ref[...] = (n * (1.0 + sc_ref[...]) + sh_ref[...]).astype(o_ref.dtype)

    row = pl.BlockSpec((tr, D), lambda i: (i, 0))
    vec = pl.BlockSpec((1, D), lambda i: (0, 0))
    return _pc(body, name=name, grid=(S // tr,), in_specs=[row, vec, vec, vec], out_specs=row,
               out_shape=SDS((S, D), out_dtype), compiler_params=_params(1))(x, gain, sc, sh)


def _norm_mod_bwd(dh, x, dxres, gain, sc, name):
    S, D = x.shape
    tr = _pick(S, (256, 128))

    def body(dh_ref, x_ref, dxres_ref, g_ref, sc_ref, dx_ref, dgain_ref, dsc_ref, dsh_ref):
        @pl.when(pl.program_id(0) == 0)
        def _():
            dgain_ref[...] = jnp.zeros_like(dgain_ref)
            dsc_ref[...] = jnp.zeros_like(dsc_ref)
            dsh_ref[...] = jnp.zeros_like(dsh_ref)

        xv = x_ref[...]
        dh = dh_ref[...].astype(F32)
        r = lax.rsqrt(jnp.mean(xv * xv, axis=-1, keepdims=True) + EPS)
        nh = xv * r
        gn = g_ref[...]
        dn = dh * (1.0 + sc_ref[...])
        dgain_ref[...] += jnp.sum(dn * nh, axis=0, keepdims=True)
        dsc_ref[...] += jnp.sum(dh * (nh * gn), axis=0, keepdims=True)
        dsh_ref[...] += jnp.sum(dh, axis=0, keepdims=True)
        dnh = dn * gn
        dx = r * (dnh - nh * jnp.mean(dnh * nh, axis=-1, keepdims=True))
        dx_ref[...] = dxres_ref[...] + dx

    row = pl.BlockSpec((tr, D), lambda i: (i, 0))
    vec = pl.BlockSpec((1, D), lambda i: (0, 0))
    return _pc(body, name=name, grid=(S // tr,), in_specs=[row, row, row, vec, vec], out_specs=[row, vec, vec, vec],
               out_shape=[SDS((S, D), F32), SDS((1, D), F32), SDS((1, D), F32), SDS((1, D), F32)],
               compiler_params=_params(1))(dh, x, dxres, gain, sc)


def _gate_bwd(dx, y, gate, name):
    S, D = dx.shape
    tr = _pick(S, (256, 128))

    def body(dx_ref, y_ref, gate_ref, dy_ref, dgate_ref):
        @pl.when(pl.program_id(0) == 0)
        def _():
            dgate_ref[...] = jnp.zeros_like(dgate_ref)

        dxv = dx_ref[...]
        dgate_ref[...] += jnp.sum(dxv * y_ref[...].astype(F32), axis=0, keepdims=True)
        dy_ref[...] = (dxv * gate_ref[...]).astype(BF16)

    row = pl.BlockSpec((tr, D), lambda i: (i, 0))
    vec = pl.BlockSpec((1, D), lambda i: (0, 0))
    return _pc(body, name=name, grid=(S // tr,), in_specs=[row, row, vec], out_specs=[row, vec],
               out_shape=[SDS((S, D), BF16), SDS((1, D), F32)], compiler_params=_params(1))(dx, y, gate)


def _loss_head(x, target, fgain, name):
    S, D = x.shape
    tr = _pick(S, (256, 128))

    def body(x_ref, t_ref, fg_ref, dx_ref, dfg_ref, loss_ref):
        @pl.when(pl.program_id(0) == 0)
        def _():
            dfg_ref[...] = jnp.zeros_like(dfg_ref)
            loss_ref[...] = jnp.zeros_like(loss_ref)

        xv = x_ref[...]
        fg = fg_ref[...]
        r = lax.rsqrt(jnp.mean(xv * xv, axis=-1, keepdims=True) + EPS)
        nh = xv * r
        e = nh * fg - t_ref[...]
        loss_ref[...] += 0.5 * jnp.sum(jnp.mean(e * e, axis=-1, keepdims=True))
        dy = e * (1.0 / D)
        dfg_ref[...] += jnp.sum(dy * nh, axis=0, keepdims=True)
        dnh = dy * fg
        dx_ref[...] = r * (dnh - nh * jnp.mean(dnh * nh, axis=-1, keepdims=True))

    row = pl.BlockSpec((tr, D), lambda i: (i, 0))
    vec = pl.BlockSpec((1, D), lambda i: (0, 0))
    tile = pl.BlockSpec((8, LANES), lambda i: (0, 0))
    return _pc(body, name=name, grid=(S // tr,), in_specs=[row, row, vec], out_specs=[row, vec, tile],
               out_shape=[SDS((S, D), F32), SDS((1, D), F32), SDS((8, LANES), F32)],
               compiler_params=_params(1))(x, target, fgain)


def _shift_down(v, k, rows):
    return jnp.where(rows >= k, pltpu.roll(v, k, axis=0), 0.0)


def _shift_up(v, k, rows):
    n = v.shape[0]
    return jnp.where(rows < n - k, pltpu.roll(v, n - k, axis=0), 0.0)


def _conv(uv, w, b, rows):
    return ((b + _shift_down(uv, 2, rows) * w[0:1]) + _shift_down(uv, 1, rows) * w[1:2]) + uv * w[2:3]


def _convgate_fwd(u, cw, cb, name):
    S, F2 = u.shape
    DFF = F2 // 2
    tc = _pick(DFF, (256, 128))
    sub = min(tc, LANES)
    nj = DFF // tc

    def body(ua_ref, ug_ref, wa_ref, wg_ref, ba_ref, bg_ref, o_ref):
        rows = lax.broadcasted_iota(jnp.int32, (S, sub), 0)
        for q in range(tc // sub):
            sl = slice(q * sub, (q + 1) * sub)
            ya = _conv(ua_ref[:, sl], wa_ref[:, sl], ba_ref[:, sl], rows)
            yg = _conv(ug_ref[:, sl], wg_ref[:, sl], bg_ref[:, sl], rows)
            o_ref[:, sl] = (yg * jax.nn.sigmoid(yg) * ya).astype(BF16)

    col = lambda off: pl.BlockSpec((S, tc), lambda j: (0, j + off))
    w3 = lambda off: pl.BlockSpec((3, tc), lambda j: (0, j + off))
    b1 = lambda off: pl.BlockSpec((1, tc), lambda j: (0, j + off))
    return _pc(body, name=name, grid=(nj,), in_specs=[col(0), col(nj), w3(0), w3(nj), b1(0), b1(nj)],
               out_specs=col(0), out_shape=SDS((S, DFF), BF16), compiler_params=_params(1))(u, u, cw, cw, cb, cb)


def _convgate_bwd(u, dact, cw, cb, name):
    S, F2 = u.shape
    DFF = F2 // 2
    tc = _pick(DFF, (256, 128))
    sub = min(tc, LANES)
    nj = DFF // tc

    def body(ua_ref, ug_ref, da_ref, wa_ref, wg_ref, ba_ref, bg_ref, du_ref, dcw_ref, dcb_ref):
        rows = lax.broadcasted_iota(jnp.int32, (S, sub), 0)
        for q in range(tc // sub):
            sl = slice(q * sub, (q + 1) * sub)
            ua, ug = ua_ref[:, sl], ug_ref[:, sl]
            wa, wg = wa_ref[:, sl], wg_ref[:, sl]
            ya = _conv(ua, wa, ba_ref[:, sl], rows)
            yg = _conv(ug, wg, bg_ref[:, sl], rows)
            s = jax.nn.sigmoid(yg)
            da = da_ref[:, sl]
            dya = da * (yg * s)
            dyg = da * ya * (s * (1.0 + yg * (1.0 - s)))
            for h, (dy, uv, w) in enumerate(((dya, ua, wa), (dyg, ug, wg))):
                du = (dy * w[2:3] + _shift_up(dy, 1, rows) * w[1:2]) + _shift_up(dy, 2, rows) * w[0:1]
                du_ref[h, :, sl] = du.astype(BF16)
                dcw_ref[h, 0:1, sl] = jnp.sum(dy * _shift_down(uv, 2, rows), axis=0, keepdims=True)
                dcw_ref[h, 1:2, sl] = jnp.sum(dy * _shift_down(uv, 1, rows), axis=0, keepdims=True)
                dcw_ref[h, 2:3, sl] = jnp.sum(dy * uv, axis=0, keepdims=True)
                dcb_ref[h, :, sl] = jnp.sum(dy, axis=0, keepdims=True)

    col = lambda off: pl.BlockSpec((S, tc), lambda j: (0, j + off))
    w3 = lambda off: pl.BlockSpec((3, tc), lambda j: (0, j + off))
    b1 = lambda off: pl.BlockSpec((1, tc), lambda j: (0, j + off))
    return _pc(body, name=name, grid=(nj,),
               in_specs=[col(0), col(nj), col(0), w3(0), w3(nj), b1(0), b1(nj)],
               out_specs=[pl.BlockSpec((2, S, tc), lambda j: (0, 0, j)), pl.BlockSpec((2, 3, tc), lambda j: (0, 0, j)),
                          pl.BlockSpec((2, 1, tc), lambda j: (0, 0, j))],
               out_shape=[SDS((2, S, DFF), BF16), SDS((2, 3, DFF), F32), SDS((2, 1, DFF), F32)],
               compiler_params=_params(1))(u, u, dact, cw, cw, cb, cb)


def _pool_diff(h, name):
    S, D = h.shape
    G = len(POOL_WINDOWS)
    CG = D // G
    tc = min(CG, LANES)
    per = CG // tc

    def body(h_ref, d_ref):
        g = pl.program_id(0)
        rows = lax.broadcasted_iota(jnp.int32, (S, tc), 0)
        for gi, w in enumerate(POOL_WINDOWS):
            @pl.when(g == gi)
            def _(w=w):
                hv = h_ref[...]
                s, k = hv, 1
                while k < w:
                    s = s + _shift_down(s, k, rows)
                    k *= 2
                count = jnp.minimum(rows + 1, w).astype(F32)
                d_ref[...] = (s / count - hv).astype(BF16)

    spec = pl.BlockSpec((S, tc), lambda g, j: (0, g * per + j))
    return _pc(body, name=name, grid=(G, per), in_specs=[spec], out_specs=spec, out_shape=SDS((S, D), BF16),
               compiler_params=_params(2))(h)


def _pool_diff_bwd(dd, name):
    S, D = dd.shape
    G = len(POOL_WINDOWS)
    CG = D // G
    tc = min(CG, LANES)
    per = CG // tc

    def body(dd_ref, o_ref):
        g = pl.program_id(0)
        rows = lax.broadcasted_iota(jnp.int32, (S, tc), 0)
        for gi, w in enumerate(POOL_WINDOWS):
            @pl.when(g == gi)
            def _(w=w):
                dv = dd_ref[...]
                count = jnp.minimum(rows + 1, w).astype(F32)
                s, k = dv / count, 1
                while k < w:
                    s = s + _shift_up(s, k, rows)
                    k *= 2
                o_ref[...] = s - dv

    spec = pl.BlockSpec((S, tc), lambda g, j: (0, g * per + j))
    return _pc(body, name=name, grid=(G, per), in_specs=[spec], out_specs=spec, out_shape=SDS((S, D), F32),
               compiler_params=_params(2))(dd)


def _pool_mm(d, w, res, gate, name):
    S, D = d.shape
    G, CG, _ = w.shape
    tm = _pick(S, (512, 256, 128))

    def body(d_ref, w_ref, res_ref, gate_ref, o_ref, e_ref):
        acc = jnp.dot(d_ref[...], w_ref[...], preferred_element_type=F32)
        o_ref[...] = res_ref[...] + gate_ref[...] * acc
        e_ref[...] = acc.astype(BF16)

    blk = pl.BlockSpec((tm, CG), lambda g, i: (i, g))
    return _pc(body, name=name, grid=(G, S // tm),
               in_specs=[blk, pl.BlockSpec((None, CG, CG), lambda g, i: (g, 0, 0)), blk,
                         pl.BlockSpec((1, CG), lambda g, i: (0, g))],
               out_specs=[blk, blk], out_shape=[SDS((S, D), F32), SDS((S, D), BF16)],
               compiler_params=_params(2))(d, w, res, gate)


def _pool_mm_bwd(de, d, w, name):
    S, D = de.shape
    G, CG, _ = w.shape
    tm = _pick(S, (512, 256, 128))
    ns = S // tm

    def body(de_ref, d_ref, w_ref, dd_ref, dw_ref, acc_ref):
        i = pl.program_id(1)

        @pl.when(i == 0)
        def _():
            acc_ref[...] = jnp.zeros_like(acc_ref)

        dev = de_ref[...]
        dd_ref[...] = lax.dot_general(dev, w_ref[...], (((1,), (1,)), ((), ())), preferred_element_type=F32)
        acc_ref[...] += lax.dot_general(d_ref[...], dev, (((0,), (0,)), ((), ())), preferred_element_type=F32)

        @pl.when(i == ns - 1)
        def _():
            dw_ref[...] = acc_ref[...].astype(BF16)

    blk = pl.BlockSpec((tm, CG), lambda g, i: (i, g))
    wsp = pl.BlockSpec((None, CG, CG), lambda g, i: (g, 0, 0))
    return _pc(body, name=name, grid=(G, ns), in_specs=[blk, blk, wsp], out_specs=[blk, wsp],
               out_shape=[SDS((S, D), F32), SDS((G, CG, CG), BF16)], scratch_shapes=[pltpu.VMEM((CG, CG), F32)],
               compiler_params=_params(2))(de, d, w)


def _log_sigmoid(z):
    return jnp.minimum(z, 0.0) - jnp.log(1.0 + jnp.exp(-jnp.abs(z)))


def _dot2(a, tri):
    hi = a.astype(BF16)
    lo = (a - hi.astype(F32)).astype(BF16)
    return jnp.dot(hi, tri, preferred_element_type=F32) + jnp.dot(lo, tri, preferred_element_type=F32)


_NT = (((1,), (1,)), ((), ()))
_TN = (((0,), (0,)), ((), ()))


def _forget_cumsum(flog, bf, name):
    S, W = flog.shape
    tb = _pick(S, (128,))

    def body(f_ref, b_ref, o_ref):
        r = lax.broadcasted_iota(jnp.int32, (tb, tb), 0)
        c = lax.broadcasted_iota(jnp.int32, (tb, tb), 1)
        tri = (c <= r).astype(F32)
        carry = jnp.zeros((1, W), F32)
        for q in range(S // tb):
            ls = _log_sigmoid(f_ref[q * tb:(q + 1) * tb, :] + b_ref[...])
            o_ref[q * tb:(q + 1) * tb, :] = carry + jnp.dot(tri, ls, preferred_element_type=F32,
                                                            precision=lax.Precision.HIGHEST)
            carry = carry + jnp.sum(ls, axis=0, keepdims=True)

    return _pc(body, name=name, out_shape=SDS((S, W), F32))(flog, bf)


def _forget_cumsum_bwd(dF, flog, bf, name):
    S, W = flog.shape
    tb = _pick(S, (128,))

    def body(d_ref, f_ref, b_ref, o_ref, db_ref):
        r = lax.broadcasted_iota(jnp.int32, (tb, tb), 0)
        c = lax.broadcasted_iota(jnp.int32, (tb, tb), 1)
        tri = (c >= r).astype(F32)
        carry = jnp.zeros((1, W), F32)
        db = jnp.zeros((1, W), F32)
        for q in reversed(range(S // tb)):
            dv = d_ref[q * tb:(q + 1) * tb, :]
            dls = carry + jnp.dot(tri, dv, preferred_element_type=F32, precision=lax.Precision.HIGHEST)
            carry = carry + jnp.sum(dv, axis=0, keepdims=True)
            dfl = dls * jax.nn.sigmoid(-(f_ref[q * tb:(q + 1) * tb, :] + b_ref[...]))
            o_ref[q * tb:(q + 1) * tb, :] = dfl
            db = db + jnp.sum(dfl, axis=0, keepdims=True)
        db_ref[...] = db

    return _pc(body, name=name, out_shape=[SDS((S, W), F32), SDS((1, W), F32)])(dF, flog, bf)


def _heads_per_step(n_heads):
    return 4 if n_heads % 4 == 0 else 2 if n_heads % 2 == 0 else 1


def _sb_fwd(qkv, NH, NSB, HD, T, name):
    S = qkv.shape[0]
    nq = S // T
    scale = HD ** -0.5
    HB = _heads_per_step(NSB)
    W = HB * HD

    def body(q_ref, k_ref, v_ref, o_ref, tot_ref):
        i = pl.program_id(1)
        row = lax.broadcasted_iota(jnp.int32, (T, T), 0)
        col = lax.broadcasted_iota(jnp.int32, (T, T), 1)
        upper = (row > col).astype(BF16)
        heads = [slice(hh * HD, (hh + 1) * HD) for hh in range(HB)]
        qs = [q_ref[:, cs] for cs in heads]

        def blk(kb, state, diag):
            sl = pl.ds(pl.multiple_of(kb * T, T), T)
            out = []
            for hh, cs in enumerate(heads):
                carry, acc = state[2 * hh], state[2 * hh + 1]
                k, v = k_ref[sl, cs], v_ref[sl, cs]
                z = lax.dot_general(qs[hh], k, _NT, preferred_element_type=F32) * scale
                ls = _log_sigmoid(z)
                lr = ls - z
                if diag:
                    lr = jnp.where(col < row, lr, 0.0)
                rest = _dot2(lr, upper) + carry
                w = jnp.exp(ls + rest)
                if diag:
                    w = jnp.where(col < row, w, 0.0)
                out += [carry + jnp.sum(lr, axis=1, keepdims=True),
                        acc + jnp.dot(w.astype(BF16), v, preferred_element_type=F32)]
            return tuple(out)

        state = blk(i, (jnp.zeros((T, 1), F32), jnp.zeros((T, HD), F32)) * HB, True)
        state = lax.fori_loop(0, i, lambda jj, st: blk(i - 1 - jj, st, False), state)
        for hh, cs in enumerate(heads):
            o_ref[:, cs] = state[2 * hh + 1].astype(BF16)
            tot_ref[hh] = state[2 * hh]

    return _pc(body, name=name, grid=(NSB // HB, nq),
               in_specs=[pl.BlockSpec((T, W), lambda h, i: (i, h)),
                         pl.BlockSpec((S, W), lambda h, i: (0, NH // HB + h)),
                         pl.BlockSpec((S, W), lambda h, i: (0, 2 * NH // HB + h))],
               out_specs=[pl.BlockSpec((T, W), lambda h, i: (i, h)), pl.BlockSpec((HB, T, 1), lambda h, i: (h, i, 0))],
               out_shape=[SDS((S, NSB * HD), BF16), SDS((NSB, S, 1), F32)],
               compiler_params=_params(2))(qkv, qkv, qkv)


def _sb_bwd(qkv, do, tot, NH, NSB, HD, T, name):
    S = qkv.shape[0]
    nq = S // T
    scale = HD ** -0.5
    HB = _heads_per_step(NSB)
    W = HB * HD

    def body(q_ref, k_ref, v_ref, do_ref, tot_ref, dq_ref, dk_ref, dv_ref, dk_acc, dv_acc):
        i = pl.program_id(1)

        @pl.when(i == 0)
        def _():
            dk_acc[...] = jnp.zeros_like(dk_acc)
            dv_acc[...] = jnp.zeros_like(dv_acc)

        row = lax.broadcasted_iota(jnp.int32, (T, T), 0)
        col = lax.broadcasted_iota(jnp.int32, (T, T), 1)
        incl = (row <= col).astype(BF16)
        strict = (row < col).astype(BF16)
        heads = [slice(hh * HD, (hh + 1) * HD) for hh in range(HB)]
        qs = [q_ref[:, cs] for cs in heads]
        dos = [do_ref[:, cs] for cs in heads]
        tots = [tot_ref[hh] for hh in range(HB)]

        def blk(kb, state, diag):
            sl = pl.ds(pl.multiple_of(kb * T, T), T)
            out = []
            for hh, cs in enumerate(heads):
                cl, cg, dq = state[3 * hh], state[3 * hh + 1], state[3 * hh + 2]
                q, do_ = qs[hh], dos[hh]
                k, v = k_ref[sl, cs], v_ref[sl, cs]
                z = lax.dot_general(q, k, _NT, preferred_element_type=F32) * scale
                ls = _log_sigmoid(z)
                lr = ls - z
                if diag:
                    lr = jnp.where(col < row, lr, 0.0)
                rest = tots[hh] - (cl + _dot2(lr, incl))
                w = jnp.exp(ls + rest)
                if diag:
                    w = jnp.where(col < row, w, 0.0)
                g = lax.dot_general(do_, v, _NT, preferred_element_type=F32) * w
                dv_acc[sl, cs] += lax.dot_general(w.astype(BF16), do_, _TN, preferred_element_type=F32)
                dlr = cg + _dot2(g, strict)
                dz = g * jnp.exp(lr) - dlr * jnp.exp(ls)
                if diag:
                    dz = jnp.where(col < row, dz, 0.0)
                dzb = (dz * scale).astype(BF16)
                dk_acc[sl, cs] += lax.dot_general(dzb, q, _TN, preferred_element_type=F32)
                out += [cl + jnp.sum(lr, axis=1, keepdims=True), cg + jnp.sum(g, axis=1, keepdims=True),
                        dq + jnp.dot(dzb, k, preferred_element_type=F32)]
            return tuple(out)

        zero = jnp.zeros((T, 1), F32)
        state = lax.fori_loop(0, i, lambda kb, st: blk(kb, st, False), (zero, zero, jnp.zeros((T, HD), F32)) * HB)
        state = blk(i, state, True)
        for hh, cs in enumerate(heads):
            dq_ref[:, cs] = state[3 * hh + 2].astype(BF16)

        @pl.when(i == nq - 1)
        def _():
            dk_ref[...] = dk_acc[...].astype(BF16)
            dv_ref[...] = dv_acc[...].astype(BF16)

    qblk = pl.BlockSpec((T, W), lambda h, i: (i, h))
    full = pl.BlockSpec((S, W), lambda h, i: (0, h))
    return _pc(body, name=name, grid=(NSB // HB, nq),
               in_specs=[qblk, pl.BlockSpec((S, W), lambda h, i: (0, NH // HB + h)),
                         pl.BlockSpec((S, W), lambda h, i: (0, 2 * NH // HB + h)), qblk,
                         pl.BlockSpec((HB, T, 1), lambda h, i: (h, i, 0))],
               out_specs=[qblk, full, full],
               out_shape=[SDS((S, NSB * HD), BF16)] * 3,
               scratch_shapes=[pltpu.VMEM((S, W), F32), pltpu.VMEM((S, W), F32)],
               compiler_params=_params(2))(qkv, qkv, qkv, do, tot)


def _fox_fwd(qkv, fcol, frow, NH, NSB, HD, T, name):
    S = qkv.shape[0]
    NFX = NH - NSB
    nq = S // T
    scale = HD ** -0.5
    HB = _heads_per_step(NFX) if NSB % _heads_per_step(NFX) == 0 else 1
    W = HB * HD

    def body(q_ref, k_ref, v_ref, fq_ref, fk_ref, o_ref, o32_ref, lse_ref):
        i = pl.program_id(1)
        row = lax.broadcasted_iota(jnp.int32, (T, T), 0)
        col = lax.broadcasted_iota(jnp.int32, (T, T), 1)
        heads = [slice(hh * HD, (hh + 1) * HD) for hh in range(HB)]
        qs = [q_ref[:, cs] for cs in heads]
        fqs = [fq_ref[hh] for hh in range(HB)]

        def blk(kb, state, diag):
            sl = pl.ds(pl.multiple_of(kb * T, T), T)
            out = []
            for hh, cs in enumerate(heads):
                m, l, acc, rem = state[4 * hh:4 * hh + 4]
                k, v = k_ref[sl, cs], v_ref[sl, cs]
                s = lax.dot_general(qs[hh], k, _NT, preferred_element_type=F32) * scale + (fqs[hh] - fk_ref[hh, kb])
                if diag:
                    s = jnp.where(col <= row, s, NEG_BIG)
                m_new = jnp.maximum(m, jnp.max(s, axis=1, keepdims=True))
                p = jnp.exp(s - m_new)
                alpha = jnp.exp(m - m_new)
                hi = p.astype(BF16)
                lo = (p - hi.astype(F32)).astype(BF16)
                out += [m_new, alpha * l + jnp.sum(p, axis=1, keepdims=True),
                        alpha * acc + jnp.dot(hi, v, preferred_element_type=F32),
                        alpha * rem + jnp.dot(lo, v, preferred_element_type=F32)]
            return tuple(out)

        zero = jnp.zeros((T, HD), F32)
        state = blk(i, (jnp.full((T, 1), NEG_BIG, F32), jnp.zeros((T, 1), F32), zero, zero) * HB, True)
        state = lax.fori_loop(0, i, lambda kb, st: blk(kb, st, False), state)
        for hh, cs in enumerate(heads):
            m, l, acc, rem = state[4 * hh:4 * hh + 4]
            o_ref[:, cs] = (acc / l).astype(BF16)
            o32_ref[:, cs] = (acc + rem) / l
            lse_ref[hh] = m + jnp.log(l)

    vec = pl.BlockSpec((HB, T, 1), lambda h, i: (h, i, 0))
    oblk = pl.BlockSpec((T, W), lambda h, i: (i, h))
    return _pc(body, name=name, grid=(NFX // HB, nq),
               in_specs=[pl.BlockSpec((T, W), lambda h, i: (i, NSB // HB + h)),
                         pl.BlockSpec((S, W), lambda h, i: (0, (NH + NSB) // HB + h)),
                         pl.BlockSpec((S, W), lambda h, i: (0, (2 * NH + NSB) // HB + h)),
                         vec, pl.BlockSpec((HB, nq, 1, T), lambda h, i: (h, 0, 0, 0))],
               out_specs=[oblk, oblk, vec],
               out_shape=[SDS((S, NFX * HD), BF16), SDS((S, NFX * HD), F32), SDS((NFX, S, 1), F32)],
               compiler_params=_params(2))(qkv, qkv, qkv, fcol, frow)


def _fox_bwd(qkv, do, o, fcol, frow, lse, NH, NSB, HD, T, name):
    S = qkv.shape[0]
    NFX = NH - NSB
    nq = S // T
    scale = HD ** -0.5
    HB = _heads_per_step(NFX) if NSB % _heads_per_step(NFX) == 0 else 1
    W = HB * HD

    def body(q_ref, k_ref, v_ref, do_ref, o_ref, fq_ref, fk_ref, lse_ref, dq_ref, dk_ref, dv_ref, dfk_ref,
             dk_acc, dv_acc, dfk_acc):
        i = pl.program_id(1)

        @pl.when(i == 0)
        def _():
            dk_acc[...] = jnp.zeros_like(dk_acc)
            dv_acc[...] = jnp.zeros_like(dv_acc)
            dfk_acc[...] = jnp.zeros_like(dfk_acc)

        row = lax.broadcasted_iota(jnp.int32, (T, T), 0)
        col = lax.broadcasted_iota(jnp.int32, (T, T), 1)
        heads = [slice(hh * HD, (hh + 1) * HD) for hh in range(HB)]
        qs = [q_ref[:, cs] for cs in heads]
        dos = [do_ref[:, cs] for cs in heads]
        fqs = [fq_ref[hh] for hh in range(HB)]
        lses = [lse_ref[hh] for hh in range(HB)]
        deltas = [jnp.sum(dos[hh].astype(F32) * o_ref[:, cs], axis=1, keepdims=True) for hh, cs in enumerate(heads)]

        def blk(kb, dqs, diag):
            sl = pl.ds(pl.multiple_of(kb * T, T), T)
            out = []
            for hh, cs in enumerate(heads):
                q, do_ = qs[hh], dos[hh]
                k, v = k_ref[sl, cs], v_ref[sl, cs]
                s = lax.dot_general(q, k, _NT, preferred_element_type=F32) * scale + (fqs[hh] - fk_ref[hh, kb])
                p = jnp.exp(s - lses[hh])
                if diag:
                    p = jnp.where(col <= row, p, 0.0)
                ds = p * (lax.dot_general(do_, v, _NT, preferred_element_type=F32) - deltas[hh])
                dv_acc[sl, cs] += lax.dot_general(p.astype(BF16), do_, _TN, preferred_element_type=F32)
                dsb = (ds * scale).astype(BF16)
                dk_acc[sl, cs] += lax.dot_general(dsb, q, _TN, preferred_element_type=F32)
                dfk_acc[hh, kb] -= jnp.sum(ds, axis=0, keepdims=True)
                out.append(dqs[hh] + jnp.dot(dsb, k, preferred_element_type=F32))
            return tuple(out)

        dqs = lax.fori_loop(0, i, lambda kb, st: blk(kb, st, False), (jnp.zeros((T, HD), F32),) * HB)
        dqs = blk(i, dqs, True)
        for hh, cs in enumerate(heads):
            dq_ref[:, cs] = dqs[hh].astype(BF16)

        @pl.when(i == nq - 1)
        def _():
            dk_ref[...] = dk_acc[...].astype(BF16)
            dv_ref[...] = dv_acc[...].astype(BF16)
            dfk_ref[...] = dfk_acc[...]

    vec = pl.BlockSpec((HB, T, 1), lambda h, i: (h, i, 0))
    rowv = pl.BlockSpec((HB, nq, 1, T), lambda h, i: (h, 0, 0, 0))
    oblk = pl.BlockSpec((T, W), lambda h, i: (i, h))
    qblk = pl.BlockSpec((T, W), lambda h, i: (i, NSB // HB + h))
    full = pl.BlockSpec((S, W), lambda h, i: (0, h))
    return _pc(body, name=name, grid=(NFX // HB, nq),
               in_specs=[qblk, pl.BlockSpec((S, W), lambda h, i: (0, (NH + NSB) // HB + h)),
                         pl.BlockSpec((S, W), lambda h, i: (0, (2 * NH + NSB) // HB + h)),
                         qblk, oblk, vec, rowv, vec],
               out_specs=[oblk, full, full, rowv],
               out_shape=[SDS((S, NFX * HD), BF16)] * 3 + [SDS((NFX, nq, 1, T), F32)],
               scratch_shapes=[pltpu.VMEM((S, W), F32), pltpu.VMEM((S, W), F32), pltpu.VMEM((HB, nq, 1, T), F32)],
               compiler_params=_params(2))(qkv, qkv, qkv, do, o, fcol, frow, lse)


def _silu(c_all, name):
    def body(c_ref, o_ref):
        cv = c_ref[...]
        o_ref[...] = cv * jax.nn.sigmoid(cv)

    return _pc(body, name=name, out_shape=SDS(c_all.shape, F32))(c_all)


def _mod_project(cond, w_mod, name):
    L, D, C = w_mod.shape
    tk = _pick(D, (512, 256, 128))

    def body(c_ref, w_ref, o_ref):
        @pl.when(pl.program_id(1) == 0)
        def _():
            o_ref[...] = jnp.zeros_like(o_ref)

        o_ref[...] += jnp.dot(c_ref[...].astype(BF16), w_ref[...].astype(BF16), preferred_element_type=F32)

    return _pc(body, name=name, grid=(L, D // tk),
               in_specs=[pl.BlockSpec((16, tk), lambda l, k: (0, k)), pl.BlockSpec((None, tk, C), lambda l, k: (l, k, 0))],
               out_specs=pl.BlockSpec((None, 16, C), lambda l, k: (l, 0, 0)),
               out_shape=SDS((L, 16, C), F32), compiler_params=_params(2))(cond, w_mod)


def _adam_math(w, g, m, v):
    m = ADAM_B1 * m + (1.0 - ADAM_B1) * g
    v = ADAM_B2 * v + (1.0 - ADAM_B2) * (g * g)
    m_hat = m / (1.0 - ADAM_B1 ** ADAM_STEP)
    v_hat = v / (1.0 - ADAM_B2 ** ADAM_STEP)
    delta = -ADAM_LR * (m_hat / (jnp.sqrt(v_hat) + ADAM_EPS) + ADAM_WD * w)
    return delta, m, v


def _adamw(w, m, v, parts, layer, prev, name):
    L, R, C = w.shape
    NP = parts.shape[0]
    tr = _pick(R, (128, 64, 88, 32, 16, 8))
    nprev = 0 if prev is None else 4

    def body(w_ref, m_ref, v_ref, p_ref, *rest):
        g_ref, d_ref, mo_ref, vo_ref = rest[nprev:]
        g = p_ref[0].astype(F32)
        for j in range(1, NP):
            g = g + p_ref[j].astype(F32)
        delta, mn, vn = _adam_math(w_ref[...], g, m_ref[...], v_ref[...])
        g_ref[...] = g
        d_ref[...] = delta
        mo_ref[...] = mn
        vo_ref[...] = vn

    if tr == R and R > 512:
        tc = _pick(C, (256, 128))
        steps = C // tc
        blk = pl.BlockSpec((None, R, tc), lambda j: (layer, 0, j))
        in_specs = [blk, blk, blk, pl.BlockSpec((NP, R, tc), lambda j: (0, 0, j))]
    else:
        steps = R // tr
        blk = pl.BlockSpec((None, tr, C), lambda i: (layer, i, 0))
        in_specs = [blk, blk, blk, pl.BlockSpec((NP, tr, C), lambda i: (0, i, 0))]
    operands = [w, m, v, parts]
    aliases = {}
    if prev is not None:
        in_specs += [ANY] * 4
        operands += list(prev)
        aliases = {4 + q: q for q in range(4)}
    return _pc(body, name=name, grid=(steps,), in_specs=in_specs, out_specs=[blk] * 4,
               out_shape=[SDS(w.shape, F32)] * 4, input_output_aliases=aliases,
               compiler_params=_params(1))(*operands)


def _adamw_mod(w, m, v, cond_t, dmod, name):
    L, D, C = w.shape
    tr = _pick(D, (128, 64))

    def body(w_ref, m_ref, v_ref, ct_ref, dm_ref, g_ref, d_ref, mo_ref, vo_ref):
        ct = ct_ref[...]
        g = ct[:, 0:1] * dm_ref[0]
        for b in range(1, NDEV):
            g = g + ct[:, b:b + 1] * dm_ref[b]
        delta, mn, vn = _adam_math(w_ref[...], g, m_ref[...], v_ref[...])
        g_ref[...] = g
        d_ref[...] = delta
        mo_ref[...] = mn
        vo_ref[...] = vn

    blk = pl.BlockSpec((None, tr, C), lambda l, i: (l, i, 0))
    return _pc(body, name=name, grid=(L, D // tr),
               in_specs=[blk, blk, blk, pl.BlockSpec((tr, LANES), lambda l, i: (i, 0)),
                         pl.BlockSpec((NDEV, None, 1, C), lambda l, i: (0, l, 0, 0))],
               out_specs=[blk] * 4, out_shape=[SDS(w.shape, F32)] * 4, compiler_params=_params(2))(w, m, v, cond_t, dmod)


def _sum_parts(parts, name):
    NP, R, C = parts.shape
    tr = _pick(R, (256, 128, 64, 32, 16, 8))

    def body(p_ref, o_ref):
        g = p_ref[0]
        for j in range(1, NP):
            g = g + p_ref[j]
        o_ref[...] = g

    return _pc(body, name=name, grid=(R // tr,), in_specs=[pl.BlockSpec((NP, tr, C), lambda i: (0, i, 0))],
               out_specs=pl.BlockSpec((tr, C), lambda i: (i, 0)), out_shape=SDS((R, C), F32),
               compiler_params=_params(1))(parts)


def _pack(vecs, rows=None):
    flat = jnp.concatenate([v.reshape(-1).astype(F32) for v in vecs])
    n = flat.shape[0]
    r = rows if rows is not None else -(-n // (256 * LANES)) * 256
    return jnp.pad(flat, (0, r * LANES - n)).reshape(r, LANES)


def _unpack(packed, shapes):
    flat = packed.reshape(-1)
    out, off = [], 0
    for s in shapes:
        n = 1
        for d in s:
            n *= d
        out.append(flat[off:off + n].reshape(s))
        off += n
    return out


def kernel(x, c, w_mod, b_mod, norm_gain, w_attn_in, b_forget, w_attn_out, w_pool, pool_scale, w_up, conv_w, conv_b, w_down, final_gain, loss_target, m_w_mod, m_b_mod, m_norm_gain, m_w_attn_in, m_b_forget, m_w_attn_out, m_w_pool, m_pool_scale, m_w_up, m_conv_w, m_conv_b, m_w_down, m_final_gain, v_w_mod, v_b_mod, v_norm_gain, v_w_attn_in, v_b_forget, v_w_attn_out, v_w_pool, v_pool_scale, v_w_up, v_conv_w, v_conv_b, v_w_down, v_final_gain):
    _, S, D = x.shape
    L = w_mod.shape[0]
    CM = w_mod.shape[2]
    NFX = b_forget.shape[1]
    NH = 2 * NFX
    NSB = NH - NFX
    HD = D // NH
    CI = w_attn_in.shape[2]
    CU = w_up.shape[2]
    F2 = NDEV * CU
    DFF = F2 // 2
    G = len(POOL_WINDOWS)
    CG = D // G
    T = _pick(S, (256, 128))
    me = _idx(_me())
    x0 = x[0]
    target = loss_target[0]

    def layer_shards(l, group):
        if group == 1:
            shards = [w_up[l], w_down[l]]
        elif l % 2 == 0:
            shards = [w_attn_in[l // 2].T, w_attn_out[l // 2]]
        else:
            shards = [w_pool[l // 2].reshape(G * (CG // NDEV), CG)]
        return [s.astype(BF16) for s in shards]

    small_shapes = [(1, D), norm_gain.shape, pool_scale.shape, conv_w.shape]
    small_all = _all_gather([_pack([c, norm_gain, pool_scale, conv_w])], "gather_small")[0]
    per_dev = [_unpack(small_all[j], small_shapes) for j in range(NDEV)]
    c_all = jnp.concatenate([p[0] for p in per_dev] + [jnp.zeros((16 - NDEV, D), F32)], axis=0)
    gain_f = jnp.concatenate([p[1] for p in per_dev], axis=2)
    pscale_f = jnp.concatenate([p[2] for p in per_dev], axis=1)
    convw_f = jnp.concatenate([p[3] for p in per_dev], axis=2)

    cond_all = _silu(c_all, "cond_silu")
    mod_part = _mod_project(cond_all, w_mod, "mod_project")
    mod_all = _all_gather([mod_part], "gather_mod")[0]
    mod = lax.dynamic_index_in_dim(mod_all, me, axis=2, keepdims=False)
    mod = mod.transpose(1, 0, 2).reshape(L, NDEV * CM) + b_mod
    mods = mod.reshape(L, 6, 1, D)

    inflight = {}

    def gather_start(l, group, after):
        inflight[l, group], t = _gather_start(layer_shards(l, group), me, after, f"gather{l}_{group}_start")
        return t

    def gather_forward(l, group, after):
        inflight[l, group], t = _gather_forward(inflight[l, group], after, f"gather{l}_{group}_forward")
        return t

    def gather_wait(l, group, after):
        return _gather_wait(inflight.pop((l, group)), after, f"gather{l}_{group}_wait")

    def starts_at(l):
        if l % 2 == 0:
            return [(k, g) for k, g in [(l + 1, 0), (l + 1, 1), (l + 2, 0)] if k < L]
        return [(k, g) for k, g in [(l + 1, 1)] if k < L]

    tok = gather_start(0, 0, [mods])
    tok = gather_start(0, 1, [tok])
    tok = gather_forward(0, 0, [tok])
    mixer_w = gather_wait(0, 0, [tok])

    saved = []
    xl = x0
    for l in range(L):
        i = l // 2
        sh1, sc1, g1, sh2, sc2, g2 = [mods[l, q] for q in range(6)]
        gn1, gn2 = gain_f[l, 0:1], gain_f[l, 1:2]
        st = {"x": xl}
        after = [mixer_w[0]]
        for k, g in starts_at(l):
            after = [gather_start(k, g, after)]
        if starts_at(l):
            sh1 = sh1 + after[0][0, 0]
        if l % 2 == 0:
            win_t = mixer_w[0].reshape(NDEV * CI, D)
            wf_t = jnp.pad(win_t[3 * D:], ((0, LANES - NFX), (0, 0)))
            wout = mixer_w[1].reshape(D, D)
            h1 = _norm_mod(xl, gn1, sc1, sh1, BF16, f"norm1_{l}")
            qkv = _matmul(h1, win_t, mode="nt", name=f"qkv_{l}", out_dtype=BF16, n=3 * D)
            flog = _matmul(h1, wf_t, mode="nt", name=f"flog_{l}", out_dtype=F32, tn=LANES)
            bfp = jnp.pad(b_forget[i], (0, LANES - NFX)).reshape(1, LANES)
            Fc = _forget_cumsum(flog, bfp, f"fcum_{l}")
            f8 = Fc[:, :NFX].T
            fcol, frow = f8[:, :, None], f8.reshape(NFX, S // T, 1, T)
            o_sb, tot = _sb_fwd(qkv, NH, NSB, HD, T, f"sb_fwd_{l}")
            o_fx, o_fx32, lse = _fox_fwd(qkv, fcol, frow, NH, NSB, HD, T, f"fox_fwd_{l}")
            o = jnp.concatenate([o_sb, o_fx], axis=1)
            g1 = g1 + gather_forward(l, 1, [o])[0, 0]
            x1, y1 = _matmul(o, wout, mode="nn", name=f"attn_out_{l}", out_dtype=F32, res=xl, gate=g1, y_dtype=BF16)
            st.update(h1=h1, qkv=qkv, flog=flog, bfp=bfp, fcol=fcol, frow=frow, tot=tot, lse=lse, o=o, o_fx=o_fx32,
                      win_t=win_t, wf_t=wf_t, wout=wout, y1=y1)
        else:
            wpool = mixer_w[0].reshape(NDEV, G, CG // NDEV, CG).transpose(1, 0, 2, 3).reshape(G, CG, CG)
            h1 = _norm_mod(xl, gn1, sc1, sh1, F32, f"norm1_{l}")
            dpool = _pool_diff(h1, f"pool_diff_{l}")
            g1 = g1 + gather_forward(l, 1, [dpool])[0, 0]
            gp = g1 * pscale_f[i:i + 1]
            x1, e1 = _pool_mm(dpool, wpool, xl, gp, f"pool_mm_{l}")
            st.update(dpool=dpool, wpool=wpool, gp=gp, y1=e1)
        wup_g, wdown_g = gather_wait(l, 1, [x1])
        wdown_f = wdown_g.reshape(DFF, D)
        h2 = _norm_mod(x1, gn2, sc2, sh2, BF16, f"norm2_{l}")
        u = _matmul(h2, wup_g, mode="nn", name=f"ffn_up_{l}", out_dtype=F32, b_blocked=True)
        cb = conv_b[l].reshape(1, F2)
        act = _convgate_fwd(u, convw_f[l], cb, f"convgate_{l}")
        if l + 1 < L:
            g2 = g2 + gather_forward(l + 1, 0, [act])[0, 0]
        x2, y2 = _matmul(act, wdown_f, mode="nn", name=f"ffn_down_{l}", out_dtype=F32, res=x1, gate=g2, y_dtype=BF16,
                         tk=CU)
        if l + 1 < L:
            mixer_w = gather_wait(l + 1, 0, [x2])
        st.update(x1=x1, h2=h2, u=u, cb=cb, act=act, y2=y2, wup_g=wup_g, wdown_f=wdown_f,
                  mod=(sh1, sc1, g1, sh2, sc2, g2), gn=(gn1, gn2))
        saved.append(st)
        xl = x2

    dx, d_fgain, loss_tile = _loss_head(xl, target, final_gain.reshape(1, D), "loss_head")
    loss = lax.psum(loss_tile[0, 0], ("x", "y", "c"))

    dmod_rows = [None] * L
    d_gain = [None] * L
    d_convw = [None] * L
    d_convb = [None] * L
    d_pscale = [None] * (L // 2)
    d_bf = [None] * ((L + 1) // 2)
    big = {"w_up": None, "w_down": None, "w_attn_out": None, "w_pool": None}
    attn_in_t = [None] * ((L + 1) // 2)

    def update(key, w, m, v, bufs, layer, tag):
        big[key] = _adamw(w, m, v, bufs, layer, big[key], f"adamw_{tag}")

    exchanges = []

    def exchange_start(l, group, parts, after):
        scat, t = _scatter_start(parts, me, after, f"scatter{l}_{group}_start")
        exchanges.append((l, group, scat))
        return t

    def exchanges_finish(after, first_layer):
        for entry in [e for e in exchanges if e[0] >= first_layer]:
            exchanges.remove(entry)
            pl_, group, scat = entry
            bufs = _scatter_wait(scat, after, f"scatter{pl_}_{group}_wait")
            if group == 1:
                update("w_up", w_up, m_w_up, v_w_up, bufs[0], pl_, f"up_{pl_}")
                update("w_down", w_down, m_w_down, v_w_down, bufs[1], pl_, f"down_{pl_}")
            elif pl_ % 2 == 0:
                slab = lambda a: a[pl_ // 2].T[None]
                attn_in_t[pl_ // 2] = _adamw(slab(w_attn_in), slab(m_w_attn_in), slab(v_w_attn_in), bufs[0], 0, None,
                                             f"adamw_attn_in_{pl_}")
                update("w_attn_out", w_attn_out, m_w_attn_out, v_w_attn_out, bufs[1], pl_ // 2, f"attn_out_{pl_}")
            else:
                wp3 = lambda a: a.reshape(a.shape[0], G * (CG // NDEV), CG)
                update("w_pool", wp3(w_pool), wp3(m_w_pool), wp3(v_w_pool), bufs[0], pl_ // 2, f"pool_{pl_}")

    tok = None
    for l in reversed(range(L)):
        i = l // 2
        st = saved[l]
        sh1, sc1, g1, sh2, sc2, g2 = st["mod"]
        if tok is not None:
            g2 = g2 + tok[0, 0]
        gn1, gn2 = st["gn"]
        dffn, dg2 = _gate_bwd(dx, st["y2"], g2, f"gate2_bwd_{l}")
        dact = _matmul(dffn, st["wdown_f"], mode="nt", name=f"ffn_down_dx_{l}", out_dtype=F32, tn=CU)
        dwdown = _matmul(st["act"], dffn, mode="tn", name=f"ffn_down_dw_{l}", out_dtype=BF16, tm=CU)
        du, dcw, dcb = _convgate_bwd(st["u"], dact, convw_f[l], st["cb"], f"convgate_bwd_{l}")
        dh2 = _matmul(du, st["wup_g"], mode="nt", name=f"ffn_up_dx_{l}", out_dtype=F32, a_split=True, b_blocked=True,
                      tn=2048)
        dwup = _matmul(st["h2"], du, mode="tn", name=f"ffn_up_dw_{l}", out_dtype=BF16, tn=CU, b_split=True,
                       out_blocked=True)
        dx, dgn2, dsc2, dsh2 = _norm_mod_bwd(dh2, st["x1"], dx, gn2, sc2, f"norm2_bwd_{l}")
        d_convw[l] = jnp.concatenate([dcw[0], dcw[1]], axis=1)
        d_convb[l] = jnp.concatenate([dcb[0], dcb[1]], axis=1)
        tok = exchange_start(l, 1, [dwup, dwdown.reshape(NDEV, DFF // NDEV, D)], [])
        if l % 2 == 0:
            dy1, dg1 = _gate_bwd(dx, st["y1"], g1 + tok[0, 0], f"gate1_bwd_{l}")
            do = _matmul(dy1, st["wout"], mode="nt", name=f"attn_out_dx_{l}", out_dtype=BF16)
            dwout = _matmul(st["o"], dy1, mode="tn", name=f"attn_out_dw_{l}", out_dtype=BF16)
            dq_s, dk_s, dv_s = _sb_bwd(st["qkv"], do, st["tot"], NH, NSB, HD, T, f"sb_bwd_{l}")
            dq_f, dk_f, dv_f, dfk = _fox_bwd(st["qkv"], do, st["o_fx"], st["fcol"], st["frow"], st["lse"], NH, NSB, HD, T,
                                             f"fox_bwd_{l}")
            dqkv = jnp.concatenate([dq_s, dq_f, dk_s, dk_f, dv_s, dv_f], axis=1)
            dF = jnp.pad(dfk.reshape(NFX, S).T, ((0, 0), (0, LANES - NFX)))
            dflog, dbf = _forget_cumsum_bwd(dF, st["flog"], st["bfp"], f"fcum_bwd_{l}")
            dflog_b = dflog.astype(BF16)
            dh1 = _matmul(dqkv, st["win_t"], mode="nn", name=f"qkv_dx_{l}", out_dtype=F32)
            dh1 = _matmul(dflog_b, st["wf_t"], mode="nn", name=f"flog_dx_{l}", out_dtype=F32, res=dh1,
                          gate=jnp.ones((1, D), F32))
            dwqkv_t = _matmul(dqkv, st["h1"], mode="tn", name=f"qkv_dw_{l}", out_dtype=BF16)
            dwf_t = _matmul(dflog_b, st["h1"], mode="tn", name=f"flog_dw_{l}", out_dtype=BF16, tm=LANES)
            dwin_t = jnp.concatenate([dwqkv_t, dwf_t[:NFX]], axis=0).reshape(NDEV, CI, D)
            d_bf[i] = dbf[0, :NFX]
            parts = [dwin_t, dwout.reshape(NDEV, D // NDEV, D)]
        else:
            de, dgp = _gate_bwd(dx, st["y1"], st["gp"] + tok[0, 0], f"gate1_bwd_{l}")
            dg1 = dgp * pscale_f[i:i + 1]
            d_pscale[i] = dgp * g1
            dd, dwp = _pool_mm_bwd(de, st["dpool"], st["wpool"], f"pool_mm_bwd_{l}")
            dh1 = _pool_diff_bwd(dd, f"pool_diff_bwd_{l}")
            parts = [dwp.reshape(G, NDEV, CG // NDEV, CG).transpose(1, 0, 2, 3).reshape(NDEV, G * (CG // NDEV), CG)]
        dx, dgn1, dsc1, dsh1 = _norm_mod_bwd(dh1, st["x"], dx, gn1, sc1, f"norm1_bwd_{l}")
        dmod_rows[l] = jnp.concatenate([dsh1, dsc1, dg1, dsh2, dsc2, dg2], axis=1)
        d_gain[l] = jnp.concatenate([dgn1, dgn2], axis=0)
        if l > 0:
            tok = exchange_start(l, 0, parts, [])
            exchanges_finish([dx, tok], l + 1)

    grad_x = dx[None]

    small_grads = [jnp.stack(dmod_rows), jnp.stack(d_gain), jnp.stack(d_pscale), jnp.stack(d_convw),
                   jnp.stack(d_convb), jnp.stack(d_bf), d_fgain]
    sg_shapes = [(L, 6 * D), (L, 2, D), (L // 2, D), (L, 3, F2), (L, F2), ((L + 1) // 2, NFX), (D,)]
    sg_all = _all_gather([_pack(small_grads)], "gather_small_grads")[0]
    tok = exchange_start(0, 0, parts, [sg_all])
    exchanges_finish([tok], 1)
    sg_all = sg_all + tok[0, 0]
    sg_sum = _unpack(_sum_parts(sg_all, "sum_small_grads"), sg_shapes)
    g_bmod, g_gain_f, g_pscale_f, g_convw_f, g_convb, g_bf, g_fgain = sg_sum
    shard = lambda a, n, axis: lax.dynamic_slice_in_dim(a, me * n, n, axis=axis)
    g_small = [g_bmod, shard(g_gain_f, D // NDEV, 2), shard(g_pscale_f, D // NDEV, 1), shard(g_convw_f, CU, 2), g_convb,
               g_bf, g_fgain]
    w_small = [b_mod, norm_gain, pool_scale, conv_w, conv_b, b_forget, final_gain]
    m_small = [m_b_mod, m_norm_gain, m_pool_scale, m_conv_w, m_conv_b, m_b_forget, m_final_gain]
    v_small = [v_b_mod, v_norm_gain, v_pool_scale, v_conv_w, v_conv_b, v_b_forget, v_final_gain]
    small_raw = _adamw(_pack(w_small)[None], _pack(m_small)[None], _pack(v_small)[None], _pack(g_small)[None], 0, None,
                       "adamw_small")
    small_out = [_unpack(a[0], [w.shape for w in w_small]) for a in small_raw]

    dmod_all = jnp.stack([_unpack(sg_all[j], sg_shapes[:1])[0] for j in range(NDEV)])
    dmod_mine = lax.dynamic_slice_in_dim(dmod_all.reshape(NDEV, L, NDEV, CM), me, 1, axis=2)
    cond_t = jnp.pad(cond_all[:NDEV].T, ((0, 0), (0, LANES - NDEV)))
    mod_out = _adamw_mod(w_mod, m_w_mod, v_w_mod, cond_t, dmod_mine, "adamw_mod")
    exchanges_finish([mod_out[0], small_raw[0]], 0)

    pool4 = lambda a: a.reshape(w_pool.shape)
    names = ["w_mod", "b_mod", "norm_gain", "w_attn_in", "b_forget", "w_attn_out", "w_pool", "pool_scale", "w_up", "conv_w",
             "conv_b", "w_down", "final_gain"]
    small_pos = {"b_mod": 0, "norm_gain": 1, "pool_scale": 2, "conv_w": 3, "conv_b": 4, "b_forget": 5, "final_gain": 6}
    outs = []
    for kind in range(4):
        for nm in names:
            if nm == "w_mod":
                outs.append(mod_out[kind])
            elif nm in small_pos:
                outs.append(small_out[kind][small_pos[nm]])
            elif nm == "w_pool":
                outs.append(pool4(big[nm][kind]))
            elif nm == "w_attn_in":
                outs.append(jnp.stack([per_layer[kind][0] for per_layer in attn_in_t]).transpose(0, 2, 1))
            else:
                outs.append(big[nm][kind])
    return (loss, grad_x, *outs)
```

```python
import jax
import jax.numpy as jnp
from jax import lax
from jax.experimental import pallas as pl
from jax.experimental.pallas import tpu as pltpu

NDEV = 8
F32 = jnp.float32
BF16 = jnp.bfloat16
MESH = pl.DeviceIdType.MESH
VMEM_LIMIT_BYTES = 56 * 1024 * 1024
LANES = 128
POOL_WINDOWS = (2, 4, 8, 16)
EPS = 1e-6
ADAM_LR = 0.001
ADAM_B1 = 0.9
ADAM_B2 = 0.999
ADAM_EPS = 1e-08
ADAM_WD = 0.01
ADAM_STEP = 10
NEG_BIG = -1e30
SDS = jax.ShapeDtypeStruct
ANY = pl.BlockSpec(memory_space=pl.ANY)


def _pc(body, **kw):
    return pl.pallas_call(body, **kw)


def _params(n_axes):
    return pltpu.CompilerParams(dimension_semantics=("arbitrary",) * n_axes, vmem_limit_bytes=VMEM_LIMIT_BYTES)


def _pick(n, prefs):
    for p in prefs:
        if p <= n and n % p == 0:
            return p
    return n


def _idx(p):
    return 4 * p[0] + 2 * p[1] + p[2]


def _me():
    return lax.axis_index("x"), lax.axis_index("y"), lax.axis_index("c")


def _all_gather(arrs, name):
    n = len(arrs)

    def body(*refs):
        ins, outs = refs[:n], refs[n:2 * n]
        send_sems, recv_sems, local_sems = refs[2 * n:]
        x, y, c = _me()
        me, sib = (x, y, c), (x, y, 1 - c)
        chips = [(1 - x, y), (x, 1 - y), (1 - x, 1 - y)]

        def copy(t, k, block, to, src=None):
            dst = outs[t].at[_idx(block)]
            return pltpu.make_async_remote_copy(
                src_ref=dst if src is None else src, dst_ref=dst,
                send_sem=send_sems.at[7 * t + k], recv_sem=recv_sems.at[7 * t + k],
                device_id=to, device_id_type=MESH)

        mine = [pltpu.make_async_copy(ins[t], outs[t].at[_idx(me)], local_sems.at[t]) for t in range(n)]
        for cp in mine:
            cp.start()
        first = []
        for t in range(n):
            first.append(copy(t, 0, me, sib, src=ins[t]))
            for j, chip in enumerate(chips):
                first.append(copy(t, 1 + j, me, (*chip, c), src=ins[t]))
        for cp in first:
            cp.start()
        passed = []
        for t in range(n):
            for j, chip in enumerate(chips):
                copy(t, 1 + j, (*chip, c), me).wait_recv()
                cp = copy(t, 4 + j, (*chip, c), sib)
                cp.start()
                passed.append(cp)
        for t in range(n):
            copy(t, 0, sib, me).wait_recv()
            for j, chip in enumerate(chips):
                copy(t, 4 + j, (*chip, 1 - c), me).wait_recv()
        for cp in first + passed:
            cp.wait_send()
        for cp in mine:
            cp.wait()

    return _pc(
        body, name=name,
        out_shape=[SDS((NDEV,) + a.shape, a.dtype) for a in arrs],
        in_specs=[ANY] * n, out_specs=[ANY] * n,
        scratch_shapes=[pltpu.SemaphoreType.DMA((7 * n,)), pltpu.SemaphoreType.DMA((7 * n,)),
                        pltpu.SemaphoreType.DMA((n,))],
    )(*arrs)


HBM = pl.BlockSpec(memory_space=pltpu.HBM)
SEM = pl.BlockSpec(memory_space=pltpu.SEMAPHORE)
EFFECT = pltpu.SideEffectType.DATAFLOW_SIDE_EFFECTING
TOKEN = SDS((8, LANES), F32)


def _hbm(a):
    return pltpu.with_memory_space_constraint(a, pltpu.HBM)


def _landing(block):
    return lax.empty((NDEV,) + block.shape, block.dtype)


def _split_call(body, name, n_thru, thru, sems_in, after, sems_out):
    n_sem = len(sems_out)
    operands = [_hbm(a) for a in thru] + list(sems_in) + list(after)
    in_specs = [HBM] * n_thru + [SEM] * len(sems_in) + [ANY] * len(after)
    out_shape = [pltpu.SemaphoreType.DMA((k,)) for k in sems_out] + [pltpu.HBM(a.shape, a.dtype) for a in thru] + [TOKEN]
    out_specs = [SEM] * n_sem + [HBM] * n_thru + [pl.BlockSpec(memory_space=pltpu.VMEM)]
    outs = _pc(body, name=name, in_specs=in_specs, out_specs=out_specs, out_shape=out_shape,
               input_output_aliases={q: n_sem + q for q in range(n_thru)},
               compiler_params=pltpu.CompilerParams(has_side_effects=EFFECT))(*operands)
    return list(outs[:n_sem]), list(outs[n_sem:n_sem + n_thru]), outs[-1]


def _gather_start(shards, me, after, name):
    n = len(shards)
    lands = [_landing(s) for s in shards]

    def body(*refs):
        shard_refs, land_refs = refs[:n], refs[n:2 * n]
        send_sems, recv_sems, local_sems = refs[2 * n + len(after):2 * n + len(after) + 3]
        x, y, c = _me()
        me_i = _idx((x, y, c))
        peers = [(x, y, 1 - c), (1 - x, y, c), (x, 1 - y, c), (1 - x, 1 - y, c)]
        for t in range(n):
            pltpu.make_async_copy(shard_refs[t], land_refs[t].at[me_i], local_sems.at[t]).start()
            for k, p in enumerate(peers):
                pltpu.make_async_remote_copy(
                    src_ref=shard_refs[t], dst_ref=land_refs[t].at[me_i], send_sem=send_sems.at[4 * t + k],
                    recv_sem=recv_sems.at[4 * t + k], device_id=p, device_id_type=MESH).start()
        refs[-1][...] = jnp.zeros((8, LANES), F32)

    sems, thru, token = _split_call(body, name, 2 * n, list(shards) + lands, [], after, [4 * n, 4 * n, n])
    return dict(n=n, sems=sems, shards=thru[:n], lands=thru[n:]), token


def _gather_forward(st, after, name):
    n = st["n"]

    def body(*refs):
        shard_refs, land_refs = refs[:n], refs[n:2 * n]
        send1, recv1, local1 = refs[2 * n:2 * n + 3]
        send2, recv2 = refs[2 * n + 3 + len(after)], refs[2 * n + 4 + len(after)]
        x, y, c = _me()
        sib = (x, y, 1 - c)
        senders = [sib, (1 - x, y, c), (x, 1 - y, c), (1 - x, 1 - y, c)]
        for t in range(n):
            pltpu.make_async_copy(shard_refs[t], land_refs[t].at[_idx((x, y, c))], local1.at[t]).wait()
            for k, p in enumerate(senders):
                cp = pltpu.make_async_remote_copy(
                    src_ref=shard_refs[t], dst_ref=land_refs[t].at[_idx(p)], send_sem=send1.at[4 * t + k],
                    recv_sem=recv1.at[4 * t + k], device_id=p, device_id_type=MESH)
                cp.wait_send()
                cp.wait_recv()
        for t in range(n):
            for j, p in enumerate(senders[1:]):
                slab = land_refs[t].at[_idx(p)]
                pltpu.make_async_remote_copy(
                    src_ref=slab, dst_ref=slab, send_sem=send2.at[3 * t + j], recv_sem=recv2.at[3 * t + j],
                    device_id=sib, device_id_type=MESH).start()
        refs[-1][...] = jnp.zeros((8, LANES), F32)

    sems, thru, token = _split_call(body, name, 2 * n, st["shards"] + st["lands"], st["sems"], after, [3 * n, 3 * n])
    return dict(n=n, sems=sems, shards=thru[:n], lands=thru[n:]), token


def _gather_wait(st, after, name):
    n = st["n"]

    def body(*refs):
        land_refs = refs[n:2 * n]
        send2, recv2 = refs[2 * n], refs[2 * n + 1]
        x, y, c = _me()
        sib = (x, y, 1 - c)
        for t in range(n):
            for j, chip in enumerate([(1 - x, y), (x, 1 - y), (1 - x, 1 - y)]):
                sent, got = land_refs[t].at[_idx((*chip, c))], land_refs[t].at[_idx((*chip, 1 - c))]
                cp = pltpu.make_async_remote_copy(
                    src_ref=sent, dst_ref=got, send_sem=send2.at[3 * t + j], recv_sem=recv2.at[3 * t + j],
                    device_id=sib, device_id_type=MESH)
                cp.wait_send()
                cp.wait_recv()
        refs[-1][...] = jnp.zeros((8, LANES), F32)

    _, thru, _ = _split_call(body, name, 2 * n, st["shards"] + st["lands"], st["sems"], after, [])
    return thru[n:]


def _scatter_start(parts, me, after, name):
    n = len(parts)
    lands = [_landing(p[0]) for p in parts]

    def body(*refs):
        part_refs, land_refs = refs[:n], refs[n:2 * n]
        send_sems, recv_sems, local_sems = refs[2 * n + len(after):2 * n + len(after) + 3]
        x, y, c = _me()
        me_i = _idx((x, y, c))
        for t in range(n):
            pltpu.make_async_copy(part_refs[t].at[me_i], land_refs[t].at[me_i], local_sems.at[t]).start()
            for r in range(1, NDEV):
                p = (1 - x if r & 4 else x, 1 - y if r & 2 else y, 1 - c if r & 1 else c)
                pltpu.make_async_remote_copy(
                    src_ref=part_refs[t].at[_idx(p)], dst_ref=land_refs[t].at[me_i], send_sem=send_sems.at[7 * t + r - 1],
                    recv_sem=recv_sems.at[7 * t + r - 1], device_id=p, device_id_type=MESH).start()
        refs[-1][...] = jnp.zeros((8, LANES), F32)

    sems, thru, token = _split_call(body, name, 2 * n, list(parts) + lands, [], after, [7 * n, 7 * n, n])
    return dict(n=n, sems=sems, parts=thru[:n], lands=thru[n:]), token


def _scatter_wait(st, after, name):
    n = st["n"]

    def body(*refs):
        part_refs, land_refs = refs[:n], refs[n:2 * n]
        send_sems, recv_sems, local_sems = refs[2 * n:2 * n + 3]
        x, y, c = _me()
        me_i = _idx((x, y, c))
        for t in range(n):
            pltpu.make_async_copy(part_refs[t].at[me_i], land_refs[t].at[me_i], local_sems.at[t]).wait()
            for r in range(1, NDEV):
                p = (1 - x if r & 4 else x, 1 - y if r & 2 else y, 1 - c if r & 1 else c)
                cp = pltpu.make_async_remote_copy(
                    src_ref=part_refs[t].at[_idx(p)], dst_ref=land_refs[t].at[_idx(p)], send_sem=send_sems.at[7 * t + r - 1],
                    recv_sem=recv_sems.at[7 * t + r - 1], device_id=p, device_id_type=MESH)
                cp.wait_send()
                cp.wait_recv()
        refs[-1][...] = jnp.zeros((8, LANES), F32)

    _, thru, _ = _split_call(body, name, 2 * n, st["parts"] + st["lands"], st["sems"], after, [])
    return thru[n:]


def _pair_start(parts, after, name):
    n = len(parts)
    recvs = [lax.empty((4,) + p.shape[1:], p.dtype) for p in parts]

    def body(*refs):
        part_refs, recv_refs = refs[:n], refs[n:2 * n]
        send_sems, recv_sems = refs[2 * n + len(after)], refs[2 * n + len(after) + 1]
        x, y, c = _me()
        for t in range(n):
            for q in range(4):
                pltpu.make_async_remote_copy(
                    src_ref=part_refs[t].at[2 * q + (1 - c)], dst_ref=recv_refs[t].at[q], send_sem=send_sems.at[4 * t + q],
                    recv_sem=recv_sems.at[4 * t + q], device_id=(x, y, 1 - c), device_id_type=MESH).start()
        refs[-1][...] = jnp.zeros((8, LANES), F32)

    sems, thru, token = _split_call(body, name, 2 * n, list(parts) + recvs, [], after, [4 * n, 4 * n])
    return dict(n=n, sems=sems, parts=thru[:n], recvs=thru[n:]), token


def _pair_wait(st, after, name):
    n = st["n"]

    def body(*refs):
        part_refs, recv_refs = refs[:n], refs[n:2 * n]
        send_sems, recv_sems = refs[2 * n], refs[2 * n + 1]
        x, y, c = _me()
        for t in range(n):
            for q in range(4):
                cp = pltpu.make_async_remote_copy(
                    src_ref=part_refs[t].at[2 * q + (1 - c)], dst_ref=recv_refs[t].at[q], send_sem=send_sems.at[4 * t + q],
                    recv_sem=recv_sems.at[4 * t + q], device_id=(x, y, 1 - c), device_id_type=MESH)
                cp.wait_send()
                cp.wait_recv()
        refs[-1][...] = jnp.zeros((8, LANES), F32)

    _, thru, _ = _split_call(body, name, 2 * n, st["parts"] + st["recvs"], st["sems"], after, [])
    return thru[:n], thru[n:]


def _pair_sum(part, recv, name):
    _, R, C = recv.shape
    tr = _pick(R, (256, 128, 64, 88, 32, 16, 8))

    def body(core_ref, a_ref, b_ref, o_ref):
        o_ref[...] = (a_ref[...].astype(F32) + b_ref[...].astype(F32)).astype(o_ref.dtype)

    grid_spec = pltpu.PrefetchScalarGridSpec(
        num_scalar_prefetch=1, grid=(4, R // tr),
        in_specs=[pl.BlockSpec((None, None, tr, C), lambda q, i, core: (q, core[0], i, 0)),
                  pl.BlockSpec((None, tr, C), lambda q, i, core: (q, i, 0))],
        out_specs=pl.BlockSpec((None, tr, C), lambda q, i, core: (q, i, 0)))
    core = lax.axis_index("c").astype(jnp.int32).reshape(1)
    return _pc(body, name=name, grid_spec=grid_spec, out_shape=SDS(recv.shape, recv.dtype),
               compiler_params=_params(2))(core, part.reshape((4, 2) + part.shape[1:]), recv)


def _chips_start(sums, after, name):
    n = len(sums)
    lands = [lax.empty(s.shape, s.dtype) for s in sums]

    def body(*refs):
        sum_refs, land_refs = refs[:n], refs[n:2 * n]
        send_sems, recv_sems, local_sems = refs[2 * n + len(after):2 * n + len(after) + 3]
        x, y, c = _me()
        my_chip = 2 * x + y
        for t in range(n):
            pltpu.make_async_copy(sum_refs[t].at[my_chip], land_refs[t].at[my_chip], local_sems.at[t]).start()
            for j, (px, py) in enumerate([(1 - x, y), (x, 1 - y), (1 - x, 1 - y)]):
                pltpu.make_async_remote_copy(
                    src_ref=sum_refs[t].at[2 * px + py], dst_ref=land_refs[t].at[my_chip], send_sem=send_sems.at[3 * t + j],
                    recv_sem=recv_sems.at[3 * t + j], device_id=(px, py, c), device_id_type=MESH).start()
        refs[-1][...] = jnp.zeros((8, LANES), F32)

    sems, thru, token = _split_call(body, name, 2 * n, list(sums) + lands, [], after, [3 * n, 3 * n, n])
    return dict(n=n, sems=sems, sums=thru[:n], lands=thru[n:]), token


def _chips_wait(st, after, name):
    n = st["n"]

    def body(*refs):
        sum_refs, land_refs = refs[:n], refs[n:2 * n]
        send_sems, recv_sems, local_sems = refs[2 * n:2 * n + 3]
        x, y, c = _me()
        my_chip = 2 * x + y
        for t in range(n):
            pltpu.make_async_copy(sum_refs[t].at[my_chip], land_refs[t].at[my_chip], local_sems.at[t]).wait()
            for j, (px, py) in enumerate([(1 - x, y), (x, 1 - y), (1 - x, 1 - y)]):
                cp = pltpu.make_async_remote_copy(
                    src_ref=sum_refs[t].at[2 * px + py], dst_ref=land_refs[t].at[2 * px + py], send_sem=send_sems.at[3 * t + j],
                    recv_sem=recv_sems.at[3 * t + j], device_id=(px, py, c), device_id_type=MESH)
                cp.wait_send()
                cp.wait_recv()
        refs[-1][...] = jnp.zeros((8, LANES), F32)

    _, thru, _ = _split_call(body, name, 2 * n, st["sums"] + st["lands"], st["sems"], after, [])
    return thru[n:]


def _matmul(a, b, *, mode, name, out_dtype, tm=1024, tn=1024, tk=2048, b_blocked=False, out_blocked=False,
            a_split=False, b_split=False, res=None, gate=None, y_dtype=None, n=None):
    if mode == "tn":
        K, M = (a.shape[0], a.shape[1]) if not a_split else (a.shape[1], 2 * a.shape[2])
    else:
        M, K = (a.shape[0], a.shape[1]) if not a_split else (a.shape[1], 2 * a.shape[2])
    if b_blocked:
        if mode == "nn":
            N, tn = b.shape[0] * b.shape[2], b.shape[2]
        else:
            N, tk = b.shape[1], b.shape[2]
    elif b_split:
        N = 2 * b.shape[2]
    else:
        N = b.shape[0] if mode == "nt" else b.shape[1]
    if n is not None:
        N = n
    tm = _pick(M, (tm, 704, 512, 384, 256, 128))
    if not (b_blocked and mode == "nn"):
        tn = _pick(N, (tn, 1024, 768, 512, 384, 256, 128))
    if not (b_blocked and mode == "nt"):
        tk = _pick(K, (tk, 1024, 512, 384, 256, 128))
    nm, nn_, nk = M // tm, N // tn, K // tk

    if mode == "tn":
        a_spec = pl.BlockSpec((tk, tm), lambda i, j, k: (k, i))
        dims = (((0,), (0,)), ((), ()))
    elif a_split:
        per = a.shape[2] // tk
        a_spec = pl.BlockSpec((None, tm, tk), lambda i, j, k: (k // per, i, k % per))
    else:
        a_spec = pl.BlockSpec((tm, tk), lambda i, j, k: (i, k))
    if mode == "nn":
        dims = (((1,), (0,)), ((), ()))
        if b_blocked:
            b_spec = pl.BlockSpec((None, tk, tn), lambda i, j, k: (j, k, 0))
        else:
            b_spec = pl.BlockSpec((tk, tn), lambda i, j, k: (k, j))
    elif mode == "nt":
        dims = (((1,), (1,)), ((), ()))
        if b_blocked:
            b_spec = pl.BlockSpec((None, tn, tk), lambda i, j, k: (k, j, 0))
        else:
            b_spec = pl.BlockSpec((tn, tk), lambda i, j, k: (j, k))
    else:
        if b_split:
            per_b = b.shape[2] // tn
            b_spec = pl.BlockSpec((None, tk, tn), lambda i, j, k: (j // per_b, k, j % per_b))
        else:
            b_spec = pl.BlockSpec((tk, tn), lambda i, j, k: (k, j))
    if out_blocked:
        o_spec = pl.BlockSpec((None, tm, tn), lambda i, j, k: (j, i, 0))
        o_shape = SDS((nn_, M, tn), out_dtype)
    else:
        o_spec = pl.BlockSpec((tm, tn), lambda i, j, k: (i, j))
        o_shape = SDS((M, N), out_dtype)
    fused = res is not None
    in_specs, operands = [a_spec, b_spec], [a, b]
    out_specs, out_shapes = [o_spec], [o_shape]
    if fused:
        in_specs += [pl.BlockSpec((tm, tn), lambda i, j, k: (i, j)), pl.BlockSpec((1, tn), lambda i, j, k: (0, j))]
        operands += [res, gate]
        if y_dtype is not None:
            out_specs.append(pl.BlockSpec((tm, tn), lambda i, j, k: (i, j)))
            out_shapes.append(SDS((M, N), y_dtype))

    def body(*refs):
        a_ref, b_ref = refs[0], refs[1]
        acc_ref = refs[-1]
        k = pl.program_id(2)

        def product():
            return lax.dot_general(a_ref[...], b_ref[...], dims, preferred_element_type=F32)

        def finish(acc):
            if fused:
                res_ref, gate_ref, o_ref = refs[2], refs[3], refs[4]
                o_ref[...] = (res_ref[...] + gate_ref[...] * acc).astype(o_ref.dtype)
                if y_dtype is not None:
                    refs[5][...] = acc.astype(y_dtype)
            else:
                refs[2][...] = acc.astype(refs[2].dtype)

        if nk == 1:
            finish(product())
        else:
            @pl.when(k == 0)
            def _():
                acc_ref[...] = product()

            @pl.when(jnp.logical_and(k > 0, k < nk - 1))
            def _():
                acc_ref[...] += product()

            @pl.when(k == nk - 1)
            def _():
                finish(acc_ref[...] + product())

    outs = _pc(body, name=name, grid=(nm, nn_, nk), in_specs=in_specs, out_specs=out_specs, out_shape=out_shapes,
               scratch_shapes=[pltpu.VMEM((tm, tn), F32)], compiler_params=_params(3))(*operands)
    return outs[0] if len(outs) == 1 else tuple(outs)


def _norm_mod(x, gain, sc, sh, out_dtype, name):
    S, D = x.shape
    tr = _pick(S, (256, 128))

    def body(x_ref, g_ref, sc_ref, sh_ref, o_ref):
        xv = x_ref[...]
        r = lax.rsqrt(jnp.mean(xv * xv, axis=-1, keepdims=True) + EPS)
        n = (xv * r) * g_ref[...]
        o_ref[...] = (n * (1.0 + sc_ref[...]) + sh_ref[...]).astype(o_ref.dtype)

    row = pl.BlockSpec((tr, D), lambda i: (i, 0))
    vec = pl.BlockSpec((1, D), lambda i: (0, 0))
    return _pc(body, name=name, grid=(S // tr,), in_specs=[row, vec, vec, vec], out_specs=row,
               out_shape=SDS((S, D), out_dtype), compiler_params=_params(1))(x, gain, sc, sh)


def _norm_mod_bwd(dh, x, dxres, gain, sc, name):
    S, D = x.shape
    tr = _pick(S, (256, 128))

    def body(dh_ref, x_ref, dxres_ref, g_ref, sc_ref, dx_ref, dgain_ref, dsc_ref, dsh_ref):
        @pl.when(pl.program_id(0) == 0)
        def _():
            dgain_ref[...] = jnp.zeros_like(dgain_ref)
            dsc_ref[...] = jnp.zeros_like(dsc_ref)
            dsh_ref[...] = jnp.zeros_like(dsh_ref)

        xv = x_ref[...]
        dh = dh_ref[...].astype(F32)
        r = lax.rsqrt(jnp.mean(xv * xv, axis=-1, keepdims=True) + EPS)
        nh = xv * r
        gn = g_ref[...]
        dn = dh * (1.0 + sc_ref[...])
        dgain_ref[...] += jnp.sum(dn * nh, axis=0, keepdims=True)
        dsc_ref[...] += jnp.sum(dh * (nh * gn), axis=0, keepdims=True)
        dsh_ref[...] += jnp.sum(dh, axis=0, keepdims=True)
        dnh = dn * gn
        dx = r * (dnh - nh * jnp.mean(dnh * nh, axis=-1, keepdims=True))
        dx_ref[...] = dxres_ref[...] + dx

    row = pl.BlockSpec((tr, D), lambda i: (i, 0))
    vec = pl.BlockSpec((1, D), lambda i: (0, 0))
    return _pc(body, name=name, grid=(S // tr,), in_specs=[row, row, row, vec, vec], out_specs=[row, vec, vec, vec],
               out_shape=[SDS((S, D), F32), SDS((1, D), F32), SDS((1, D), F32), SDS((1, D), F32)],
               compiler_params=_params(1))(dh, x, dxres, gain, sc)


def _gate_bwd(dx, y, gate, name):
    S, D = dx.shape
    tr = _pick(S, (256, 128))

    def body(dx_ref, y_ref, gate_ref, dy_ref, dgate_ref):
        @pl.when(pl.program_id(0) == 0)
        def _():
            dgate_ref[...] = jnp.zeros_like(dgate_ref)

        dxv = dx_ref[...]
        dgate_ref[...] += jnp.sum(dxv * y_ref[...].astype(F32), axis=0, keepdims=True)
        dy_ref[...] = (dxv * gate_ref[...]).astype(BF16)

    row = pl.BlockSpec((tr, D), lambda i: (i, 0))
    vec = pl.BlockSpec((1, D), lambda i: (0, 0))
    return _pc(body, name=name, grid=(S // tr,), in_specs=[row, row, vec], out_specs=[row, vec],
               out_shape=[SDS((S, D), BF16), SDS((1, D), F32)], compiler_params=_params(1))(dx, y, gate)


def _loss_head(x, target, fgain, name):
    S, D = x.shape
    tr = _pick(S, (256, 128))

    def body(x_ref, t_ref, fg_ref, dx_ref, dfg_ref, loss_ref):
        @pl.when(pl.program_id(0) == 0)
        def _():
            dfg_ref[...] = jnp.zeros_like(dfg_ref)
            loss_ref[...] = jnp.zeros_like(loss_ref)

        xv = x_ref[...]
        fg = fg_ref[...]
        r = lax.rsqrt(jnp.mean(xv * xv, axis=-1, keepdims=True) + EPS)
        nh = xv * r
        e = nh * fg - t_ref[...]
        loss_ref[...] += 0.5 * jnp.sum(jnp.mean(e * e, axis=-1, keepdims=True))
        dy = e * (1.0 / D)
        dfg_ref[...] += jnp.sum(dy * nh, axis=0, keepdims=True)
        dnh = dy * fg
        dx_ref[...] = r * (dnh - nh * jnp.mean(dnh * nh, axis=-1, keepdims=True))

    row = pl.BlockSpec((tr, D), lambda i: (i, 0))
    vec = pl.BlockSpec((1, D), lambda i: (0, 0))
    tile = pl.BlockSpec((8, LANES), lambda i: (0, 0))
    return _pc(body, name=name, grid=(S // tr,), in_specs=[row, row, vec], out_specs=[row, vec, tile],
               out_shape=[SDS((S, D), F32), SDS((1, D), F32), SDS((8, LANES), F32)],
               compiler_params=_params(1))(x, target, fgain)


def _shift_down(v, k, rows):
    return jnp.where(rows >= k, pltpu.roll(v, k, axis=0), 0.0)


def _shift_up(v, k, rows):
    n = v.shape[0]
    return jnp.where(rows < n - k, pltpu.roll(v, n - k, axis=0), 0.0)


def _conv(uv, w, b, rows):
    return ((b + _shift_down(uv, 2, rows) * w[0:1]) + _shift_down(uv, 1, rows) * w[1:2]) + uv * w[2:3]


def _convgate_fwd(u, cw, cb, name):
    S, F2 = u.shape
    DFF = F2 // 2
    tc = _pick(DFF, (256, 128))
    sub = min(tc, LANES)
    nj = DFF // tc

    def body(ua_ref, ug_ref, wa_ref, wg_ref, ba_ref, bg_ref, o_ref):
        rows = lax.broadcasted_iota(jnp.int32, (S, sub), 0)
        for q in range(tc // sub):
            sl = slice(q * sub, (q + 1) * sub)
            ya = _conv(ua_ref[:, sl], wa_ref[:, sl], ba_ref[:, sl], rows)
            yg = _conv(ug_ref[:, sl], wg_ref[:, sl], bg_ref[:, sl], rows)
            o_ref[:, sl] = (yg * jax.nn.sigmoid(yg) * ya).astype(BF16)

    col = lambda off: pl.BlockSpec((S, tc), lambda j: (0, j + off))
    w3 = lambda off: pl.BlockSpec((3, tc), lambda j: (0, j + off))
    b1 = lambda off: pl.BlockSpec((1, tc), lambda j: (0, j + off))
    return _pc(body, name=name, grid=(nj,), in_specs=[col(0), col(nj), w3(0), w3(nj), b1(0), b1(nj)],
               out_specs=col(0), out_shape=SDS((S, DFF), BF16), compiler_params=_params(1))(u, u, cw, cw, cb, cb)


def _convgate_bwd(u, dact, cw, cb, name):
    S, F2 = u.shape
    DFF = F2 // 2
    tc = _pick(DFF, (256, 128))
    sub = min(tc, LANES)
    nj = DFF // tc

    def body(ua_ref, ug_ref, da_ref, wa_ref, wg_ref, ba_ref, bg_ref, du_ref, dcw_ref, dcb_ref):
        rows = lax.broadcasted_iota(jnp.int32, (S, sub), 0)
        for q in range(tc // sub):
            sl = slice(q * sub, (q + 1) * sub)
            ua, ug = ua_ref[:, sl], ug_ref[:, sl]
            wa, wg = wa_ref[:, sl], wg_ref[:, sl]
            ya = _conv(ua, wa, ba_ref[:, sl], rows)
            yg = _conv(ug, wg, bg_ref[:, sl], rows)
            s = jax.nn.sigmoid(yg)
            da = da_ref[:, sl]
            dya = da * (yg * s)
            dyg = da * ya * (s * (1.0 + yg * (1.0 - s)))
            for h, (dy, uv, w) in enumerate(((dya, ua, wa), (dyg, ug, wg))):
                du = (dy * w[2:3] + _shift_up(dy, 1, rows) * w[1:2]) + _shift_up(dy, 2, rows) * w[0:1]
                du_ref[h, :, sl] = du.astype(BF16)
                dcw_ref[h, 0:1, sl] = jnp.sum(dy * _shift_down(uv, 2, rows), axis=0, keepdims=True)
                dcw_ref[h, 1:2, sl] = jnp.sum(dy * _shift_down(uv, 1, rows), axis=0, keepdims=True)
                dcw_ref[h, 2:3, sl] = jnp.sum(dy * uv, axis=0, keepdims=True)
                dcb_ref[h, :, sl] = jnp.sum(dy, axis=0, keepdims=True)

    col = lambda off: pl.BlockSpec((S, tc), lambda j: (0, j + off))
    w3 = lambda off: pl.BlockSpec((3, tc), lambda j: (0, j + off))
    b1 = lambda off: pl.BlockSpec((1, tc), lambda j: (0, j + off))
    return _pc(body, name=name, grid=(nj,),
               in_specs=[col(0), col(nj), col(0), w3(0), w3(nj), b1(0), b1(nj)],
               out_specs=[pl.BlockSpec((2, S, tc), lambda j: (0, 0, j)), pl.BlockSpec((2, 3, tc), lambda j: (0, 0, j)),
                          pl.BlockSpec((2, 1, tc), lambda j: (0, 0, j))],
               out_shape=[SDS((2, S, DFF), BF16), SDS((2, 3, DFF), F32), SDS((2, 1, DFF), F32)],
               compiler_params=_params(1))(u, u, dact, cw, cw, cb, cb)


def _pool_diff(h, name):
    S, D = h.shape
    G = len(POOL_WINDOWS)
    CG = D // G
    tc = min(CG, LANES)
    per = CG // tc

    def body(h_ref, d_ref):
        g = pl.program_id(0)
        rows = lax.broadcasted_iota(jnp.int32, (S, tc), 0)
        for gi, w in enumerate(POOL_WINDOWS):
            @pl.when(g == gi)
            def _(w=w):
                hv = h_ref[...]
                s, k = hv, 1
                while k < w:
                    s = s + _shift_down(s, k, rows)
                    k *= 2
                count = jnp.minimum(rows + 1, w).astype(F32)
                d_ref[...] = (s / count - hv).astype(BF16)

    spec = pl.BlockSpec((S, tc), lambda g, j: (0, g * per + j))
    return _pc(body, name=name, grid=(G, per), in_specs=[spec], out_specs=spec, out_shape=SDS((S, D), BF16),
               compiler_params=_params(2))(h)


def _pool_diff_bwd(dd, name):
    S, D = dd.shape
    G = len(POOL_WINDOWS)
    CG = D // G
    tc = min(CG, LANES)
    per = CG // tc

    def body(dd_ref, o_ref):
        g = pl.program_id(0)
        rows = lax.broadcasted_iota(jnp.int32, (S, tc), 0)
        for gi, w in enumerate(POOL_WINDOWS):
            @pl.when(g == gi)
            def _(w=w):
                dv = dd_ref[...]
                count = jnp.minimum(rows + 1, w).astype(F32)
                s, k = dv / count, 1
                while k < w:
                    s = s + _shift_up(s, k, rows)
                    k *= 2
                o_ref[...] = s - dv

    spec = pl.BlockSpec((S, tc), lambda g, j: (0, g * per + j))
    return _pc(body, name=name, grid=(G, per), in_specs=[spec], out_specs=spec, out_shape=SDS((S, D), F32),
               compiler_params=_params(2))(dd)


def _pool_mm(d, w, res, gate, name):
    S, D = d.shape
    G, CG, _ = w.shape
    tm = _pick(S, (512, 256, 128))

    def body(d_ref, w_ref, res_ref, gate_ref, o_ref, e_ref):
        acc = jnp.dot(d_ref[...], w_ref[...], preferred_element_type=F32)
        o_ref[...] = res_ref[...] + gate_ref[...] * acc
        e_ref[...] = acc.astype(BF16)

    blk = pl.BlockSpec((tm, CG), lambda g, i: (i, g))
    return _pc(body, name=name, grid=(G, S // tm),
               in_specs=[blk, pl.BlockSpec((None, CG, CG), lambda g, i: (g, 0, 0)), blk,
                         pl.BlockSpec((1, CG), lambda g, i: (0, g))],
               out_specs=[blk, blk], out_shape=[SDS((S, D), F32), SDS((S, D), BF16)],
               compiler_params=_params(2))(d, w, res, gate)


def _pool_mm_bwd(de, d, w, name):
    S, D = de.shape
    G, CG, _ = w.shape
    tm = _pick(S, (512, 256, 128))
    ns = S // tm

    def body(de_ref, d_ref, w_ref, dd_ref, dw_ref, acc_ref):
        i = pl.program_id(1)

        @pl.when(i == 0)
        def _():
            acc_ref[...] = jnp.zeros_like(acc_ref)

        dev = de_ref[...]
        dd_ref[...] = lax.dot_general(dev, w_ref[...], (((1,), (1,)), ((), ())), preferred_element_type=F32)
        acc_ref[...] += lax.dot_general(d_ref[...], dev, (((0,), (0,)), ((), ())), preferred_element_type=F32)

        @pl.when(i == ns - 1)
        def _():
            dw_ref[...] = acc_ref[...].astype(BF16)

    blk = pl.BlockSpec((tm, CG), lambda g, i: (i, g))
    wsp = pl.BlockSpec((None, CG, CG), lambda g, i: (g, 0, 0))
    return _pc(body, name=name, grid=(G, ns), in_specs=[blk, blk, wsp], out_specs=[blk, wsp],
               out_shape=[SDS((S, D), F32), SDS((G, CG, CG), BF16)], scratch_shapes=[pltpu.VMEM((CG, CG), F32)],
               compiler_params=_params(2))(de, d, w)


def _log_sigmoid(z):
    return jnp.minimum(z, 0.0) - jnp.log(1.0 + jnp.exp(-jnp.abs(z)))


def _dot2(a, tri):
    hi = a.astype(BF16)
    lo = (a - hi.astype(F32)).astype(BF16)
    return jnp.dot(hi, tri, preferred_element_type=F32) + jnp.dot(lo, tri, preferred_element_type=F32)


_NT = (((1,), (1,)), ((), ()))
_TN = (((0,), (0,)), ((), ()))


def _forget_cumsum(flog, bf, name):
    S, W = flog.shape
    tb = _pick(S, (128,))

    def body(f_ref, b_ref, o_ref):
        r = lax.broadcasted_iota(jnp.int32, (tb, tb), 0)
        c = lax.broadcasted_iota(jnp.int32, (tb, tb), 1)
        tri = (c <= r).astype(F32)
        carry = jnp.zeros((1, W), F32)
        for q in range(S // tb):
            ls = _log_sigmoid(f_ref[q * tb:(q + 1) * tb, :] + b_ref[...])
            o_ref[q * tb:(q + 1) * tb, :] = carry + jnp.dot(tri, ls, preferred_element_type=F32,
                                                            precision=lax.Precision.HIGHEST)
            carry = carry + jnp.sum(ls, axis=0, keepdims=True)

    return _pc(body, name=name, out_shape=SDS((S, W), F32))(flog, bf)


def _forget_cumsum_bwd(dF, flog, bf, name):
    S, W = flog.shape
    tb = _pick(S, (128,))

    def body(d_ref, f_ref, b_ref, o_ref, db_ref):
        r = lax.broadcasted_iota(jnp.int32, (tb, tb), 0)
        c = lax.broadcasted_iota(jnp.int32, (tb, tb), 1)
        tri = (c >= r).astype(F32)
        carry = jnp.zeros((1, W), F32)
        db = jnp.zeros((1, W), F32)
        for q in reversed(range(S // tb)):
            dv = d_ref[q * tb:(q + 1) * tb, :]
            dls = carry + jnp.dot(tri, dv, preferred_element_type=F32, precision=lax.Precision.HIGHEST)
            carry = carry + jnp.sum(dv, axis=0, keepdims=True)
            dfl = dls * jax.nn.sigmoid(-(f_ref[q * tb:(q + 1) * tb, :] + b_ref[...]))
            o_ref[q * tb:(q + 1) * tb, :] = dfl
            db = db + jnp.sum(dfl, axis=0, keepdims=True)
        db_ref[...] = db

    return _pc(body, name=name, out_shape=[SDS((S, W), F32), SDS((1, W), F32)])(dF, flog, bf)


def _heads_per_step(n_heads):
    return 4 if n_heads % 4 == 0 else 2 if n_heads % 2 == 0 else 1


def _sb_fwd(qkv, NH, NSB, HD, T, name):
    S = qkv.shape[0]
    nq = S // T
    scale = HD ** -0.5
    HB = _heads_per_step(NSB)
    W = HB * HD

    def body(q_ref, k_ref, v_ref, o_ref, tot_ref):
        i = pl.program_id(1)
        row = lax.broadcasted_iota(jnp.int32, (T, T), 0)
        col = lax.broadcasted_iota(jnp.int32, (T, T), 1)
        upper = (row > col).astype(BF16)
        heads = [slice(hh * HD, (hh + 1) * HD) for hh in range(HB)]
        qs = [q_ref[:, cs] for cs in heads]

        def blk(kb, state, diag):
            sl = pl.ds(pl.multiple_of(kb * T, T), T)
            out = []
            for hh, cs in enumerate(heads):
                carry, acc = state[2 * hh], state[2 * hh + 1]
                k, v = k_ref[sl, cs], v_ref[sl, cs]
                z = lax.dot_general(qs[hh], k, _NT, preferred_element_type=F32) * scale
                ls = _log_sigmoid(z)
                lr = ls - z
                if diag:
                    lr = jnp.where(col < row, lr, 0.0)
                rest = _dot2(lr, upper) + carry
                w = jnp.exp(ls + rest)
                if diag:
                    w = jnp.where(col < row, w, 0.0)
                out += [carry + jnp.sum(lr, axis=1, keepdims=True),
                        acc + jnp.dot(w.astype(BF16), v, preferred_element_type=F32)]
            return tuple(out)

        state = blk(i, (jnp.zeros((T, 1), F32), jnp.zeros((T, HD), F32)) * HB, True)
        state = lax.fori_loop(0, i, lambda jj, st: blk(i - 1 - jj, st, False), state)
        for hh, cs in enumerate(heads):
            o_ref[:, cs] = state[2 * hh + 1].astype(BF16)
            tot_ref[hh] = state[2 * hh]

    return _pc(body, name=name, grid=(NSB // HB, nq),
               in_specs=[pl.BlockSpec((T, W), lambda h, i: (i, h)),
                         pl.BlockSpec((S, W), lambda h, i: (0, NH // HB + h)),
                         pl.BlockSpec((S, W), lambda h, i: (0, 2 * NH // HB + h))],
               out_specs=[pl.BlockSpec((T, W), lambda h, i: (i, h)), pl.BlockSpec((HB, T, 1), lambda h, i: (h, i, 0))],
               out_shape=[SDS((S, NSB * HD), BF16), SDS((NSB, S, 1), F32)],
               compiler_params=_params(2))(qkv, qkv, qkv)


def _sb_bwd(qkv, do, tot, NH, NSB, HD, T, name):
    S = qkv.shape[0]
    nq = S // T
    scale = HD ** -0.5
    HB = _heads_per_step(NSB)
    W = HB * HD

    def body(q_ref, k_ref, v_ref, do_ref, tot_ref, dq_ref, dk_ref, dv_ref, dk_acc, dv_acc):
        i = pl.program_id(1)

        @pl.when(i == 0)
        def _():
            dk_acc[...] = jnp.zeros_like(dk_acc)
            dv_acc[...] = jnp.zeros_like(dv_acc)

        row = lax.broadcasted_iota(jnp.int32, (T, T), 0)
        col = lax.broadcasted_iota(jnp.int32, (T, T), 1)
        incl = (row <= col).astype(BF16)
        strict = (row < col).astype(BF16)
        heads = [slice(hh * HD, (hh + 1) * HD) for hh in range(HB)]
        qs = [q_ref[:, cs] for cs in heads]
        dos = [do_ref[:, cs] for cs in heads]
        tots = [tot_ref[hh] for hh in range(HB)]

        def blk(kb, state, diag):
            sl = pl.ds(pl.multiple_of(kb * T, T), T)
            out = []
            for hh, cs in enumerate(heads):
                cl, cg, dq = state[3 * hh], state[3 * hh + 1], state[3 * hh + 2]
                q, do_ = qs[hh], dos[hh]
                k, v = k_ref[sl, cs], v_ref[sl, cs]
                z = lax.dot_general(q, k, _NT, preferred_element_type=F32) * scale
                ls = _log_sigmoid(z)
                lr = ls - z
                if diag:
                    lr = jnp.where(col < row, lr, 0.0)
                rest = tots[hh] - (cl + _dot2(lr, incl))
                w = jnp.exp(ls + rest)
                if diag:
                    w = jnp.where(col < row, w, 0.0)
                g = lax.dot_general(do_, v, _NT, preferred_element_type=F32) * w
                dv_acc[sl, cs] += lax.dot_general(w.astype(BF16), do_, _TN, preferred_element_type=F32)
                dlr = cg + _dot2(g, strict)
                dz = g * jnp.exp(lr) - dlr * jnp.exp(ls)
                if diag:
                    dz = jnp.where(col < row, dz, 0.0)
                dzb = (dz * scale).astype(BF16)
                dk_acc[sl, cs] += lax.dot_general(dzb, q, _TN, preferred_element_type=F32)
                out += [cl + jnp.sum(lr, axis=1, keepdims=True), cg + jnp.sum(g, axis=1, keepdims=True),
                        dq + jnp.dot(dzb, k, preferred_element_type=F32)]
            return tuple(out)

        zero = jnp.zeros((T, 1), F32)
        state = lax.fori_loop(0, i, lambda kb, st: blk(kb, st, False), (zero, zero, jnp.zeros((T, HD), F32)) * HB)
        state = blk(i, state, True)
        for hh, cs in enumerate(heads):
            dq_ref[:, cs] = state[3 * hh + 2].astype(BF16)

        @pl.when(i == nq - 1)
        def _():
            dk_ref[...] = dk_acc[...].astype(BF16)
            dv_ref[...] = dv_acc[...].astype(BF16)

    qblk = pl.BlockSpec((T, W), lambda h, i: (i, h))
    full = pl.BlockSpec((S, W), lambda h, i: (0, h))
    return _pc(body, name=name, grid=(NSB // HB, nq),
               in_specs=[qblk, pl.BlockSpec((S, W), lambda h, i: (0, NH // HB + h)),
                         pl.BlockSpec((S, W), lambda h, i: (0, 2 * NH // HB + h)), qblk,
                         pl.BlockSpec((HB, T, 1), lambda h, i: (h, i, 0))],
               out_specs=[qblk, full, full],
               out_shape=[SDS((S, NSB * HD), BF16)] * 3,
               scratch_shapes=[pltpu.VMEM((S, W), F32), pltpu.VMEM((S, W), F32)],
               compiler_params=_params(2))(qkv, qkv, qkv, do, tot)


def _fox_fwd(qkv, fcol, frow, NH, NSB, HD, T, name):
    S = qkv.shape[0]
    NFX = NH - NSB
    nq = S // T
    scale = HD ** -0.5
    HB = _heads_per_step(NFX) if NSB % _heads_per_step(NFX) == 0 else 1
    W = HB * HD

    def body(q_ref, k_ref, v_ref, fq_ref, fk_ref, o_ref, o32_ref, lse_ref):
        i = pl.program_id(1)
        row = lax.broadcasted_iota(jnp.int32, (T, T), 0)
        col = lax.broadcasted_iota(jnp.int32, (T, T), 1)
        heads = [slice(hh * HD, (hh + 1) * HD) for hh in range(HB)]
        qs = [q_ref[:, cs] for cs in heads]
        fqs = [fq_ref[hh] for hh in range(HB)]

        def blk(kb, state, diag):
            sl = pl.ds(pl.multiple_of(kb * T, T), T)
            out = []
            for hh, cs in enumerate(heads):
                m, l, acc, rem = state[4 * hh:4 * hh + 4]
                k, v = k_ref[sl, cs], v_ref[sl, cs]
                s = lax.dot_general(qs[hh], k, _NT, preferred_element_type=F32) * scale + (fqs[hh] - fk_ref[hh, kb])
                if diag:
                    s = jnp.where(col <= row, s, NEG_BIG)
                m_new = jnp.maximum(m, jnp.max(s, axis=1, keepdims=True))
                p = jnp.exp(s - m_new)
                alpha = jnp.exp(m - m_new)
                hi = p.astype(BF16)
                lo = (p - hi.astype(F32)).astype(BF16)
                out += [m_new, alpha * l + jnp.sum(p, axis=1, keepdims=True),
                        alpha * acc + jnp.dot(hi, v, preferred_element_type=F32),
                        alpha * rem + jnp.dot(lo, v, preferred_element_type=F32)]
            return tuple(out)

        zero = jnp.zeros((T, HD), F32)
        state = blk(i, (jnp.full((T, 1), NEG_BIG, F32), jnp.zeros((T, 1), F32), zero, zero) * HB, True)
        state = lax.fori_loop(0, i, lambda kb, st: blk(kb, st, False), state)
        for hh, cs in enumerate(heads):
            m, l, acc, rem = state[4 * hh:4 * hh + 4]
            o_ref[:, cs] = (acc / l).astype(BF16)
            o32_ref[:, cs] = (acc + rem) / l
            lse_ref[hh] = m + jnp.log(l)

    vec = pl.BlockSpec((HB, T, 1), lambda h, i: (h, i, 0))
    oblk = pl.BlockSpec((T, W), lambda h, i: (i, h))
    return _pc(body, name=name, grid=(NFX // HB, nq),
               in_specs=[pl.BlockSpec((T, W), lambda h, i: (i, NSB // HB + h)),
                         pl.BlockSpec((S, W), lambda h, i: (0, (NH + NSB) // HB + h)),
                         pl.BlockSpec((S, W), lambda h, i: (0, (2 * NH + NSB) // HB + h)),
                         vec, pl.BlockSpec((HB, nq, 1, T), lambda h, i: (h, 0, 0, 0))],
               out_specs=[oblk, oblk, vec],
               out_shape=[SDS((S, NFX * HD), BF16), SDS((S, NFX * HD), F32), SDS((NFX, S, 1), F32)],
               compiler_params=_params(2))(qkv, qkv, qkv, fcol, frow)


def _fox_bwd(qkv, do, o, fcol, frow, lse, NH, NSB, HD, T, name):
    S = qkv.shape[0]
    NFX = NH - NSB
    nq = S // T
    scale = HD ** -0.5
    HB = _heads_per_step(NFX) if NSB % _heads_per_step(NFX) == 0 else 1
    W = HB * HD

    def body(q_ref, k_ref, v_ref, do_ref, o_ref, fq_ref, fk_ref, lse_ref, dq_ref, dk_ref, dv_ref, dfk_ref,
             dk_acc, dv_acc, dfk_acc):
        i = pl.program_id(1)

        @pl.when(i == 0)
        def _():
            dk_acc[...] = jnp.zeros_like(dk_acc)
            dv_acc[...] = jnp.zeros_like(dv_acc)
            dfk_acc[...] = jnp.zeros_like(dfk_acc)

        row = lax.broadcasted_iota(jnp.int32, (T, T), 0)
        col = lax.broadcasted_iota(jnp.int32, (T, T), 1)
        heads = [slice(hh * HD, (hh + 1) * HD) for hh in range(HB)]
        qs = [q_ref[:, cs] for cs in heads]
        dos = [do_ref[:, cs] for cs in heads]
        fqs = [fq_ref[hh] for hh in range(HB)]
        lses = [lse_ref[hh] for hh in range(HB)]
        deltas = [jnp.sum(dos[hh].astype(F32) * o_ref[:, cs], axis=1, keepdims=True) for hh, cs in enumerate(heads)]

        def blk(kb, dqs, diag):
            sl = pl.ds(pl.multiple_of(kb * T, T), T)
            out = []
            for hh, cs in enumerate(heads):
                q, do_ = qs[hh], dos[hh]
                k, v = k_ref[sl, cs], v_ref[sl, cs]
                s = lax.dot_general(q, k, _NT, preferred_element_type=F32) * scale + (fqs[hh] - fk_ref[hh, kb])
                p = jnp.exp(s - lses[hh])
                if diag:
                    p = jnp.where(col <= row, p, 0.0)
                ds = p * (lax.dot_general(do_, v, _NT, preferred_element_type=F32) - deltas[hh])
                dv_acc[sl, cs] += lax.dot_general(p.astype(BF16), do_, _TN, preferred_element_type=F32)
                dsb = (ds * scale).astype(BF16)
                dk_acc[sl, cs] += lax.dot_general(dsb, q, _TN, preferred_element_type=F32)
                dfk_acc[hh, kb] -= jnp.sum(ds, axis=0, keepdims=True)
                out.append(dqs[hh] + jnp.dot(dsb, k, preferred_element_type=F32))
            return tuple(out)

        dqs = lax.fori_loop(0, i, lambda kb, st: blk(kb, st, False), (jnp.zeros((T, HD), F32),) * HB)
        dqs = blk(i, dqs, True)
        for hh, cs in enumerate(heads):
            dq_ref[:, cs] = dqs[hh].astype(BF16)

        @pl.when(i == nq - 1)
        def _():
            dk_ref[...] = dk_acc[...].astype(BF16)
            dv_ref[...] = dv_acc[...].astype(BF16)
            dfk_ref[...] = dfk_acc[...]

    vec = pl.BlockSpec((HB, T, 1), lambda h, i: (h, i, 0))
    rowv = pl.BlockSpec((HB, nq, 1, T), lambda h, i: (h, 0, 0, 0))
    oblk = pl.BlockSpec((T, W), lambda h, i: (i, h))
    qblk = pl.BlockSpec((T, W), lambda h, i: (i, NSB // HB + h))
    full = pl.BlockSpec((S, W), lambda h, i: (0, h))
    return _pc(body, name=name, grid=(NFX // HB, nq),
               in_specs=[qblk, pl.BlockSpec((S, W), lambda h, i: (0, (NH + NSB) // HB + h)),
                         pl.BlockSpec((S, W), lambda h, i: (0, (2 * NH + NSB) // HB + h)),
                         qblk, oblk, vec, rowv, vec],
               out_specs=[oblk, full, full, rowv],
               out_shape=[SDS((S, NFX * HD), BF16)] * 3 + [SDS((NFX, nq, 1, T), F32)],
               scratch_shapes=[pltpu.VMEM((S, W), F32), pltpu.VMEM((S, W), F32), pltpu.VMEM((HB, nq, 1, T), F32)],
               compiler_params=_params(2))(qkv, qkv, qkv, do, o, fcol, frow, lse)


def _silu(c_all, name):
    def body(c_ref, o_ref):
        cv = c_ref[...]
        o_ref[...] = cv * jax.nn.sigmoid(cv)

    return _pc(body, name=name, out_shape=SDS(c_all.shape, F32))(c_all)


def _mod_project(cond, w_mod, name):
    L, D, C = w_mod.shape
    tk = _pick(D, (512, 256, 128))

    def body(c_ref, w_ref, o_ref):
        @pl.when(pl.program_id(1) == 0)
        def _():
            o_ref[...] = jnp.zeros_like(o_ref)

        o_ref[...] += jnp.dot(c_ref[...].astype(BF16), w_ref[...].astype(BF16), preferred_element_type=F32)

    return _pc(body, name=name, grid=(L, D // tk),
               in_specs=[pl.BlockSpec((16, tk), lambda l, k: (0, k)), pl.BlockSpec((None, tk, C), lambda l, k: (l, k, 0))],
               out_specs=pl.BlockSpec((None, 16, C), lambda l, k: (l, 0, 0)),
               out_shape=SDS((L, 16, C), F32), compiler_params=_params(2))(cond, w_mod)


def _adam_math(w, g, m, v):
    m = ADAM_B1 * m + (1.0 - ADAM_B1) * g
    v = ADAM_B2 * v + (1.0 - ADAM_B2) * (g * g)
    m_hat = m / (1.0 - ADAM_B1 ** ADAM_STEP)
    v_hat = v / (1.0 - ADAM_B2 ** ADAM_STEP)
    delta = -ADAM_LR * (m_hat / (jnp.sqrt(v_hat) + ADAM_EPS) + ADAM_WD * w)
    return delta, m, v


def _adamw(w, m, v, parts, layer, prev, name):
    L, R, C = w.shape
    NP = parts.shape[0]
    tr = _pick(R, (128, 64, 88, 32, 16, 8))
    nprev = 0 if prev is None else 4

    def body(w_ref, m_ref, v_ref, p_ref, *rest):
        g_ref, d_ref, mo_ref, vo_ref = rest[nprev:]
        g = p_ref[0].astype(F32)
        for j in range(1, NP):
            g = g + p_ref[j].astype(F32)
        delta, mn, vn = _adam_math(w_ref[...], g, m_ref[...], v_ref[...])
        g_ref[...] = g
        d_ref[...] = delta
        mo_ref[...] = mn
        vo_ref[...] = vn

    if tr == R and R > 512:
        tc = _pick(C, (256, 128))
        steps = C // tc
        blk = pl.BlockSpec((None, R, tc), lambda j: (layer, 0, j))
        in_specs = [blk, blk, blk, pl.BlockSpec((NP, R, tc), lambda j: (0, 0, j))]
    else:
        steps = R // tr
        blk = pl.BlockSpec((None, tr, C), lambda i: (layer, i, 0))
        in_specs = [blk, blk, blk, pl.BlockSpec((NP, tr, C), lambda i: (0, i, 0))]
    operands = [w, m, v, parts]
    aliases = {}
    if prev is not None:
        in_specs += [ANY] * 4
        operands += list(prev)
        aliases = {4 + q: q for q in range(4)}
    return _pc(body, name=name, grid=(steps,), in_specs=in_specs, out_specs=[blk] * 4,
               out_shape=[SDS(w.shape, F32)] * 4, input_output_aliases=aliases,
               compiler_params=_params(1))(*operands)


def _adamw_mod(w, m, v, cond_t, dmod, name):
    L, D, C = w.shape
    tr = _pick(D, (128, 64))

    def body(w_ref, m_ref, v_ref, ct_ref, dm_ref, g_ref, d_ref, mo_ref, vo_ref):
        ct = ct_ref[...]
        g = ct[:, 0:1] * dm_ref[0]
        for b in range(1, NDEV):
            g = g + ct[:, b:b + 1] * dm_ref[b]
        delta, mn, vn = _adam_math(w_ref[...], g, m_ref[...], v_ref[...])
        g_ref[...] = g
        d_ref[...] = delta
        mo_ref[...] = mn
        vo_ref[...] = vn

    blk = pl.BlockSpec((None, tr, C), lambda l, i: (l, i, 0))
    return _pc(body, name=name, grid=(L, D // tr),
               in_specs=[blk, blk, blk, pl.BlockSpec((tr, LANES), lambda l, i: (i, 0)),
                         pl.BlockSpec((NDEV, None, 1, C), lambda l, i: (0, l, 0, 0))],
               out_specs=[blk] * 4, out_shape=[SDS(w.shape, F32)] * 4, compiler_params=_params(2))(w, m, v, cond_t, dmod)


def _sum_parts(parts, name):
    NP, R, C = parts.shape
    tr = _pick(R, (256, 128, 64, 32, 16, 8))

    def body(p_ref, o_ref):
        g = p_ref[0]
        for j in range(1, NP):
            g = g + p_ref[j]
        o_ref[...] = g

    return _pc(body, name=name, grid=(R // tr,), in_specs=[pl.BlockSpec((NP, tr, C), lambda i: (0, i, 0))],
               out_specs=pl.BlockSpec((tr, C), lambda i: (i, 0)), out_shape=SDS((R, C), F32),
               compiler_params=_params(1))(parts)


def _pack(vecs, rows=None):
    flat = jnp.concatenate([v.reshape(-1).astype(F32) for v in vecs])
    n = flat.shape[0]
    r = rows if rows is not None else -(-n // (256 * LANES)) * 256
    return jnp.pad(flat, (0, r * LANES - n)).reshape(r, LANES)


def _unpack(packed, shapes):
    flat = packed.reshape(-1)
    out, off = [], 0
    for s in shapes:
        n = 1
        for d in s:
            n *= d
        out.append(flat[off:off + n].reshape(s))
        off += n
    return out


def kernel(x, c, w_mod, b_mod, norm_gain, w_attn_in, b_forget, w_attn_out, w_pool, pool_scale, w_up, conv_w, conv_b, w_down, final_gain, loss_target, m_w_mod, m_b_mod, m_norm_gain, m_w_attn_in, m_b_forget, m_w_attn_out, m_w_pool, m_pool_scale, m_w_up, m_conv_w, m_conv_b, m_w_down, m_final_gain, v_w_mod, v_b_mod, v_norm_gain, v_w_attn_in, v_b_forget, v_w_attn_out, v_w_pool, v_pool_scale, v_w_up, v_conv_w, v_conv_b, v_w_down, v_final_gain):
    _, S, D = x.shape
    L = w_mod.shape[0]
    CM = w_mod.shape[2]
    NFX = b_forget.shape[1]
    NH = 2 * NFX
    NSB = NH - NFX
    HD = D // NH
    CI = w_attn_in.shape[2]
    CU = w_up.shape[2]
    F2 = NDEV * CU
    DFF = F2 // 2
    G = len(POOL_WINDOWS)
    CG = D // G
    T = _pick(S, (256, 128))
    me = _idx(_me())
    x0 = x[0]
    target = loss_target[0]

    def layer_shards(l, group):
        if group == 1:
            shards = [w_up[l], w_down[l]]
        elif l % 2 == 0:
            shards = [w_attn_in[l // 2].T, w_attn_out[l // 2]]
        else:
            shards = [w_pool[l // 2].reshape(G * (CG // NDEV), CG)]
        return [s.astype(BF16) for s in shards]

    small_shapes = [(1, D), norm_gain.shape, pool_scale.shape, conv_w.shape]
    small_all = _all_gather([_pack([c, norm_gain, pool_scale, conv_w])], "gather_small")[0]
    per_dev = [_unpack(small_all[j], small_shapes) for j in range(NDEV)]
    c_all = jnp.concatenate([p[0] for p in per_dev] + [jnp.zeros((16 - NDEV, D), F32)], axis=0)
    gain_f = jnp.concatenate([p[1] for p in per_dev], axis=2)
    pscale_f = jnp.concatenate([p[2] for p in per_dev], axis=1)
    convw_f = jnp.concatenate([p[3] for p in per_dev], axis=2)

    cond_all = _silu(c_all, "cond_silu")
    mod_part = _mod_project(cond_all, w_mod, "mod_project")
    mod_all = _all_gather([mod_part], "gather_mod")[0]
    mod = lax.dynamic_index_in_dim(mod_all, me, axis=2, keepdims=False)
    mod = mod.transpose(1, 0, 2).reshape(L, NDEV * CM) + b_mod
    mods = mod.reshape(L, 6, 1, D)

    inflight = {}

    def gather_start(l, group, after):
        inflight[l, group], t = _gather_start(layer_shards(l, group), me, after, f"gather{l}_{group}_start")
        return t

    def gather_forward(l, group, after):
        inflight[l, group], t = _gather_forward(inflight[l, group], after, f"gather{l}_{group}_forward")
        return t

    def gather_wait(l, group, after):
        return _gather_wait(inflight.pop((l, group)), after, f"gather{l}_{group}_wait")

    def starts_at(l):
        if l % 2 == 0:
            return [(k, g) for k, g in [(l + 1, 0), (l + 1, 1), (l + 2, 0)] if k < L]
        return [(k, g) for k, g in [(l + 1, 1)] if k < L]

    tok = gather_start(0, 0, [mods])
    tok = gather_start(0, 1, [tok])
    tok = gather_forward(0, 0, [tok])
    mixer_w = gather_wait(0, 0, [tok])

    saved = []
    xl = x0
    for l in range(L):
        i = l // 2
        sh1, sc1, g1, sh2, sc2, g2 = [mods[l, q] for q in range(6)]
        gn1, gn2 = gain_f[l, 0:1], gain_f[l, 1:2]
        st = {"x": xl}
        after = [mixer_w[0]]
        for k, g in starts_at(l):
            after = [gather_start(k, g, after)]
        if starts_at(l):
            sh1 = sh1 + after[0][0, 0]
        if l % 2 == 0:
            win_t = mixer_w[0].reshape(NDEV * CI, D)
            wf_t = jnp.pad(win_t[3 * D:], ((0, LANES - NFX), (0, 0)))
            wout = mixer_w[1].reshape(D, D)
            h1 = _norm_mod(xl, gn1, sc1, sh1, BF16, f"norm1_{l}")
            qkv = _matmul(h1, win_t, mode="nt", name=f"qkv_{l}", out_dtype=BF16, n=3 * D)
            flog = _matmul(h1, wf_t, mode="nt", name=f"flog_{l}", out_dtype=F32, tn=LANES)
            bfp = jnp.pad(b_forget[i], (0, LANES - NFX)).reshape(1, LANES)
            Fc = _forget_cumsum(flog, bfp, f"fcum_{l}")
            f8 = Fc[:, :NFX].T
            fcol, frow = f8[:, :, None], f8.reshape(NFX, S // T, 1, T)
            o_sb, tot = _sb_fwd(qkv, NH, NSB, HD, T, f"sb_fwd_{l}")
            o_fx, o_fx32, lse = _fox_fwd(qkv, fcol, frow, NH, NSB, HD, T, f"fox_fwd_{l}")
            o = jnp.concatenate([o_sb, o_fx], axis=1)
            g1 = g1 + gather_forward(l, 1, [o])[0, 0]
            x1, y1 = _matmul(o, wout, mode="nn", name=f"attn_out_{l}", out_dtype=F32, res=xl, gate=g1, y_dtype=BF16)
            st.update(h1=h1, qkv=qkv, flog=flog, bfp=bfp, fcol=fcol, frow=frow, tot=tot, lse=lse, o=o, o_fx=o_fx32,
                      win_t=win_t, wf_t=wf_t, wout=wout, y1=y1)
        else:
            wpool = mixer_w[0].reshape(NDEV, G, CG // NDEV, CG).transpose(1, 0, 2, 3).reshape(G, CG, CG)
            h1 = _norm_mod(xl, gn1, sc1, sh1, F32, f"norm1_{l}")
            dpool = _pool_diff(h1, f"pool_diff_{l}")
            g1 = g1 + gather_forward(l, 1, [dpool])[0, 0]
            gp = g1 * pscale_f[i:i + 1]
            x1, e1 = _pool_mm(dpool, wpool, xl, gp, f"pool_mm_{l}")
            st.update(dpool=dpool, wpool=wpool, gp=gp, y1=e1)
        wup_g, wdown_g = gather_wait(l, 1, [x1])
        wdown_f = wdown_g.reshape(DFF, D)
        h2 = _norm_mod(x1, gn2, sc2, sh2, BF16, f"norm2_{l}")
        u = _matmul(h2, wup_g, mode="nn", name=f"ffn_up_{l}", out_dtype=F32, b_blocked=True)
        cb = conv_b[l].reshape(1, F2)
        act = _convgate_fwd(u, convw_f[l], cb, f"convgate_{l}")
        if l + 1 < L:
            g2 = g2 + gather_forward(l + 1, 0, [act])[0, 0]
        x2, y2 = _matmul(act, wdown_f, mode="nn", name=f"ffn_down_{l}", out_dtype=F32, res=x1, gate=g2, y_dtype=BF16,
                         tk=CU)
        if l + 1 < L:
            mixer_w = gather_wait(l + 1, 0, [x2])
        st.update(x1=x1, h2=h2, u=u, cb=cb, act=act, y2=y2, wup_g=wup_g, wdown_f=wdown_f,
                  mod=(sh1, sc1, g1, sh2, sc2, g2), gn=(gn1, gn2))
        saved.append(st)
        xl = x2

    dx, d_fgain, loss_tile = _loss_head(xl, target, final_gain.reshape(1, D), "loss_head")
    loss = lax.psum(loss_tile[0, 0], ("x", "y", "c"))

    dmod_rows = [None] * L
    d_gain = [None] * L
    d_convw = [None] * L
    d_convb = [None] * L
    d_pscale = [None] * (L // 2)
    d_bf = [None] * ((L + 1) // 2)
    big = {"w_up": None, "w_down": None, "w_attn_out": None, "w_pool": None}
    attn_in_t = [None] * ((L + 1) // 2)

    def update(key, w, m, v, bufs, layer, tag):
        big[key] = _adamw(w, m, v, bufs, layer, big[key], f"adamw_{tag}")

    exchanges = []

    def exchange_start(l, keys, parts, after, two_level=False):
        tag = f"{l}_{'_'.join(keys)}"
        if two_level:
            state, t = _pair_start(parts, after, f"pair{tag}_start")
        else:
            state, t = _scatter_start(parts, me, after, f"scatter{tag}_start")
        exchanges.append([l, keys, tag, "pair" if two_level else "direct", state])
        return t

    def exchanges_advance(after):
        t = None
        for entry in [e for e in exchanges if e[3] == "pair"]:
            tag = entry[2]
            mine, theirs = _pair_wait(entry[4], after, f"pair{tag}_wait")
            sums = [_pair_sum(a, b, f"pair{tag}_sum{q}") for q, (a, b) in enumerate(zip(mine, theirs))]
            entry[4], t = _chips_start(sums, [], f"chips{tag}_start")
            entry[3] = "chips"
        return t

    def exchanges_finish(after, first_layer):
        for entry in [e for e in exchanges if e[0] >= first_layer]:
            exchanges.remove(entry)
            pl_, keys, tag, stage, state = entry
            if stage == "chips":
                bufs = _chips_wait(state, after, f"chips{tag}_wait")
            else:
                bufs = _scatter_wait(state, after, f"scatter{tag}_wait")
            for key, buf in zip(keys, bufs):
                if key == "up":
                    update("w_up", w_up, m_w_up, v_w_up, buf, pl_, f"up_{pl_}")
                elif key == "down":
                    update("w_down", w_down, m_w_down, v_w_down, buf, pl_, f"down_{pl_}")
                elif key == "out":
                    update("w_attn_out", w_attn_out, m_w_attn_out, v_w_attn_out, buf, pl_ // 2, f"attn_out_{pl_}")
                elif key == "in":
                    slab = lambda a: a[pl_ // 2].T[None]
                    attn_in_t[pl_ // 2] = _adamw(slab(w_attn_in), slab(m_w_attn_in), slab(v_w_attn_in), buf, 0, None,
                                                 f"adamw_attn_in_{pl_}")
                else:
                    wp3 = lambda a: a.reshape(a.shape[0], G * (CG // NDEV), CG)
                    update("w_pool", wp3(w_pool), wp3(m_w_pool), wp3(v_w_pool), buf, pl_ // 2, f"pool_{pl_}")

    tok = None
    for l in reversed(range(L)):
        i = l // 2
        st = saved[l]
        sh1, sc1, g1, sh2, sc2, g2 = st["mod"]
        if tok is not None:
            g2 = g2 + tok[0, 0]
        gn1, gn2 = st["gn"]
        dffn, dg2 = _gate_bwd(dx, st["y2"], g2, f"gate2_bwd_{l}")
        dact = _matmul(dffn, st["wdown_f"], mode="nt", name=f"ffn_down_dx_{l}", out_dtype=F32, tn=CU)
        dwdown = _matmul(st["act"], dffn, mode="tn", name=f"ffn_down_dw_{l}", out_dtype=BF16, tm=CU)
        dwdown = dwdown.reshape(NDEV, DFF // NDEV, D)
        du, dcw, dcb = _convgate_bwd(st["u"], dact, convw_f[l], st["cb"], f"convgate_bwd_{l}")
        dh2 = _matmul(du, st["wup_g"], mode="nt", name=f"ffn_up_dx_{l}", out_dtype=F32, a_split=True, b_blocked=True,
                      tn=2048)
        dwup = _matmul(st["h2"], du, mode="tn", name=f"ffn_up_dw_{l}", out_dtype=BF16, tn=CU, b_split=True,
                       out_blocked=True)
        dx, dgn2, dsc2, dsh2 = _norm_mod_bwd(dh2, st["x1"], dx, gn2, sc2, f"norm2_bwd_{l}")
        d_convw[l] = jnp.concatenate([dcw[0], dcw[1]], axis=1)
        d_convb[l] = jnp.concatenate([dcb[0], dcb[1]], axis=1)
        two_level = l < 2
        tok = exchange_start(l, ["up", "down"], [dwup, dwdown], [], two_level=two_level)
        if l % 2 == 0:
            dy1, dg1 = _gate_bwd(dx, st["y1"], g1 + tok[0, 0], f"gate1_bwd_{l}")
            do = _matmul(dy1, st["wout"], mode="nt", name=f"attn_out_dx_{l}", out_dtype=BF16)
            dwout = _matmul(st["o"], dy1, mode="tn", name=f"attn_out_dw_{l}", out_dtype=BF16).reshape(NDEV, D // NDEV, D)
            tot = st["tot"]
            if l == 0:
                tot = tot + exchange_start(l, ["out"], [dwout], [])[0, 0]
            dq_s, dk_s, dv_s = _sb_bwd(st["qkv"], do, tot, NH, NSB, HD, T, f"sb_bwd_{l}")
            lse = st["lse"] + exchanges_advance([dq_s])[0, 0] if two_level else st["lse"]
            dq_f, dk_f, dv_f, dfk = _fox_bwd(st["qkv"], do, st["o_fx"], st["fcol"], st["frow"], lse, NH, NSB, HD, T,
                                             f"fox_bwd_{l}")
            dqkv = jnp.concatenate([dq_s, dq_f, dk_s, dk_f, dv_s, dv_f], axis=1)
            dF = jnp.pad(dfk.reshape(NFX, S).T, ((0, 0), (0, LANES - NFX)))
            dflog, dbf = _forget_cumsum_bwd(dF, st["flog"], st["bfp"], f"fcum_bwd_{l}")
            dflog_b = dflog.astype(BF16)
            dh1 = _matmul(dqkv, st["win_t"], mode="nn", name=f"qkv_dx_{l}", out_dtype=F32)
            dh1 = _matmul(dflog_b, st["wf_t"], mode="nn", name=f"flog_dx_{l}", out_dtype=F32, res=dh1,
                          gate=jnp.ones((1, D), F32))
            dwqkv_t = _matmul(dqkv, st["h1"], mode="tn", name=f"qkv_dw_{l}", out_dtype=BF16)
            dwf_t = _matmul(dflog_b, st["h1"], mode="tn", name=f"flog_dw_{l}", out_dtype=BF16, tm=LANES)
            dwin_t = jnp.concatenate([dwqkv_t, dwf_t[:NFX]], axis=0).reshape(NDEV, CI, D)
            d_bf[i] = dbf[0, :NFX]
            keys, parts = (["in"], [dwin_t]) if l == 0 else (["in", "out"], [dwin_t, dwout])
        else:
            de, dgp = _gate_bwd(dx, st["y1"], st["gp"] + tok[0, 0], f"gate1_bwd_{l}")
            dg1 = dgp * pscale_f[i:i + 1]
            d_pscale[i] = dgp * g1
            dd, dwp = _pool_mm_bwd(de, st["dpool"], st["wpool"], f"pool_mm_bwd_{l}")
            if two_level:
                gn1 = gn1 + exchanges_advance([dd])[0, 0]
            dh1 = _pool_diff_bwd(dd, f"pool_diff_bwd_{l}")
            keys = ["pool"]
            parts = [dwp.reshape(G, NDEV, CG // NDEV, CG).transpose(1, 0, 2, 3).reshape(NDEV, G * (CG // NDEV), CG)]
        dx, dgn1, dsc1, dsh1 = _norm_mod_bwd(dh1, st["x"], dx, gn1, sc1, f"norm1_bwd_{l}")
        dmod_rows[l] = jnp.concatenate([dsh1, dsc1, dg1, dsh2, dsc2, dg2], axis=1)
        d_gain[l] = jnp.concatenate([dgn1, dgn2], axis=0)
        if l > 0:
            tok = exchange_start(l, keys, parts, [])
            exchanges_finish([dx, tok], l + 1)

    grad_x = dx[None]

    small_grads = [jnp.stack(dmod_rows), jnp.stack(d_gain), jnp.stack(d_pscale), jnp.stack(d_convw),
                   jnp.stack(d_convb), jnp.stack(d_bf), d_fgain]
    sg_shapes = [(L, 6 * D), (L, 2, D), (L // 2, D), (L, 3, F2), (L, F2), ((L + 1) // 2, NFX), (D,)]
    sg_all = _all_gather([_pack(small_grads)], "gather_small_grads")[0]
    tok = exchange_start(0, keys, parts, [sg_all])
    exchanges_finish([tok], 1)
    sg_all = sg_all + tok[0, 0]
    sg_sum = _unpack(_sum_parts(sg_all, "sum_small_grads"), sg_shapes)
    g_bmod, g_gain_f, g_pscale_f, g_convw_f, g_convb, g_bf, g_fgain = sg_sum
    shard = lambda a, n, axis: lax.dynamic_slice_in_dim(a, me * n, n, axis=axis)
    g_small = [g_bmod, shard(g_gain_f, D // NDEV, 2), shard(g_pscale_f, D // NDEV, 1), shard(g_convw_f, CU, 2), g_convb,
               g_bf, g_fgain]
    w_small = [b_mod, norm_gain, pool_scale, conv_w, conv_b, b_forget, final_gain]
    m_small = [m_b_mod, m_norm_gain, m_pool_scale, m_conv_w, m_conv_b, m_b_forget, m_final_gain]
    v_small = [v_b_mod, v_norm_gain, v_pool_scale, v_conv_w, v_conv_b, v_b_forget, v_final_gain]
    small_raw = _adamw(_pack(w_small)[None], _pack(m_small)[None], _pack(v_small)[None], _pack(g_small)[None], 0, None,
                       "adamw_small")
    small_out = [_unpack(a[0], [w.shape for w in w_small]) for a in small_raw]

    dmod_all = jnp.stack([_unpack(sg_all[j], sg_shapes[:1])[0] for j in range(NDEV)])
    dmod_mine = lax.dynamic_slice_in_dim(dmod_all.reshape(NDEV, L, NDEV, CM), me, 1, axis=2)
    cond_t = jnp.pad(cond_all[:NDEV].T, ((0, 0), (0, LANES - NDEV)))
    mod_out = _adamw_mod(w_mod, m_w_mod, v_w_mod, cond_t, dmod_mine, "adamw_mod")
    exchanges_finish([mod_out[0], small_raw[0]], 0)

    pool4 = lambda a: a.reshape(w_pool.shape)
    names = ["w_mod", "b_mod", "norm_gain", "w_attn_in", "b_forget", "w_attn_out", "w_pool", "pool_scale", "w_up", "conv_w",
             "conv_b", "w_down", "final_gain"]
    small_pos = {"b_mod": 0, "norm_gain": 1, "pool_scale": 2, "conv_w": 3, "conv_b": 4, "b_forget": 5, "final_gain": 6}
    outs = []
    for kind in range(4):
        for nm in names:
            if nm == "w_mod":
                outs.append(mod_out[kind])
            elif nm in small_pos:
                outs.append(small_out[kind][small_pos[nm]])
            elif nm == "w_pool":
                outs.append(pool4(big[nm][kind]))
            elif nm == "w_attn_in":
                outs.append(jnp.stack([per_layer[kind][0] for per_layer in attn_in_t]).transpose(0, 2, 1))
            else:
                outs.append(big[nm][kind])
    return (loss, grad_x, *outs)
```

```python
import jax
import jax.numpy as jnp
from jax import lax
from jax.experimental import pallas as pl
from jax.experimental.pallas import tpu as pltpu

NDEV = 8
F32 = jnp.float32
BF16 = jnp.bfloat16
MESH = pl.DeviceIdType.MESH
VMEM_LIMIT_BYTES = 56 * 1024 * 1024
LANES = 128
POOL_WINDOWS = (2, 4, 8, 16)
EPS = 1e-6
ADAM_LR = 0.001
ADAM_B1 = 0.9
ADAM_B2 = 0.999
ADAM_EPS = 1e-08
ADAM_WD = 0.01
ADAM_STEP = 10
NEG_BIG = -1e30
SDS = jax.ShapeDtypeStruct
ANY = pl.BlockSpec(memory_space=pl.ANY)


def _pc(body, **kw):
    return pl.pallas_call(body, **kw)


def _params(n_axes):
    return pltpu.CompilerParams(dimension_semantics=("arbitrary",) * n_axes, vmem_limit_bytes=VMEM_LIMIT_BYTES)


def _pick(n, prefs):
    for p in prefs:
        if p <= n and n % p == 0:
            return p
    return n


def _idx(p):
    return 4 * p[0] + 2 * p[1] + p[2]


def _me():
    return lax.axis_index("x"), lax.axis_index("y"), lax.axis_index("c")


def _all_gather(arrs, name):
    n = len(arrs)

    def body(*refs):
        ins, outs = refs[:n], refs[n:2 * n]
        send_sems, recv_sems, local_sems = refs[2 * n:]
        x, y, c = _me()
        me, sib = (x, y, c), (x, y, 1 - c)
        chips = [(1 - x, y), (x, 1 - y), (1 - x, 1 - y)]

        def copy(t, k, block, to, src=None):
            dst = outs[t].at[_idx(block)]
            return pltpu.make_async_remote_copy(
                src_ref=dst if src is None else src, dst_ref=dst,
                send_sem=send_sems.at[7 * t + k], recv_sem=recv_sems.at[7 * t + k],
                device_id=to, device_id_type=MESH)

        mine = [pltpu.make_async_copy(ins[t], outs[t].at[_idx(me)], local_sems.at[t]) for t in range(n)]
        for cp in mine:
            cp.start()
        first = []
        for t in range(n):
            first.append(copy(t, 0, me, sib, src=ins[t]))
            for j, chip in enumerate(chips):
                first.append(copy(t, 1 + j, me, (*chip, c), src=ins[t]))
        for cp in first:
            cp.start()
        passed = []
        for t in range(n):
            for j, chip in enumerate(chips):
                copy(t, 1 + j, (*chip, c), me).wait_recv()
                cp = copy(t, 4 + j, (*chip, c), sib)
                cp.start()
                passed.append(cp)
        for t in range(n):
            copy(t, 0, sib, me).wait_recv()
            for j, chip in enumerate(chips):
                copy(t, 4 + j, (*chip, 1 - c), me).wait_recv()
        for cp in first + passed:
            cp.wait_send()
        for cp in mine:
            cp.wait()

    return _pc(
        body, name=name,
        out_shape=[SDS((NDEV,) + a.shape, a.dtype) for a in arrs],
        in_specs=[ANY] * n, out_specs=[ANY] * n,
        scratch_shapes=[pltpu.SemaphoreType.DMA((7 * n,)), pltpu.SemaphoreType.DMA((7 * n,)),
                        pltpu.SemaphoreType.DMA((n,))],
    )(*arrs)


HBM = pl.BlockSpec(memory_space=pltpu.HBM)
SEM = pl.BlockSpec(memory_space=pltpu.SEMAPHORE)
EFFECT = pltpu.SideEffectType.DATAFLOW_SIDE_EFFECTING
TOKEN = SDS((8, LANES), F32)


def _hbm(a):
    return pltpu.with_memory_space_constraint(a, pltpu.HBM)


def _landing(block):
    return lax.empty((NDEV,) + block.shape, block.dtype)


def _split_call(body, name, n_thru, thru, sems_in, after, sems_out):
    n_sem = len(sems_out)
    operands = [_hbm(a) for a in thru] + list(sems_in) + list(after)
    in_specs = [HBM] * n_thru + [SEM] * len(sems_in) + [ANY] * len(after)
    out_shape = [pltpu.SemaphoreType.DMA((k,)) for k in sems_out] + [pltpu.HBM(a.shape, a.dtype) for a in thru] + [TOKEN]
    out_specs = [SEM] * n_sem + [HBM] * n_thru + [pl.BlockSpec(memory_space=pltpu.VMEM)]
    outs = _pc(body, name=name, in_specs=in_specs, out_specs=out_specs, out_shape=out_shape,
               input_output_aliases={q: n_sem + q for q in range(n_thru)},
               compiler_params=pltpu.CompilerParams(has_side_effects=EFFECT))(*operands)
    return list(outs[:n_sem]), list(outs[n_sem:n_sem + n_thru]), outs[-1]


def _gather_start(shards, me, after, name):
    n = len(shards)
    lands = [_landing(s) for s in shards]

    def body(*refs):
        shard_refs, land_refs = refs[:n], refs[n:2 * n]
        send_sems, recv_sems, local_sems = refs[2 * n + len(after):2 * n + len(after) + 3]
        x, y, c = _me()
        me_i = _idx((x, y, c))
        peers = [(x, y, 1 - c), (1 - x, y, c), (x, 1 - y, c), (1 - x, 1 - y, c)]
        for t in range(n):
            pltpu.make_async_copy(shard_refs[t], land_refs[t].at[me_i], local_sems.at[t]).start()
            for k, p in enumerate(peers):
                pltpu.make_async_remote_copy(
                    src_ref=shard_refs[t], dst_ref=land_refs[t].at[me_i], send_sem=send_sems.at[4 * t + k],
                    recv_sem=recv_sems.at[4 * t + k], device_id=p, device_id_type=MESH).start()
        refs[-1][...] = jnp.zeros((8, LANES), F32)

    sems, thru, token = _split_call(body, name, 2 * n, list(shards) + lands, [], after, [4 * n, 4 * n, n])
    return dict(n=n, sems=sems, shards=thru[:n], lands=thru[n:]), token


def _gather_forward(st, after, name):
    n = st["n"]

    def body(*refs):
        shard_refs, land_refs = refs[:n], refs[n:2 * n]
        send1, recv1, local1 = refs[2 * n:2 * n + 3]
        send2, recv2 = refs[2 * n + 3 + len(after)], refs[2 * n + 4 + len(after)]
        x, y, c = _me()
        sib = (x, y, 1 - c)
        senders = [sib, (1 - x, y, c), (x, 1 - y, c), (1 - x, 1 - y, c)]
        for t in range(n):
            pltpu.make_async_copy(shard_refs[t], land_refs[t].at[_idx((x, y, c))], local1.at[t]).wait()
            for k, p in enumerate(senders):
                cp = pltpu.make_async_remote_copy(
                    src_ref=shard_refs[t], dst_ref=land_refs[t].at[_idx(p)], send_sem=send1.at[4 * t + k],
                    recv_sem=recv1.at[4 * t + k], device_id=p, device_id_type=MESH)
                cp.wait_send()
                cp.wait_recv()
        for t in range(n):
            for j, p in enumerate(senders[1:]):
                slab = land_refs[t].at[_idx(p)]
                pltpu.make_async_remote_copy(
                    src_ref=slab, dst_ref=slab, send_sem=send2.at[3 * t + j], recv_sem=recv2.at[3 * t + j],
                    device_id=sib, device_id_type=MESH).start()
        refs[-1][...] = jnp.zeros((8, LANES), F32)

    sems, thru, token = _split_call(body, name, 2 * n, st["shards"] + st["lands"], st["sems"], after, [3 * n, 3 * n])
    return dict(n=n, sems=sems, shards=thru[:n], lands=thru[n:]), token


def _gather_wait(st, after, name):
    n = st["n"]

    def body(*refs):
        land_refs = refs[n:2 * n]
        send2, recv2 = refs[2 * n], refs[2 * n + 1]
        x, y, c = _me()
        sib = (x, y, 1 - c)
        for t in range(n):
            for j, chip in enumerate([(1 - x, y), (x, 1 - y), (1 - x, 1 - y)]):
                sent, got = land_refs[t].at[_idx((*chip, c))], land_refs[t].at[_idx((*chip, 1 - c))]
                cp = pltpu.make_async_remote_copy(
                    src_ref=sent, dst_ref=got, send_sem=send2.at[3 * t + j], recv_sem=recv2.at[3 * t + j],
                    device_id=sib, device_id_type=MESH)
                cp.wait_send()
                cp.wait_recv()
        refs[-1][...] = jnp.zeros((8, LANES), F32)

    _, thru, _ = _split_call(body, name, 2 * n, st["shards"] + st["lands"], st["sems"], after, [])
    return thru[n:]


def _scatter_start(parts, me, after, name):
    n = len(parts)
    lands = [_landing(p[0]) for p in parts]

    def body(*refs):
        part_refs, land_refs = refs[:n], refs[n:2 * n]
        send_sems, recv_sems, local_sems = refs[2 * n + len(after):2 * n + len(after) + 3]
        x, y, c = _me()
        me_i = _idx((x, y, c))
        for t in range(n):
            pltpu.make_async_copy(part_refs[t].at[me_i], land_refs[t].at[me_i], local_sems.at[t]).start()
            for r in range(1, NDEV):
                p = (1 - x if r & 4 else x, 1 - y if r & 2 else y, 1 - c if r & 1 else c)
                pltpu.make_async_remote_copy(
                    src_ref=part_refs[t].at[_idx(p)], dst_ref=land_refs[t].at[me_i], send_sem=send_sems.at[7 * t + r - 1],
                    recv_sem=recv_sems.at[7 * t + r - 1], device_id=p, device_id_type=MESH).start()
        refs[-1][...] = jnp.zeros((8, LANES), F32)

    sems, thru, token = _split_call(body, name, 2 * n, list(parts) + lands, [], after, [7 * n, 7 * n, n])
    return dict(n=n, sems=sems, parts=thru[:n], lands=thru[n:]), token


def _scatter_wait(st, after, name):
    n = st["n"]

    def body(*refs):
        part_refs, land_refs = refs[:n], refs[n:2 * n]
        send_sems, recv_sems, local_sems = refs[2 * n:2 * n + 3]
        x, y, c = _me()
        me_i = _idx((x, y, c))
        for t in range(n):
            pltpu.make_async_copy(part_refs[t].at[me_i], land_refs[t].at[me_i], local_sems.at[t]).wait()
            for r in range(1, NDEV):
                p = (1 - x if r & 4 else x, 1 - y if r & 2 else y, 1 - c if r & 1 else c)
                cp = pltpu.make_async_remote_copy(
                    src_ref=part_refs[t].at[_idx(p)], dst_ref=land_refs[t].at[_idx(p)], send_sem=send_sems.at[7 * t + r - 1],
                    recv_sem=recv_sems.at[7 * t + r - 1], device_id=p, device_id_type=MESH)
                cp.wait_send()
                cp.wait_recv()
        refs[-1][...] = jnp.zeros((8, LANES), F32)

    _, thru, _ = _split_call(body, name, 2 * n, st["parts"] + st["lands"], st["sems"], after, [])
    return thru[n:]


def _pair_start(parts, after, name):
    n = len(parts)
    recvs = [lax.empty((4,) + p.shape[1:], p.dtype) for p in parts]

    def body(*refs):
        part_refs, recv_refs = refs[:n], refs[n:2 * n]
        send_sems, recv_sems = refs[2 * n + len(after)], refs[2 * n + len(after) + 1]
        x, y, c = _me()
        for t in range(n):
            for q in range(4):
                pltpu.make_async_remote_copy(
                    src_ref=part_refs[t].at[2 * q + (1 - c)], dst_ref=recv_refs[t].at[q], send_sem=send_sems.at[4 * t + q],
                    recv_sem=recv_sems.at[4 * t + q], device_id=(x, y, 1 - c), device_id_type=MESH).start()
        refs[-1][...] = jnp.zeros((8, LANES), F32)

    sems, thru, token = _split_call(body, name, 2 * n, list(parts) + recvs, [], after, [4 * n, 4 * n])
    return dict(n=n, sems=sems, parts=thru[:n], recvs=thru[n:]), token


def _pair_wait(st, after, name):
    n = st["n"]

    def body(*refs):
        part_refs, recv_refs = refs[:n], refs[n:2 * n]
        send_sems, recv_sems = refs[2 * n], refs[2 * n + 1]
        x, y, c = _me()
        for t in range(n):
            for q in range(4):
                cp = pltpu.make_async_remote_copy(
                    src_ref=part_refs[t].at[2 * q + (1 - c)], dst_ref=recv_refs[t].at[q], send_sem=send_sems.at[4 * t + q],
                    recv_sem=recv_sems.at[4 * t + q], device_id=(x, y, 1 - c), device_id_type=MESH)
                cp.wait_send()
                cp.wait_recv()
        refs[-1][...] = jnp.zeros((8, LANES), F32)

    _, thru, _ = _split_call(body, name, 2 * n, st["parts"] + st["recvs"], st["sems"], after, [])
    return thru[:n], thru[n:]


def _pair_sum(part, recv, name):
    _, R, C = recv.shape
    tr = _pick(R, (256, 128, 64, 88, 32, 16, 8))

    def body(core_ref, a_ref, b_ref, o_ref):
        o_ref[...] = (a_ref[...].astype(F32) + b_ref[...].astype(F32)).astype(o_ref.dtype)

    grid_spec = pltpu.PrefetchScalarGridSpec(
        num_scalar_prefetch=1, grid=(4, R // tr),
        in_specs=[pl.BlockSpec((None, None, tr, C), lambda q, i, core: (q, core[0], i, 0)),
                  pl.BlockSpec((None, tr, C), lambda q, i, core: (q, i, 0))],
        out_specs=pl.BlockSpec((None, tr, C), lambda q, i, core: (q, i, 0)))
    core = lax.axis_index("c").astype(jnp.int32).reshape(1)
    return _pc(body, name=name, grid_spec=grid_spec, out_shape=SDS(recv.shape, recv.dtype),
               compiler_params=_params(2))(core, part.reshape((4, 2) + part.shape[1:]), recv)


def _chips_start(sums, after, name):
    n = len(sums)
    lands = [lax.empty(s.shape, s.dtype) for s in sums]

    def body(*refs):
        sum_refs, land_refs = refs[:n], refs[n:2 * n]
        send_sems, recv_sems, local_sems = refs[2 * n + len(after):2 * n + len(after) + 3]
        x, y, c = _me()
        my_chip = 2 * x + y
        for t in range(n):
            pltpu.make_async_copy(sum_refs[t].at[my_chip], land_refs[t].at[my_chip], local_sems.at[t]).start()
            for j, (px, py) in enumerate([(1 - x, y), (x, 1 - y), (1 - x, 1 - y)]):
                pltpu.make_async_remote_copy(
                    src_ref=sum_refs[t].at[2 * px + py], dst_ref=land_refs[t].at[my_chip], send_sem=send_sems.at[3 * t + j],
                    recv_sem=recv_sems.at[3 * t + j], device_id=(px, py, c), device_id_type=MESH).start()
        refs[-1][...] = jnp.zeros((8, LANES), F32)

    sems, thru, token = _split_call(body, name, 2 * n, list(sums) + lands, [], after, [3 * n, 3 * n, n])
    return dict(n=n, sems=sems, sums=thru[:n], lands=thru[n:]), token


def _chips_wait(st, after, name):
    n = st["n"]

    def body(*refs):
        sum_refs, land_refs = refs[:n], refs[n:2 * n]
        send_sems, recv_sems, local_sems = refs[2 * n:2 * n + 3]
        x, y, c = _me()
        my_chip = 2 * x + y
        for t in range(n):
            pltpu.make_async_copy(sum_refs[t].at[my_chip], land_refs[t].at[my_chip], local_sems.at[t]).wait()
            for j, (px, py) in enumerate([(1 - x, y), (x, 1 - y), (1 - x, 1 - y)]):
                cp = pltpu.make_async_remote_copy(
                    src_ref=sum_refs[t].at[2 * px + py], dst_ref=land_refs[t].at[2 * px + py], send_sem=send_sems.at[3 * t + j],
                    recv_sem=recv_sems.at[3 * t + j], device_id=(px, py, c), device_id_type=MESH)
                cp.wait_send()
                cp.wait_recv()
        refs[-1][...] = jnp.zeros((8, LANES), F32)

    _, thru, _ = _split_call(body, name, 2 * n, st["sums"] + st["lands"], st["sems"], after, [])
    return thru[n:]


def _matmul(a, b, *, mode, name, out_dtype, tm=1024, tn=1024, tk=2048, b_blocked=False, out_blocked=False,
            a_split=False, b_split=False, res=None, gate=None, y_dtype=None, n=None):
    if mode == "tn":
        K, M = (a.shape[0], a.shape[1]) if not a_split else (a.shape[1], 2 * a.shape[2])
    else:
        M, K = (a.shape[0], a.shape[1]) if not a_split else (a.shape[1], 2 * a.shape[2])
    if b_blocked:
        if mode == "nn":
            N, tn = b.shape[0] * b.shape[2], b.shape[2]
        else:
            N, tk = b.shape[1], b.shape[2]
    elif b_split:
        N = 2 * b.shape[2]
    else:
        N = b.shape[0] if mode == "nt" else b.shape[1]
    if n is not None:
        N = n
    tm = _pick(M, (tm, 704, 512, 384, 256, 128))
    if not (b_blocked and mode == "nn"):
        tn = _pick(N, (tn, 1024, 768, 512, 384, 256, 128))
    if not (b_blocked and mode == "nt"):
        tk = _pick(K, (tk, 1024, 512, 384, 256, 128))
    nm, nn_, nk = M // tm, N // tn, K // tk

    if mode == "tn":
        a_spec = pl.BlockSpec((tk, tm), lambda i, j, k: (k, i))
        dims = (((0,), (0,)), ((), ()))
    elif a_split:
        per = a.shape[2] // tk
        a_spec = pl.BlockSpec((None, tm, tk), lambda i, j, k: (k // per, i, k % per))
    else:
        a_spec = pl.BlockSpec((tm, tk), lambda i, j, k: (i, k))
    if mode == "nn":
        dims = (((1,), (0,)), ((), ()))
        if b_blocked:
            b_spec = pl.BlockSpec((None, tk, tn), lambda i, j, k: (j, k, 0))
        else:
            b_spec = pl.BlockSpec((tk, tn), lambda i, j, k: (k, j))
    elif mode == "nt":
        dims = (((1,), (1,)), ((), ()))
        if b_blocked:
            b_spec = pl.BlockSpec((None, tn, tk), lambda i, j, k: (k, j, 0))
        else:
            b_spec = pl.BlockSpec((tn, tk), lambda i, j, k: (j, k))
    else:
        if b_split:
            per_b = b.shape[2] // tn
            b_spec = pl.BlockSpec((None, tk, tn), lambda i, j, k: (j // per_b, k, j % per_b))
        else:
            b_spec = pl.BlockSpec((tk, tn), lambda i, j, k: (k, j))
    if out_blocked:
        o_spec = pl.BlockSpec((None, tm, tn), lambda i, j, k: (j, i, 0))
        o_shape = SDS((nn_, M, tn), out_dtype)
    else:
        o_spec = pl.BlockSpec((tm, tn), lambda i, j, k: (i, j))
        o_shape = SDS((M, N), out_dtype)
    fused = res is not None
    in_specs, operands = [a_spec, b_spec], [a, b]
    out_specs, out_shapes = [o_spec], [o_shape]
    if fused:
        in_specs += [pl.BlockSpec((tm, tn), lambda i, j, k: (i, j)), pl.BlockSpec((1, tn), lambda i, j, k: (0, j))]
        operands += [res, gate]
        if y_dtype is not None:
            out_specs.append(pl.BlockSpec((tm, tn), lambda i, j, k: (i, j)))
            out_shapes.append(SDS((M, N), y_dtype))

    def body(*refs):
        a_ref, b_ref = refs[0], refs[1]
        acc_ref = refs[-1]
        k = pl.program_id(2)

        def product():
            return lax.dot_general(a_ref[...], b_ref[...], dims, preferred_element_type=F32)

        def finish(acc):
            if fused:
                res_ref, gate_ref, o_ref = refs[2], refs[3], refs[4]
                o_ref[...] = (res_ref[...] + gate_ref[...] * acc).astype(o_ref.dtype)
                if y_dtype is not None:
                    refs[5][...] = acc.astype(y_dtype)
            else:
                refs[2][...] = acc.astype(refs[2].dtype)

        if nk == 1:
            finish(product())
        else:
            @pl.when(k == 0)
            def _():
                acc_ref[...] = product()

            @pl.when(jnp.logical_and(k > 0, k < nk - 1))
            def _():
                acc_ref[...] += product()

            @pl.when(k == nk - 1)
            def _():
                finish(acc_ref[...] + product())

    outs = _pc(body, name=name, grid=(nm, nn_, nk), in_specs=in_specs, out_specs=out_specs, out_shape=out_shapes,
               scratch_shapes=[pltpu.VMEM((tm, tn), F32)], compiler_params=_params(3))(*operands)
    return outs[0] if len(outs) == 1 else tuple(outs)


def _norm_mod(x, gain, sc, sh, out_dtype, name):
    S, D = x.shape
    tr = _pick(S, (256, 128))

    def body(x_ref, g_ref, sc_ref, sh_ref, o_ref):
        xv = x_ref[...]
        r = lax.rsqrt(jnp.mean(xv * xv, axis=-1, keepdims=True) + EPS)
        n = (xv * r) * g_ref[...]
        o_ref[...] = (n * (1.0 + sc_ref[...]) + sh_ref[...]).astype(o_ref.dtype)

    row = pl.BlockSpec((tr, D), lambda i: (i, 0))
    vec = pl.BlockSpec((1, D), lambda i: (0, 0))
    return _pc(body, name=name, grid=(S // tr,), in_specs=[row, vec, vec, vec], out_specs=row,
               out_shape=SDS((S, D), out_dtype), compiler_params=_params(1))(x, gain, sc, sh)


def _norm_mod_bwd(dh, x, dxres, gain, sc, name):
    S, D = x.shape
    tr = _pick(S, (256, 128))

    def body(dh_ref, x_ref, dxres_ref, g_ref, sc_ref, dx_ref, dgain_ref, dsc_ref, dsh_ref):
        @pl.when(pl.program_id(0) == 0)
        def _():
            dgain_ref[...] = jnp.zeros_like(dgain_ref)
            dsc_ref[...] = jnp.zeros_like(dsc_ref)
            dsh_ref[...] = jnp.zeros_like(dsh_ref)

        xv = x_ref[...]
        dh = dh_ref[...].astype(F32)
        r = lax.rsqrt(jnp.mean(xv * xv, axis=-1, keepdims=True) + EPS)
        nh = xv * r
        gn = g_ref[...]
        dn = dh * (1.0 + sc_ref[...])
        dgain_ref[...] += jnp.sum(dn * nh, axis=0, keepdims=True)
        dsc_ref[...] += jnp.sum(dh * (nh * gn), axis=0, keepdims=True)
        dsh_ref[...] += jnp.sum(dh, axis=0, keepdims=True)
        dnh = dn * gn
        dx = r * (dnh - nh * jnp.mean(dnh * nh, axis=-1, keepdims=True))
        dx_ref[...] = dxres_ref[...] + dx

    row = pl.BlockSpec((tr, D), lambda i: (i, 0))
    vec = pl.BlockSpec((1, D), lambda i: (0, 0))
    return _pc(body, name=name, grid=(S // tr,), in_specs=[row, row, row, vec, vec], out_specs=[row, vec, vec, vec],
               out_shape=[SDS((S, D), F32), SDS((1, D), F32), SDS((1, D), F32), SDS((1, D), F32)],
               compiler_params=_params(1))(dh, x, dxres, gain, sc)


def _gate_bwd(dx, y, gate, name):
    S, D = dx.shape
    tr = _pick(S, (256, 128))

    def body(dx_ref, y_ref, gate_ref, dy_ref, dgate_ref):
        @pl.when(pl.program_id(0) == 0)
        def _():
            dgate_ref[...] = jnp.zeros_like(dgate_ref)

        dxv = dx_ref[...]
        dgate_ref[...] += jnp.sum(dxv * y_ref[...].astype(F32), axis=0, keepdims=True)
        dy_ref[...] = (dxv * gate_ref[...]).astype(BF16)

    row = pl.BlockSpec((tr, D), lambda i: (i, 0))
    vec = pl.BlockSpec((1, D), lambda i: (0, 0))
    return _pc(body, name=name, grid=(S // tr,), in_specs=[row, row, vec], out_specs=[row, vec],
               out_shape=[SDS((S, D), BF16), SDS((1, D), F32)], compiler_params=_params(1))(dx, y, gate)


def _loss_head(x, target, fgain, name):
    S, D = x.shape
    tr = _pick(S, (256, 128))

    def body(x_ref, t_ref, fg_ref, dx_ref, dfg_ref, loss_ref):
        @pl.when(pl.program_id(0) == 0)
        def _():
            dfg_ref[...] = jnp.zeros_like(dfg_ref)
            loss_ref[...] = jnp.zeros_like(loss_ref)

        xv = x_ref[...]
        fg = fg_ref[...]
        r = lax.rsqrt(jnp.mean(xv * xv, axis=-1, keepdims=True) + EPS)
        nh = xv * r
        e = nh * fg - t_ref[...]
        loss_ref[...] += 0.5 * jnp.sum(jnp.mean(e * e, axis=-1, keepdims=True))
        dy = e * (1.0 / D)
        dfg_ref[...] += jnp.sum(dy * nh, axis=0, keepdims=True)
        dnh = dy * fg
        dx_ref[...] = r * (dnh - nh * jnp.mean(dnh * nh, axis=-1, keepdims=True))

    row = pl.BlockSpec((tr, D), lambda i: (i, 0))
    vec = pl.BlockSpec((1, D), lambda i: (0, 0))
    tile = pl.BlockSpec((8, LANES), lambda i: (0, 0))
    return _pc(body, name=name, grid=(S // tr,), in_specs=[row, row, vec], out_specs=[row, vec, tile],
               out_shape=[SDS((S, D), F32), SDS((1, D), F32), SDS((8, LANES), F32)],
               compiler_params=_params(1))(x, target, fgain)


def _shift_down(v, k, rows):
    return jnp.where(rows >= k, pltpu.roll(v, k, axis=0), 0.0)


def _shift_up(v, k, rows):
    n = v.shape[0]
    return jnp.where(rows < n - k, pltpu.roll(v, n - k, axis=0), 0.0)


CONV_HALO = 8


def _down(v, k, rows, at_top):
    r = pltpu.roll(v, k, axis=0)
    return jnp.where(rows >= k, r, 0.0) if at_top else r


def _up(v, k, rows, at_bottom):
    n = v.shape[0]
    r = pltpu.roll(v, n - k, axis=0)
    return jnp.where(rows < n - k, r, 0.0) if at_bottom else r


def _conv(uv, w, b, rows, at_top):
    return ((b + _down(uv, 2, rows, at_top) * w[0:1]) + _down(uv, 1, rows, at_top) * w[1:2]) + uv * w[2:3]


def _conv_chunk_rows(S):
    return _pick(max(S // 2, 1), (256, 128))


def _convgate_fwd(u, cw, cb, name):
    S, F2 = u.shape
    DFF = F2 // 2
    tc = _pick(DFF, (256, 128))
    sub = min(tc, LANES)
    nj = DFF // tc
    R = _conv_chunk_rows(S)

    def body(ua_ref, ug_ref, wa_ref, wg_ref, ba_ref, bg_ref, o_ref):
        for q in range(tc // sub):
            sl = slice(q * sub, (q + 1) * sub)
            wa, wg, ba, bg = wa_ref[:, sl], wg_ref[:, sl], ba_ref[:, sl], bg_ref[:, sl]
            for r0 in range(0, S, R):
                lo = max(r0 - CONV_HALO, 0)
                rows = lax.broadcasted_iota(jnp.int32, (r0 + R - lo, sub), 0)
                ya = _conv(ua_ref[lo:r0 + R, sl], wa, ba, rows, lo == 0)[r0 - lo:]
                yg = _conv(ug_ref[lo:r0 + R, sl], wg, bg, rows, lo == 0)[r0 - lo:]
                o_ref[r0:r0 + R, sl] = (yg * jax.nn.sigmoid(yg) * ya).astype(BF16)

    col = lambda off: pl.BlockSpec((S, tc), lambda j: (0, j + off))
    w3 = lambda off: pl.BlockSpec((3, tc), lambda j: (0, j + off))
    b1 = lambda off: pl.BlockSpec((1, tc), lambda j: (0, j + off))
    return _pc(body, name=name, grid=(nj,), in_specs=[col(0), col(nj), w3(0), w3(nj), b1(0), b1(nj)],
               out_specs=col(0), out_shape=SDS((S, DFF), BF16), compiler_params=_params(1))(u, u, cw, cw, cb, cb)


def _convgate_bwd(u, dact, cw, cb, name):
    S, F2 = u.shape
    DFF = F2 // 2
    tc = _pick(DFF, (256, 128))
    sub = min(tc, LANES)
    nj = DFF // tc
    R = _conv_chunk_rows(S)

    def body(ua_ref, ug_ref, da_ref, wa_ref, wg_ref, ba_ref, bg_ref, du_ref, dcw_ref, dcb_ref):
        for q in range(tc // sub):
            sl = slice(q * sub, (q + 1) * sub)
            wa, wg, ba, bg = wa_ref[:, sl], wg_ref[:, sl], ba_ref[:, sl], bg_ref[:, sl]
            sums = [[None] * 4, [None] * 4]
            for r0 in range(0, S, R):
                lo, hi = max(r0 - CONV_HALO, 0), min(r0 + R + CONV_HALO, S)
                top, bottom, inner = lo == 0, hi == S, slice(r0 - lo, r0 - lo + R)
                rows = lax.broadcasted_iota(jnp.int32, (hi - lo, sub), 0)
                ua, ug, da = ua_ref[lo:hi, sl], ug_ref[lo:hi, sl], da_ref[lo:hi, sl]
                ya = _conv(ua, wa, ba, rows, top)
                yg = _conv(ug, wg, bg, rows, top)
                s = jax.nn.sigmoid(yg)
                dya = da * (yg * s)
                dyg = da * ya * (s * (1.0 + yg * (1.0 - s)))
                for h, (dy, uv, w) in enumerate(((dya, ua, wa), (dyg, ug, wg))):
                    du = (dy * w[2:3] + _up(dy, 1, rows, bottom) * w[1:2]) + _up(dy, 2, rows, bottom) * w[0:1]
                    du_ref[h, r0:r0 + R, sl] = du[inner].astype(BF16)
                    terms = [dy * _down(uv, 2, rows, top), dy * _down(uv, 1, rows, top), dy * uv, dy]
                    for i, term in enumerate(terms):
                        part = jnp.sum(term[inner], axis=0, keepdims=True)
                        sums[h][i] = part if sums[h][i] is None else sums[h][i] + part
            for h in range(2):
                for i in range(3):
                    dcw_ref[h, i:i + 1, sl] = sums[h][i]
                dcb_ref[h, :, sl] = sums[h][3]

    col = lambda off: pl.BlockSpec((S, tc), lambda j: (0, j + off))
    w3 = lambda off: pl.BlockSpec((3, tc), lambda j: (0, j + off))
    b1 = lambda off: pl.BlockSpec((1, tc), lambda j: (0, j + off))
    return _pc(body, name=name, grid=(nj,),
               in_specs=[col(0), col(nj), col(0), w3(0), w3(nj), b1(0), b1(nj)],
               out_specs=[pl.BlockSpec((2, S, tc), lambda j: (0, 0, j)), pl.BlockSpec((2, 3, tc), lambda j: (0, 0, j)),
                          pl.BlockSpec((2, 1, tc), lambda j: (0, 0, j))],
               out_shape=[SDS((2, S, DFF), BF16), SDS((2, 3, DFF), F32), SDS((2, 1, DFF), F32)],
               compiler_params=_params(1))(u, u, dact, cw, cw, cb, cb)


def _pool_diff(h, name):
    S, D = h.shape
    G = len(POOL_WINDOWS)
    CG = D // G
    tc = min(CG, LANES)
    per = CG // tc

    def body(h_ref, d_ref):
        g = pl.program_id(0)
        rows = lax.broadcasted_iota(jnp.int32, (S, tc), 0)
        for gi, w in enumerate(POOL_WINDOWS):
            @pl.when(g == gi)
            def _(w=w):
                hv = h_ref[...]
                s, k = hv, 1
                while k < w:
                    s = s + _shift_down(s, k, rows)
                    k *= 2
                count = jnp.minimum(rows + 1, w).astype(F32)
                d_ref[...] = (s / count - hv).astype(BF16)

    spec = pl.BlockSpec((S, tc), lambda g, j: (0, g * per + j))
    return _pc(body, name=name, grid=(G, per), in_specs=[spec], out_specs=spec, out_shape=SDS((S, D), BF16),
               compiler_params=_params(2))(h)


def _pool_diff_bwd(dd, name):
    S, D = dd.shape
    G = len(POOL_WINDOWS)
    CG = D // G
    tc = min(CG, LANES)
    per = CG // tc

    def body(dd_ref, o_ref):
        g = pl.program_id(0)
        rows = lax.broadcasted_iota(jnp.int32, (S, tc), 0)
        for gi, w in enumerate(POOL_WINDOWS):
            @pl.when(g == gi)
            def _(w=w):
                dv = dd_ref[...]
                count = jnp.minimum(rows + 1, w).astype(F32)
                s, k = dv / count, 1
                while k < w:
                    s = s + _shift_up(s, k, rows)
                    k *= 2
                o_ref[...] = s - dv

    spec = pl.BlockSpec((S, tc), lambda g, j: (0, g * per + j))
    return _pc(body, name=name, grid=(G, per), in_specs=[spec], out_specs=spec, out_shape=SDS((S, D), F32),
               compiler_params=_params(2))(dd)


def _pool_mm(d, w, res, gate, name):
    S, D = d.shape
    G, CG, _ = w.shape
    tm = _pick(S, (512, 256, 128))

    def body(d_ref, w_ref, res_ref, gate_ref, o_ref, e_ref):
        acc = jnp.dot(d_ref[...], w_ref[...], preferred_element_type=F32)
        o_ref[...] = res_ref[...] + gate_ref[...] * acc
        e_ref[...] = acc.astype(BF16)

    blk = pl.BlockSpec((tm, CG), lambda g, i: (i, g))
    return _pc(body, name=name, grid=(G, S // tm),
               in_specs=[blk, pl.BlockSpec((None, CG, CG), lambda g, i: (g, 0, 0)), blk,
                         pl.BlockSpec((1, CG), lambda g, i: (0, g))],
               out_specs=[blk, blk], out_shape=[SDS((S, D), F32), SDS((S, D), BF16)],
               compiler_params=_params(2))(d, w, res, gate)


def _pool_mm_bwd(de, d, w, name):
    S, D = de.shape
    G, CG, _ = w.shape
    tm = _pick(S, (512, 256, 128))
    ns = S // tm

    def body(de_ref, d_ref, w_ref, dd_ref, dw_ref, acc_ref):
        i = pl.program_id(1)

        @pl.when(i == 0)
        def _():
            acc_ref[...] = jnp.zeros_like(acc_ref)

        dev = de_ref[...]
        dd_ref[...] = lax.dot_general(dev, w_ref[...], (((1,), (1,)), ((), ())), preferred_element_type=F32)
        acc_ref[...] += lax.dot_general(d_ref[...], dev, (((0,), (0,)), ((), ())), preferred_element_type=F32)

        @pl.when(i == ns - 1)
        def _():
            dw_ref[...] = acc_ref[...].astype(BF16)

    blk = pl.BlockSpec((tm, CG), lambda g, i: (i, g))
    wsp = pl.BlockSpec((None, CG, CG), lambda g, i: (g, 0, 0))
    return _pc(body, name=name, grid=(G, ns), in_specs=[blk, blk, wsp], out_specs=[blk, wsp],
               out_shape=[SDS((S, D), F32), SDS((G, CG, CG), BF16)], scratch_shapes=[pltpu.VMEM((CG, CG), F32)],
               compiler_params=_params(2))(de, d, w)


def _log_sigmoid(z):
    return jnp.minimum(z, 0.0) - jnp.log(1.0 + jnp.exp(-jnp.abs(z)))


def _dot2(a, tri):
    hi = a.astype(BF16)
    lo = (a - hi.astype(F32)).astype(BF16)
    return jnp.dot(hi, tri, preferred_element_type=F32) + jnp.dot(lo, tri, preferred_element_type=F32)


_NT = (((1,), (1,)), ((), ()))
_TN = (((0,), (0,)), ((), ()))


def _forget_cumsum(flog, bf, name):
    S, W = flog.shape
    tb = _pick(S, (128,))

    def body(f_ref, b_ref, o_ref):
        r = lax.broadcasted_iota(jnp.int32, (tb, tb), 0)
        c = lax.broadcasted_iota(jnp.int32, (tb, tb), 1)
        tri = (c <= r).astype(F32)
        carry = jnp.zeros((1, W), F32)
        for q in range(S // tb):
            ls = _log_sigmoid(f_ref[q * tb:(q + 1) * tb, :] + b_ref[...])
            o_ref[q * tb:(q + 1) * tb, :] = carry + jnp.dot(tri, ls, preferred_element_type=F32,
                                                            precision=lax.Precision.HIGHEST)
            carry = carry + jnp.sum(ls, axis=0, keepdims=True)

    return _pc(body, name=name, out_shape=SDS((S, W), F32))(flog, bf)


def _forget_cumsum_bwd(dF, flog, bf, name):
    S, W = flog.shape
    tb = _pick(S, (128,))

    def body(d_ref, f_ref, b_ref, o_ref, db_ref):
        r = lax.broadcasted_iota(jnp.int32, (tb, tb), 0)
        c = lax.broadcasted_iota(jnp.int32, (tb, tb), 1)
        tri = (c >= r).astype(F32)
        carry = jnp.zeros((1, W), F32)
        db = jnp.zeros((1, W), F32)
        for q in reversed(range(S // tb)):
            dv = d_ref[q * tb:(q + 1) * tb, :]
            dls = carry + jnp.dot(tri, dv, preferred_element_type=F32, precision=lax.Precision.HIGHEST)
            carry = carry + jnp.sum(dv, axis=0, keepdims=True)
            dfl = dls * jax.nn.sigmoid(-(f_ref[q * tb:(q + 1) * tb, :] + b_ref[...]))
            o_ref[q * tb:(q + 1) * tb, :] = dfl
            db = db + jnp.sum(dfl, axis=0, keepdims=True)
        db_ref[...] = db

    return _pc(body, name=name, out_shape=[SDS((S, W), F32), SDS((1, W), F32)])(dF, flog, bf)


def _heads_per_step(n_heads):
    return 4 if n_heads % 4 == 0 else 2 if n_heads % 2 == 0 else 1


def _sb_fwd(qkv, NH, NSB, HD, T, name):
    S = qkv.shape[0]
    nq = S // T
    scale = HD ** -0.5
    HB = _heads_per_step(NSB)
    W = HB * HD

    def body(q_ref, k_ref, v_ref, o_ref, tot_ref):
        i = pl.program_id(1)
        row = lax.broadcasted_iota(jnp.int32, (T, T), 0)
        col = lax.broadcasted_iota(jnp.int32, (T, T), 1)
        upper = (row > col).astype(BF16)
        heads = [slice(hh * HD, (hh + 1) * HD) for hh in range(HB)]
        qs = [q_ref[:, cs] for cs in heads]

        def blk(kb, state, diag):
            sl = pl.ds(pl.multiple_of(kb * T, T), T)
            out = []
            for hh, cs in enumerate(heads):
                carry, acc = state[2 * hh], state[2 * hh + 1]
                k, v = k_ref[sl, cs], v_ref[sl, cs]
                z = lax.dot_general(qs[hh], k, _NT, preferred_element_type=F32) * scale
                ls = _log_sigmoid(z)
                lr = ls - z
                if diag:
                    lr = jnp.where(col < row, lr, 0.0)
                rest = _dot2(lr, upper) + carry
                w = jnp.exp(ls + rest)
                if diag:
                    w = jnp.where(col < row, w, 0.0)
                out += [carry + jnp.sum(lr, axis=1, keepdims=True),
                        acc + jnp.dot(w.astype(BF16), v, preferred_element_type=F32)]
            return tuple(out)

        state = blk(i, (jnp.zeros((T, 1), F32), jnp.zeros((T, HD), F32)) * HB, True)
        state = lax.fori_loop(0, i, lambda jj, st: blk(i - 1 - jj, st, False), state)
        for hh, cs in enumerate(heads):
            o_ref[:, cs] = state[2 * hh + 1].astype(BF16)
            tot_ref[hh] = state[2 * hh]

    return _pc(body, name=name, grid=(NSB // HB, nq),
               in_specs=[pl.BlockSpec((T, W), lambda h, i: (i, h)),
                         pl.BlockSpec((S, W), lambda h, i: (0, NH // HB + h)),
                         pl.BlockSpec((S, W), lambda h, i: (0, 2 * NH // HB + h))],
               out_specs=[pl.BlockSpec((T, W), lambda h, i: (i, h)), pl.BlockSpec((HB, T, 1), lambda h, i: (h, i, 0))],
               out_shape=[SDS((S, NSB * HD), BF16), SDS((NSB, S, 1), F32)],
               compiler_params=_params(2))(qkv, qkv, qkv)


def _sb_bwd(qkv, do, tot, NH, NSB, HD, T, name):
    S = qkv.shape[0]
    nq = S // T
    scale = HD ** -0.5
    HB = _heads_per_step(NSB)
    W = HB * HD

    def body(q_ref, k_ref, v_ref, do_ref, tot_ref, dq_ref, dk_ref, dv_ref, dk_acc, dv_acc):
        i = pl.program_id(1)

        @pl.when(i == 0)
        def _():
            dk_acc[...] = jnp.zeros_like(dk_acc)
            dv_acc[...] = jnp.zeros_like(dv_acc)

        row = lax.broadcasted_iota(jnp.int32, (T, T), 0)
        col = lax.broadcasted_iota(jnp.int32, (T, T), 1)
        incl = (row <= col).astype(BF16)
        strict = (row < col).astype(BF16)
        heads = [slice(hh * HD, (hh + 1) * HD) for hh in range(HB)]
        qs = [q_ref[:, cs] for cs in heads]
        dos = [do_ref[:, cs] for cs in heads]
        tots = [tot_ref[hh] for hh in range(HB)]

        def blk(kb, state, diag):
            sl = pl.ds(pl.multiple_of(kb * T, T), T)
            out = []
            for hh, cs in enumerate(heads):
                cl, cg, dq = state[3 * hh], state[3 * hh + 1], state[3 * hh + 2]
                q, do_ = qs[hh], dos[hh]
                k, v = k_ref[sl, cs], v_ref[sl, cs]
                z = lax.dot_general(q, k, _NT, preferred_element_type=F32) * scale
                ls = _log_sigmoid(z)
                lr = ls - z
                if diag:
                    lr = jnp.where(col < row, lr, 0.0)
                rest = tots[hh] - (cl + _dot2(lr, incl))
                w = jnp.exp(ls + rest)
                if diag:
                    w = jnp.where(col < row, w, 0.0)
                g = lax.dot_general(do_, v, _NT, preferred_element_type=F32) * w
                dv_acc[sl, cs] += lax.dot_general(w.astype(BF16), do_, _TN, preferred_element_type=F32)
                dlr = cg + _dot2(g, strict)
                dz = g * jnp.exp(lr) - dlr * jnp.exp(ls)
                if diag:
                    dz = jnp.where(col < row, dz, 0.0)
                dzb = (dz * scale).astype(BF16)
                dk_acc[sl, cs] += lax.dot_general(dzb, q, _TN, preferred_element_type=F32)
                out += [cl + jnp.sum(lr, axis=1, keepdims=True), cg + jnp.sum(g, axis=1, keepdims=True),
                        dq + jnp.dot(dzb, k, preferred_element_type=F32)]
            return tuple(out)

        zero = jnp.zeros((T, 1), F32)
        state = lax.fori_loop(0, i, lambda kb, st: blk(kb, st, False), (zero, zero, jnp.zeros((T, HD), F32)) * HB)
        state = blk(i, state, True)
        for hh, cs in enumerate(heads):
            dq_ref[:, cs] = state[3 * hh + 2].astype(BF16)

        @pl.when(i == nq - 1)
        def _():
            dk_ref[...] = dk_acc[...].astype(BF16)
            dv_ref[...] = dv_acc[...].astype(BF16)

    qblk = pl.BlockSpec((T, W), lambda h, i: (i, h))
    full = pl.BlockSpec((S, W), lambda h, i: (0, h))
    return _pc(body, name=name, grid=(NSB // HB, nq),
               in_specs=[qblk, pl.BlockSpec((S, W), lambda h, i: (0, NH // HB + h)),
                         pl.BlockSpec((S, W), lambda h, i: (0, 2 * NH // HB + h)), qblk,
                         pl.BlockSpec((HB, T, 1), lambda h, i: (h, i, 0))],
               out_specs=[qblk, full, full],
               out_shape=[SDS((S, NSB * HD), BF16)] * 3,
               scratch_shapes=[pltpu.VMEM((S, W), F32), pltpu.VMEM((S, W), F32)],
               compiler_params=_params(2))(qkv, qkv, qkv, do, tot)


def _fox_fwd(qkv, fcol, frow, NH, NSB, HD, T, name):
    S = qkv.shape[0]
    NFX = NH - NSB
    nq = S // T
    scale = HD ** -0.5
    HB = _heads_per_step(NFX) if NSB % _heads_per_step(NFX) == 0 else 1
    W = HB * HD

    def body(q_ref, k_ref, v_ref, fq_ref, fk_ref, o_ref, o32_ref, lse_ref):
        i = pl.program_id(1)
        row = lax.broadcasted_iota(jnp.int32, (T, T), 0)
        col = lax.broadcasted_iota(jnp.int32, (T, T), 1)
        heads = [slice(hh * HD, (hh + 1) * HD) for hh in range(HB)]
        qs = [q_ref[:, cs] for cs in heads]
        fqs = [fq_ref[hh] for hh in range(HB)]

        def blk(kb, state, diag):
            sl = pl.ds(pl.multiple_of(kb * T, T), T)
            out = []
            for hh, cs in enumerate(heads):
                m, l, acc, rem = state[4 * hh:4 * hh + 4]
                k, v = k_ref[sl, cs], v_ref[sl, cs]
                s = lax.dot_general(qs[hh], k, _NT, preferred_element_type=F32) * scale + (fqs[hh] - fk_ref[hh, kb])
                if diag:
                    s = jnp.where(col <= row, s, NEG_BIG)
                m_new = jnp.maximum(m, jnp.max(s, axis=1, keepdims=True))
                p = jnp.exp(s - m_new)
                alpha = jnp.exp(m - m_new)
                hi = p.astype(BF16)
                lo = (p - hi.astype(F32)).astype(BF16)
                out += [m_new, alpha * l + jnp.sum(p, axis=1, keepdims=True),
                        alpha * acc + jnp.dot(hi, v, preferred_element_type=F32),
                        alpha * rem + jnp.dot(lo, v, preferred_element_type=F32)]
            return tuple(out)

        zero = jnp.zeros((T, HD), F32)
        state = blk(i, (jnp.full((T, 1), NEG_BIG, F32), jnp.zeros((T, 1), F32), zero, zero) * HB, True)
        state = lax.fori_loop(0, i, lambda kb, st: blk(kb, st, False), state)
        for hh, cs in enumerate(heads):
            m, l, acc, rem = state[4 * hh:4 * hh + 4]
            o_ref[:, cs] = (acc / l).astype(BF16)
            o32_ref[:, cs] = (acc + rem) / l
            lse_ref[hh] = m + jnp.log(l)

    vec = pl.BlockSpec((HB, T, 1), lambda h, i: (h, i, 0))
    oblk = pl.BlockSpec((T, W), lambda h, i: (i, h))
    return _pc(body, name=name, grid=(NFX // HB, nq),
               in_specs=[pl.BlockSpec((T, W), lambda h, i: (i, NSB // HB + h)),
                         pl.BlockSpec((S, W), lambda h, i: (0, (NH + NSB) // HB + h)),
                         pl.BlockSpec((S, W), lambda h, i: (0, (2 * NH + NSB) // HB + h)),
                         vec, pl.BlockSpec((HB, nq, 1, T), lambda h, i: (h, 0, 0, 0))],
               out_specs=[oblk, oblk, vec],
               out_shape=[SDS((S, NFX * HD), BF16), SDS((S, NFX * HD), F32), SDS((NFX, S, 1), F32)],
               compiler_params=_params(2))(qkv, qkv, qkv, fcol, frow)


def _fox_bwd(qkv, do, o, fcol, frow, lse, NH, NSB, HD, T, name):
    S = qkv.shape[0]
    NFX = NH - NSB
    nq = S // T
    scale = HD ** -0.5
    HB = _heads_per_step(NFX) if NSB % _heads_per_step(NFX) == 0 else 1
    W = HB * HD

    def body(q_ref, k_ref, v_ref, do_ref, o_ref, fq_ref, fk_ref, lse_ref, dq_ref, dk_ref, dv_ref, dfk_ref,
             dk_acc, dv_acc, dfk_acc):
        i = pl.program_id(1)

        @pl.when(i == 0)
        def _():
            dk_acc[...] = jnp.zeros_like(dk_acc)
            dv_acc[...] = jnp.zeros_like(dv_acc)
            dfk_acc[...] = jnp.zeros_like(dfk_acc)

        row = lax.broadcasted_iota(jnp.int32, (T, T), 0)
        col = lax.broadcasted_iota(jnp.int32, (T, T), 1)
        heads = [slice(hh * HD, (hh + 1) * HD) for hh in range(HB)]
        qs = [q_ref[:, cs] for cs in heads]
        dos = [do_ref[:, cs] for cs in heads]
        fqs = [fq_ref[hh] for hh in range(HB)]
        lses = [lse_ref[hh] for hh in range(HB)]
        deltas = [jnp.sum(dos[hh].astype(F32) * o_ref[:, cs], axis=1, keepdims=True) for hh, cs in enumerate(heads)]

        def blk(kb, dqs, diag):
            sl = pl.ds(pl.multiple_of(kb * T, T), T)
            out = []
            for hh, cs in enumerate(heads):
                q, do_ = qs[hh], dos[hh]
                k, v = k_ref[sl, cs], v_ref[sl, cs]
                s = lax.dot_general(q, k, _NT, preferred_element_type=F32) * scale + (fqs[hh] - fk_ref[hh, kb])
                p = jnp.exp(s - lses[hh])
                if diag:
                    p = jnp.where(col <= row, p, 0.0)
                ds = p * (lax.dot_general(do_, v, _NT, preferred_element_type=F32) - deltas[hh])
                dv_acc[sl, cs] += lax.dot_general(p.astype(BF16), do_, _TN, preferred_element_type=F32)
                dsb = (ds * scale).astype(BF16)
                dk_acc[sl, cs] += lax.dot_general(dsb, q, _TN, preferred_element_type=F32)
                dfk_acc[hh, kb] -= jnp.sum(ds, axis=0, keepdims=True)
                out.append(dqs[hh] + jnp.dot(dsb, k, preferred_element_type=F32))
            return tuple(out)

        dqs = lax.fori_loop(0, i, lambda kb, st: blk(kb, st, False), (jnp.zeros((T, HD), F32),) * HB)
        dqs = blk(i, dqs, True)
        for hh, cs in enumerate(heads):
            dq_ref[:, cs] = dqs[hh].astype(BF16)

        @pl.when(i == nq - 1)
        def _():
            dk_ref[...] = dk_acc[...].astype(BF16)
            dv_ref[...] = dv_acc[...].astype(BF16)
            dfk_ref[...] = dfk_acc[...]

    vec = pl.BlockSpec((HB, T, 1), lambda h, i: (h, i, 0))
    rowv = pl.BlockSpec((HB, nq, 1, T), lambda h, i: (h, 0, 0, 0))
    oblk = pl.BlockSpec((T, W), lambda h, i: (i, h))
    qblk = pl.BlockSpec((T, W), lambda h, i: (i, NSB // HB + h))
    full = pl.BlockSpec((S, W), lambda h, i: (0, h))
    return _pc(body, name=name, grid=(NFX // HB, nq),
               in_specs=[qblk, pl.BlockSpec((S, W), lambda h, i: (0, (NH + NSB) // HB + h)),
                         pl.BlockSpec((S, W), lambda h, i: (0, (2 * NH + NSB) // HB + h)),
                         qblk, oblk, vec, rowv, vec],
               out_specs=[oblk, full, full, rowv],
               out_shape=[SDS((S, NFX * HD), BF16)] * 3 + [SDS((NFX, nq, 1, T), F32)],
               scratch_shapes=[pltpu.VMEM((S, W), F32), pltpu.VMEM((S, W), F32), pltpu.VMEM((HB, nq, 1, T), F32)],
               compiler_params=_params(2))(qkv, qkv, qkv, do, o, fcol, frow, lse)


def _silu(c_all, name):
    def body(c_ref, o_ref):
        cv = c_ref[...]
        o_ref[...] = cv * jax.nn.sigmoid(cv)

    return _pc(body, name=name, out_shape=SDS(c_all.shape, F32))(c_all)


def _mod_project(cond, w_mod, name):
    L, D, C = w_mod.shape
    tk = _pick(D, (512, 256, 128))

    def body(c_ref, w_ref, o_ref):
        @pl.when(pl.program_id(1) == 0)
        def _():
            o_ref[...] = jnp.zeros_like(o_ref)

        o_ref[...] += jnp.dot(c_ref[...].astype(BF16), w_ref[...].astype(BF16), preferred_element_type=F32)

    return _pc(body, name=name, grid=(L, D // tk),
               in_specs=[pl.BlockSpec((16, tk), lambda l, k: (0, k)), pl.BlockSpec((None, tk, C), lambda l, k: (l, k, 0))],
               out_specs=pl.BlockSpec((None, 16, C), lambda l, k: (l, 0, 0)),
               out_shape=SDS((L, 16, C), F32), compiler_params=_params(2))(cond, w_mod)


def _adam_math(w, g, m, v):
    m = ADAM_B1 * m + (1.0 - ADAM_B1) * g
    v = ADAM_B2 * v + (1.0 - ADAM_B2) * (g * g)
    m_hat = m / (1.0 - ADAM_B1 ** ADAM_STEP)
    v_hat = v / (1.0 - ADAM_B2 ** ADAM_STEP)
    delta = -ADAM_LR * (m_hat / (jnp.sqrt(v_hat) + ADAM_EPS) + ADAM_WD * w)
    return delta, m, v


def _adamw(w, m, v, parts, layer, prev, name):
    L, R, C = w.shape
    NP = parts.shape[0]
    tr = _pick(R, (128, 64, 88, 32, 16, 8))
    nprev = 0 if prev is None else 4

    def body(w_ref, m_ref, v_ref, p_ref, *rest):
        g_ref, d_ref, mo_ref, vo_ref = rest[nprev:]
        g = p_ref[0].astype(F32)
        for j in range(1, NP):
            g = g + p_ref[j].astype(F32)
        delta, mn, vn = _adam_math(w_ref[...], g, m_ref[...], v_ref[...])
        g_ref[...] = g
        d_ref[...] = delta
        mo_ref[...] = mn
        vo_ref[...] = vn

    if tr == R and R > 512:
        tc = _pick(C, (256, 128))
        steps = C // tc
        blk = pl.BlockSpec((None, R, tc), lambda j: (layer, 0, j))
        in_specs = [blk, blk, blk, pl.BlockSpec((NP, R, tc), lambda j: (0, 0, j))]
    else:
        steps = R // tr
        blk = pl.BlockSpec((None, tr, C), lambda i: (layer, i, 0))
        in_specs = [blk, blk, blk, pl.BlockSpec((NP, tr, C), lambda i: (0, i, 0))]
    operands = [w, m, v, parts]
    aliases = {}
    if prev is not None:
        in_specs += [ANY] * 4
        operands += list(prev)
        aliases = {4 + q: q for q in range(4)}
    return _pc(body, name=name, grid=(steps,), in_specs=in_specs, out_specs=[blk] * 4,
               out_shape=[SDS(w.shape, F32)] * 4, input_output_aliases=aliases,
               compiler_params=_params(1))(*operands)


def _adamw_mod(w, m, v, cond_t, dmod, name):
    L, D, C = w.shape
    tr = _pick(D, (128, 64))

    def body(w_ref, m_ref, v_ref, ct_ref, dm_ref, g_ref, d_ref, mo_ref, vo_ref):
        ct = ct_ref[...]
        g = ct[:, 0:1] * dm_ref[0]
        for b in range(1, NDEV):
            g = g + ct[:, b:b + 1] * dm_ref[b]
        delta, mn, vn = _adam_math(w_ref[...], g, m_ref[...], v_ref[...])
        g_ref[...] = g
        d_ref[...] = delta
        mo_ref[...] = mn
        vo_ref[...] = vn

    blk = pl.BlockSpec((None, tr, C), lambda l, i: (l, i, 0))
    return _pc(body, name=name, grid=(L, D // tr),
               in_specs=[blk, blk, blk, pl.BlockSpec((tr, LANES), lambda l, i: (i, 0)),
                         pl.BlockSpec((NDEV, None, 1, C), lambda l, i: (0, l, 0, 0))],
               out_specs=[blk] * 4, out_shape=[SDS(w.shape, F32)] * 4, compiler_params=_params(2))(w, m, v, cond_t, dmod)


def _sum_parts(parts, name):
    NP, R, C = parts.shape
    tr = _pick(R, (256, 128, 64, 32, 16, 8))

    def body(p_ref, o_ref):
        g = p_ref[0]
        for j in range(1, NP):
            g = g + p_ref[j]
        o_ref[...] = g

    return _pc(body, name=name, grid=(R // tr,), in_specs=[pl.BlockSpec((NP, tr, C), lambda i: (0, i, 0))],
               out_specs=pl.BlockSpec((tr, C), lambda i: (i, 0)), out_shape=SDS((R, C), F32),
               compiler_params=_params(1))(parts)


def _pack(vecs, rows=None):
    flat = jnp.concatenate([v.reshape(-1).astype(F32) for v in vecs])
    n = flat.shape[0]
    r = rows if rows is not None else -(-n // (256 * LANES)) * 256
    return jnp.pad(flat, (0, r * LANES - n)).reshape(r, LANES)


def _unpack(packed, shapes):
    flat = packed.reshape(-1)
    out, off = [], 0
    for s in shapes:
        n = 1
        for d in s:
            n *= d
        out.append(flat[off:off + n].reshape(s))
        off += n
    return out


def kernel(x, c, w_mod, b_mod, norm_gain, w_attn_in, b_forget, w_attn_out, w_pool, pool_scale, w_up, conv_w, conv_b, w_down, final_gain, loss_target, m_w_mod, m_b_mod, m_norm_gain, m_w_attn_in, m_b_forget, m_w_attn_out, m_w_pool, m_pool_scale, m_w_up, m_conv_w, m_conv_b, m_w_down, m_final_gain, v_w_mod, v_b_mod, v_norm_gain, v_w_attn_in, v_b_forget, v_w_attn_out, v_w_pool, v_pool_scale, v_w_up, v_conv_w, v_conv_b, v_w_down, v_final_gain):
    _, S, D = x.shape
    L = w_mod.shape[0]
    CM = w_mod.shape[2]
    NFX = b_forget.shape[1]
    NH = 2 * NFX
    NSB = NH - NFX
    HD = D // NH
    CI = w_attn_in.shape[2]
    CU = w_up.shape[2]
    F2 = NDEV * CU
    DFF = F2 // 2
    G = len(POOL_WINDOWS)
    CG = D // G
    T = _pick(S, (256, 128))
    me = _idx(_me())
    x0 = x[0]
    target = loss_target[0]

    def layer_shards(l, group):
        if group == 1:
            shards = [w_up[l], w_down[l]]
        elif l % 2 == 0:
            shards = [w_attn_in[l // 2].T, w_attn_out[l // 2]]
        else:
            shards = [w_pool[l // 2].reshape(G * (CG // NDEV), CG)]
        return [s.astype(BF16) for s in shards]

    small_shapes = [(1, D), norm_gain.shape, pool_scale.shape, conv_w.shape]
    small_all = _all_gather([_pack([c, norm_gain, pool_scale, conv_w])], "gather_small")[0]
    per_dev = [_unpack(small_all[j], small_shapes) for j in range(NDEV)]
    c_all = jnp.concatenate([p[0] for p in per_dev] + [jnp.zeros((16 - NDEV, D), F32)], axis=0)
    gain_f = jnp.concatenate([p[1] for p in per_dev], axis=2)
    pscale_f = jnp.concatenate([p[2] for p in per_dev], axis=1)
    convw_f = jnp.concatenate([p[3] for p in per_dev], axis=2)

    cond_all = _silu(c_all, "cond_silu")
    mod_part = _mod_project(cond_all, w_mod, "mod_project")
    mod_all = _all_gather([mod_part], "gather_mod")[0]
    mod = lax.dynamic_index_in_dim(mod_all, me, axis=2, keepdims=False)
    mod = mod.transpose(1, 0, 2).reshape(L, NDEV * CM) + b_mod
    mods = mod.reshape(L, 6, 1, D)

    inflight = {}

    def gather_start(l, group, after):
        inflight[l, group], t = _gather_start(layer_shards(l, group), me, after, f"gather{l}_{group}_start")
        return t

    def gather_forward(l, group, after):
        inflight[l, group], t = _gather_forward(inflight[l, group], after, f"gather{l}_{group}_forward")
        return t

    def gather_wait(l, group, after):
        return _gather_wait(inflight.pop((l, group)), after, f"gather{l}_{group}_wait")

    def starts_at(l):
        if l % 2 == 0:
            return [(k, g) for k, g in [(l + 1, 0), (l + 1, 1), (l + 2, 0)] if k < L]
        return [(k, g) for k, g in [(l + 1, 1)] if k < L]

    tok = gather_start(0, 0, [mods])
    tok = gather_start(0, 1, [tok])
    tok = gather_forward(0, 0, [tok])
    mixer_w = gather_wait(0, 0, [tok])

    saved = []
    xl = x0
    for l in range(L):
        i = l // 2
        sh1, sc1, g1, sh2, sc2, g2 = [mods[l, q] for q in range(6)]
        gn1, gn2 = gain_f[l, 0:1], gain_f[l, 1:2]
        st = {"x": xl}
        after = [mixer_w[0]]
        for k, g in starts_at(l):
            after = [gather_start(k, g, after)]
        if starts_at(l):
            sh1 = sh1 + after[0][0, 0]
        if l % 2 == 0:
            win_t = mixer_w[0].reshape(NDEV * CI, D)
            wf_t = jnp.pad(win_t[3 * D:], ((0, LANES - NFX), (0, 0)))
            wout = mixer_w[1].reshape(D, D)
            h1 = _norm_mod(xl, gn1, sc1, sh1, BF16, f"norm1_{l}")
            qkv = _matmul(h1, win_t, mode="nt", name=f"qkv_{l}", out_dtype=BF16, n=3 * D)
            flog = _matmul(h1, wf_t, mode="nt", name=f"flog_{l}", out_dtype=F32, tn=LANES)
            bfp = jnp.pad(b_forget[i], (0, LANES - NFX)).reshape(1, LANES)
            Fc = _forget_cumsum(flog, bfp, f"fcum_{l}")
            f8 = Fc[:, :NFX].T
            fcol, frow = f8[:, :, None], f8.reshape(NFX, S // T, 1, T)
            o_sb, tot = _sb_fwd(qkv, NH, NSB, HD, T, f"sb_fwd_{l}")
            o_fx, o_fx32, lse = _fox_fwd(qkv, fcol, frow, NH, NSB, HD, T, f"fox_fwd_{l}")
            o = jnp.concatenate([o_sb, o_fx], axis=1)
            g1 = g1 + gather_forward(l, 1, [o])[0, 0]
            x1, y1 = _matmul(o, wout, mode="nn", name=f"attn_out_{l}", out_dtype=F32, res=xl, gate=g1, y_dtype=BF16)
            st.update(h1=h1, qkv=qkv, flog=flog, bfp=bfp, fcol=fcol, frow=frow, tot=tot, lse=lse, o=o, o_fx=o_fx32,
                      win_t=win_t, wf_t=wf_t, wout=wout, y1=y1)
        else:
            wpool = mixer_w[0].reshape(NDEV, G, CG // NDEV, CG).transpose(1, 0, 2, 3).reshape(G, CG, CG)
            h1 = _norm_mod(xl, gn1, sc1, sh1, F32, f"norm1_{l}")
            dpool = _pool_diff(h1, f"pool_diff_{l}")
            g1 = g1 + gather_forward(l, 1, [dpool])[0, 0]
            gp = g1 * pscale_f[i:i + 1]
            x1, e1 = _pool_mm(dpool, wpool, xl, gp, f"pool_mm_{l}")
            st.update(dpool=dpool, wpool=wpool, gp=gp, y1=e1)
        wup_g, wdown_g = gather_wait(l, 1, [x1])
        wdown_f = wdown_g.reshape(DFF, D)
        h2 = _norm_mod(x1, gn2, sc2, sh2, BF16, f"norm2_{l}")
        u = _matmul(h2, wup_g, mode="nn", name=f"ffn_up_{l}", out_dtype=F32, b_blocked=True)
        cb = conv_b[l].reshape(1, F2)
        act = _convgate_fwd(u, convw_f[l], cb, f"convgate_{l}")
        if l + 1 < L:
            g2 = g2 + gather_forward(l + 1, 0, [act])[0, 0]
        x2, y2 = _matmul(act, wdown_f, mode="nn", name=f"ffn_down_{l}", out_dtype=F32, res=x1, gate=g2, y_dtype=BF16,
                         tk=CU)
        if l + 1 < L:
            mixer_w = gather_wait(l + 1, 0, [x2])
        st.update(x1=x1, h2=h2, u=u, cb=cb, act=act, y2=y2, wup_g=wup_g, wdown_f=wdown_f,
                  mod=(sh1, sc1, g1, sh2, sc2, g2), gn=(gn1, gn2))
        saved.append(st)
        xl = x2

    dx, d_fgain, loss_tile = _loss_head(xl, target, final_gain.reshape(1, D), "loss_head")
    loss = lax.psum(loss_tile[0, 0], ("x", "y", "c"))

    dmod_rows = [None] * L
    d_gain = [None] * L
    d_convw = [None] * L
    d_convb = [None] * L
    d_pscale = [None] * (L // 2)
    d_bf = [None] * ((L + 1) // 2)
    big = {"w_up": None, "w_down": None, "w_attn_out": None, "w_pool": None}
    attn_in_t = [None] * ((L + 1) // 2)

    def update(key, w, m, v, bufs, layer, tag):
        big[key] = _adamw(w, m, v, bufs, layer, big[key], f"adamw_{tag}")

    exchanges = []

    def exchange_start(l, keys, parts, after, two_level=False):
        tag = f"{l}_{'_'.join(keys)}"
        if two_level:
            state, t = _pair_start(parts, after, f"pair{tag}_start")
        else:
            state, t = _scatter_start(parts, me, after, f"scatter{tag}_start")
        exchanges.append([l, keys, tag, "pair" if two_level else "direct", state])
        return t

    def exchanges_advance(after):
        t = None
        for entry in [e for e in exchanges if e[3] == "pair"]:
            tag = entry[2]
            mine, theirs = _pair_wait(entry[4], after, f"pair{tag}_wait")
            sums = [_pair_sum(a, b, f"pair{tag}_sum{q}") for q, (a, b) in enumerate(zip(mine, theirs))]
            entry[4], t = _chips_start(sums, [], f"chips{tag}_start")
            entry[3] = "chips"
        return t

    def exchanges_finish(after, first_layer):
        for entry in [e for e in exchanges if e[0] >= first_layer]:
            exchanges.remove(entry)
            pl_, keys, tag, stage, state = entry
            if stage == "chips":
                bufs = _chips_wait(state, after, f"chips{tag}_wait")
            else:
                bufs = _scatter_wait(state, after, f"scatter{tag}_wait")
            for key, buf in zip(keys, bufs):
                if key == "up":
                    update("w_up", w_up, m_w_up, v_w_up, buf, pl_, f"up_{pl_}")
                elif key == "down":
                    update("w_down", w_down, m_w_down, v_w_down, buf, pl_, f"down_{pl_}")
                elif key == "out":
                    update("w_attn_out", w_attn_out, m_w_attn_out, v_w_attn_out, buf, pl_ // 2, f"attn_out_{pl_}")
                elif key == "in":
                    slab = lambda a: a[pl_ // 2].T[None]
                    attn_in_t[pl_ // 2] = _adamw(slab(w_attn_in), slab(m_w_attn_in), slab(v_w_attn_in), buf, 0, None,
                                                 f"adamw_attn_in_{pl_}")
                else:
                    wp3 = lambda a: a.reshape(a.shape[0], G * (CG // NDEV), CG)
                    update("w_pool", wp3(w_pool), wp3(m_w_pool), wp3(v_w_pool), buf, pl_ // 2, f"pool_{pl_}")

    tok = None
    for l in reversed(range(L)):
        i = l // 2
        st = saved[l]
        sh1, sc1, g1, sh2, sc2, g2 = st["mod"]
        if tok is not None:
            g2 = g2 + tok[0, 0]
        gn1, gn2 = st["gn"]
        dffn, dg2 = _gate_bwd(dx, st["y2"], g2, f"gate2_bwd_{l}")
        dact = _matmul(dffn, st["wdown_f"], mode="nt", name=f"ffn_down_dx_{l}", out_dtype=F32, tn=CU)
        dwdown = _matmul(st["act"], dffn, mode="tn", name=f"ffn_down_dw_{l}", out_dtype=BF16, tm=CU)
        dwdown = dwdown.reshape(NDEV, DFF // NDEV, D)
        du, dcw, dcb = _convgate_bwd(st["u"], dact, convw_f[l], st["cb"], f"convgate_bwd_{l}")
        dh2 = _matmul(du, st["wup_g"], mode="nt", name=f"ffn_up_dx_{l}", out_dtype=F32, a_split=True, b_blocked=True,
                      tn=2048)
        dwup = _matmul(st["h2"], du, mode="tn", name=f"ffn_up_dw_{l}", out_dtype=BF16, tn=CU, b_split=True,
                       out_blocked=True)
        dx, dgn2, dsc2, dsh2 = _norm_mod_bwd(dh2, st["x1"], dx, gn2, sc2, f"norm2_bwd_{l}")
        d_convw[l] = jnp.concatenate([dcw[0], dcw[1]], axis=1)
        d_convb[l] = jnp.concatenate([dcb[0], dcb[1]], axis=1)
        two_level = l < 2
        tok = exchange_start(l, ["up", "down"], [dwup, dwdown], [], two_level=two_level)
        if l % 2 == 0:
            dy1, dg1 = _gate_bwd(dx, st["y1"], g1 + tok[0, 0], f"gate1_bwd_{l}")
            do = _matmul(dy1, st["wout"], mode="nt", name=f"attn_out_dx_{l}", out_dtype=BF16)
            dwout = _matmul(st["o"], dy1, mode="tn", name=f"attn_out_dw_{l}", out_dtype=BF16).reshape(NDEV, D // NDEV, D)
            tot = st["tot"]
            if l == 0:
                tot = tot + exchange_start(l, ["out"], [dwout], [])[0, 0]
            dq_s, dk_s, dv_s = _sb_bwd(st["qkv"], do, tot, NH, NSB, HD, T, f"sb_bwd_{l}")
            lse = st["lse"] + exchanges_advance([dq_s])[0, 0] if two_level else st["lse"]
            dq_f, dk_f, dv_f, dfk = _fox_bwd(st["qkv"], do, st["o_fx"], st["fcol"], st["frow"], lse, NH, NSB, HD, T,
                                             f"fox_bwd_{l}")
            dqkv = jnp.concatenate([dq_s, dq_f, dk_s, dk_f, dv_s, dv_f], axis=1)
            dF = jnp.pad(dfk.reshape(NFX, S).T, ((0, 0), (0, LANES - NFX)))
            dflog, dbf = _forget_cumsum_bwd(dF, st["flog"], st["bfp"], f"fcum_bwd_{l}")
            dflog_b = dflog.astype(BF16)
            dh1 = _matmul(dqkv, st["win_t"], mode="nn", name=f"qkv_dx_{l}", out_dtype=F32)
            dh1 = _matmul(dflog_b, st["wf_t"], mode="nn", name=f"flog_dx_{l}", out_dtype=F32, res=dh1,
                          gate=jnp.ones((1, D), F32))
            dwqkv_t = _matmul(dqkv, st["h1"], mode="tn", name=f"qkv_dw_{l}", out_dtype=BF16)
            dwf_t = _matmul(dflog_b, st["h1"], mode="tn", name=f"flog_dw_{l}", out_dtype=BF16, tm=LANES)
            dwin_t = jnp.concatenate([dwqkv_t, dwf_t[:NFX]], axis=0).reshape(NDEV, CI, D)
            d_bf[i] = dbf[0, :NFX]
            keys, parts = (["in"], [dwin_t]) if l == 0 else (["in", "out"], [dwin_t, dwout])
        else:
            de, dgp = _gate_bwd(dx, st["y1"], st["gp"] + tok[0, 0], f"gate1_bwd_{l}")
            dg1 = dgp * pscale_f[i:i + 1]
            d_pscale[i] = dgp * g1
            dd, dwp = _pool_mm_bwd(de, st["dpool"], st["wpool"], f"pool_mm_bwd_{l}")
            if two_level:
                gn1 = gn1 + exchanges_advance([dd])[0, 0]
            dh1 = _pool_diff_bwd(dd, f"pool_diff_bwd_{l}")
            keys = ["pool"]
            parts = [dwp.reshape(G, NDEV, CG // NDEV, CG).transpose(1, 0, 2, 3).reshape(NDEV, G * (CG // NDEV), CG)]
        dx, dgn1, dsc1, dsh1 = _norm_mod_bwd(dh1, st["x"], dx, gn1, sc1, f"norm1_bwd_{l}")
        dmod_rows[l] = jnp.concatenate([dsh1, dsc1, dg1, dsh2, dsc2, dg2], axis=1)
        d_gain[l] = jnp.concatenate([dgn1, dgn2], axis=0)
        if l > 0:
            tok = exchange_start(l, keys, parts, [])
            exchanges_finish([dx, tok], l + 1)

    grad_x = dx[None]

    small_grads = [jnp.stack(dmod_rows), jnp.stack(d_gain), jnp.stack(d_pscale), jnp.stack(d_convw),
                   jnp.stack(d_convb), jnp.stack(d_bf), d_fgain]
    sg_shapes = [(L, 6 * D), (L, 2, D), (L // 2, D), (L, 3, F2), (L, F2), ((L + 1) // 2, NFX), (D,)]
    sg_all = _all_gather([_pack(small_grads)], "gather_small_grads")[0]
    tok = exchange_start(0, keys, parts, [sg_all])
    exchanges_finish([tok], 1)
    sg_all = sg_all + tok[0, 0]
    sg_sum = _unpack(_sum_parts(sg_all, "sum_small_grads"), sg_shapes)
    g_bmod, g_gain_f, g_pscale_f, g_convw_f, g_convb, g_bf, g_fgain = sg_sum
    shard = lambda a, n, axis: lax.dynamic_slice_in_dim(a, me * n, n, axis=axis)
    g_small = [g_bmod, shard(g_gain_f, D // NDEV, 2), shard(g_pscale_f, D // NDEV, 1), shard(g_convw_f, CU, 2), g_convb,
               g_bf, g_fgain]
    w_small = [b_mod, norm_gain, pool_scale, conv_w, conv_b, b_forget, final_gain]
    m_small = [m_b_mod, m_norm_gain, m_pool_scale, m_conv_w, m_conv_b, m_b_forget, m_final_gain]
    v_small = [v_b_mod, v_norm_gain, v_pool_scale, v_conv_w, v_conv_b, v_b_forget, v_final_gain]
    small_raw = _adamw(_pack(w_small)[None], _pack(m_small)[None], _pack(v_small)[None], _pack(g_small)[None], 0, None,
                       "adamw_small")
    small_out = [_unpack(a[0], [w.shape for w in w_small]) for a in small_raw]

    dmod_all = jnp.stack([_unpack(sg_all[j], sg_shapes[:1])[0] for j in range(NDEV)])
    dmod_mine = lax.dynamic_slice_in_dim(dmod_all.reshape(NDEV, L, NDEV, CM), me, 1, axis=2)
    cond_t = jnp.pad(cond_all[:NDEV].T, ((0, 0), (0, LANES - NDEV)))
    mod_out = _adamw_mod(w_mod, m_w_mod, v_w_mod, cond_t, dmod_mine, "adamw_mod")
    exchanges_finish([mod_out[0], small_raw[0]], 0)

    pool4 = lambda a: a.reshape(w_pool.shape)
    names = ["w_mod", "b_mod", "norm_gain", "w_attn_in", "b_forget", "w_attn_out", "w_pool", "pool_scale", "w_up", "conv_w",
             "conv_b", "w_down", "final_gain"]
    small_pos = {"b_mod": 0, "norm_gain": 1, "pool_scale": 2, "conv_w": 3, "conv_b": 4, "b_forget": 5, "final_gain": 6}
    outs = []
    for kind in range(4):
        for nm in names:
            if nm == "w_mod":
                outs.append(mod_out[kind])
            elif nm in small_pos:
                outs.append(small_out[kind][small_pos[nm]])
            elif nm == "w_pool":
                outs.append(pool4(big[nm][kind]))
            elif nm == "w_attn_in":
                outs.append(jnp.stack([per_layer[kind][0] for per_layer in attn_in_t]).transpose(0, 2, 1))
            else:
                outs.append(big[nm][kind])
    return (loss, grad_x, *outs)
```

```python
import jax
import jax.numpy as jnp
from jax import lax
from jax.experimental import pallas as pl
from jax.experimental.pallas import tpu as pltpu

NDEV = 8
F32 = jnp.float32
BF16 = jnp.bfloat16
MESH = pl.DeviceIdType.MESH
VMEM_LIMIT_BYTES = 56 * 1024 * 1024
LANES = 128
POOL_WINDOWS = (2, 4, 8, 16)
EPS = 1e-6
ADAM_LR = 0.001
ADAM_B1 = 0.9
ADAM_B2 = 0.999
ADAM_EPS = 1e-08
ADAM_WD = 0.01
ADAM_STEP = 10
NEG_BIG = -1e30
SDS = jax.ShapeDtypeStruct
ANY = pl.BlockSpec(memory_space=pl.ANY)


def _pc(body, **kw):
    return pl.pallas_call(body, **kw)


def _params(n_axes):
    return pltpu.CompilerParams(dimension_semantics=("arbitrary",) * n_axes, vmem_limit_bytes=VMEM_LIMIT_BYTES)


def _pick(n, prefs):
    for p in prefs:
        if p <= n and n % p == 0:
            return p
    return n


def _idx(p):
    return 4 * p[0] + 2 * p[1] + p[2]


def _me():
    return lax.axis_index("x"), lax.axis_index("y"), lax.axis_index("c")


def _all_gather(arrs, name):
    n = len(arrs)

    def body(*refs):
        ins, outs = refs[:n], refs[n:2 * n]
        send_sems, recv_sems, local_sems = refs[2 * n:]
        x, y, c = _me()
        me, sib = (x, y, c), (x, y, 1 - c)
        chips = [(1 - x, y), (x, 1 - y), (1 - x, 1 - y)]

        def copy(t, k, block, to, src=None):
            dst = outs[t].at[_idx(block)]
            return pltpu.make_async_remote_copy(
                src_ref=dst if src is None else src, dst_ref=dst,
                send_sem=send_sems.at[7 * t + k], recv_sem=recv_sems.at[7 * t + k],
                device_id=to, device_id_type=MESH)

        mine = [pltpu.make_async_copy(ins[t], outs[t].at[_idx(me)], local_sems.at[t]) for t in range(n)]
        for cp in mine:
            cp.start()
        first = []
        for t in range(n):
            first.append(copy(t, 0, me, sib, src=ins[t]))
            for j, chip in enumerate(chips):
                first.append(copy(t, 1 + j, me, (*chip, c), src=ins[t]))
        for cp in first:
            cp.start()
        passed = []
        for t in range(n):
            for j, chip in enumerate(chips):
                copy(t, 1 + j, (*chip, c), me).wait_recv()
                cp = copy(t, 4 + j, (*chip, c), sib)
                cp.start()
                passed.append(cp)
        for t in range(n):
            copy(t, 0, sib, me).wait_recv()
            for j, chip in enumerate(chips):
                copy(t, 4 + j, (*chip, 1 - c), me).wait_recv()
        for cp in first + passed:
            cp.wait_send()
        for cp in mine:
            cp.wait()

    return _pc(
        body, name=name,
        out_shape=[SDS((NDEV,) + a.shape, a.dtype) for a in arrs],
        in_specs=[ANY] * n, out_specs=[ANY] * n,
        scratch_shapes=[pltpu.SemaphoreType.DMA((7 * n,)), pltpu.SemaphoreType.DMA((7 * n,)),
                        pltpu.SemaphoreType.DMA((n,))],
    )(*arrs)


HBM = pl.BlockSpec(memory_space=pltpu.HBM)
SEM = pl.BlockSpec(memory_space=pltpu.SEMAPHORE)
EFFECT = pltpu.SideEffectType.DATAFLOW_SIDE_EFFECTING
TOKEN = SDS((8, LANES), F32)


def _hbm(a):
    return pltpu.with_memory_space_constraint(a, pltpu.HBM)


def _landing(block):
    return lax.empty((NDEV,) + block.shape, block.dtype)


def _split_call(body, name, n_thru, thru, sems_in, after, sems_out):
    n_sem = len(sems_out)
    operands = [_hbm(a) for a in thru] + list(sems_in) + list(after)
    in_specs = [HBM] * n_thru + [SEM] * len(sems_in) + [ANY] * len(after)
    out_shape = [pltpu.SemaphoreType.DMA((k,)) for k in sems_out] + [pltpu.HBM(a.shape, a.dtype) for a in thru] + [TOKEN]
    out_specs = [SEM] * n_sem + [HBM] * n_thru + [pl.BlockSpec(memory_space=pltpu.VMEM)]
    outs = _pc(body, name=name, in_specs=in_specs, out_specs=out_specs, out_shape=out_shape,
               input_output_aliases={q: n_sem + q for q in range(n_thru)},
               compiler_params=pltpu.CompilerParams(has_side_effects=EFFECT))(*operands)
    return list(outs[:n_sem]), list(outs[n_sem:n_sem + n_thru]), outs[-1]


def _gather_start(shards, me, after, name):
    n = len(shards)
    lands = [_landing(s) for s in shards]

    def body(*refs):
        shard_refs, land_refs = refs[:n], refs[n:2 * n]
        send_sems, recv_sems, local_sems = refs[2 * n + len(after):2 * n + len(after) + 3]
        x, y, c = _me()
        me_i = _idx((x, y, c))
        peers = [(x, y, 1 - c), (1 - x, y, c), (x, 1 - y, c), (1 - x, 1 - y, c)]
        for t in range(n):
            pltpu.make_async_copy(shard_refs[t], land_refs[t].at[me_i], local_sems.at[t]).start()
            for k, p in enumerate(peers):
                pltpu.make_async_remote_copy(
                    src_ref=shard_refs[t], dst_ref=land_refs[t].at[me_i], send_sem=send_sems.at[4 * t + k],
                    recv_sem=recv_sems.at[4 * t + k], device_id=p, device_id_type=MESH).start()
        refs[-1][...] = jnp.zeros((8, LANES), F32)

    sems, thru, token = _split_call(body, name, 2 * n, list(shards) + lands, [], after, [4 * n, 4 * n, n])
    return dict(n=n, sems=sems, shards=thru[:n], lands=thru[n:]), token


def _gather_forward(st, after, name):
    n = st["n"]

    def body(*refs):
        shard_refs, land_refs = refs[:n], refs[n:2 * n]
        send1, recv1, local1 = refs[2 * n:2 * n + 3]
        send2, recv2 = refs[2 * n + 3 + len(after)], refs[2 * n + 4 + len(after)]
        x, y, c = _me()
        sib = (x, y, 1 - c)
        senders = [sib, (1 - x, y, c), (x, 1 - y, c), (1 - x, 1 - y, c)]
        for t in range(n):
            pltpu.make_async_copy(shard_refs[t], land_refs[t].at[_idx((x, y, c))], local1.at[t]).wait()
            for k, p in enumerate(senders):
                cp = pltpu.make_async_remote_copy(
                    src_ref=shard_refs[t], dst_ref=land_refs[t].at[_idx(p)], send_sem=send1.at[4 * t + k],
                    recv_sem=recv1.at[4 * t + k], device_id=p, device_id_type=MESH)
                cp.wait_send()
                cp.wait_recv()
        for t in range(n):
            for j, p in enumerate(senders[1:]):
                slab = land_refs[t].at[_idx(p)]
                pltpu.make_async_remote_copy(
                    src_ref=slab, dst_ref=slab, send_sem=send2.at[3 * t + j], recv_sem=recv2.at[3 * t + j],
                    device_id=sib, device_id_type=MESH).start()
        refs[-1][...] = jnp.zeros((8, LANES), F32)

    sems, thru, token = _split_call(body, name, 2 * n, st["shards"] + st["lands"], st["sems"], after, [3 * n, 3 * n])
    return dict(n=n, sems=sems, shards=thru[:n], lands=thru[n:]), token


def _gather_wait(st, after, name):
    n = st["n"]

    def body(*refs):
        land_refs = refs[n:2 * n]
        send2, recv2 = refs[2 * n], refs[2 * n + 1]
        x, y, c = _me()
        sib = (x, y, 1 - c)
        for t in range(n):
            for j, chip in enumerate([(1 - x, y), (x, 1 - y), (1 - x, 1 - y)]):
                sent, got = land_refs[t].at[_idx((*chip, c))], land_refs[t].at[_idx((*chip, 1 - c))]
                cp = pltpu.make_async_remote_copy(
                    src_ref=sent, dst_ref=got, send_sem=send2.at[3 * t + j], recv_sem=recv2.at[3 * t + j],
                    device_id=sib, device_id_type=MESH)
                cp.wait_send()
                cp.wait_recv()
        refs[-1][...] = jnp.zeros((8, LANES), F32)

    _, thru, _ = _split_call(body, name, 2 * n, st["shards"] + st["lands"], st["sems"], after, [])
    return thru[n:]


def _scatter_start(parts, me, after, name):
    n = len(parts)
    lands = [_landing(p[0]) for p in parts]

    def body(*refs):
        part_refs, land_refs = refs[:n], refs[n:2 * n]
        send_sems, recv_sems, local_sems = refs[2 * n + len(after):2 * n + len(after) + 3]
        x, y, c = _me()
        me_i = _idx((x, y, c))
        for t in range(n):
            pltpu.make_async_copy(part_refs[t].at[me_i], land_refs[t].at[me_i], local_sems.at[t]).start()
            for r in range(1, NDEV):
                p = (1 - x if r & 4 else x, 1 - y if r & 2 else y, 1 - c if r & 1 else c)
                pltpu.make_async_remote_copy(
                    src_ref=part_refs[t].at[_idx(p)], dst_ref=land_refs[t].at[me_i], send_sem=send_sems.at[7 * t + r - 1],
                    recv_sem=recv_sems.at[7 * t + r - 1], device_id=p, device_id_type=MESH).start()
        refs[-1][...] = jnp.zeros((8, LANES), F32)

    sems, thru, token = _split_call(body, name, 2 * n, list(parts) + lands, [], after, [7 * n, 7 * n, n])
    return dict(n=n, sems=sems, parts=thru[:n], lands=thru[n:]), token


def _scatter_wait(st, after, name):
    n = st["n"]

    def body(*refs):
        part_refs, land_refs = refs[:n], refs[n:2 * n]
        send_sems, recv_sems, local_sems = refs[2 * n:2 * n + 3]
        x, y, c = _me()
        me_i = _idx((x, y, c))
        for t in range(n):
            pltpu.make_async_copy(part_refs[t].at[me_i], land_refs[t].at[me_i], local_sems.at[t]).wait()
            for r in range(1, NDEV):
                p = (1 - x if r & 4 else x, 1 - y if r & 2 else y, 1 - c if r & 1 else c)
                cp = pltpu.make_async_remote_copy(
                    src_ref=part_refs[t].at[_idx(p)], dst_ref=land_refs[t].at[_idx(p)], send_sem=send_sems.at[7 * t + r - 1],
                    recv_sem=recv_sems.at[7 * t + r - 1], device_id=p, device_id_type=MESH)
                cp.wait_send()
                cp.wait_recv()
        refs[-1][...] = jnp.zeros((8, LANES), F32)

    _, thru, _ = _split_call(body, name, 2 * n, st["parts"] + st["lands"], st["sems"], after, [])
    return thru[n:]


def _pair_start(parts, after, name):
    n = len(parts)
    recvs = [lax.empty((4,) + p.shape[1:], p.dtype) for p in parts]

    def body(*refs):
        part_refs, recv_refs = refs[:n], refs[n:2 * n]
        send_sems, recv_sems = refs[2 * n + len(after)], refs[2 * n + len(after) + 1]
        x, y, c = _me()
        for t in range(n):
            for q in range(4):
                pltpu.make_async_remote_copy(
                    src_ref=part_refs[t].at[2 * q + (1 - c)], dst_ref=recv_refs[t].at[q], send_sem=send_sems.at[4 * t + q],
                    recv_sem=recv_sems.at[4 * t + q], device_id=(x, y, 1 - c), device_id_type=MESH).start()
        refs[-1][...] = jnp.zeros((8, LANES), F32)

    sems, thru, token = _split_call(body, name, 2 * n, list(parts) + recvs, [], after, [4 * n, 4 * n])
    return dict(n=n, sems=sems, parts=thru[:n], recvs=thru[n:]), token


def _pair_wait(st, after, name):
    n = st["n"]

    def body(*refs):
        part_refs, recv_refs = refs[:n], refs[n:2 * n]
        send_sems, recv_sems = refs[2 * n], refs[2 * n + 1]
        x, y, c = _me()
        for t in range(n):
            for q in range(4):
                cp = pltpu.make_async_remote_copy(
                    src_ref=part_refs[t].at[2 * q + (1 - c)], dst_ref=recv_refs[t].at[q], send_sem=send_sems.at[4 * t + q],
                    recv_sem=recv_sems.at[4 * t + q], device_id=(x, y, 1 - c), device_id_type=MESH)
                cp.wait_send()
                cp.wait_recv()
        refs[-1][...] = jnp.zeros((8, LANES), F32)

    _, thru, _ = _split_call(body, name, 2 * n, st["parts"] + st["recvs"], st["sems"], after, [])
    return thru[:n], thru[n:]


def _pair_sum(part, recv, name):
    _, R, C = recv.shape
    tr = _pick(R, (256, 128, 64, 88, 32, 16, 8))

    def body(core_ref, a_ref, b_ref, o_ref):
        o_ref[...] = (a_ref[...].astype(F32) + b_ref[...].astype(F32)).astype(o_ref.dtype)

    grid_spec = pltpu.PrefetchScalarGridSpec(
        num_scalar_prefetch=1, grid=(4, R // tr),
        in_specs=[pl.BlockSpec((None, None, tr, C), lambda q, i, core: (q, core[0], i, 0)),
                  pl.BlockSpec((None, tr, C), lambda q, i, core: (q, i, 0))],
        out_specs=pl.BlockSpec((None, tr, C), lambda q, i, core: (q, i, 0)))
    core = lax.axis_index("c").astype(jnp.int32).reshape(1)
    return _pc(body, name=name, grid_spec=grid_spec, out_shape=SDS(recv.shape, recv.dtype),
               compiler_params=_params(2))(core, part.reshape((4, 2) + part.shape[1:]), recv)


def _chips_start(sums, after, name):
    n = len(sums)
    lands = [lax.empty(s.shape, s.dtype) for s in sums]

    def body(*refs):
        sum_refs, land_refs = refs[:n], refs[n:2 * n]
        send_sems, recv_sems, local_sems = refs[2 * n + len(after):2 * n + len(after) + 3]
        x, y, c = _me()
        my_chip = 2 * x + y
        for t in range(n):
            pltpu.make_async_copy(sum_refs[t].at[my_chip], land_refs[t].at[my_chip], local_sems.at[t]).start()
            for j, (px, py) in enumerate([(1 - x, y), (x, 1 - y), (1 - x, 1 - y)]):
                pltpu.make_async_remote_copy(
                    src_ref=sum_refs[t].at[2 * px + py], dst_ref=land_refs[t].at[my_chip], send_sem=send_sems.at[3 * t + j],
                    recv_sem=recv_sems.at[3 * t + j], device_id=(px, py, c), device_id_type=MESH).start()
        refs[-1][...] = jnp.zeros((8, LANES), F32)

    sems, thru, token = _split_call(body, name, 2 * n, list(sums) + lands, [], after, [3 * n, 3 * n, n])
    return dict(n=n, sems=sems, sums=thru[:n], lands=thru[n:]), token


def _chips_wait(st, after, name):
    n = st["n"]

    def body(*refs):
        sum_refs, land_refs = refs[:n], refs[n:2 * n]
        send_sems, recv_sems, local_sems = refs[2 * n:2 * n + 3]
        x, y, c = _me()
        my_chip = 2 * x + y
        for t in range(n):
            pltpu.make_async_copy(sum_refs[t].at[my_chip], land_refs[t].at[my_chip], local_sems.at[t]).wait()
            for j, (px, py) in enumerate([(1 - x, y), (x, 1 - y), (1 - x, 1 - y)]):
                cp = pltpu.make_async_remote_copy(
                    src_ref=sum_refs[t].at[2 * px + py], dst_ref=land_refs[t].at[2 * px + py], send_sem=send_sems.at[3 * t + j],
                    recv_sem=recv_sems.at[3 * t + j], device_id=(px, py, c), device_id_type=MESH)
                cp.wait_send()
                cp.wait_recv()
        refs[-1][...] = jnp.zeros((8, LANES), F32)

    _, thru, _ = _split_call(body, name, 2 * n, st["sums"] + st["lands"], st["sems"], after, [])
    return thru[n:]


def _matmul(a, b, *, mode, name, out_dtype, tm=1024, tn=1024, tk=2048, b_blocked=False, out_blocked=False,
            a_split=False, b_split=False, res=None, gate=None, y_dtype=None, n=None):
    if mode == "tn":
        K, M = (a.shape[0], a.shape[1]) if not a_split else (a.shape[1], 2 * a.shape[2])
    else:
        M, K = (a.shape[0], a.shape[1]) if not a_split else (a.shape[1], 2 * a.shape[2])
    if b_blocked:
        if mode == "nn":
            N, tn = b.shape[0] * b.shape[2], b.shape[2]
        else:
            N, tk = b.shape[1], b.shape[2]
    elif b_split:
        N = 2 * b.shape[2]
    else:
        N = b.shape[0] if mode == "nt" else b.shape[1]
    if n is not None:
        N = n
    tm = _pick(M, (tm, 704, 512, 384, 256, 128))
    if not (b_blocked and mode == "nn"):
        tn = _pick(N, (tn, 1024, 768, 512, 384, 256, 128))
    if not (b_blocked and mode == "nt"):
        tk = _pick(K, (tk, 1024, 512, 384, 256, 128))
    nm, nn_, nk = M // tm, N // tn, K // tk

    if mode == "tn":
        a_spec = pl.BlockSpec((tk, tm), lambda i, j, k: (k, i))
        dims = (((0,), (0,)), ((), ()))
    elif a_split:
        per = a.shape[2] // tk
        a_spec = pl.BlockSpec((None, tm, tk), lambda i, j, k: (k // per, i, k % per))
    else:
        a_spec = pl.BlockSpec((tm, tk), lambda i, j, k: (i, k))
    if mode == "nn":
        dims = (((1,), (0,)), ((), ()))
        if b_blocked:
            b_spec = pl.BlockSpec((None, tk, tn), lambda i, j, k: (j, k, 0))
        else:
            b_spec = pl.BlockSpec((tk, tn), lambda i, j, k: (k, j))
    elif mode == "nt":
        dims = (((1,), (1,)), ((), ()))
        if b_blocked:
            b_spec = pl.BlockSpec((None, tn, tk), lambda i, j, k: (k, j, 0))
        else:
            b_spec = pl.BlockSpec((tn, tk), lambda i, j, k: (j, k))
    else:
        if b_split:
            per_b = b.shape[2] // tn
            b_spec = pl.BlockSpec((None, tk, tn), lambda i, j, k: (j // per_b, k, j % per_b))
        else:
            b_spec = pl.BlockSpec((tk, tn), lambda i, j, k: (k, j))
    if out_blocked:
        o_spec = pl.BlockSpec((None, tm, tn), lambda i, j, k: (j, i, 0))
        o_shape = SDS((nn_, M, tn), out_dtype)
    else:
        o_spec = pl.BlockSpec((tm, tn), lambda i, j, k: (i, j))
        o_shape = SDS((M, N), out_dtype)
    fused = res is not None
    in_specs, operands = [a_spec, b_spec], [a, b]
    out_specs, out_shapes = [o_spec], [o_shape]
    if fused:
        in_specs += [pl.BlockSpec((tm, tn), lambda i, j, k: (i, j)), pl.BlockSpec((1, tn), lambda i, j, k: (0, j))]
        operands += [res, gate]
        if y_dtype is not None:
            out_specs.append(pl.BlockSpec((tm, tn), lambda i, j, k: (i, j)))
            out_shapes.append(SDS((M, N), y_dtype))

    def body(*refs):
        a_ref, b_ref = refs[0], refs[1]
        acc_ref = refs[-1]
        k = pl.program_id(2)

        def product():
            return lax.dot_general(a_ref[...], b_ref[...], dims, preferred_element_type=F32)

        def finish(acc):
            if fused:
                res_ref, gate_ref, o_ref = refs[2], refs[3], refs[4]
                o_ref[...] = (res_ref[...] + gate_ref[...] * acc).astype(o_ref.dtype)
                if y_dtype is not None:
                    refs[5][...] = acc.astype(y_dtype)
            else:
                refs[2][...] = acc.astype(refs[2].dtype)

        if nk == 1:
            finish(product())
        else:
            @pl.when(k == 0)
            def _():
                acc_ref[...] = product()

            @pl.when(jnp.logical_and(k > 0, k < nk - 1))
            def _():
                acc_ref[...] += product()

            @pl.when(k == nk - 1)
            def _():
                finish(acc_ref[...] + product())

    outs = _pc(body, name=name, grid=(nm, nn_, nk), in_specs=in_specs, out_specs=out_specs, out_shape=out_shapes,
               scratch_shapes=[pltpu.VMEM((tm, tn), F32)], compiler_params=_params(3))(*operands)
    return outs[0] if len(outs) == 1 else tuple(outs)


def _norm_mod(x, gain, sc, sh, out_dtype, name):
    S, D = x.shape
    tr = _pick(S, (256, 128))

    def body(x_ref, g_ref, sc_ref, sh_ref, o_ref):
        xv = x_ref[...]
        r = lax.rsqrt(jnp.mean(xv * xv, axis=-1, keepdims=True) + EPS)
        n = (xv * r) * g_ref[...]
        o_ref[...] = (n * (1.0 + sc_ref[...]) + sh_ref[...]).astype(o_ref.dtype)

    row = pl.BlockSpec((tr, D), lambda i: (i, 0))
    vec = pl.BlockSpec((1, D), lambda i: (0, 0))
    return _pc(body, name=name, grid=(S // tr,), in_specs=[row, vec, vec, vec], out_specs=row,
               out_shape=SDS((S, D), out_dtype), compiler_params=_params(1))(x, gain, sc, sh)


def _norm_mod_bwd(dh, x, dxres, gain, sc, name):
    S, D = x.shape
    tr = _pick(S, (256, 128))

    def body(dh_ref, x_ref, dxres_ref, g_ref, sc_ref, dx_ref, dgain_ref, dsc_ref, dsh_ref):
        @pl.when(pl.program_id(0) == 0)
        def _():
            dgain_ref[...] = jnp.zeros_like(dgain_ref)
            dsc_ref[...] = jnp.zeros_like(dsc_ref)
            dsh_ref[...] = jnp.zeros_like(dsh_ref)

        xv = x_ref[...]
        dh = dh_ref[...].astype(F32)
        r = lax.rsqrt(jnp.mean(xv * xv, axis=-1, keepdims=True) + EPS)
        nh = xv * r
        gn = g_ref[...]
        dn = dh * (1.0 + sc_ref[...])
        dgain_ref[...] += jnp.sum(dn * nh, axis=0, keepdims=True)
        dsc_ref[...] += jnp.sum(dh * (nh * gn), axis=0, keepdims=True)
        dsh_ref[...] += jnp.sum(dh, axis=0, keepdims=True)
        dnh = dn * gn
        dx = r * (dnh - nh * jnp.mean(dnh * nh, axis=-1, keepdims=True))
        dx_ref[...] = dxres_ref[...] + dx

    row = pl.BlockSpec((tr, D), lambda i: (i, 0))
    vec = pl.BlockSpec((1, D), lambda i: (0, 0))
    return _pc(body, name=name, grid=(S // tr,), in_specs=[row, row, row, vec, vec], out_specs=[row, vec, vec, vec],
               out_shape=[SDS((S, D), F32), SDS((1, D), F32), SDS((1, D), F32), SDS((1, D), F32)],
               compiler_params=_params(1))(dh, x, dxres, gain, sc)


def _gate_bwd(dx, y, gate, name):
    S, D = dx.shape
    tr = _pick(S, (256, 128))

    def body(dx_ref, y_ref, gate_ref, dy_ref, dgate_ref):
        @pl.when(pl.program_id(0) == 0)
        def _():
            dgate_ref[...] = jnp.zeros_like(dgate_ref)

        dxv = dx_ref[...]
        dgate_ref[...] += jnp.sum(dxv * y_ref[...].astype(F32), axis=0, keepdims=True)
        dy_ref[...] = (dxv * gate_ref[...]).astype(BF16)

    row = pl.BlockSpec((tr, D), lambda i: (i, 0))
    vec = pl.BlockSpec((1, D), lambda i: (0, 0))
    return _pc(body, name=name, grid=(S // tr,), in_specs=[row, row, vec], out_specs=[row, vec],
               out_shape=[SDS((S, D), BF16), SDS((1, D), F32)], compiler_params=_params(1))(dx, y, gate)


def _loss_head(x, target, fgain, name):
    S, D = x.shape
    tr = _pick(S, (256, 128))

    def body(x_ref, t_ref, fg_ref, dx_ref, dfg_ref, loss_ref):
        @pl.when(pl.program_id(0) == 0)
        def _():
            dfg_ref[...] = jnp.zeros_like(dfg_ref)
            loss_ref[...] = jnp.zeros_like(loss_ref)

        xv = x_ref[...]
        fg = fg_ref[...]
        r = lax.rsqrt(jnp.mean(xv * xv, axis=-1, keepdims=True) + EPS)
        nh = xv * r
        e = nh * fg - t_ref[...]
        loss_ref[...] += 0.5 * jnp.sum(jnp.mean(e * e, axis=-1, keepdims=True))
        dy = e * (1.0 / D)
        dfg_ref[...] += jnp.sum(dy * nh, axis=0, keepdims=True)
        dnh = dy * fg
        dx_ref[...] = r * (dnh - nh * jnp.mean(dnh * nh, axis=-1, keepdims=True))

    row = pl.BlockSpec((tr, D), lambda i: (i, 0))
    vec = pl.BlockSpec((1, D), lambda i: (0, 0))
    tile = pl.BlockSpec((8, LANES), lambda i: (0, 0))
    return _pc(body, name=name, grid=(S // tr,), in_specs=[row, row, vec], out_specs=[row, vec, tile],
               out_shape=[SDS((S, D), F32), SDS((1, D), F32), SDS((8, LANES), F32)],
               compiler_params=_params(1))(x, target, fgain)


def _shift_down(v, k, rows):
    return jnp.where(rows >= k, pltpu.roll(v, k, axis=0), 0.0)


def _shift_up(v, k, rows):
    n = v.shape[0]
    return jnp.where(rows < n - k, pltpu.roll(v, n - k, axis=0), 0.0)


CONV_HALO = 8


def _down(v, k, rows, at_top):
    r = pltpu.roll(v, k, axis=0)
    return jnp.where(rows >= k, r, 0.0) if at_top else r


def _up(v, k, rows, at_bottom):
    n = v.shape[0]
    r = pltpu.roll(v, n - k, axis=0)
    return jnp.where(rows < n - k, r, 0.0) if at_bottom else r


def _conv(uv, w, b, rows, at_top):
    return ((b + _down(uv, 2, rows, at_top) * w[0:1]) + _down(uv, 1, rows, at_top) * w[1:2]) + uv * w[2:3]


def _conv_chunk_rows(S):
    return _pick(max(S // 2, 1), (128,))


def _convgate_fwd(u, cw, cb, name):
    S, F2 = u.shape
    DFF = F2 // 2
    tc = _pick(DFF, (256, 128))
    sub = min(tc, LANES)
    nj = DFF // tc
    R = _conv_chunk_rows(S)

    def body(ua_ref, ug_ref, wa_ref, wg_ref, ba_ref, bg_ref, o_ref):
        for q in range(tc // sub):
            sl = slice(q * sub, (q + 1) * sub)
            wa, wg, ba, bg = wa_ref[:, sl], wg_ref[:, sl], ba_ref[:, sl], bg_ref[:, sl]
            for r0 in range(0, S, R):
                lo = max(r0 - CONV_HALO, 0)
                rows = lax.broadcasted_iota(jnp.int32, (r0 + R - lo, sub), 0)
                ya = _conv(ua_ref[lo:r0 + R, sl], wa, ba, rows, lo == 0)[r0 - lo:]
                yg = _conv(ug_ref[lo:r0 + R, sl], wg, bg, rows, lo == 0)[r0 - lo:]
                o_ref[r0:r0 + R, sl] = (yg * jax.nn.sigmoid(yg) * ya).astype(BF16)

    col = lambda off: pl.BlockSpec((S, tc), lambda j: (0, j + off))
    w3 = lambda off: pl.BlockSpec((3, tc), lambda j: (0, j + off))
    b1 = lambda off: pl.BlockSpec((1, tc), lambda j: (0, j + off))
    return _pc(body, name=name, grid=(nj,), in_specs=[col(0), col(nj), w3(0), w3(nj), b1(0), b1(nj)],
               out_specs=col(0), out_shape=SDS((S, DFF), BF16), compiler_params=_params(1))(u, u, cw, cw, cb, cb)


def _convgate_bwd(u, dact, cw, cb, name):
    S, F2 = u.shape
    DFF = F2 // 2
    tc = _pick(DFF, (256, 128))
    sub = min(tc, LANES)
    nj = DFF // tc
    R = _conv_chunk_rows(S)

    def body(ua_ref, ug_ref, da_ref, wa_ref, wg_ref, ba_ref, bg_ref, du_ref, dcw_ref, dcb_ref):
        for q in range(tc // sub):
            sl = slice(q * sub, (q + 1) * sub)
            wa, wg, ba, bg = wa_ref[:, sl], wg_ref[:, sl], ba_ref[:, sl], bg_ref[:, sl]
            sums = [[None] * 4, [None] * 4]
            for r0 in range(0, S, R):
                lo, hi = max(r0 - CONV_HALO, 0), min(r0 + R + CONV_HALO, S)
                top, bottom, inner = lo == 0, hi == S, slice(r0 - lo, r0 - lo + R)
                rows = lax.broadcasted_iota(jnp.int32, (hi - lo, sub), 0)
                ua, ug, da = ua_ref[lo:hi, sl], ug_ref[lo:hi, sl], da_ref[lo:hi, sl]
                ya = _conv(ua, wa, ba, rows, top)
                yg = _conv(ug, wg, bg, rows, top)
                s = jax.nn.sigmoid(yg)
                dya = da * (yg * s)
                dyg = da * ya * (s * (1.0 + yg * (1.0 - s)))
                for h, (dy, uv, w) in enumerate(((dya, ua, wa), (dyg, ug, wg))):
                    du = (dy * w[2:3] + _up(dy, 1, rows, bottom) * w[1:2]) + _up(dy, 2, rows, bottom) * w[0:1]
                    du_ref[h, r0:r0 + R, sl] = du[inner].astype(BF16)
                    terms = [dy * _down(uv, 2, rows, top), dy * _down(uv, 1, rows, top), dy * uv, dy]
                    for i, term in enumerate(terms):
                        part = jnp.sum(term[inner], axis=0, keepdims=True)
                        sums[h][i] = part if sums[h][i] is None else sums[h][i] + part
            for h in range(2):
                for i in range(3):
                    dcw_ref[h, i:i + 1, sl] = sums[h][i]
                dcb_ref[h, :, sl] = sums[h][3]

    col = lambda off: pl.BlockSpec((S, tc), lambda j: (0, j + off))
    w3 = lambda off: pl.BlockSpec((3, tc), lambda j: (0, j + off))
    b1 = lambda off: pl.BlockSpec((1, tc), lambda j: (0, j + off))
    return _pc(body, name=name, grid=(nj,),
               in_specs=[col(0), col(nj), col(0), w3(0), w3(nj), b1(0), b1(nj)],
               out_specs=[pl.BlockSpec((2, S, tc), lambda j: (0, 0, j)), pl.BlockSpec((2, 3, tc), lambda j: (0, 0, j)),
                          pl.BlockSpec((2, 1, tc), lambda j: (0, 0, j))],
               out_shape=[SDS((2, S, DFF), BF16), SDS((2, 3, DFF), F32), SDS((2, 1, DFF), F32)],
               compiler_params=_params(1))(u, u, dact, cw, cw, cb, cb)


def _pool_diff(h, name):
    S, D = h.shape
    G = len(POOL_WINDOWS)
    CG = D // G
    tc = min(CG, LANES)
    per = CG // tc

    def body(h_ref, d_ref):
        g = pl.program_id(0)
        rows = lax.broadcasted_iota(jnp.int32, (S, tc), 0)
        for gi, w in enumerate(POOL_WINDOWS):
            @pl.when(g == gi)
            def _(w=w):
                hv = h_ref[...]
                s, k = hv, 1
                while k < w:
                    s = s + _shift_down(s, k, rows)
                    k *= 2
                count = jnp.minimum(rows + 1, w).astype(F32)
                d_ref[...] = (s / count - hv).astype(BF16)

    spec = pl.BlockSpec((S, tc), lambda g, j: (0, g * per + j))
    return _pc(body, name=name, grid=(G, per), in_specs=[spec], out_specs=spec, out_shape=SDS((S, D), BF16),
               compiler_params=_params(2))(h)


def _pool_diff_bwd(dd, name):
    S, D = dd.shape
    G = len(POOL_WINDOWS)
    CG = D // G
    tc = min(CG, LANES)
    per = CG // tc

    def body(dd_ref, o_ref):
        g = pl.program_id(0)
        rows = lax.broadcasted_iota(jnp.int32, (S, tc), 0)
        for gi, w in enumerate(POOL_WINDOWS):
            @pl.when(g == gi)
            def _(w=w):
                dv = dd_ref[...]
                count = jnp.minimum(rows + 1, w).astype(F32)
                s, k = dv / count, 1
                while k < w:
                    s = s + _shift_up(s, k, rows)
                    k *= 2
                o_ref[...] = s - dv

    spec = pl.BlockSpec((S, tc), lambda g, j: (0, g * per + j))
    return _pc(body, name=name, grid=(G, per), in_specs=[spec], out_specs=spec, out_shape=SDS((S, D), F32),
               compiler_params=_params(2))(dd)


def _pool_mm(d, w, res, gate, name):
    S, D = d.shape
    G, CG, _ = w.shape
    tm = _pick(S, (512, 256, 128))

    def body(d_ref, w_ref, res_ref, gate_ref, o_ref, e_ref):
        acc = jnp.dot(d_ref[...], w_ref[...], preferred_element_type=F32)
        o_ref[...] = res_ref[...] + gate_ref[...] * acc
        e_ref[...] = acc.astype(BF16)

    blk = pl.BlockSpec((tm, CG), lambda g, i: (i, g))
    return _pc(body, name=name, grid=(G, S // tm),
               in_specs=[blk, pl.BlockSpec((None, CG, CG), lambda g, i: (g, 0, 0)), blk,
                         pl.BlockSpec((1, CG), lambda g, i: (0, g))],
               out_specs=[blk, blk], out_shape=[SDS((S, D), F32), SDS((S, D), BF16)],
               compiler_params=_params(2))(d, w, res, gate)


def _pool_mm_bwd(de, d, w, name):
    S, D = de.shape
    G, CG, _ = w.shape
    tm = _pick(S, (512, 256, 128))
    ns = S // tm

    def body(de_ref, d_ref, w_ref, dd_ref, dw_ref, acc_ref):
        i = pl.program_id(1)

        @pl.when(i == 0)
        def _():
            acc_ref[...] = jnp.zeros_like(acc_ref)

        dev = de_ref[...]
        dd_ref[...] = lax.dot_general(dev, w_ref[...], (((1,), (1,)), ((), ())), preferred_element_type=F32)
        acc_ref[...] += lax.dot_general(d_ref[...], dev, (((0,), (0,)), ((), ())), preferred_element_type=F32)

        @pl.when(i == ns - 1)
        def _():
            dw_ref[...] = acc_ref[...].astype(BF16)

    blk = pl.BlockSpec((tm, CG), lambda g, i: (i, g))
    wsp = pl.BlockSpec((None, CG, CG), lambda g, i: (g, 0, 0))
    return _pc(body, name=name, grid=(G, ns), in_specs=[blk, blk, wsp], out_specs=[blk, wsp],
               out_shape=[SDS((S, D), F32), SDS((G, CG, CG), BF16)], scratch_shapes=[pltpu.VMEM((CG, CG), F32)],
               compiler_params=_params(2))(de, d, w)


def _log_sigmoid(z):
    return jnp.minimum(z, 0.0) - jnp.log(1.0 + jnp.exp(-jnp.abs(z)))


def _dot2(a, tri):
    hi = a.astype(BF16)
    lo = (a - hi.astype(F32)).astype(BF16)
    return jnp.dot(hi, tri, preferred_element_type=F32) + jnp.dot(lo, tri, preferred_element_type=F32)


_NT = (((1,), (1,)), ((), ()))
_TN = (((0,), (0,)), ((), ()))


def _forget_cumsum(flog, bf, name):
    S, W = flog.shape
    tb = _pick(S, (128,))

    def body(f_ref, b_ref, o_ref):
        r = lax.broadcasted_iota(jnp.int32, (tb, tb), 0)
        c = lax.broadcasted_iota(jnp.int32, (tb, tb), 1)
        tri = (c <= r).astype(F32)
        carry = jnp.zeros((1, W), F32)
        for q in range(S // tb):
            ls = _log_sigmoid(f_ref[q * tb:(q + 1) * tb, :] + b_ref[...])
            o_ref[q * tb:(q + 1) * tb, :] = carry + jnp.dot(tri, ls, preferred_element_type=F32,
                                                            precision=lax.Precision.HIGHEST)
            carry = carry + jnp.sum(ls, axis=0, keepdims=True)

    return _pc(body, name=name, out_shape=SDS((S, W), F32))(flog, bf)


def _forget_cumsum_bwd(dF, flog, bf, name):
    S, W = flog.shape
    tb = _pick(S, (128,))

    def body(d_ref, f_ref, b_ref, o_ref, db_ref):
        r = lax.broadcasted_iota(jnp.int32, (tb, tb), 0)
        c = lax.broadcasted_iota(jnp.int32, (tb, tb), 1)
        tri = (c >= r).astype(F32)
        carry = jnp.zeros((1, W), F32)
        db = jnp.zeros((1, W), F32)
        for q in reversed(range(S // tb)):
            dv = d_ref[q * tb:(q + 1) * tb, :]
            dls = carry + jnp.dot(tri, dv, preferred_element_type=F32, precision=lax.Precision.HIGHEST)
            carry = carry + jnp.sum(dv, axis=0, keepdims=True)
            dfl = dls * jax.nn.sigmoid(-(f_ref[q * tb:(q + 1) * tb, :] + b_ref[...]))
            o_ref[q * tb:(q + 1) * tb, :] = dfl
            db = db + jnp.sum(dfl, axis=0, keepdims=True)
        db_ref[...] = db

    return _pc(body, name=name, out_shape=[SDS((S, W), F32), SDS((1, W), F32)])(dF, flog, bf)


def _heads_per_step(n_heads):
    return 4 if n_heads % 4 == 0 else 2 if n_heads % 2 == 0 else 1


def _sb_fwd(qkv, NH, NSB, HD, T, name):
    S = qkv.shape[0]
    nq = S // T
    scale = HD ** -0.5
    HB = _heads_per_step(NSB)
    W = HB * HD

    def body(q_ref, k_ref, v_ref, o_ref, tot_ref):
        i = pl.program_id(1)
        row = lax.broadcasted_iota(jnp.int32, (T, T), 0)
        col = lax.broadcasted_iota(jnp.int32, (T, T), 1)
        upper = (row > col).astype(BF16)
        heads = [slice(hh * HD, (hh + 1) * HD) for hh in range(HB)]
        qs = [q_ref[:, cs] for cs in heads]

        def blk(kb, state, diag):
            sl = pl.ds(pl.multiple_of(kb * T, T), T)
            out = []
            for hh, cs in enumerate(heads):
                carry, acc = state[2 * hh], state[2 * hh + 1]
                k, v = k_ref[sl, cs], v_ref[sl, cs]
                z = lax.dot_general(qs[hh], k, _NT, preferred_element_type=F32) * scale
                ls = _log_sigmoid(z)
                lr = ls - z
                if diag:
                    lr = jnp.where(col < row, lr, 0.0)
                rest = _dot2(lr, upper) + carry
                w = jnp.exp(ls + rest)
                if diag:
                    w = jnp.where(col < row, w, 0.0)
                out += [carry + jnp.sum(lr, axis=1, keepdims=True),
                        acc + jnp.dot(w.astype(BF16), v, preferred_element_type=F32)]
            return tuple(out)

        state = blk(i, (jnp.zeros((T, 1), F32), jnp.zeros((T, HD), F32)) * HB, True)
        state = lax.fori_loop(0, i, lambda jj, st: blk(i - 1 - jj, st, False), state)
        for hh, cs in enumerate(heads):
            o_ref[:, cs] = state[2 * hh + 1].astype(BF16)
            tot_ref[hh] = state[2 * hh]

    return _pc(body, name=name, grid=(NSB // HB, nq),
               in_specs=[pl.BlockSpec((T, W), lambda h, i: (i, h)),
                         pl.BlockSpec((S, W), lambda h, i: (0, NH // HB + h)),
                         pl.BlockSpec((S, W), lambda h, i: (0, 2 * NH // HB + h))],
               out_specs=[pl.BlockSpec((T, W), lambda h, i: (i, h)), pl.BlockSpec((HB, T, 1), lambda h, i: (h, i, 0))],
               out_shape=[SDS((S, NSB * HD), BF16), SDS((NSB, S, 1), F32)],
               compiler_params=_params(2))(qkv, qkv, qkv)


def _sb_bwd(qkv, do, tot, NH, NSB, HD, T, name):
    S = qkv.shape[0]
    nq = S // T
    scale = HD ** -0.5
    HB = _heads_per_step(NSB)
    W = HB * HD

    def body(q_ref, k_ref, v_ref, do_ref, tot_ref, dq_ref, dk_ref, dv_ref, dk_acc, dv_acc):
        i = pl.program_id(1)

        @pl.when(i == 0)
        def _():
            dk_acc[...] = jnp.zeros_like(dk_acc)
            dv_acc[...] = jnp.zeros_like(dv_acc)

        row = lax.broadcasted_iota(jnp.int32, (T, T), 0)
        col = lax.broadcasted_iota(jnp.int32, (T, T), 1)
        incl = (row <= col).astype(BF16)
        strict = (row < col).astype(BF16)
        heads = [slice(hh * HD, (hh + 1) * HD) for hh in range(HB)]
        qs = [q_ref[:, cs] for cs in heads]
        dos = [do_ref[:, cs] for cs in heads]
        tots = [tot_ref[hh] for hh in range(HB)]

        def blk(kb, state, diag):
            sl = pl.ds(pl.multiple_of(kb * T, T), T)
            out = []
            for hh, cs in enumerate(heads):
                cl, cg, dq = state[3 * hh], state[3 * hh + 1], state[3 * hh + 2]
                q, do_ = qs[hh], dos[hh]
                k, v = k_ref[sl, cs], v_ref[sl, cs]
                z = lax.dot_general(q, k, _NT, preferred_element_type=F32) * scale
                ls = _log_sigmoid(z)
                lr = ls - z
                if diag:
                    lr = jnp.where(col < row, lr, 0.0)
                rest = tots[hh] - (cl + _dot2(lr, incl))
                w = jnp.exp(ls + rest)
                if diag:
                    w = jnp.where(col < row, w, 0.0)
                g = lax.dot_general(do_, v, _NT, preferred_element_type=F32) * w
                dv_acc[sl, cs] += lax.dot_general(w.astype(BF16), do_, _TN, preferred_element_type=F32)
                dlr = cg + _dot2(g, strict)
                dz = g * jnp.exp(lr) - dlr * jnp.exp(ls)
                if diag:
                    dz = jnp.where(col < row, dz, 0.0)
                dzb = (dz * scale).astype(BF16)
                dk_acc[sl, cs] += lax.dot_general(dzb, q, _TN, preferred_element_type=F32)
                out += [cl + jnp.sum(lr, axis=1, keepdims=True), cg + jnp.sum(g, axis=1, keepdims=True),
                        dq + jnp.dot(dzb, k, preferred_element_type=F32)]
            return tuple(out)

        zero = jnp.zeros((T, 1), F32)
        state = lax.fori_loop(0, i, lambda kb, st: blk(kb, st, False), (zero, zero, jnp.zeros((T, HD), F32)) * HB)
        state = blk(i, state, True)
        for hh, cs in enumerate(heads):
            dq_ref[:, cs] = state[3 * hh + 2].astype(BF16)

        @pl.when(i == nq - 1)
        def _():
            dk_ref[...] = dk_acc[...].astype(BF16)
            dv_ref[...] = dv_acc[...].astype(BF16)

    qblk = pl.BlockSpec((T, W), lambda h, i: (i, h))
    full = pl.BlockSpec((S, W), lambda h, i: (0, h))
    return _pc(body, name=name, grid=(NSB // HB, nq),
               in_specs=[qblk, pl.BlockSpec((S, W), lambda h, i: (0, NH // HB + h)),
                         pl.BlockSpec((S, W), lambda h, i: (0, 2 * NH // HB + h)), qblk,
                         pl.BlockSpec((HB, T, 1), lambda h, i: (h, i, 0))],
               out_specs=[qblk, full, full],
               out_shape=[SDS((S, NSB * HD), BF16)] * 3,
               scratch_shapes=[pltpu.VMEM((S, W), F32), pltpu.VMEM((S, W), F32)],
               compiler_params=_params(2))(qkv, qkv, qkv, do, tot)


def _fox_fwd(qkv, fcol, frow, NH, NSB, HD, T, name):
    S = qkv.shape[0]
    NFX = NH - NSB
    nq = S // T
    scale = HD ** -0.5
    HB = _heads_per_step(NFX) if NSB % _heads_per_step(NFX) == 0 else 1
    W = HB * HD

    def body(q_ref, k_ref, v_ref, fq_ref, fk_ref, o_ref, o32_ref, lse_ref):
        i = pl.program_id(1)
        row = lax.broadcasted_iota(jnp.int32, (T, T), 0)
        col = lax.broadcasted_iota(jnp.int32, (T, T), 1)
        heads = [slice(hh * HD, (hh + 1) * HD) for hh in range(HB)]
        qs = [q_ref[:, cs] for cs in heads]
        fqs = [fq_ref[hh] for hh in range(HB)]

        def blk(kb, state, diag):
            sl = pl.ds(pl.multiple_of(kb * T, T), T)
            out = []
            for hh, cs in enumerate(heads):
                m, l, acc, rem = state[4 * hh:4 * hh + 4]
                k, v = k_ref[sl, cs], v_ref[sl, cs]
                s = lax.dot_general(qs[hh], k, _NT, preferred_element_type=F32) * scale + (fqs[hh] - fk_ref[hh, kb])
                if diag:
                    s = jnp.where(col <= row, s, NEG_BIG)
                m_new = jnp.maximum(m, jnp.max(s, axis=1, keepdims=True))
                p = jnp.exp(s - m_new)
                alpha = jnp.exp(m - m_new)
                hi = p.astype(BF16)
                lo = (p - hi.astype(F32)).astype(BF16)
                out += [m_new, alpha * l + jnp.sum(p, axis=1, keepdims=True),
                        alpha * acc + jnp.dot(hi, v, preferred_element_type=F32),
                        alpha * rem + jnp.dot(lo, v, preferred_element_type=F32)]
            return tuple(out)

        zero = jnp.zeros((T, HD), F32)
        state = blk(i, (jnp.full((T, 1), NEG_BIG, F32), jnp.zeros((T, 1), F32), zero, zero) * HB, True)
        state = lax.fori_loop(0, i, lambda kb, st: blk(kb, st, False), state)
        for hh, cs in enumerate(heads):
            m, l, acc, rem = state[4 * hh:4 * hh + 4]
            o_ref[:, cs] = (acc / l).astype(BF16)
            o32_ref[:, cs] = (acc + rem) / l
            lse_ref[hh] = m + jnp.log(l)

    vec = pl.BlockSpec((HB, T, 1), lambda h, i: (h, i, 0))
    oblk = pl.BlockSpec((T, W), lambda h, i: (i, h))
    return _pc(body, name=name, grid=(NFX // HB, nq),
               in_specs=[pl.BlockSpec((T, W), lambda h, i: (i, NSB // HB + h)),
                         pl.BlockSpec((S, W), lambda h, i: (0, (NH + NSB) // HB + h)),
                         pl.BlockSpec((S, W), lambda h, i: (0, (2 * NH + NSB) // HB + h)),
                         vec, pl.BlockSpec((HB, nq, 1, T), lambda h, i: (h, 0, 0, 0))],
               out_specs=[oblk, oblk, vec],
               out_shape=[SDS((S, NFX * HD), BF16), SDS((S, NFX * HD), F32), SDS((NFX, S, 1), F32)],
               compiler_params=_params(2))(qkv, qkv, qkv, fcol, frow)


def _fox_bwd(qkv, do, o, fcol, frow, lse, NH, NSB, HD, T, name):
    S = qkv.shape[0]
    NFX = NH - NSB
    nq = S // T
    scale = HD ** -0.5
    HB = _heads_per_step(NFX) if NSB % _heads_per_step(NFX) == 0 else 1
    W = HB * HD

    def body(q_ref, k_ref, v_ref, do_ref, o_ref, fq_ref, fk_ref, lse_ref, dq_ref, dk_ref, dv_ref, dfk_ref,
             dk_acc, dv_acc, dfk_acc):
        i = pl.program_id(1)

        @pl.when(i == 0)
        def _():
            dk_acc[...] = jnp.zeros_like(dk_acc)
            dv_acc[...] = jnp.zeros_like(dv_acc)
            dfk_acc[...] = jnp.zeros_like(dfk_acc)

        row = lax.broadcasted_iota(jnp.int32, (T, T), 0)
        col = lax.broadcasted_iota(jnp.int32, (T, T), 1)
        heads = [slice(hh * HD, (hh + 1) * HD) for hh in range(HB)]
        qs = [q_ref[:, cs] for cs in heads]
        dos = [do_ref[:, cs] for cs in heads]
        fqs = [fq_ref[hh] for hh in range(HB)]
        lses = [lse_ref[hh] for hh in range(HB)]
        deltas = [jnp.sum(dos[hh].astype(F32) * o_ref[:, cs], axis=1, keepdims=True) for hh, cs in enumerate(heads)]

        def blk(kb, dqs, diag):
            sl = pl.ds(pl.multiple_of(kb * T, T), T)
            out = []
            for hh, cs in enumerate(heads):
                q, do_ = qs[hh], dos[hh]
                k, v = k_ref[sl, cs], v_ref[sl, cs]
                s = lax.dot_general(q, k, _NT, preferred_element_type=F32) * scale + (fqs[hh] - fk_ref[hh, kb])
                p = jnp.exp(s - lses[hh])
                if diag:
                    p = jnp.where(col <= row, p, 0.0)
                ds = p * (lax.dot_general(do_, v, _NT, preferred_element_type=F32) - deltas[hh])
                dv_acc[sl, cs] += lax.dot_general(p.astype(BF16), do_, _TN, preferred_element_type=F32)
                dsb = (ds * scale).astype(BF16)
                dk_acc[sl, cs] += lax.dot_general(dsb, q, _TN, preferred_element_type=F32)
                dfk_acc[hh, kb] -= jnp.sum(ds, axis=0, keepdims=True)
                out.append(dqs[hh] + jnp.dot(dsb, k, preferred_element_type=F32))
            return tuple(out)

        dqs = lax.fori_loop(0, i, lambda kb, st: blk(kb, st, False), (jnp.zeros((T, HD), F32),) * HB)
        dqs = blk(i, dqs, True)
        for hh, cs in enumerate(heads):
            dq_ref[:, cs] = dqs[hh].astype(BF16)

        @pl.when(i == nq - 1)
        def _():
            dk_ref[...] = dk_acc[...].astype(BF16)
            dv_ref[...] = dv_acc[...].astype(BF16)
            dfk_ref[...] = dfk_acc[...]

    vec = pl.BlockSpec((HB, T, 1), lambda h, i: (h, i, 0))
    rowv = pl.BlockSpec((HB, nq, 1, T), lambda h, i: (h, 0, 0, 0))
    oblk = pl.BlockSpec((T, W), lambda h, i: (i, h))
    qblk = pl.BlockSpec((T, W), lambda h, i: (i, NSB // HB + h))
    full = pl.BlockSpec((S, W), lambda h, i: (0, h))
    return _pc(body, name=name, grid=(NFX // HB, nq),
               in_specs=[qblk, pl.BlockSpec((S, W), lambda h, i: (0, (NH + NSB) // HB + h)),
                         pl.BlockSpec((S, W), lambda h, i: (0, (2 * NH + NSB) // HB + h)),
                         qblk, oblk, vec, rowv, vec],
               out_specs=[oblk, full, full, rowv],
               out_shape=[SDS((S, NFX * HD), BF16)] * 3 + [SDS((NFX, nq, 1, T), F32)],
               scratch_shapes=[pltpu.VMEM((S, W), F32), pltpu.VMEM((S, W), F32), pltpu.VMEM((HB, nq, 1, T), F32)],
               compiler_params=_params(2))(qkv, qkv, qkv, do, o, fcol, frow, lse)


def _silu(c_all, name):
    def body(c_ref, o_ref):
        cv = c_ref[...]
        o_ref[...] = cv * jax.nn.sigmoid(cv)

    return _pc(body, name=name, out_shape=SDS(c_all.shape, F32))(c_all)


def _mod_project(cond, w_mod, name):
    L, D, C = w_mod.shape
    tk = _pick(D, (512, 256, 128))

    def body(c_ref, w_ref, o_ref):
        @pl.when(pl.program_id(1) == 0)
        def _():
            o_ref[...] = jnp.zeros_like(o_ref)

        o_ref[...] += jnp.dot(c_ref[...].astype(BF16), w_ref[...].astype(BF16), preferred_element_type=F32)

    return _pc(body, name=name, grid=(L, D // tk),
               in_specs=[pl.BlockSpec((16, tk), lambda l, k: (0, k)), pl.BlockSpec((None, tk, C), lambda l, k: (l, k, 0))],
               out_specs=pl.BlockSpec((None, 16, C), lambda l, k: (l, 0, 0)),
               out_shape=SDS((L, 16, C), F32), compiler_params=_params(2))(cond, w_mod)


def _adam_math(w, g, m, v):
    m = ADAM_B1 * m + (1.0 - ADAM_B1) * g
    v = ADAM_B2 * v + (1.0 - ADAM_B2) * (g * g)
    m_hat = m / (1.0 - ADAM_B1 ** ADAM_STEP)
    v_hat = v / (1.0 - ADAM_B2 ** ADAM_STEP)
    delta = -ADAM_LR * (m_hat / (jnp.sqrt(v_hat) + ADAM_EPS) + ADAM_WD * w)
    return delta, m, v


def _adamw(w, m, v, parts, layer, prev, name):
    L, R, C = w.shape
    NP = parts.shape[0]
    tr = _pick(R, (128, 64, 88, 32, 16, 8))
    nprev = 0 if prev is None else 4

    def body(w_ref, m_ref, v_ref, p_ref, *rest):
        g_ref, d_ref, mo_ref, vo_ref = rest[nprev:]
        g = p_ref[0].astype(F32)
        for j in range(1, NP):
            g = g + p_ref[j].astype(F32)
        delta, mn, vn = _adam_math(w_ref[...], g, m_ref[...], v_ref[...])
        g_ref[...] = g
        d_ref[...] = delta
        mo_ref[...] = mn
        vo_ref[...] = vn

    if tr == R and R > 512:
        tc = _pick(C, (256, 128))
        steps = C // tc
        blk = pl.BlockSpec((None, R, tc), lambda j: (layer, 0, j))
        in_specs = [blk, blk, blk, pl.BlockSpec((NP, R, tc), lambda j: (0, 0, j))]
    else:
        steps = R // tr
        blk = pl.BlockSpec((None, tr, C), lambda i: (layer, i, 0))
        in_specs = [blk, blk, blk, pl.BlockSpec((NP, tr, C), lambda i: (0, i, 0))]
    operands = [w, m, v, parts]
    aliases = {}
    if prev is not None:
        in_specs += [ANY] * 4
        operands += list(prev)
        aliases = {4 + q: q for q in range(4)}
    return _pc(body, name=name, grid=(steps,), in_specs=in_specs, out_specs=[blk] * 4,
               out_shape=[SDS(w.shape, F32)] * 4, input_output_aliases=aliases,
               compiler_params=_params(1))(*operands)


def _adamw_mod(w, m, v, cond_t, dmod, name):
    L, D, C = w.shape
    tr = _pick(D, (128, 64))

    def body(w_ref, m_ref, v_ref, ct_ref, dm_ref, g_ref, d_ref, mo_ref, vo_ref):
        ct = ct_ref[...]
        g = ct[:, 0:1] * dm_ref[0]
        for b in range(1, NDEV):
            g = g + ct[:, b:b + 1] * dm_ref[b]
        delta, mn, vn = _adam_math(w_ref[...], g, m_ref[...], v_ref[...])
        g_ref[...] = g
        d_ref[...] = delta
        mo_ref[...] = mn
        vo_ref[...] = vn

    blk = pl.BlockSpec((None, tr, C), lambda l, i: (l, i, 0))
    return _pc(body, name=name, grid=(L, D // tr),
               in_specs=[blk, blk, blk, pl.BlockSpec((tr, LANES), lambda l, i: (i, 0)),
                         pl.BlockSpec((NDEV, None, 1, C), lambda l, i: (0, l, 0, 0))],
               out_specs=[blk] * 4, out_shape=[SDS(w.shape, F32)] * 4, compiler_params=_params(2))(w, m, v, cond_t, dmod)


def _sum_parts(parts, name):
    NP, R, C = parts.shape
    tr = _pick(R, (256, 128, 64, 32, 16, 8))

    def body(p_ref, o_ref):
        g = p_ref[0]
        for j in range(1, NP):
            g = g + p_ref[j]
        o_ref[...] = g

    return _pc(body, name=name, grid=(R // tr,), in_specs=[pl.BlockSpec((NP, tr, C), lambda i: (0, i, 0))],
               out_specs=pl.BlockSpec((tr, C), lambda i: (i, 0)), out_shape=SDS((R, C), F32),
               compiler_params=_params(1))(parts)


def _pack(vecs, rows=None):
    flat = jnp.concatenate([v.reshape(-1).astype(F32) for v in vecs])
    n = flat.shape[0]
    r = rows if rows is not None else -(-n // (256 * LANES)) * 256
    return jnp.pad(flat, (0, r * LANES - n)).reshape(r, LANES)


def _unpack(packed, shapes):
    flat = packed.reshape(-1)
    out, off = [], 0
    for s in shapes:
        n = 1
        for d in s:
            n *= d
        out.append(flat[off:off + n].reshape(s))
        off += n
    return out


def kernel(x, c, w_mod, b_mod, norm_gain, w_attn_in, b_forget, w_attn_out, w_pool, pool_scale, w_up, conv_w, conv_b, w_down, final_gain, loss_target, m_w_mod, m_b_mod, m_norm_gain, m_w_attn_in, m_b_forget, m_w_attn_out, m_w_pool, m_pool_scale, m_w_up, m_conv_w, m_conv_b, m_w_down, m_final_gain, v_w_mod, v_b_mod, v_norm_gain, v_w_attn_in, v_b_forget, v_w_attn_out, v_w_pool, v_pool_scale, v_w_up, v_conv_w, v_conv_b, v_w_down, v_final_gain):
    _, S, D = x.shape
    L = w_mod.shape[0]
    CM = w_mod.shape[2]
    NFX = b_forget.shape[1]
    NH = 2 * NFX
    NSB = NH - NFX
    HD = D // NH
    CI = w_attn_in.shape[2]
    CU = w_up.shape[2]
    F2 = NDEV * CU
    DFF = F2 // 2
    G = len(POOL_WINDOWS)
    CG = D // G
    T = _pick(S, (256, 128))
    me = _idx(_me())
    x0 = x[0]
    target = loss_target[0]

    def layer_shards(l, group):
        if group == 1:
            shards = [w_up[l], w_down[l]]
        elif l % 2 == 0:
            shards = [w_attn_in[l // 2].T, w_attn_out[l // 2]]
        else:
            shards = [w_pool[l // 2].reshape(G * (CG // NDEV), CG)]
        return [s.astype(BF16) for s in shards]

    small_shapes = [(1, D), norm_gain.shape, pool_scale.shape, conv_w.shape]
    small_all = _all_gather([_pack([c, norm_gain, pool_scale, conv_w])], "gather_small")[0]
    per_dev = [_unpack(small_all[j], small_shapes) for j in range(NDEV)]
    c_all = jnp.concatenate([p[0] for p in per_dev] + [jnp.zeros((16 - NDEV, D), F32)], axis=0)
    gain_f = jnp.concatenate([p[1] for p in per_dev], axis=2)
    pscale_f = jnp.concatenate([p[2] for p in per_dev], axis=1)
    convw_f = jnp.concatenate([p[3] for p in per_dev], axis=2)

    cond_all = _silu(c_all, "cond_silu")
    mod_part = _mod_project(cond_all, w_mod, "mod_project")
    mod_all = _all_gather([mod_part], "gather_mod")[0]
    mod = lax.dynamic_index_in_dim(mod_all, me, axis=2, keepdims=False)
    mod = mod.transpose(1, 0, 2).reshape(L, NDEV * CM) + b_mod
    mods = mod.reshape(L, 6, 1, D)

    inflight = {}

    def gather_start(l, group, after):
        inflight[l, group], t = _gather_start(layer_shards(l, group), me, after, f"gather{l}_{group}_start")
        return t

    def gather_forward(l, group, after):
        inflight[l, group], t = _gather_forward(inflight[l, group], after, f"gather{l}_{group}_forward")
        return t

    def gather_wait(l, group, after):
        return _gather_wait(inflight.pop((l, group)), after, f"gather{l}_{group}_wait")

    def starts_at(l):
        if l % 2 == 0:
            return [(k, g) for k, g in [(l + 1, 0), (l + 1, 1), (l + 2, 0)] if k < L]
        return [(k, g) for k, g in [(l + 1, 1)] if k < L]

    tok = gather_start(0, 0, [mods])
    tok = gather_start(0, 1, [tok])
    tok = gather_forward(0, 0, [tok])
    mixer_w = gather_wait(0, 0, [tok])

    saved = []
    xl = x0
    for l in range(L):
        i = l // 2
        sh1, sc1, g1, sh2, sc2, g2 = [mods[l, q] for q in range(6)]
        gn1, gn2 = gain_f[l, 0:1], gain_f[l, 1:2]
        st = {"x": xl}
        after = [mixer_w[0]]
        for k, g in starts_at(l):
            after = [gather_start(k, g, after)]
        if starts_at(l):
            sh1 = sh1 + after[0][0, 0]
        if l % 2 == 0:
            win_t = mixer_w[0].reshape(NDEV * CI, D)
            wf_t = jnp.pad(win_t[3 * D:], ((0, LANES - NFX), (0, 0)))
            wout = mixer_w[1].reshape(D, D)
            h1 = _norm_mod(xl, gn1, sc1, sh1, BF16, f"norm1_{l}")
            qkv = _matmul(h1, win_t, mode="nt", name=f"qkv_{l}", out_dtype=BF16, n=3 * D)
            flog = _matmul(h1, wf_t, mode="nt", name=f"flog_{l}", out_dtype=F32, tn=LANES)
            bfp = jnp.pad(b_forget[i], (0, LANES - NFX)).reshape(1, LANES)
            Fc = _forget_cumsum(flog, bfp, f"fcum_{l}")
            f8 = Fc[:, :NFX].T
            fcol, frow = f8[:, :, None], f8.reshape(NFX, S // T, 1, T)
            o_sb, tot = _sb_fwd(qkv, NH, NSB, HD, T, f"sb_fwd_{l}")
            o_fx, o_fx32, lse = _fox_fwd(qkv, fcol, frow, NH, NSB, HD, T, f"fox_fwd_{l}")
            o = jnp.concatenate([o_sb, o_fx], axis=1)
            g1 = g1 + gather_forward(l, 1, [o])[0, 0]
            x1, y1 = _matmul(o, wout, mode="nn", name=f"attn_out_{l}", out_dtype=F32, res=xl, gate=g1, y_dtype=BF16)
            st.update(h1=h1, qkv=qkv, flog=flog, bfp=bfp, fcol=fcol, frow=frow, tot=tot, lse=lse, o=o, o_fx=o_fx32,
                      win_t=win_t, wf_t=wf_t, wout=wout, y1=y1)
        else:
            wpool = mixer_w[0].reshape(NDEV, G, CG // NDEV, CG).transpose(1, 0, 2, 3).reshape(G, CG, CG)
            h1 = _norm_mod(xl, gn1, sc1, sh1, F32, f"norm1_{l}")
            dpool = _pool_diff(h1, f"pool_diff_{l}")
            g1 = g1 + gather_forward(l, 1, [dpool])[0, 0]
            gp = g1 * pscale_f[i:i + 1]
            x1, e1 = _pool_mm(dpool, wpool, xl, gp, f"pool_mm_{l}")
            st.update(dpool=dpool, wpool=wpool, gp=gp, y1=e1)
        wup_g, wdown_g = gather_wait(l, 1, [x1])
        wdown_f = wdown_g.reshape(DFF, D)
        h2 = _norm_mod(x1, gn2, sc2, sh2, BF16, f"norm2_{l}")
        u = _matmul(h2, wup_g, mode="nn", name=f"ffn_up_{l}", out_dtype=F32, b_blocked=True)
        cb = conv_b[l].reshape(1, F2)
        act = _convgate_fwd(u, convw_f[l], cb, f"convgate_{l}")
        if l + 1 < L:
            g2 = g2 + gather_forward(l + 1, 0, [act])[0, 0]
        x2, y2 = _matmul(act, wdown_f, mode="nn", name=f"ffn_down_{l}", out_dtype=F32, res=x1, gate=g2, y_dtype=BF16,
                         tk=CU)
        if l + 1 < L:
            mixer_w = gather_wait(l + 1, 0, [x2])
        st.update(x1=x1, h2=h2, u=u, cb=cb, act=act, y2=y2, wup_g=wup_g, wdown_f=wdown_f,
                  mod=(sh1, sc1, g1, sh2, sc2, g2), gn=(gn1, gn2))
        saved.append(st)
        xl = x2

    dx, d_fgain, loss_tile = _loss_head(xl, target, final_gain.reshape(1, D), "loss_head")
    loss = lax.psum(loss_tile[0, 0], ("x", "y", "c"))

    dmod_rows = [None] * L
    d_gain = [None] * L
    d_convw = [None] * L
    d_convb = [None] * L
    d_pscale = [None] * (L // 2)
    d_bf = [None] * ((L + 1) // 2)
    big = {"w_up": None, "w_down": None, "w_attn_out": None, "w_pool": None}
    attn_in_t = [None] * ((L + 1) // 2)

    def update(key, w, m, v, bufs, layer, tag):
        big[key] = _adamw(w, m, v, bufs, layer, big[key], f"adamw_{tag}")

    exchanges = []

    def exchange_start(l, keys, parts, after, two_level=False):
        tag = f"{l}_{'_'.join(keys)}"
        if two_level:
            state, t = _pair_start(parts, after, f"pair{tag}_start")
        else:
            state, t = _scatter_start(parts, me, after, f"scatter{tag}_start")
        exchanges.append([l, keys, tag, "pair" if two_level else "direct", state])
        return t

    def exchanges_advance(after):
        t = None
        for entry in [e for e in exchanges if e[3] == "pair"]:
            tag = entry[2]
            mine, theirs = _pair_wait(entry[4], after, f"pair{tag}_wait")
            sums = [_pair_sum(a, b, f"pair{tag}_sum{q}") for q, (a, b) in enumerate(zip(mine, theirs))]
            entry[4], t = _chips_start(sums, [], f"chips{tag}_start")
            entry[3] = "chips"
        return t

    def exchanges_finish(after, first_layer):
        for entry in [e for e in exchanges if e[0] >= first_layer]:
            exchanges.remove(entry)
            pl_, keys, tag, stage, state = entry
            if stage == "chips":
                bufs = _chips_wait(state, after, f"chips{tag}_wait")
            else:
                bufs = _scatter_wait(state, after, f"scatter{tag}_wait")
            for key, buf in zip(keys, bufs):
                if key == "up":
                    update("w_up", w_up, m_w_up, v_w_up, buf, pl_, f"up_{pl_}")
                elif key == "down":
                    update("w_down", w_down, m_w_down, v_w_down, buf, pl_, f"down_{pl_}")
                elif key == "out":
                    update("w_attn_out", w_attn_out, m_w_attn_out, v_w_attn_out, buf, pl_ // 2, f"attn_out_{pl_}")
                elif key == "in":
                    slab = lambda a: a[pl_ // 2].T[None]
                    attn_in_t[pl_ // 2] = _adamw(slab(w_attn_in), slab(m_w_attn_in), slab(v_w_attn_in), buf, 0, None,
                                                 f"adamw_attn_in_{pl_}")
                else:
                    wp3 = lambda a: a.reshape(a.shape[0], G * (CG // NDEV), CG)
                    update("w_pool", wp3(w_pool), wp3(m_w_pool), wp3(v_w_pool), buf, pl_ // 2, f"pool_{pl_}")

    tok = None
    for l in reversed(range(L)):
        i = l // 2
        st = saved[l]
        sh1, sc1, g1, sh2, sc2, g2 = st["mod"]
        if tok is not None:
            g2 = g2 + tok[0, 0]
        gn1, gn2 = st["gn"]
        dffn, dg2 = _gate_bwd(dx, st["y2"], g2, f"gate2_bwd_{l}")
        dact = _matmul(dffn, st["wdown_f"], mode="nt", name=f"ffn_down_dx_{l}", out_dtype=F32, tn=CU)
        dwdown = _matmul(st["act"], dffn, mode="tn", name=f"ffn_down_dw_{l}", out_dtype=BF16, tm=CU)
        dwdown = dwdown.reshape(NDEV, DFF // NDEV, D)
        du, dcw, dcb = _convgate_bwd(st["u"], dact, convw_f[l], st["cb"], f"convgate_bwd_{l}")
        dh2 = _matmul(du, st["wup_g"], mode="nt", name=f"ffn_up_dx_{l}", out_dtype=F32, a_split=True, b_blocked=True,
                      tn=2048)
        dwup = _matmul(st["h2"], du, mode="tn", name=f"ffn_up_dw_{l}", out_dtype=BF16, tn=CU, b_split=True,
                       out_blocked=True)
        dx, dgn2, dsc2, dsh2 = _norm_mod_bwd(dh2, st["x1"], dx, gn2, sc2, f"norm2_bwd_{l}")
        d_convw[l] = jnp.concatenate([dcw[0], dcw[1]], axis=1)
        d_convb[l] = jnp.concatenate([dcb[0], dcb[1]], axis=1)
        two_level = l < 2
        tok = exchange_start(l, ["up", "down"], [dwup, dwdown], [], two_level=two_level)
        if l % 2 == 0:
            dy1, dg1 = _gate_bwd(dx, st["y1"], g1 + tok[0, 0], f"gate1_bwd_{l}")
            do = _matmul(dy1, st["wout"], mode="nt", name=f"attn_out_dx_{l}", out_dtype=BF16)
            dwout = _matmul(st["o"], dy1, mode="tn", name=f"attn_out_dw_{l}", out_dtype=BF16).reshape(NDEV, D // NDEV, D)
            tot = st["tot"]
            if l == 0:
                tot = tot + exchange_start(l, ["out"], [dwout], [])[0, 0]
            dq_s, dk_s, dv_s = _sb_bwd(st["qkv"], do, tot, NH, NSB, HD, T, f"sb_bwd_{l}")
            lse = st["lse"] + exchanges_advance([dq_s])[0, 0] if two_level else st["lse"]
            dq_f, dk_f, dv_f, dfk = _fox_bwd(st["qkv"], do, st["o_fx"], st["fcol"], st["frow"], lse, NH, NSB, HD, T,
                                             f"fox_bwd_{l}")
            dqkv = jnp.concatenate([dq_s, dq_f, dk_s, dk_f, dv_s, dv_f], axis=1)
            dF = jnp.pad(dfk.reshape(NFX, S).T, ((0, 0), (0, LANES - NFX)))
            dflog, dbf = _forget_cumsum_bwd(dF, st["flog"], st["bfp"], f"fcum_bwd_{l}")
            dflog_b = dflog.astype(BF16)
            dh1 = _matmul(dqkv, st["win_t"], mode="nn", name=f"qkv_dx_{l}", out_dtype=F32)
            dh1 = _matmul(dflog_b, st["wf_t"], mode="nn", name=f"flog_dx_{l}", out_dtype=F32, res=dh1,
                          gate=jnp.ones((1, D), F32))
            dwqkv_t = _matmul(dqkv, st["h1"], mode="tn", name=f"qkv_dw_{l}", out_dtype=BF16)
            dwf_t = _matmul(dflog_b, st["h1"], mode="tn", name=f"flog_dw_{l}", out_dtype=BF16, tm=LANES)
            dwin_t = jnp.concatenate([dwqkv_t, dwf_t[:NFX]], axis=0).reshape(NDEV, CI, D)
            d_bf[i] = dbf[0, :NFX]
            keys, parts = (["in"], [dwin_t]) if l == 0 else (["in", "out"], [dwin_t, dwout])
        else:
            de, dgp = _gate_bwd(dx, st["y1"], st["gp"] + tok[0, 0], f"gate1_bwd_{l}")
            dg1 = dgp * pscale_f[i:i + 1]
            d_pscale[i] = dgp * g1
            dd, dwp = _pool_mm_bwd(de, st["dpool"], st["wpool"], f"pool_mm_bwd_{l}")
            if two_level:
                gn1 = gn1 + exchanges_advance([dd])[0, 0]
            dh1 = _pool_diff_bwd(dd, f"pool_diff_bwd_{l}")
            keys = ["pool"]
            parts = [dwp.reshape(G, NDEV, CG // NDEV, CG).transpose(1, 0, 2, 3).reshape(NDEV, G * (CG // NDEV), CG)]
        dx, dgn1, dsc1, dsh1 = _norm_mod_bwd(dh1, st["x"], dx, gn1, sc1, f"norm1_bwd_{l}")
        dmod_rows[l] = jnp.concatenate([dsh1, dsc1, dg1, dsh2, dsc2, dg2], axis=1)
        d_gain[l] = jnp.concatenate([dgn1, dgn2], axis=0)
        if l > 0:
            tok = exchange_start(l, keys, parts, [])
            exchanges_finish([dx, tok], l + 1)

    grad_x = dx[None]

    small_grads = [jnp.stack(dmod_rows), jnp.stack(d_gain), jnp.stack(d_pscale), jnp.stack(d_convw),
                   jnp.stack(d_convb), jnp.stack(d_bf), d_fgain]
    sg_shapes = [(L, 6 * D), (L, 2, D), (L // 2, D), (L, 3, F2), (L, F2), ((L + 1) // 2, NFX), (D,)]
    sg_all = _all_gather([_pack(small_grads)], "gather_small_grads")[0]
    tok = exchange_start(0, keys, parts, [sg_all])
    exchanges_finish([tok], 1)
    sg_all = sg_all + tok[0, 0]
    sg_sum = _unpack(_sum_parts(sg_all, "sum_small_grads"), sg_shapes)
    g_bmod, g_gain_f, g_pscale_f, g_convw_f, g_convb, g_bf, g_fgain = sg_sum
    shard = lambda a, n, axis: lax.dynamic_slice_in_dim(a, me * n, n, axis=axis)
    g_small = [g_bmod, shard(g_gain_f, D // NDEV, 2), shard(g_pscale_f, D // NDEV, 1), shard(g_convw_f, CU, 2), g_convb,
               g_bf, g_fgain]
    w_small = [b_mod, norm_gain, pool_scale, conv_w, conv_b, b_forget, final_gain]
    m_small = [m_b_mod, m_norm_gain, m_pool_scale, m_conv_w, m_conv_b, m_b_forget, m_final_gain]
    v_small = [v_b_mod, v_norm_gain, v_pool_scale, v_conv_w, v_conv_b, v_b_forget, v_final_gain]
    small_raw = _adamw(_pack(w_small)[None], _pack(m_small)[None], _pack(v_small)[None], _pack(g_small)[None], 0, None,
                       "adamw_small")
    small_out = [_unpack(a[0], [w.shape for w in w_small]) for a in small_raw]

    dmod_all = jnp.stack([_unpack(sg_all[j], sg_shapes[:1])[0] for j in range(NDEV)])
    dmod_mine = lax.dynamic_slice_in_dim(dmod_all.reshape(NDEV, L, NDEV, CM), me, 1, axis=2)
    cond_t = jnp.pad(cond_all[:NDEV].T, ((0, 0), (0, LANES - NDEV)))
    mod_out = _adamw_mod(w_mod, m_w_mod, v_w_mod, cond_t, dmod_mine, "adamw_mod")
    exchanges_finish([mod_out[0], small_raw[0]], 0)

    pool4 = lambda a: a.reshape(w_pool.shape)
    names = ["w_mod", "b_mod", "norm_gain", "w_attn_in", "b_forget", "w_attn_out", "w_pool", "pool_scale", "w_up", "conv_w",
             "conv_b", "w_down", "final_gain"]
    small_pos = {"b_mod": 0, "norm_gain": 1, "pool_scale": 2, "conv_w": 3, "conv_b": 4, "b_forget": 5, "final_gain": 6}
    outs = []
    for kind in range(4):
        for nm in names:
            if nm == "w_mod":
                outs.append(mod_out[kind])
            elif nm in small_pos:
                outs.append(small_out[kind][small_pos[nm]])
            elif nm == "w_pool":
                outs.append(pool4(big[nm][kind]))
            elif nm == "w_attn_in":
                outs.append(jnp.stack([per_layer[kind][0] for per_layer in attn_in_t]).transpose(0, 2, 1))
            else:
                outs.append(big[nm][kind])
    return (loss, grad_x, *outs)
```

```python
import jax
import jax.numpy as jnp
from jax import lax
from jax.experimental import pallas as pl
from jax.experimental.pallas import tpu as pltpu

NDEV = 8
F32 = jnp.float32
BF16 = jnp.bfloat16
MESH = pl.DeviceIdType.MESH
VMEM_LIMIT_BYTES = 56 * 1024 * 1024
LANES = 128
POOL_WINDOWS = (2, 4, 8, 16)
EPS = 1e-6
ADAM_LR = 0.001
ADAM_B1 = 0.9
ADAM_B2 = 0.999
ADAM_EPS = 1e-08
ADAM_WD = 0.01
ADAM_STEP = 10
NEG_BIG = -1e30
SDS = jax.ShapeDtypeStruct
ANY = pl.BlockSpec(memory_space=pl.ANY)


def _pc(body, **kw):
    return pl.pallas_call(body, **kw)


def _params(n_axes):
    return pltpu.CompilerParams(dimension_semantics=("arbitrary",) * n_axes, vmem_limit_bytes=VMEM_LIMIT_BYTES)


def _pick(n, prefs):
    for p in prefs:
        if p <= n and n % p == 0:
            return p
    return n


def _idx(p):
    return 4 * p[0] + 2 * p[1] + p[2]


def _me():
    return lax.axis_index("x"), lax.axis_index("y"), lax.axis_index("c")


def _all_gather(arrs, name):
    n = len(arrs)

    def body(*refs):
        ins, outs = refs[:n], refs[n:2 * n]
        send_sems, recv_sems, local_sems = refs[2 * n:]
        x, y, c = _me()
        me, sib = (x, y, c), (x, y, 1 - c)
        chips = [(1 - x, y), (x, 1 - y), (1 - x, 1 - y)]

        def copy(t, k, block, to, src=None):
            dst = outs[t].at[_idx(block)]
            return pltpu.make_async_remote_copy(
                src_ref=dst if src is None else src, dst_ref=dst,
                send_sem=send_sems.at[7 * t + k], recv_sem=recv_sems.at[7 * t + k],
                device_id=to, device_id_type=MESH)

        mine = [pltpu.make_async_copy(ins[t], outs[t].at[_idx(me)], local_sems.at[t]) for t in range(n)]
        for cp in mine:
            cp.start()
        first = []
        for t in range(n):
            first.append(copy(t, 0, me, sib, src=ins[t]))
            for j, chip in enumerate(chips):
                first.append(copy(t, 1 + j, me, (*chip, c), src=ins[t]))
        for cp in first:
            cp.start()
        passed = []
        for t in range(n):
            for j, chip in enumerate(chips):
                copy(t, 1 + j, (*chip, c), me).wait_recv()
                cp = copy(t, 4 + j, (*chip, c), sib)
                cp.start()
                passed.append(cp)
        for t in range(n):
            copy(t, 0, sib, me).wait_recv()
            for j, chip in enumerate(chips):
                copy(t, 4 + j, (*chip, 1 - c), me).wait_recv()
        for cp in first + passed:
            cp.wait_send()
        for cp in mine:
            cp.wait()

    return _pc(
        body, name=name,
        out_shape=[SDS((NDEV,) + a.shape, a.dtype) for a in arrs],
        in_specs=[ANY] * n, out_specs=[ANY] * n,
        scratch_shapes=[pltpu.SemaphoreType.DMA((7 * n,)), pltpu.SemaphoreType.DMA((7 * n,)),
                        pltpu.SemaphoreType.DMA((n,))],
    )(*arrs)


HBM = pl.BlockSpec(memory_space=pltpu.HBM)
SEM = pl.BlockSpec(memory_space=pltpu.SEMAPHORE)
EFFECT = pltpu.SideEffectType.DATAFLOW_SIDE_EFFECTING
TOKEN = SDS((8, LANES), F32)


def _hbm(a):
    return pltpu.with_memory_space_constraint(a, pltpu.HBM)


def _landing(block):
    return lax.empty((NDEV,) + block.shape, block.dtype)


def _split_call(body, name, n_thru, thru, sems_in, after, sems_out):
    n_sem = len(sems_out)
    operands = [_hbm(a) for a in thru] + list(sems_in) + list(after)
    in_specs = [HBM] * n_thru + [SEM] * len(sems_in) + [ANY] * len(after)
    out_shape = [pltpu.SemaphoreType.DMA((k,)) for k in sems_out] + [pltpu.HBM(a.shape, a.dtype) for a in thru] + [TOKEN]
    out_specs = [SEM] * n_sem + [HBM] * n_thru + [pl.BlockSpec(memory_space=pltpu.VMEM)]
    outs = _pc(body, name=name, in_specs=in_specs, out_specs=out_specs, out_shape=out_shape,
               input_output_aliases={q: n_sem + q for q in range(n_thru)},
               compiler_params=pltpu.CompilerParams(has_side_effects=EFFECT))(*operands)
    return list(outs[:n_sem]), list(outs[n_sem:n_sem + n_thru]), outs[-1]


def _gather_start(shards, me, after, name):
    n = len(shards)
    lands = [_landing(s) for s in shards]

    def body(*refs):
        shard_refs, land_refs = refs[:n], refs[n:2 * n]
        send_sems, recv_sems, local_sems = refs[2 * n + len(after):2 * n + len(after) + 3]
        x, y, c = _me()
        me_i = _idx((x, y, c))
        peers = [(x, y, 1 - c), (1 - x, y, c), (x, 1 - y, c), (1 - x, 1 - y, c)]
        for t in range(n):
            pltpu.make_async_copy(shard_refs[t], land_refs[t].at[me_i], local_sems.at[t]).start()
            for k, p in enumerate(peers):
                pltpu.make_async_remote_copy(
                    src_ref=shard_refs[t], dst_ref=land_refs[t].at[me_i], send_sem=send_sems.at[4 * t + k],
                    recv_sem=recv_sems.at[4 * t + k], device_id=p, device_id_type=MESH).start()
        refs[-1][...] = jnp.zeros((8, LANES), F32)

    sems, thru, token = _split_call(body, name, 2 * n, list(shards) + lands, [], after, [4 * n, 4 * n, n])
    return dict(n=n, sems=sems, shards=thru[:n], lands=thru[n:]), token


def _gather_forward(st, after, name):
    n = st["n"]

    def body(*refs):
        shard_refs, land_refs = refs[:n], refs[n:2 * n]
        send1, recv1, local1 = refs[2 * n:2 * n + 3]
        send2, recv2 = refs[2 * n + 3 + len(after)], refs[2 * n + 4 + len(after)]
        x, y, c = _me()
        sib = (x, y, 1 - c)
        senders = [sib, (1 - x, y, c), (x, 1 - y, c), (1 - x, 1 - y, c)]
        for t in range(n):
            pltpu.make_async_copy(shard_refs[t], land_refs[t].at[_idx((x, y, c))], local1.at[t]).wait()
            for k, p in enumerate(senders):
                cp = pltpu.make_async_remote_copy(
                    src_ref=shard_refs[t], dst_ref=land_refs[t].at[_idx(p)], send_sem=send1.at[4 * t + k],
                    recv_sem=recv1.at[4 * t + k], device_id=p, device_id_type=MESH)
                cp.wait_send()
                cp.wait_recv()
        for t in range(n):
            for j, p in enumerate(senders[1:]):
                slab = land_refs[t].at[_idx(p)]
                pltpu.make_async_remote_copy(
                    src_ref=slab, dst_ref=slab, send_sem=send2.at[3 * t + j], recv_sem=recv2.at[3 * t + j],
                    device_id=sib, device_id_type=MESH).start()
        refs[-1][...] = jnp.zeros((8, LANES), F32)

    sems, thru, token = _split_call(body, name, 2 * n, st["shards"] + st["lands"], st["sems"], after, [3 * n, 3 * n])
    return dict(n=n, sems=sems, shards=thru[:n], lands=thru[n:]), token


def _gather_wait(st, after, name):
    n = st["n"]

    def body(*refs):
        land_refs = refs[n:2 * n]
        send2, recv2 = refs[2 * n], refs[2 * n + 1]
        x, y, c = _me()
        sib = (x, y, 1 - c)
        for t in range(n):
            for j, chip in enumerate([(1 - x, y), (x, 1 - y), (1 - x, 1 - y)]):
                sent, got = land_refs[t].at[_idx((*chip, c))], land_refs[t].at[_idx((*chip, 1 - c))]
                cp = pltpu.make_async_remote_copy(
                    src_ref=sent, dst_ref=got, send_sem=send2.at[3 * t + j], recv_sem=recv2.at[3 * t + j],
                    device_id=sib, device_id_type=MESH)
                cp.wait_send()
                cp.wait_recv()
        refs[-1][...] = jnp.zeros((8, LANES), F32)

    _, thru, _ = _split_call(body, name, 2 * n, st["shards"] + st["lands"], st["sems"], after, [])
    return thru[n:]


def _scatter_start(parts, me, after, name):
    n = len(parts)
    lands = [_landing(p[0]) for p in parts]

    def body(*refs):
        part_refs, land_refs = refs[:n], refs[n:2 * n]
        send_sems, recv_sems, local_sems = refs[2 * n + len(after):2 * n + len(after) + 3]
        x, y, c = _me()
        me_i = _idx((x, y, c))
        for t in range(n):
            pltpu.make_async_copy(part_refs[t].at[me_i], land_refs[t].at[me_i], local_sems.at[t]).start()
            for r in range(1, NDEV):
                p = (1 - x if r & 4 else x, 1 - y if r & 2 else y, 1 - c if r & 1 else c)
                pltpu.make_async_remote_copy(
                    src_ref=part_refs[t].at[_idx(p)], dst_ref=land_refs[t].at[me_i], send_sem=send_sems.at[7 * t + r - 1],
                    recv_sem=recv_sems.at[7 * t + r - 1], device_id=p, device_id_type=MESH).start()
        refs[-1][...] = jnp.zeros((8, LANES), F32)

    sems, thru, token = _split_call(body, name, 2 * n, list(parts) + lands, [], after, [7 * n, 7 * n, n])
    return dict(n=n, sems=sems, parts=thru[:n], lands=thru[n:]), token


def _scatter_wait(st, after, name):
    n = st["n"]

    def body(*refs):
        part_refs, land_refs = refs[:n], refs[n:2 * n]
        send_sems, recv_sems, local_sems = refs[2 * n:2 * n + 3]
        x, y, c = _me()
        me_i = _idx((x, y, c))
        for t in range(n):
            pltpu.make_async_copy(part_refs[t].at[me_i], land_refs[t].at[me_i], local_sems.at[t]).wait()
            for r in range(1, NDEV):
                p = (1 - x if r & 4 else x, 1 - y if r & 2 else y, 1 - c if r & 1 else c)
                cp = pltpu.make_async_remote_copy(
                    src_ref=part_refs[t].at[_idx(p)], dst_ref=land_refs[t].at[_idx(p)], send_sem=send_sems.at[7 * t + r - 1],
                    recv_sem=recv_sems.at[7 * t + r - 1], device_id=p, device_id_type=MESH)
                cp.wait_send()
                cp.wait_recv()
        refs[-1][...] = jnp.zeros((8, LANES), F32)

    _, thru, _ = _split_call(body, name, 2 * n, st["parts"] + st["lands"], st["sems"], after, [])
    return thru[n:]


def _pair_start(parts, after, name):
    n = len(parts)
    recvs = [lax.empty((4,) + p.shape[1:], p.dtype) for p in parts]

    def body(*refs):
        part_refs, recv_refs = refs[:n], refs[n:2 * n]
        send_sems, recv_sems = refs[2 * n + len(after)], refs[2 * n + len(after) + 1]
        x, y, c = _me()
        for t in range(n):
            for q in range(4):
                pltpu.make_async_remote_copy(
                    src_ref=part_refs[t].at[2 * q + (1 - c)], dst_ref=recv_refs[t].at[q], send_sem=send_sems.at[4 * t + q],
                    recv_sem=recv_sems.at[4 * t + q], device_id=(x, y, 1 - c), device_id_type=MESH).start()
        refs[-1][...] = jnp.zeros((8, LANES), F32)

    sems, thru, token = _split_call(body, name, 2 * n, list(parts) + recvs, [], after, [4 * n, 4 * n])
    return dict(n=n, sems=sems, parts=thru[:n], recvs=thru[n:]), token


def _pair_wait(st, after, name):
    n = st["n"]

    def body(*refs):
        part_refs, recv_refs = refs[:n], refs[n:2 * n]
        send_sems, recv_sems = refs[2 * n], refs[2 * n + 1]
        x, y, c = _me()
        for t in range(n):
            for q in range(4):
                cp = pltpu.make_async_remote_copy(
                    src_ref=part_refs[t].at[2 * q + (1 - c)], dst_ref=recv_refs[t].at[q], send_sem=send_sems.at[4 * t + q],
                    recv_sem=recv_sems.at[4 * t + q], device_id=(x, y, 1 - c), device_id_type=MESH)
                cp.wait_send()
                cp.wait_recv()
        refs[-1][...] = jnp.zeros((8, LANES), F32)

    _, thru, _ = _split_call(body, name, 2 * n, st["parts"] + st["recvs"], st["sems"], after, [])
    return thru[:n], thru[n:]


def _pair_sum(part, recv, name):
    _, R, C = recv.shape
    tr = _pick(R, (256, 128, 64, 88, 32, 16, 8))

    def body(core_ref, a_ref, b_ref, o_ref):
        o_ref[...] = (a_ref[...].astype(F32) + b_ref[...].astype(F32)).astype(o_ref.dtype)

    grid_spec = pltpu.PrefetchScalarGridSpec(
        num_scalar_prefetch=1, grid=(4, R // tr),
        in_specs=[pl.BlockSpec((None, None, tr, C), lambda q, i, core: (q, core[0], i, 0)),
                  pl.BlockSpec((None, tr, C), lambda q, i, core: (q, i, 0))],
        out_specs=pl.BlockSpec((None, tr, C), lambda q, i, core: (q, i, 0)))
    core = lax.axis_index("c").astype(jnp.int32).reshape(1)
    return _pc(body, name=name, grid_spec=grid_spec, out_shape=SDS(recv.shape, recv.dtype),
               compiler_params=_params(2))(core, part.reshape((4, 2) + part.shape[1:]), recv)


def _chips_start(sums, after, name):
    n = len(sums)
    lands = [lax.empty(s.shape, s.dtype) for s in sums]

    def body(*refs):
        sum_refs, land_refs = refs[:n], refs[n:2 * n]
        send_sems, recv_sems, local_sems = refs[2 * n + len(after):2 * n + len(after) + 3]
        x, y, c = _me()
        my_chip = 2 * x + y
        for t in range(n):
            pltpu.make_async_copy(sum_refs[t].at[my_chip], land_refs[t].at[my_chip], local_sems.at[t]).start()
            for j, (px, py) in enumerate([(1 - x, y), (x, 1 - y), (1 - x, 1 - y)]):
                pltpu.make_async_remote_copy(
                    src_ref=sum_refs[t].at[2 * px + py], dst_ref=land_refs[t].at[my_chip], send_sem=send_sems.at[3 * t + j],
                    recv_sem=recv_sems.at[3 * t + j], device_id=(px, py, c), device_id_type=MESH).start()
        refs[-1][...] = jnp.zeros((8, LANES), F32)

    sems, thru, token = _split_call(body, name, 2 * n, list(sums) + lands, [], after, [3 * n, 3 * n, n])
    return dict(n=n, sems=sems, sums=thru[:n], lands=thru[n:]), token


def _chips_wait(st, after, name):
    n = st["n"]

    def body(*refs):
        sum_refs, land_refs = refs[:n], refs[n:2 * n]
        send_sems, recv_sems, local_sems = refs[2 * n:2 * n + 3]
        x, y, c = _me()
        my_chip = 2 * x + y
        for t in range(n):
            pltpu.make_async_copy(sum_refs[t].at[my_chip], land_refs[t].at[my_chip], local_sems.at[t]).wait()
            for j, (px, py) in enumerate([(1 - x, y), (x, 1 - y), (1 - x, 1 - y)]):
                cp = pltpu.make_async_remote_copy(
                    src_ref=sum_refs[t].at[2 * px + py], dst_ref=land_refs[t].at[2 * px + py], send_sem=send_sems.at[3 * t + j],
                    recv_sem=recv_sems.at[3 * t + j], device_id=(px, py, c), device_id_type=MESH)
                cp.wait_send()
                cp.wait_recv()
        refs[-1][...] = jnp.zeros((8, LANES), F32)

    _, thru, _ = _split_call(body, name, 2 * n, st["sums"] + st["lands"], st["sems"], after, [])
    return thru[n:]


def _matmul(a, b, *, mode, name, out_dtype, tm=1024, tn=1024, tk=2048, b_blocked=False, out_blocked=False,
            a_split=False, b_split=False, res=None, gate=None, y_dtype=None, n=None):
    if mode == "tn":
        K, M = (a.shape[0], a.shape[1]) if not a_split else (a.shape[1], 2 * a.shape[2])
    else:
        M, K = (a.shape[0], a.shape[1]) if not a_split else (a.shape[1], 2 * a.shape[2])
    if b_blocked:
        if mode == "nn":
            N, tn = b.shape[0] * b.shape[2], b.shape[2]
        else:
            N, tk = b.shape[1], b.shape[2]
    elif b_split:
        N = 2 * b.shape[2]
    else:
        N = b.shape[0] if mode == "nt" else b.shape[1]
    if n is not None:
        N = n
    tm = _pick(M, (tm, 704, 512, 384, 256, 128))
    if not (b_blocked and mode == "nn"):
        tn = _pick(N, (tn, 1024, 768, 512, 384, 256, 128))
    if not (b_blocked and mode == "nt"):
        tk = _pick(K, (tk, 1024, 512, 384, 256, 128))
    nm, nn_, nk = M // tm, N // tn, K // tk

    if mode == "tn":
        a_spec = pl.BlockSpec((tk, tm), lambda i, j, k: (k, i))
        dims = (((0,), (0,)), ((), ()))
    elif a_split:
        per = a.shape[2] // tk
        a_spec = pl.BlockSpec((None, tm, tk), lambda i, j, k: (k // per, i, k % per))
    else:
        a_spec = pl.BlockSpec((tm, tk), lambda i, j, k: (i, k))
    if mode == "nn":
        dims = (((1,), (0,)), ((), ()))
        if b_blocked:
            b_spec = pl.BlockSpec((None, tk, tn), lambda i, j, k: (j, k, 0))
        else:
            b_spec = pl.BlockSpec((tk, tn), lambda i, j, k: (k, j))
    elif mode == "nt":
        dims = (((1,), (1,)), ((), ()))
        if b_blocked:
            b_spec = pl.BlockSpec((None, tn, tk), lambda i, j, k: (k, j, 0))
        else:
            b_spec = pl.BlockSpec((tn, tk), lambda i, j, k: (j, k))
    else:
        if b_split:
            per_b = b.shape[2] // tn
            b_spec = pl.BlockSpec((None, tk, tn), lambda i, j, k: (j // per_b, k, j % per_b))
        else:
            b_spec = pl.BlockSpec((tk, tn), lambda i, j, k: (k, j))
    if out_blocked:
        o_spec = pl.BlockSpec((None, tm, tn), lambda i, j, k: (j, i, 0))
        o_shape = SDS((nn_, M, tn), out_dtype)
    else:
        o_spec = pl.BlockSpec((tm, tn), lambda i, j, k: (i, j))
        o_shape = SDS((M, N), out_dtype)
    fused = res is not None
    in_specs, operands = [a_spec, b_spec], [a, b]
    out_specs, out_shapes = [o_spec], [o_shape]
    if fused:
        in_specs += [pl.BlockSpec((tm, tn), lambda i, j, k: (i, j)), pl.BlockSpec((1, tn), lambda i, j, k: (0, j))]
        operands += [res, gate]
        if y_dtype is not None:
            out_specs.append(pl.BlockSpec((tm, tn), lambda i, j, k: (i, j)))
            out_shapes.append(SDS((M, N), y_dtype))

    def body(*refs):
        a_ref, b_ref = refs[0], refs[1]
        acc_ref = refs[-1]
        k = pl.program_id(2)

        def product():
            return lax.dot_general(a_ref[...], b_ref[...], dims, preferred_element_type=F32)

        def finish(acc):
            if fused:
                res_ref, gate_ref, o_ref = refs[2], refs[3], refs[4]
                o_ref[...] = (res_ref[...] + gate_ref[...] * acc).astype(o_ref.dtype)
                if y_dtype is not None:
                    refs[5][...] = acc.astype(y_dtype)
            else:
                refs[2][...] = acc.astype(refs[2].dtype)

        if nk == 1:
            finish(product())
        else:
            @pl.when(k == 0)
            def _():
                acc_ref[...] = product()

            @pl.when(jnp.logical_and(k > 0, k < nk - 1))
            def _():
                acc_ref[...] += product()

            @pl.when(k == nk - 1)
            def _():
                finish(acc_ref[...] + product())

    outs = _pc(body, name=name, grid=(nm, nn_, nk), in_specs=in_specs, out_specs=out_specs, out_shape=out_shapes,
               scratch_shapes=[pltpu.VMEM((tm, tn), F32)], compiler_params=_params(3))(*operands)
    return outs[0] if len(outs) == 1 else tuple(outs)


def _norm_mod(x, gain, sc, sh, out_dtype, name):
    S, D = x.shape
    tr = _pick(S, (256, 128))

    def body(x_ref, g_ref, sc_ref, sh_ref, o_ref):
        xv = x_ref[...]
        r = lax.rsqrt(jnp.mean(xv * xv, axis=-1, keepdims=True) + EPS)
        n = (xv * r) * g_ref[...]
        o_ref[...] = (n * (1.0 + sc_ref[...]) + sh_ref[...]).astype(o_ref.dtype)

    row = pl.BlockSpec((tr, D), lambda i: (i, 0))
    vec = pl.BlockSpec((1, D), lambda i: (0, 0))
    return _pc(body, name=name, grid=(S // tr,), in_specs=[row, vec, vec, vec], out_specs=row,
               out_shape=SDS((S, D), out_dtype), compiler_params=_params(1))(x, gain, sc, sh)


def _norm_mod_bwd(dh, x, dxres, gain, sc, name):
    S, D = x.shape
    tr = _pick(S, (256, 128))

    def body(dh_ref, x_ref, dxres_ref, g_ref, sc_ref, dx_ref, dgain_ref, dsc_ref, dsh_ref):
        @pl.when(pl.program_id(0) == 0)
        def _():
            dgain_ref[...] = jnp.zeros_like(dgain_ref)
            dsc_ref[...] = jnp.zeros_like(dsc_ref)
            dsh_ref[...] = jnp.zeros_like(dsh_ref)

        xv = x_ref[...]
        dh = dh_ref[...].astype(F32)
        r = lax.rsqrt(jnp.mean(xv * xv, axis=-1, keepdims=True) + EPS)
        nh = xv * r
        gn = g_ref[...]
        dn = dh * (1.0 + sc_ref[...])
        dgain_ref[...] += jnp.sum(dn * nh, axis=0, keepdims=True)
        dsc_ref[...] += jnp.sum(dh * (nh * gn), axis=0, keepdims=True)
        dsh_ref[...] += jnp.sum(dh, axis=0, keepdims=True)
        dnh = dn * gn
        dx = r * (dnh - nh * jnp.mean(dnh * nh, axis=-1, keepdims=True))
        dx_ref[...] = dxres_ref[...] + dx

    row = pl.BlockSpec((tr, D), lambda i: (i, 0))
    vec = pl.BlockSpec((1, D), lambda i: (0, 0))
    return _pc(body, name=name, grid=(S // tr,), in_specs=[row, row, row, vec, vec], out_specs=[row, vec, vec, vec],
               out_shape=[SDS((S, D), F32), SDS((1, D), F32), SDS((1, D), F32), SDS((1, D), F32)],
               compiler_params=_params(1))(dh, x, dxres, gain, sc)


def _gate_bwd(dx, y, gate, name):
    S, D = dx.shape
    tr = _pick(S, (256, 128))

    def body(dx_ref, y_ref, gate_ref, dy_ref, dgate_ref):
        @pl.when(pl.program_id(0) == 0)
        def _():
            dgate_ref[...] = jnp.zeros_like(dgate_ref)

        dxv = dx_ref[...]
        dgate_ref[...] += jnp.sum(dxv * y_ref[...].astype(F32), axis=0, keepdims=True)
        dy_ref[...] = (dxv * gate_ref[...]).astype(BF16)

    row = pl.BlockSpec((tr, D), lambda i: (i, 0))
    vec = pl.BlockSpec((1, D), lambda i: (0, 0))
    return _pc(body, name=name, grid=(S // tr,), in_specs=[row, row, vec], out_specs=[row, vec],
               out_shape=[SDS((S, D), BF16), SDS((1, D), F32)], compiler_params=_params(1))(dx, y, gate)


def _loss_head(x, target, fgain, name):
    S, D = x.shape
    tr = _pick(S, (256, 128))

    def body(x_ref, t_ref, fg_ref, dx_ref, dfg_ref, loss_ref):
        @pl.when(pl.program_id(0) == 0)
        def _():
            dfg_ref[...] = jnp.zeros_like(dfg_ref)
            loss_ref[...] = jnp.zeros_like(loss_ref)

        xv = x_ref[...]
        fg = fg_ref[...]
        r = lax.rsqrt(jnp.mean(xv * xv, axis=-1, keepdims=True) + EPS)
        nh = xv * r
        e = nh * fg - t_ref[...]
        loss_ref[...] += 0.5 * jnp.sum(jnp.mean(e * e, axis=-1, keepdims=True))
        dy = e * (1.0 / D)
        dfg_ref[...] += jnp.sum(dy * nh, axis=0, keepdims=True)
        dnh = dy * fg
        dx_ref[...] = r * (dnh - nh * jnp.mean(dnh * nh, axis=-1, keepdims=True))

    row = pl.BlockSpec((tr, D), lambda i: (i, 0))
    vec = pl.BlockSpec((1, D), lambda i: (0, 0))
    tile = pl.BlockSpec((8, LANES), lambda i: (0, 0))
    return _pc(body, name=name, grid=(S // tr,), in_specs=[row, row, vec], out_specs=[row, vec, tile],
               out_shape=[SDS((S, D), F32), SDS((1, D), F32), SDS((8, LANES), F32)],
               compiler_params=_params(1))(x, target, fgain)


def _shift_down(v, k, rows):
    return jnp.where(rows >= k, pltpu.roll(v, k, axis=0), 0.0)


def _shift_up(v, k, rows):
    n = v.shape[0]
    return jnp.where(rows < n - k, pltpu.roll(v, n - k, axis=0), 0.0)


CONV_HALO = 8


def _down(v, k, rows, at_top):
    r = pltpu.roll(v, k, axis=0)
    return jnp.where(rows >= k, r, 0.0) if at_top else r


def _up(v, k, rows, at_bottom):
    n = v.shape[0]
    r = pltpu.roll(v, n - k, axis=0)
    return jnp.where(rows < n - k, r, 0.0) if at_bottom else r


def _conv(uv, w, b, rows, at_top):
    return ((b + _down(uv, 2, rows, at_top) * w[0:1]) + _down(uv, 1, rows, at_top) * w[1:2]) + uv * w[2:3]


def _conv_chunk_rows(S):
    return _pick(max(S // 2, 1), (128,))


def _convgate_fwd(u, cw, cb, name):
    S, F2 = u.shape
    DFF = F2 // 2
    tc = _pick(DFF, (256, 128))
    sub = min(tc, LANES)
    nj = DFF // tc
    R = _conv_chunk_rows(S)

    def body(ua_ref, ug_ref, wa_ref, wg_ref, ba_ref, bg_ref, o_ref):
        for q in range(tc // sub):
            sl = slice(q * sub, (q + 1) * sub)
            wa, wg, ba, bg = wa_ref[:, sl], wg_ref[:, sl], ba_ref[:, sl], bg_ref[:, sl]
            for r0 in range(0, S, R):
                lo = max(r0 - CONV_HALO, 0)
                rows = lax.broadcasted_iota(jnp.int32, (r0 + R - lo, sub), 0)
                ya = _conv(ua_ref[lo:r0 + R, sl], wa, ba, rows, lo == 0)[r0 - lo:]
                yg = _conv(ug_ref[lo:r0 + R, sl], wg, bg, rows, lo == 0)[r0 - lo:]
                o_ref[r0:r0 + R, sl] = (yg * jax.nn.sigmoid(yg) * ya).astype(BF16)

    col = lambda off: pl.BlockSpec((S, tc), lambda j: (0, j + off))
    w3 = lambda off: pl.BlockSpec((3, tc), lambda j: (0, j + off))
    b1 = lambda off: pl.BlockSpec((1, tc), lambda j: (0, j + off))
    return _pc(body, name=name, grid=(nj,), in_specs=[col(0), col(nj), w3(0), w3(nj), b1(0), b1(nj)],
               out_specs=col(0), out_shape=SDS((S, DFF), BF16), compiler_params=_params(1))(u, u, cw, cw, cb, cb)


def _convgate_bwd(u, dact, cw, cb, name):
    S, F2 = u.shape
    DFF = F2 // 2
    tc = _pick(DFF, (256, 128))
    sub = min(tc, LANES)
    nj = DFF // tc
    R = _conv_chunk_rows(S)

    def body(ua_ref, ug_ref, da_ref, wa_ref, wg_ref, ba_ref, bg_ref, du_ref, dcw_ref, dcb_ref):
        for q in range(tc // sub):
            sl = slice(q * sub, (q + 1) * sub)
            wa, wg, ba, bg = wa_ref[:, sl], wg_ref[:, sl], ba_ref[:, sl], bg_ref[:, sl]
            sums = [[None] * 4, [None] * 4]
            for r0 in range(0, S, R):
                lo, hi = max(r0 - CONV_HALO, 0), min(r0 + R + CONV_HALO, S)
                top, bottom, inner = lo == 0, hi == S, slice(r0 - lo, r0 - lo + R)
                rows = lax.broadcasted_iota(jnp.int32, (hi - lo, sub), 0)
                ua, ug, da = ua_ref[lo:hi, sl], ug_ref[lo:hi, sl], da_ref[lo:hi, sl]
                ya = _conv(ua, wa, ba, rows, top)
                yg = _conv(ug, wg, bg, rows, top)
                s = jax.nn.sigmoid(yg)
                dya = da * (yg * s)
                dyg = da * ya * (s * (1.0 + yg * (1.0 - s)))
                for h, (dy, uv, w) in enumerate(((dya, ua, wa), (dyg, ug, wg))):
                    du = (dy * w[2:3] + _up(dy, 1, rows, bottom) * w[1:2]) + _up(dy, 2, rows, bottom) * w[0:1]
                    du_ref[h, r0:r0 + R, sl] = du[inner].astype(BF16)
                    terms = [dy * _down(uv, 2, rows, top), dy * _down(uv, 1, rows, top), dy * uv, dy]
                    for i, term in enumerate(terms):
                        part = jnp.sum(term[inner], axis=0, keepdims=True)
                        sums[h][i] = part if sums[h][i] is None else sums[h][i] + part
            for h in range(2):
                for i in range(3):
                    dcw_ref[h, i:i + 1, sl] = sums[h][i]
                dcb_ref[h, :, sl] = sums[h][3]

    col = lambda off: pl.BlockSpec((S, tc), lambda j: (0, j + off))
    w3 = lambda off: pl.BlockSpec((3, tc), lambda j: (0, j + off))
    b1 = lambda off: pl.BlockSpec((1, tc), lambda j: (0, j + off))
    return _pc(body, name=name, grid=(nj,),
               in_specs=[col(0), col(nj), col(0), w3(0), w3(nj), b1(0), b1(nj)],
               out_specs=[pl.BlockSpec((2, S, tc), lambda j: (0, 0, j)), pl.BlockSpec((2, 3, tc), lambda j: (0, 0, j)),
                          pl.BlockSpec((2, 1, tc), lambda j: (0, 0, j))],
               out_shape=[SDS((2, S, DFF), BF16), SDS((2, 3, DFF), F32), SDS((2, 1, DFF), F32)],
               compiler_params=_params(1))(u, u, dact, cw, cw, cb, cb)


def _pool_diff(h, name):
    S, D = h.shape
    G = len(POOL_WINDOWS)
    CG = D // G
    tc = min(CG, LANES)
    per = CG // tc

    def body(h_ref, d_ref):
        g = pl.program_id(0)
        rows = lax.broadcasted_iota(jnp.int32, (S, tc), 0)
        for gi, w in enumerate(POOL_WINDOWS):
            @pl.when(g == gi)
            def _(w=w):
                hv = h_ref[...]
                s, k = hv, 1
                while k < w:
                    s = s + _shift_down(s, k, rows)
                    k *= 2
                count = jnp.minimum(rows + 1, w).astype(F32)
                d_ref[...] = (s / count - hv).astype(BF16)

    spec = pl.BlockSpec((S, tc), lambda g, j: (0, g * per + j))
    return _pc(body, name=name, grid=(G, per), in_specs=[spec], out_specs=spec, out_shape=SDS((S, D), BF16),
               compiler_params=_params(2))(h)


def _pool_diff_bwd(dd, name):
    S, D = dd.shape
    G = len(POOL_WINDOWS)
    CG = D // G
    tc = min(CG, LANES)
    per = CG // tc

    def body(dd_ref, o_ref):
        g = pl.program_id(0)
        rows = lax.broadcasted_iota(jnp.int32, (S, tc), 0)
        for gi, w in enumerate(POOL_WINDOWS):
            @pl.when(g == gi)
            def _(w=w):
                dv = dd_ref[...]
                count = jnp.minimum(rows + 1, w).astype(F32)
                s, k = dv / count, 1
                while k < w:
                    s = s + _shift_up(s, k, rows)
                    k *= 2
                o_ref[...] = s - dv

    spec = pl.BlockSpec((S, tc), lambda g, j: (0, g * per + j))
    return _pc(body, name=name, grid=(G, per), in_specs=[spec], out_specs=spec, out_shape=SDS((S, D), F32),
               compiler_params=_params(2))(dd)


def _pool_mm(d, w, res, gate, name):
    S, D = d.shape
    G, CG, _ = w.shape
    tm = _pick(S, (1024, 512, 256, 128))

    def body(d_ref, w_ref, res_ref, gate_ref, o_ref, e_ref):
        acc = jnp.dot(d_ref[...], w_ref[...], preferred_element_type=F32)
        o_ref[...] = res_ref[...] + gate_ref[...] * acc
        e_ref[...] = acc.astype(BF16)

    blk = pl.BlockSpec((tm, CG), lambda g, i: (i, g))
    return _pc(body, name=name, grid=(G, S // tm),
               in_specs=[blk, pl.BlockSpec((None, CG, CG), lambda g, i: (g, 0, 0)), blk,
                         pl.BlockSpec((1, CG), lambda g, i: (0, g))],
               out_specs=[blk, blk], out_shape=[SDS((S, D), F32), SDS((S, D), BF16)],
               compiler_params=_params(2))(d, w, res, gate)


def _pool_mm_bwd(de, d, w, name):
    S, D = de.shape
    G, CG, _ = w.shape
    tm = _pick(S, (1024, 512, 256, 128))
    ns = S // tm

    def body(de_ref, d_ref, w_ref, dd_ref, dw_ref, acc_ref):
        i = pl.program_id(1)

        @pl.when(i == 0)
        def _():
            acc_ref[...] = jnp.zeros_like(acc_ref)

        dev = de_ref[...]
        dd_ref[...] = lax.dot_general(dev, w_ref[...], (((1,), (1,)), ((), ())), preferred_element_type=F32)
        acc_ref[...] += lax.dot_general(d_ref[...], dev, (((0,), (0,)), ((), ())), preferred_element_type=F32)

        @pl.when(i == ns - 1)
        def _():
            dw_ref[...] = acc_ref[...].astype(BF16)

    blk = pl.BlockSpec((tm, CG), lambda g, i: (i, g))
    wsp = pl.BlockSpec((None, CG, CG), lambda g, i: (g, 0, 0))
    return _pc(body, name=name, grid=(G, ns), in_specs=[blk, blk, wsp], out_specs=[blk, wsp],
               out_shape=[SDS((S, D), F32), SDS((G, CG, CG), BF16)], scratch_shapes=[pltpu.VMEM((CG, CG), F32)],
               compiler_params=_params(2))(de, d, w)


def _log_sigmoid(z):
    return jnp.minimum(z, 0.0) - jnp.log(1.0 + jnp.exp(-jnp.abs(z)))


def _dot2(a, tri):
    hi = a.astype(BF16)
    lo = (a - hi.astype(F32)).astype(BF16)
    return jnp.dot(hi, tri, preferred_element_type=F32) + jnp.dot(lo, tri, preferred_element_type=F32)


_NT = (((1,), (1,)), ((), ()))
_TN = (((0,), (0,)), ((), ()))


def _forget_cumsum(flog, bf, name):
    S, W = flog.shape
    tb = _pick(S, (128,))

    def body(f_ref, b_ref, o_ref):
        r = lax.broadcasted_iota(jnp.int32, (tb, tb), 0)
        c = lax.broadcasted_iota(jnp.int32, (tb, tb), 1)
        tri = (c <= r).astype(F32)
        carry = jnp.zeros((1, W), F32)
        for q in range(S // tb):
            ls = _log_sigmoid(f_ref[q * tb:(q + 1) * tb, :] + b_ref[...])
            o_ref[q * tb:(q + 1) * tb, :] = carry + jnp.dot(tri, ls, preferred_element_type=F32,
                                                            precision=lax.Precision.HIGHEST)
            carry = carry + jnp.sum(ls, axis=0, keepdims=True)

    return _pc(body, name=name, out_shape=SDS((S, W), F32))(flog, bf)


def _forget_cumsum_bwd(dF, flog, bf, name):
    S, W = flog.shape
    tb = _pick(S, (128,))

    def body(d_ref, f_ref, b_ref, o_ref, db_ref):
        r = lax.broadcasted_iota(jnp.int32, (tb, tb), 0)
        c = lax.broadcasted_iota(jnp.int32, (tb, tb), 1)
        tri = (c >= r).astype(F32)
        carry = jnp.zeros((1, W), F32)
        db = jnp.zeros((1, W), F32)
        for q in reversed(range(S // tb)):
            dv = d_ref[q * tb:(q + 1) * tb, :]
            dls = carry + jnp.dot(tri, dv, preferred_element_type=F32, precision=lax.Precision.HIGHEST)
            carry = carry + jnp.sum(dv, axis=0, keepdims=True)
            dfl = dls * jax.nn.sigmoid(-(f_ref[q * tb:(q + 1) * tb, :] + b_ref[...]))
            o_ref[q * tb:(q + 1) * tb, :] = dfl
            db = db + jnp.sum(dfl, axis=0, keepdims=True)
        db_ref[...] = db

    return _pc(body, name=name, out_shape=[SDS((S, W), F32), SDS((1, W), F32)])(dF, flog, bf)


def _heads_per_step(n_heads):
    return 4 if n_heads % 4 == 0 else 2 if n_heads % 2 == 0 else 1


def _sb_fwd(qkv, NH, NSB, HD, T, name):
    S = qkv.shape[0]
    nq = S // T
    scale = HD ** -0.5
    HB = _heads_per_step(NSB)
    W = HB * HD

    def body(q_ref, k_ref, v_ref, o_ref, tot_ref):
        i = pl.program_id(1)
        row = lax.broadcasted_iota(jnp.int32, (T, T), 0)
        col = lax.broadcasted_iota(jnp.int32, (T, T), 1)
        upper = (row > col).astype(BF16)
        heads = [slice(hh * HD, (hh + 1) * HD) for hh in range(HB)]
        qs = [q_ref[:, cs] for cs in heads]

        def blk(kb, state, diag):
            sl = pl.ds(pl.multiple_of(kb * T, T), T)
            out = []
            for hh, cs in enumerate(heads):
                carry, acc = state[2 * hh], state[2 * hh + 1]
                k, v = k_ref[sl, cs], v_ref[sl, cs]
                z = lax.dot_general(qs[hh], k, _NT, preferred_element_type=F32) * scale
                ls = _log_sigmoid(z)
                lr = ls - z
                if diag:
                    lr = jnp.where(col < row, lr, 0.0)
                rest = _dot2(lr, upper) + carry
                w = jnp.exp(ls + rest)
                if diag:
                    w = jnp.where(col < row, w, 0.0)
                out += [carry + jnp.sum(lr, axis=1, keepdims=True),
                        acc + jnp.dot(w.astype(BF16), v, preferred_element_type=F32)]
            return tuple(out)

        state = blk(i, (jnp.zeros((T, 1), F32), jnp.zeros((T, HD), F32)) * HB, True)
        state = lax.fori_loop(0, i, lambda jj, st: blk(i - 1 - jj, st, False), state)
        for hh, cs in enumerate(heads):
            o_ref[:, cs] = state[2 * hh + 1].astype(BF16)
            tot_ref[hh] = state[2 * hh]

    return _pc(body, name=name, grid=(NSB // HB, nq),
               in_specs=[pl.BlockSpec((T, W), lambda h, i: (i, h)),
                         pl.BlockSpec((S, W), lambda h, i: (0, NH // HB + h)),
                         pl.BlockSpec((S, W), lambda h, i: (0, 2 * NH // HB + h))],
               out_specs=[pl.BlockSpec((T, W), lambda h, i: (i, h)), pl.BlockSpec((HB, T, 1), lambda h, i: (h, i, 0))],
               out_shape=[SDS((S, NSB * HD), BF16), SDS((NSB, S, 1), F32)],
               compiler_params=_params(2))(qkv, qkv, qkv)


def _sb_bwd(qkv, do, tot, NH, NSB, HD, T, name):
    S = qkv.shape[0]
    nq = S // T
    scale = HD ** -0.5
    HB = _heads_per_step(NSB)
    W = HB * HD

    def body(q_ref, k_ref, v_ref, do_ref, tot_ref, dq_ref, dk_ref, dv_ref, dk_acc, dv_acc):
        i = pl.program_id(1)

        @pl.when(i == 0)
        def _():
            dk_acc[...] = jnp.zeros_like(dk_acc)
            dv_acc[...] = jnp.zeros_like(dv_acc)

        row = lax.broadcasted_iota(jnp.int32, (T, T), 0)
        col = lax.broadcasted_iota(jnp.int32, (T, T), 1)
        incl = (row <= col).astype(BF16)
        strict = (row < col).astype(BF16)
        heads = [slice(hh * HD, (hh + 1) * HD) for hh in range(HB)]
        qs = [q_ref[:, cs] for cs in heads]
        dos = [do_ref[:, cs] for cs in heads]
        tots = [tot_ref[hh] for hh in range(HB)]

        def blk(kb, state, diag):
            sl = pl.ds(pl.multiple_of(kb * T, T), T)
            out = []
            for hh, cs in enumerate(heads):
                cl, cg, dq = state[3 * hh], state[3 * hh + 1], state[3 * hh + 2]
                q, do_ = qs[hh], dos[hh]
                k, v = k_ref[sl, cs], v_ref[sl, cs]
                z = lax.dot_general(q, k, _NT, preferred_element_type=F32) * scale
                ls = _log_sigmoid(z)
                lr = ls - z
                if diag:
                    lr = jnp.where(col < row, lr, 0.0)
                rest = tots[hh] - (cl + _dot2(lr, incl))
                w = jnp.exp(ls + rest)
                if diag:
                    w = jnp.where(col < row, w, 0.0)
                g = lax.dot_general(do_, v, _NT, preferred_element_type=F32) * w
                dv_acc[sl, cs] += lax.dot_general(w.astype(BF16), do_, _TN, preferred_element_type=F32)
                dlr = cg + _dot2(g, strict)
                dz = g * jnp.exp(lr) - dlr * jnp.exp(ls)
                if diag:
                    dz = jnp.where(col < row, dz, 0.0)
                dzb = (dz * scale).astype(BF16)
                dk_acc[sl, cs] += lax.dot_general(dzb, q, _TN, preferred_element_type=F32)
                out += [cl + jnp.sum(lr, axis=1, keepdims=True), cg + jnp.sum(g, axis=1, keepdims=True),
                        dq + jnp.dot(dzb, k, preferred_element_type=F32)]
            return tuple(out)

        zero = jnp.zeros((T, 1), F32)
        state = lax.fori_loop(0, i, lambda kb, st: blk(kb, st, False), (zero, zero, jnp.zeros((T, HD), F32)) * HB)
        state = blk(i, state, True)
        for hh, cs in enumerate(heads):
            dq_ref[:, cs] = state[3 * hh + 2].astype(BF16)

        @pl.when(i == nq - 1)
        def _():
            dk_ref[...] = dk_acc[...].astype(BF16)
            dv_ref[...] = dv_acc[...].astype(BF16)

    qblk = pl.BlockSpec((T, W), lambda h, i: (i, h))
    full = pl.BlockSpec((S, W), lambda h, i: (0, h))
    return _pc(body, name=name, grid=(NSB // HB, nq),
               in_specs=[qblk, pl.BlockSpec((S, W), lambda h, i: (0, NH // HB + h)),
                         pl.BlockSpec((S, W), lambda h, i: (0, 2 * NH // HB + h)), qblk,
                         pl.BlockSpec((HB, T, 1), lambda h, i: (h, i, 0))],
               out_specs=[qblk, full, full],
               out_shape=[SDS((S, NSB * HD), BF16)] * 3,
               scratch_shapes=[pltpu.VMEM((S, W), F32), pltpu.VMEM((S, W), F32)],
               compiler_params=_params(2))(qkv, qkv, qkv, do, tot)


def _fox_fwd(qkv, fcol, frow, NH, NSB, HD, T, name):
    S = qkv.shape[0]
    NFX = NH - NSB
    nq = S // T
    scale = HD ** -0.5
    HB = _heads_per_step(NFX) if NSB % _heads_per_step(NFX) == 0 else 1
    W = HB * HD

    def body(q_ref, k_ref, v_ref, fq_ref, fk_ref, o_ref, o32_ref, lse_ref):
        i = pl.program_id(1)
        row = lax.broadcasted_iota(jnp.int32, (T, T), 0)
        col = lax.broadcasted_iota(jnp.int32, (T, T), 1)
        heads = [slice(hh * HD, (hh + 1) * HD) for hh in range(HB)]
        qs = [q_ref[:, cs] for cs in heads]
        fqs = [fq_ref[hh] for hh in range(HB)]

        def blk(kb, state, diag):
            sl = pl.ds(pl.multiple_of(kb * T, T), T)
            out = []
            for hh, cs in enumerate(heads):
                m, l, acc, rem = state[4 * hh:4 * hh + 4]
                k, v = k_ref[sl, cs], v_ref[sl, cs]
                s = lax.dot_general(qs[hh], k, _NT, preferred_element_type=F32) * scale + (fqs[hh] - fk_ref[hh, kb])
                if diag:
                    s = jnp.where(col <= row, s, NEG_BIG)
                m_new = jnp.maximum(m, jnp.max(s, axis=1, keepdims=True))
                p = jnp.exp(s - m_new)
                alpha = jnp.exp(m - m_new)
                hi = p.astype(BF16)
                lo = (p - hi.astype(F32)).astype(BF16)
                out += [m_new, alpha * l + jnp.sum(p, axis=1, keepdims=True),
                        alpha * acc + jnp.dot(hi, v, preferred_element_type=F32),
                        alpha * rem + jnp.dot(lo, v, preferred_element_type=F32)]
            return tuple(out)

        zero = jnp.zeros((T, HD), F32)
        state = blk(i, (jnp.full((T, 1), NEG_BIG, F32), jnp.zeros((T, 1), F32), zero, zero) * HB, True)
        state = lax.fori_loop(0, i, lambda kb, st: blk(kb, st, False), state)
        for hh, cs in enumerate(heads):
            m, l, acc, rem = state[4 * hh:4 * hh + 4]
            o_ref[:, cs] = (acc / l).astype(BF16)
            o32_ref[:, cs] = (acc + rem) / l
            lse_ref[hh] = m + jnp.log(l)

    vec = pl.BlockSpec((HB, T, 1), lambda h, i: (h, i, 0))
    oblk = pl.BlockSpec((T, W), lambda h, i: (i, h))
    return _pc(body, name=name, grid=(NFX // HB, nq),
               in_specs=[pl.BlockSpec((T, W), lambda h, i: (i, NSB // HB + h)),
                         pl.BlockSpec((S, W), lambda h, i: (0, (NH + NSB) // HB + h)),
                         pl.BlockSpec((S, W), lambda h, i: (0, (2 * NH + NSB) // HB + h)),
                         vec, pl.BlockSpec((HB, nq, 1, T), lambda h, i: (h, 0, 0, 0))],
               out_specs=[oblk, oblk, vec],
               out_shape=[SDS((S, NFX * HD), BF16), SDS((S, NFX * HD), F32), SDS((NFX, S, 1), F32)],
               compiler_params=_params(2))(qkv, qkv, qkv, fcol, frow)


def _fox_bwd(qkv, do, o, fcol, frow, lse, NH, NSB, HD, T, name):
    S = qkv.shape[0]
    NFX = NH - NSB
    nq = S // T
    scale = HD ** -0.5
    HB = _heads_per_step(NFX) if NSB % _heads_per_step(NFX) == 0 else 1
    W = HB * HD

    def body(q_ref, k_ref, v_ref, do_ref, o_ref, fq_ref, fk_ref, lse_ref, dq_ref, dk_ref, dv_ref, dfk_ref,
             dk_acc, dv_acc, dfk_acc):
        i = pl.program_id(1)

        @pl.when(i == 0)
        def _():
            dk_acc[...] = jnp.zeros_like(dk_acc)
            dv_acc[...] = jnp.zeros_like(dv_acc)
            dfk_acc[...] = jnp.zeros_like(dfk_acc)

        row = lax.broadcasted_iota(jnp.int32, (T, T), 0)
        col = lax.broadcasted_iota(jnp.int32, (T, T), 1)
        heads = [slice(hh * HD, (hh + 1) * HD) for hh in range(HB)]
        qs = [q_ref[:, cs] for cs in heads]
        dos = [do_ref[:, cs] for cs in heads]
        fqs = [fq_ref[hh] for hh in range(HB)]
        lses = [lse_ref[hh] for hh in range(HB)]
        deltas = [jnp.sum(dos[hh].astype(F32) * o_ref[:, cs], axis=1, keepdims=True) for hh, cs in enumerate(heads)]

        def blk(kb, dqs, diag):
            sl = pl.ds(pl.multiple_of(kb * T, T), T)
            out = []
            for hh, cs in enumerate(heads):
                q, do_ = qs[hh], dos[hh]
                k, v = k_ref[sl, cs], v_ref[sl, cs]
                s = lax.dot_general(q, k, _NT, preferred_element_type=F32) * scale + (fqs[hh] - fk_ref[hh, kb])
                p = jnp.exp(s - lses[hh])
                if diag:
                    p = jnp.where(col <= row, p, 0.0)
                ds = p * (lax.dot_general(do_, v, _NT, preferred_element_type=F32) - deltas[hh])
                dv_acc[sl, cs] += lax.dot_general(p.astype(BF16), do_, _TN, preferred_element_type=F32)
                dsb = (ds * scale).astype(BF16)
                dk_acc[sl, cs] += lax.dot_general(dsb, q, _TN, preferred_element_type=F32)
                dfk_acc[hh, kb] -= jnp.sum(ds, axis=0, keepdims=True)
                out.append(dqs[hh] + jnp.dot(dsb, k, preferred_element_type=F32))
            return tuple(out)

        dqs = lax.fori_loop(0, i, lambda kb, st: blk(kb, st, False), (jnp.zeros((T, HD), F32),) * HB)
        dqs = blk(i, dqs, True)
        for hh, cs in enumerate(heads):
            dq_ref[:, cs] = dqs[hh].astype(BF16)

        @pl.when(i == nq - 1)
        def _():
            dk_ref[...] = dk_acc[...].astype(BF16)
            dv_ref[...] = dv_acc[...].astype(BF16)
            dfk_ref[...] = dfk_acc[...]

    vec = pl.BlockSpec((HB, T, 1), lambda h, i: (h, i, 0))
    rowv = pl.BlockSpec((HB, nq, 1, T), lambda h, i: (h, 0, 0, 0))
    oblk = pl.BlockSpec((T, W), lambda h, i: (i, h))
    qblk = pl.BlockSpec((T, W), lambda h, i: (i, NSB // HB + h))
    full = pl.BlockSpec((S, W), lambda h, i: (0, h))
    return _pc(body, name=name, grid=(NFX // HB, nq),
               in_specs=[qblk, pl.BlockSpec((S, W), lambda h, i: (0, (NH + NSB) // HB + h)),
                         pl.BlockSpec((S, W), lambda h, i: (0, (2 * NH + NSB) // HB + h)),
                         qblk, oblk, vec, rowv, vec],
               out_specs=[oblk, full, full, rowv],
               out_shape=[SDS((S, NFX * HD), BF16)] * 3 + [SDS((NFX, nq, 1, T), F32)],
               scratch_shapes=[pltpu.VMEM((S, W), F32), pltpu.VMEM((S, W), F32), pltpu.VMEM((HB, nq, 1, T), F32)],
               compiler_params=_params(2))(qkv, qkv, qkv, do, o, fcol, frow, lse)


def _silu(c_all, name):
    def body(c_ref, o_ref):
        cv = c_ref[...]
        o_ref[...] = cv * jax.nn.sigmoid(cv)

    return _pc(body, name=name, out_shape=SDS(c_all.shape, F32))(c_all)


def _mod_project(cond, w_mod, name):
    L, D, C = w_mod.shape
    tk = _pick(D, (512, 256, 128))

    def body(c_ref, w_ref, o_ref):
        @pl.when(pl.program_id(1) == 0)
        def _():
            o_ref[...] = jnp.zeros_like(o_ref)

        o_ref[...] += jnp.dot(c_ref[...].astype(BF16), w_ref[...].astype(BF16), preferred_element_type=F32)

    return _pc(body, name=name, grid=(L, D // tk),
               in_specs=[pl.BlockSpec((16, tk), lambda l, k: (0, k)), pl.BlockSpec((None, tk, C), lambda l, k: (l, k, 0))],
               out_specs=pl.BlockSpec((None, 16, C), lambda l, k: (l, 0, 0)),
               out_shape=SDS((L, 16, C), F32), compiler_params=_params(2))(cond, w_mod)


def _adam_math(w, g, m, v):
    m = ADAM_B1 * m + (1.0 - ADAM_B1) * g
    v = ADAM_B2 * v + (1.0 - ADAM_B2) * (g * g)
    m_hat = m / (1.0 - ADAM_B1 ** ADAM_STEP)
    v_hat = v / (1.0 - ADAM_B2 ** ADAM_STEP)
    delta = -ADAM_LR * (m_hat / (jnp.sqrt(v_hat) + ADAM_EPS) + ADAM_WD * w)
    return delta, m, v


def _adamw(w, m, v, parts, layer, prev, name):
    L, R, C = w.shape
    NP = parts.shape[0]
    tr = _pick(R, (128, 64, 88, 32, 16, 8))
    nprev = 0 if prev is None else 4

    def body(w_ref, m_ref, v_ref, p_ref, *rest):
        g_ref, d_ref, mo_ref, vo_ref = rest[nprev:]
        g = p_ref[0].astype(F32)
        for j in range(1, NP):
            g = g + p_ref[j].astype(F32)
        delta, mn, vn = _adam_math(w_ref[...], g, m_ref[...], v_ref[...])
        g_ref[...] = g
        d_ref[...] = delta
        mo_ref[...] = mn
        vo_ref[...] = vn

    if tr == R and R > 512:
        tc = _pick(C, (256, 128))
        steps = C // tc
        blk = pl.BlockSpec((None, R, tc), lambda j: (layer, 0, j))
        in_specs = [blk, blk, blk, pl.BlockSpec((NP, R, tc), lambda j: (0, 0, j))]
    else:
        steps = R // tr
        blk = pl.BlockSpec((None, tr, C), lambda i: (layer, i, 0))
        in_specs = [blk, blk, blk, pl.BlockSpec((NP, tr, C), lambda i: (0, i, 0))]
    operands = [w, m, v, parts]
    aliases = {}
    if prev is not None:
        in_specs += [ANY] * 4
        operands += list(prev)
        aliases = {4 + q: q for q in range(4)}
    return _pc(body, name=name, grid=(steps,), in_specs=in_specs, out_specs=[blk] * 4,
               out_shape=[SDS(w.shape, F32)] * 4, input_output_aliases=aliases,
               compiler_params=_params(1))(*operands)


def _adamw_mod(w, m, v, cond_t, dmod, name):
    L, D, C = w.shape
    tr = _pick(D, (128, 64))

    def body(w_ref, m_ref, v_ref, ct_ref, dm_ref, g_ref, d_ref, mo_ref, vo_ref):
        ct = ct_ref[...]
        g = ct[:, 0:1] * dm_ref[0]
        for b in range(1, NDEV):
            g = g + ct[:, b:b + 1] * dm_ref[b]
        delta, mn, vn = _adam_math(w_ref[...], g, m_ref[...], v_ref[...])
        g_ref[...] = g
        d_ref[...] = delta
        mo_ref[...] = mn
        vo_ref[...] = vn

    blk = pl.BlockSpec((None, tr, C), lambda l, i: (l, i, 0))
    return _pc(body, name=name, grid=(L, D // tr),
               in_specs=[blk, blk, blk, pl.BlockSpec((tr, LANES), lambda l, i: (i, 0)),
                         pl.BlockSpec((NDEV, None, 1, C), lambda l, i: (0, l, 0, 0))],
               out_specs=[blk] * 4, out_shape=[SDS(w.shape, F32)] * 4, compiler_params=_params(2))(w, m, v, cond_t, dmod)


def _sum_parts(parts, name):
    NP, R, C = parts.shape
    tr = _pick(R, (256, 128, 64, 32, 16, 8))

    def body(p_ref, o_ref):
        g = p_ref[0]
        for j in range(1, NP):
            g = g + p_ref[j]
        o_ref[...] = g

    return _pc(body, name=name, grid=(R // tr,), in_specs=[pl.BlockSpec((NP, tr, C), lambda i: (0, i, 0))],
               out_specs=pl.BlockSpec((tr, C), lambda i: (i, 0)), out_shape=SDS((R, C), F32),
               compiler_params=_params(1))(parts)


def _pack(vecs, rows=None):
    flat = jnp.concatenate([v.reshape(-1).astype(F32) for v in vecs])
    n = flat.shape[0]
    r = rows if rows is not None else -(-n // (256 * LANES)) * 256
    return jnp.pad(flat, (0, r * LANES - n)).reshape(r, LANES)


def _unpack(packed, shapes):
    flat = packed.reshape(-1)
    out, off = [], 0
    for s in shapes:
        n = 1
        for d in s:
            n *= d
        out.append(flat[off:off + n].reshape(s))
        off += n
    return out


def kernel(x, c, w_mod, b_mod, norm_gain, w_attn_in, b_forget, w_attn_out, w_pool, pool_scale, w_up, conv_w, conv_b, w_down, final_gain, loss_target, m_w_mod, m_b_mod, m_norm_gain, m_w_attn_in, m_b_forget, m_w_attn_out, m_w_pool, m_pool_scale, m_w_up, m_conv_w, m_conv_b, m_w_down, m_final_gain, v_w_mod, v_b_mod, v_norm_gain, v_w_attn_in, v_b_forget, v_w_attn_out, v_w_pool, v_pool_scale, v_w_up, v_conv_w, v_conv_b, v_w_down, v_final_gain):
    _, S, D = x.shape
    L = w_mod.shape[0]
    CM = w_mod.shape[2]
    NFX = b_forget.shape[1]
    NH = 2 * NFX
    NSB = NH - NFX
    HD = D // NH
    CI = w_attn_in.shape[2]
    CU = w_up.shape[2]
    F2 = NDEV * CU
    DFF = F2 // 2
    G = len(POOL_WINDOWS)
    CG = D // G
    T = _pick(S, (256, 128))
    me = _idx(_me())
    x0 = x[0]
    target = loss_target[0]

    def layer_shards(l, group):
        if group == 1:
            shards = [w_up[l], w_down[l]]
        elif l % 2 == 0:
            shards = [w_attn_in[l // 2].T, w_attn_out[l // 2]]
        else:
            shards = [w_pool[l // 2].reshape(G * (CG // NDEV), CG)]
        return [s.astype(BF16) for s in shards]

    small_shapes = [(1, D), norm_gain.shape, pool_scale.shape, conv_w.shape]
    small_all = _all_gather([_pack([c, norm_gain, pool_scale, conv_w])], "gather_small")[0]
    per_dev = [_unpack(small_all[j], small_shapes) for j in range(NDEV)]
    c_all = jnp.concatenate([p[0] for p in per_dev] + [jnp.zeros((16 - NDEV, D), F32)], axis=0)
    gain_f = jnp.concatenate([p[1] for p in per_dev], axis=2)
    pscale_f = jnp.concatenate([p[2] for p in per_dev], axis=1)
    convw_f = jnp.concatenate([p[3] for p in per_dev], axis=2)

    cond_all = _silu(c_all, "cond_silu")
    mod_part = _mod_project(cond_all, w_mod, "mod_project")
    mod_all = _all_gather([mod_part], "gather_mod")[0]
    mod = lax.dynamic_index_in_dim(mod_all, me, axis=2, keepdims=False)
    mod = mod.transpose(1, 0, 2).reshape(L, NDEV * CM) + b_mod
    mods = mod.reshape(L, 6, 1, D)

    inflight = {}

    def gather_start(l, group, after):
        inflight[l, group], t = _gather_start(layer_shards(l, group), me, after, f"gather{l}_{group}_start")
        return t

    def gather_forward(l, group, after):
        inflight[l, group], t = _gather_forward(inflight[l, group], after, f"gather{l}_{group}_forward")
        return t

    def gather_wait(l, group, after):
        return _gather_wait(inflight.pop((l, group)), after, f"gather{l}_{group}_wait")

    def starts_at(l):
        if l % 2 == 0:
            return [(k, g) for k, g in [(l + 1, 0), (l + 1, 1), (l + 2, 0)] if k < L]
        return [(k, g) for k, g in [(l + 1, 1)] if k < L]

    tok = gather_start(0, 0, [mods])
    tok = gather_start(0, 1, [tok])
    tok = gather_forward(0, 0, [tok])
    mixer_w = gather_wait(0, 0, [tok])

    saved = []
    xl = x0
    for l in range(L):
        i = l // 2
        sh1, sc1, g1, sh2, sc2, g2 = [mods[l, q] for q in range(6)]
        gn1, gn2 = gain_f[l, 0:1], gain_f[l, 1:2]
        st = {"x": xl}
        after = [mixer_w[0]]
        for k, g in starts_at(l):
            after = [gather_start(k, g, after)]
        if starts_at(l):
            sh1 = sh1 + after[0][0, 0]
        if l % 2 == 0:
            win_t = mixer_w[0].reshape(NDEV * CI, D)
            wf_t = jnp.pad(win_t[3 * D:], ((0, LANES - NFX), (0, 0)))
            wout = mixer_w[1].reshape(D, D)
            h1 = _norm_mod(xl, gn1, sc1, sh1, BF16, f"norm1_{l}")
            qkv = _matmul(h1, win_t, mode="nt", name=f"qkv_{l}", out_dtype=BF16, n=3 * D)
            flog = _matmul(h1, wf_t, mode="nt", name=f"flog_{l}", out_dtype=F32, tn=LANES)
            bfp = jnp.pad(b_forget[i], (0, LANES - NFX)).reshape(1, LANES)
            Fc = _forget_cumsum(flog, bfp, f"fcum_{l}")
            f8 = Fc[:, :NFX].T
            fcol, frow = f8[:, :, None], f8.reshape(NFX, S // T, 1, T)
            o_sb, tot = _sb_fwd(qkv, NH, NSB, HD, T, f"sb_fwd_{l}")
            o_fx, o_fx32, lse = _fox_fwd(qkv, fcol, frow, NH, NSB, HD, T, f"fox_fwd_{l}")
            o = jnp.concatenate([o_sb, o_fx], axis=1)
            g1 = g1 + gather_forward(l, 1, [o])[0, 0]
            x1, y1 = _matmul(o, wout, mode="nn", name=f"attn_out_{l}", out_dtype=F32, res=xl, gate=g1, y_dtype=BF16)
            st.update(h1=h1, qkv=qkv, flog=flog, bfp=bfp, fcol=fcol, frow=frow, tot=tot, lse=lse, o=o, o_fx=o_fx32,
                      win_t=win_t, wf_t=wf_t, wout=wout, y1=y1)
        else:
            wpool = mixer_w[0].reshape(NDEV, G, CG // NDEV, CG).transpose(1, 0, 2, 3).reshape(G, CG, CG)
            h1 = _norm_mod(xl, gn1, sc1, sh1, F32, f"norm1_{l}")
            dpool = _pool_diff(h1, f"pool_diff_{l}")
            g1 = g1 + gather_forward(l, 1, [dpool])[0, 0]
            gp = g1 * pscale_f[i:i + 1]
            x1, e1 = _pool_mm(dpool, wpool, xl, gp, f"pool_mm_{l}")
            st.update(dpool=dpool, wpool=wpool, gp=gp, y1=e1)
        wup_g, wdown_g = gather_wait(l, 1, [x1])
        wdown_f = wdown_g.reshape(DFF, D)
        h2 = _norm_mod(x1, gn2, sc2, sh2, BF16, f"norm2_{l}")
        u = _matmul(h2, wup_g, mode="nn", name=f"ffn_up_{l}", out_dtype=F32, b_blocked=True)
        cb = conv_b[l].reshape(1, F2)
        act = _convgate_fwd(u, convw_f[l], cb, f"convgate_{l}")
        if l + 1 < L:
            g2 = g2 + gather_forward(l + 1, 0, [act])[0, 0]
        x2, y2 = _matmul(act, wdown_f, mode="nn", name=f"ffn_down_{l}", out_dtype=F32, res=x1, gate=g2, y_dtype=BF16,
                         tk=2 * CU)
        if l + 1 < L:
            mixer_w = gather_wait(l + 1, 0, [x2])
        st.update(x1=x1, h2=h2, u=u, cb=cb, act=act, y2=y2, wup_g=wup_g, wdown_f=wdown_f,
                  mod=(sh1, sc1, g1, sh2, sc2, g2), gn=(gn1, gn2))
        saved.append(st)
        xl = x2

    dx, d_fgain, loss_tile = _loss_head(xl, target, final_gain.reshape(1, D), "loss_head")
    loss = lax.psum(loss_tile[0, 0], ("x", "y", "c"))

    dmod_rows = [None] * L
    d_gain = [None] * L
    d_convw = [None] * L
    d_convb = [None] * L
    d_pscale = [None] * (L // 2)
    d_bf = [None] * ((L + 1) // 2)
    big = {"w_up": None, "w_down": None, "w_attn_out": None, "w_pool": None}
    attn_in_t = [None] * ((L + 1) // 2)

    def update(key, w, m, v, bufs, layer, tag):
        big[key] = _adamw(w, m, v, bufs, layer, big[key], f"adamw_{tag}")

    exchanges = []

    def exchange_start(l, keys, parts, after, two_level=False):
        tag = f"{l}_{'_'.join(keys)}"
        if two_level:
            state, t = _pair_start(parts, after, f"pair{tag}_start")
        else:
            state, t = _scatter_start(parts, me, after, f"scatter{tag}_start")
        exchanges.append([l, keys, tag, "pair" if two_level else "direct", state])
        return t

    def exchanges_advance(after):
        t = None
        for entry in [e for e in exchanges if e[3] == "pair"]:
            tag = entry[2]
            mine, theirs = _pair_wait(entry[4], after, f"pair{tag}_wait")
            sums = [_pair_sum(a, b, f"pair{tag}_sum{q}") for q, (a, b) in enumerate(zip(mine, theirs))]
            entry[4], t = _chips_start(sums, [], f"chips{tag}_start")
            entry[3] = "chips"
        return t

    def exchanges_finish(after, first_layer):
        for entry in [e for e in exchanges if e[0] >= first_layer]:
            exchanges.remove(entry)
            pl_, keys, tag, stage, state = entry
            if stage == "chips":
                bufs = _chips_wait(state, after, f"chips{tag}_wait")
            else:
                bufs = _scatter_wait(state, after, f"scatter{tag}_wait")
            for key, buf in zip(keys, bufs):
                if key == "up":
                    update("w_up", w_up, m_w_up, v_w_up, buf, pl_, f"up_{pl_}")
                elif key == "down":
                    update("w_down", w_down, m_w_down, v_w_down, buf, pl_, f"down_{pl_}")
                elif key == "out":
                    update("w_attn_out", w_attn_out, m_w_attn_out, v_w_attn_out, buf, pl_ // 2, f"attn_out_{pl_}")
                elif key == "in":
                    slab = lambda a: a[pl_ // 2].T[None]
                    attn_in_t[pl_ // 2] = _adamw(slab(w_attn_in), slab(m_w_attn_in), slab(v_w_attn_in), buf, 0, None,
                                                 f"adamw_attn_in_{pl_}")
                else:
                    wp3 = lambda a: a.reshape(a.shape[0], G * (CG // NDEV), CG)
                    update("w_pool", wp3(w_pool), wp3(m_w_pool), wp3(v_w_pool), buf, pl_ // 2, f"pool_{pl_}")

    tok = None
    for l in reversed(range(L)):
        i = l // 2
        st = saved[l]
        sh1, sc1, g1, sh2, sc2, g2 = st["mod"]
        if tok is not None:
            g2 = g2 + tok[0, 0]
        gn1, gn2 = st["gn"]
        dffn, dg2 = _gate_bwd(dx, st["y2"], g2, f"gate2_bwd_{l}")
        dact = _matmul(dffn, st["wdown_f"], mode="nt", name=f"ffn_down_dx_{l}", out_dtype=F32, tn=CU)
        dwdown = _matmul(st["act"], dffn, mode="tn", name=f"ffn_down_dw_{l}", out_dtype=BF16, tm=CU)
        dwdown = dwdown.reshape(NDEV, DFF // NDEV, D)
        du, dcw, dcb = _convgate_bwd(st["u"], dact, convw_f[l], st["cb"], f"convgate_bwd_{l}")
        dh2 = _matmul(du, st["wup_g"], mode="nt", name=f"ffn_up_dx_{l}", out_dtype=F32, a_split=True, b_blocked=True,
                      tn=2048)
        dwup = _matmul(st["h2"], du, mode="tn", name=f"ffn_up_dw_{l}", out_dtype=BF16, tn=CU, b_split=True,
                       out_blocked=True)
        dx, dgn2, dsc2, dsh2 = _norm_mod_bwd(dh2, st["x1"], dx, gn2, sc2, f"norm2_bwd_{l}")
        d_convw[l] = jnp.concatenate([dcw[0], dcw[1]], axis=1)
        d_convb[l] = jnp.concatenate([dcb[0], dcb[1]], axis=1)
        two_level = l < 2
        tok = exchange_start(l, ["up", "down"], [dwup, dwdown], [], two_level=two_level)
        if l % 2 == 0:
            dy1, dg1 = _gate_bwd(dx, st["y1"], g1 + tok[0, 0], f"gate1_bwd_{l}")
            do = _matmul(dy1, st["wout"], mode="nt", name=f"attn_out_dx_{l}", out_dtype=BF16)
            dwout = _matmul(st["o"], dy1, mode="tn", name=f"attn_out_dw_{l}", out_dtype=BF16).reshape(NDEV, D // NDEV, D)
            tot = st["tot"]
            if l == 0:
                tot = tot + exchange_start(l, ["out"], [dwout], [])[0, 0]
            dq_s, dk_s, dv_s = _sb_bwd(st["qkv"], do, tot, NH, NSB, HD, T, f"sb_bwd_{l}")
            lse = st["lse"] + exchanges_advance([dq_s])[0, 0] if two_level else st["lse"]
            dq_f, dk_f, dv_f, dfk = _fox_bwd(st["qkv"], do, st["o_fx"], st["fcol"], st["frow"], lse, NH, NSB, HD, T,
                                             f"fox_bwd_{l}")
            dqkv = jnp.concatenate([dq_s, dq_f, dk_s, dk_f, dv_s, dv_f], axis=1)
            dF = jnp.pad(dfk.reshape(NFX, S).T, ((0, 0), (0, LANES - NFX)))
            dflog, dbf = _forget_cumsum_bwd(dF, st["flog"], st["bfp"], f"fcum_bwd_{l}")
            dflog_b = dflog.astype(BF16)
            dh1 = _matmul(dqkv, st["win_t"], mode="nn", name=f"qkv_dx_{l}", out_dtype=F32)
            dh1 = _matmul(dflog_b, st["wf_t"], mode="nn", name=f"flog_dx_{l}", out_dtype=F32, res=dh1,
                          gate=jnp.ones((1, D), F32))
            dwqkv_t = _matmul(dqkv, st["h1"], mode="tn", name=f"qkv_dw_{l}", out_dtype=BF16)
            dwf_t = _matmul(dflog_b, st["h1"], mode="tn", name=f"flog_dw_{l}", out_dtype=BF16, tm=LANES)
            dwin_t = jnp.concatenate([dwqkv_t, dwf_t[:NFX]], axis=0).reshape(NDEV, CI, D)
            d_bf[i] = dbf[0, :NFX]
            keys, parts = (["in"], [dwin_t]) if l == 0 else (["in", "out"], [dwin_t, dwout])
        else:
            de, dgp = _gate_bwd(dx, st["y1"], st["gp"] + tok[0, 0], f"gate1_bwd_{l}")
            dg1 = dgp * pscale_f[i:i + 1]
            d_pscale[i] = dgp * g1
            dd, dwp = _pool_mm_bwd(de, st["dpool"], st["wpool"], f"pool_mm_bwd_{l}")
            if two_level:
                gn1 = gn1 + exchanges_advance([dd])[0, 0]
            dh1 = _pool_diff_bwd(dd, f"pool_diff_bwd_{l}")
            keys = ["pool"]
            parts = [dwp.reshape(G, NDEV, CG // NDEV, CG).transpose(1, 0, 2, 3).reshape(NDEV, G * (CG // NDEV), CG)]
        dx, dgn1, dsc1, dsh1 = _norm_mod_bwd(dh1, st["x"], dx, gn1, sc1, f"norm1_bwd_{l}")
        dmod_rows[l] = jnp.concatenate([dsh1, dsc1, dg1, dsh2, dsc2, dg2], axis=1)
        d_gain[l] = jnp.concatenate([dgn1, dgn2], axis=0)
        if l > 0:
            tok = exchange_start(l, keys, parts, [])
            exchanges_finish([dx, tok], l + 1)

    grad_x = dx[None]

    small_grads = [jnp.stack(dmod_rows), jnp.stack(d_gain), jnp.stack(d_pscale), jnp.stack(d_convw),
                   jnp.stack(d_convb), jnp.stack(d_bf), d_fgain]
    sg_shapes = [(L, 6 * D), (L, 2, D), (L // 2, D), (L, 3, F2), (L, F2), ((L + 1) // 2, NFX), (D,)]
    sg_all = _all_gather([_pack(small_grads)], "gather_small_grads")[0]
    tok = exchange_start(0, keys, parts, [sg_all])
    exchanges_finish([tok], 1)
    sg_all = sg_all + tok[0, 0]
    sg_sum = _unpack(_sum_parts(sg_all, "sum_small_grads"), sg_shapes)
    g_bmod, g_gain_f, g_pscale_f, g_convw_f, g_convb, g_bf, g_fgain = sg_sum
    shard = lambda a, n, axis: lax.dynamic_slice_in_dim(a, me * n, n, axis=axis)
    g_small = [g_bmod, shard(g_gain_f, D // NDEV, 2), shard(g_pscale_f, D // NDEV, 1), shard(g_convw_f, CU, 2), g_convb,
               g_bf, g_fgain]
    w_small = [b_mod, norm_gain, pool_scale, conv_w, conv_b, b_forget, final_gain]
    m_small = [m_b_mod, m_norm_gain, m_pool_scale, m_conv_w, m_conv_b, m_b_forget, m_final_gain]
    v_small = [v_b_mod, v_norm_gain, v_pool_scale, v_conv_w, v_conv_b, v_b_forget, v_final_gain]
    small_raw = _adamw(_pack(w_small)[None], _pack(m_small)[None], _pack(v_small)[None], _pack(g_small)[None], 0, None,
                       "adamw_small")
    small_out = [_unpack(a[0], [w.shape for w in w_small]) for a in small_raw]

    dmod_all = jnp.stack([_unpack(sg_all[j], sg_shapes[:1])[0] for j in range(NDEV)])
    dmod_mine = lax.dynamic_slice_in_dim(dmod_all.reshape(NDEV, L, NDEV, CM), me, 1, axis=2)
    cond_t = jnp.pad(cond_all[:NDEV].T, ((0, 0), (0, LANES - NDEV)))
    mod_out = _adamw_mod(w_mod, m_w_mod, v_w_mod, cond_t, dmod_mine, "adamw_mod")
    exchanges_finish([mod_out[0], small_raw[0]], 0)

    pool4 = lambda a: a.reshape(w_pool.shape)
    names = ["w_mod", "b_mod", "norm_gain", "w_attn_in", "b_forget", "w_attn_out", "w_pool", "pool_scale", "w_up", "conv_w",
             "conv_b", "w_down", "final_gain"]
    small_pos = {"b_mod": 0, "norm_gain": 1, "pool_scale": 2, "conv_w": 3, "conv_b": 4, "b_forget": 5, "final_gain": 6}
    outs = []
    for kind in range(4):
        for nm in names:
            if nm == "w_mod":
                outs.append(mod_out[kind])
            elif nm in small_pos:
                outs.append(small_out[kind][small_pos[nm]])
            elif nm == "w_pool":
                outs.append(pool4(big[nm][kind]))
            elif nm == "w_attn_in":
                outs.append(jnp.stack([per_layer[kind][0] for per_layer in attn_in_t]).transpose(0, 2, 1))
            else:
                outs.append(big[nm][kind])
    return (loss, grad_x, *outs)
```
